```python
import math
import jax, jax.numpy as jnp
from jax import lax
import numpy as np

D_MODEL = 1024
BATCH = 8
SEQ = 2048
DEPTH = 2

GRID_W = 64
CTX_LEN = 256
SSD_WIDTH = D_MODEL // 2
SSD_HEAD_DIM = 64
SSD_HEADS = SSD_WIDTH // SSD_HEAD_DIM
SSD_GROUPS = 2
SSD_STATE = 128
SSD_CONV = 5
SSD_CHUNK = 128
SSD_XBC = SSD_WIDTH + 2 * SSD_GROUPS * SSD_STATE
SSD_COLS = SSD_WIDTH + SSD_XBC + 2 * SSD_HEADS
CONF_WIDTH = D_MODEL // 4
CONF_KERNEL = 31
CONF_COLS = 2 * CONF_WIDTH
S5_WIDTH = D_MODEL // 4
S5_GROUP_CH = 16
S5_GROUPS = S5_WIDTH // S5_GROUP_CH
S5_STATE = 64
S5_COLS = S5_WIDTH
IN_COLS = SSD_COLS + CONF_COLS + S5_COLS
MIX_WIDTH = SSD_WIDTH + CONF_WIDTH + S5_WIDTH
N_EXPERTS = 16
N_EXPERT_GROUPS = 4
EXPERTS_PER_GROUP = N_EXPERTS // N_EXPERT_GROUPS
TOP_K = 2
EXPERT_FF = D_MODEL
N_MOD = 6
EPS = 1e-6
F32 = jnp.float32

kernel_name = "hybrid_ssd_conformer_s5_moe_block"


def rms_norm(x, g):
    x32 = x.astype(F32)
    y = x32 * lax.rsqrt(jnp.mean(x32 * x32, axis=-1, keepdims=True) + EPS)
    return (y * g.astype(F32)).astype(x.dtype)


def layer_norm(x, g, b):
    x32 = x.astype(F32)
    xc = x32 - jnp.mean(x32, axis=-1, keepdims=True)
    y = xc * lax.rsqrt(jnp.mean(xc * xc, axis=-1, keepdims=True) + EPS)
    return (y * g.astype(F32) + b.astype(F32)).astype(x.dtype)


def depthwise_conv(x, w, b):
    k = w.shape[0]
    y = lax.conv_general_dilated(
        x, w[:, None, :].astype(x.dtype), (1,), [(k // 2, k // 2)],
        dimension_numbers=("NWC", "WIO", "NWC"), feature_group_count=x.shape[-1])
    return y + b.astype(x.dtype)


def grid_pos_embed(rows, dtype):
    rr, cc = jnp.meshgrid(jnp.arange(rows, dtype=F32), jnp.arange(GRID_W, dtype=F32), indexing="ij")
    quarter = D_MODEL // 4
    inv_freq = jnp.exp(-math.log(10000.0) * jnp.arange(quarter, dtype=F32) / quarter)

    def emb(pos):
        ang = pos.reshape(-1)[:, None] * inv_freq[None, :]
        return jnp.concatenate([jnp.sin(ang), jnp.cos(ang)], axis=-1)

    return jnp.concatenate([emb(rr), emb(cc)], axis=-1).astype(dtype)


def segsum(a):
    t = a.shape[-1]
    rep = jnp.broadcast_to(a[..., :, None], a.shape + (t,))
    strict = jnp.tril(jnp.ones((t, t), dtype=bool), -1)
    cs = jnp.cumsum(jnp.where(strict, rep, 0.0), axis=-2)
    return jnp.where(jnp.tril(jnp.ones((t, t), dtype=bool)), cs, -jnp.inf)


def ssd_chunked_scan(xdt, a, bm, cm, h0):
    bsz, n, nh, hp = xdt.shape
    nc = n // SSD_CHUNK
    X = xdt.reshape(bsz, nc, SSD_CHUNK, nh, hp)
    A = a.reshape(bsz, nc, SSD_CHUNK, nh).transpose(0, 3, 1, 2)
    Bc = bm.reshape(bsz, nc, SSD_CHUNK, nh, -1)
    Cc = cm.reshape(bsz, nc, SSD_CHUNK, nh, -1)
    a_cs = jnp.cumsum(A, axis=-1)
    scores = jnp.einsum("bclhn,bcshn->bhcls", Cc, Bc) * jnp.exp(segsum(A))
    y_diag = jnp.einsum("bhcls,bcshp->bclhp", scores, X)
    decay_states = jnp.exp(a_cs[..., -1:] - a_cs)
    states = jnp.einsum("bclhn,bhcl,bclhp->bchpn", Bc, decay_states, X)
    states = jnp.concatenate([h0[:, None], states], axis=1)
    chunk_tot = jnp.pad(a_cs[..., -1], ((0, 0), (0, 0), (1, 0)))
    new_states = jnp.einsum("bhzc,bchpn->bzhpn", jnp.exp(segsum(chunk_tot)), states)
    states_in, final = new_states[:, :-1], new_states[:, -1]
    y_off = jnp.einsum("bclhn,bchpn,bhcl->bclhp", Cc, states_in, jnp.exp(a_cs))
    return (y_diag + y_off).reshape(bsz, n, nh, hp), final


def ssd_mixer(cols, conv_w, conv_b, dt_bias, a_log, d_skip, norm_g, h0):
    bsz, n, _ = cols.shape
    z = cols[..., :SSD_WIDTH]
    xbc = cols[..., SSD_WIDTH:SSD_WIDTH + SSD_XBC]
    dt_raw = cols[..., SSD_WIDTH + SSD_XBC:]
    xbc = jax.nn.silu(depthwise_conv(xbc, conv_w, conv_b)).astype(F32)
    gn = SSD_GROUPS * SSD_STATE
    rep = SSD_HEADS // SSD_GROUPS
    xs = xbc[..., :SSD_WIDTH].reshape(bsz, n, SSD_HEADS, SSD_HEAD_DIM)
    bm = jnp.repeat(xbc[..., SSD_WIDTH:SSD_WIDTH + gn].reshape(bsz, n, SSD_GROUPS, SSD_STATE), rep, axis=2)
    cm = jnp.repeat(xbc[..., SSD_WIDTH + gn:].reshape(bsz, n, SSD_GROUPS, SSD_STATE), rep, axis=2)
    dt = jax.nn.softplus(dt_raw.astype(F32).reshape(bsz, n, 2, SSD_HEADS) + dt_bias.astype(F32))
    a = -jnp.exp(a_log.astype(F32))
    y = d_skip.astype(F32)[:, None] * xs
    finals = []
    for d in range(2):
        dt_d = dt[:, :, d]
        args = (xs * dt_d[..., None], dt_d * a[d], bm, cm)
        if d == 1:
            args = tuple(jnp.flip(t, axis=1) for t in args)
        y_d, fin = ssd_chunked_scan(*args, h0[d])
        y = y + (y_d if d == 0 else jnp.flip(y_d, axis=1))
        finals.append(fin)
    y = y.reshape(bsz, n, SSD_WIDTH) * jax.nn.silu(z.astype(F32))
    return rms_norm(y, norm_g).astype(cols.dtype), finals


def conformer_conv(cols, dw_w, dw_b, ln_g, ln_b, pw_w, pw_b):
    u = cols[..., :CONF_WIDTH] * jax.nn.sigmoid(cols[..., CONF_WIDTH:])
    u = depthwise_conv(u, dw_w, dw_b)
    u = jax.nn.silu(layer_norm(u, ln_g, ln_b))
    return u @ pw_w + pw_b


def _complex_linear_combine(e1, e2):
    a1r, a1i, b1r, b1i = e1
    a2r, a2i, b2r, b2i = e2
    return (a2r * a1r - a2i * a1i,
            a2r * a1i + a2i * a1r,
            a2r * b1r - a2i * b1i + b2r,
            a2r * b1i + a2i * b1r + b2i)


def s5_mixer(u, lam_re, lam_im, log_step, b_re, b_im, c_re, c_im, d_skip, glu_w, glu_b, h0):
    bsz, n, _ = u.shape
    u32 = u.astype(F32)
    ug = u32.reshape(bsz, n, S5_GROUPS, S5_GROUP_CH)
    lam_re, lam_im = lam_re.astype(F32), lam_im.astype(F32)
    b_re, b_im, c_re, c_im = (t.astype(F32) for t in (b_re, b_im, c_re, c_im))
    y = jnp.zeros((bsz, n, S5_GROUPS, S5_GROUP_CH), F32)
    finals = []
    for d in range(2):
        step = jnp.exp(log_step[d].astype(F32))[:, None]
        lr, li = lam_re[d], lam_im[d]
        mag = jnp.exp(lr * step)
        ar, ai = mag * jnp.cos(li * step), mag * jnp.sin(li * step)
        inv_den = 1.0 / (lr * lr + li * li)
        cr = ((ar - 1.0) * lr + ai * li) * inv_den
        ci = (ai * lr - (ar - 1.0) * li) * inv_den
        bbr = cr[..., None] * b_re - ci[..., None] * b_im
        bbi = cr[..., None] * b_im + ci[..., None] * b_re
        src = ug if d == 0 else jnp.flip(ug, axis=1)
        bur = jnp.einsum("blgh,gph->blgp", src, bbr)
        bui = jnp.einsum("blgh,gph->blgp", src, bbi)
        h0r, h0i = h0[d]
        bur = bur.at[:, 0].add(ar * h0r - ai * h0i)
        bui = bui.at[:, 0].add(ar * h0i + ai * h0r)
        _, _, hr, hi = lax.associative_scan(
            _complex_linear_combine,
            (jnp.broadcast_to(ar, bur.shape), jnp.broadcast_to(ai, bur.shape), bur, bui), axis=1)
        finals.append((hr[:, -1], hi[:, -1]))
        yd = jnp.einsum("blgp,ghp->blgh", hr, c_re) - jnp.einsum("blgp,ghp->blgh", hi, c_im)
        y = y + (yd if d == 0 else jnp.flip(yd, axis=1))
    y = jax.nn.gelu(y.reshape(bsz, n, S5_WIDTH) + d_skip.astype(F32) * u32)
    gv = y @ glu_w.astype(F32) + glu_b.astype(F32)
    out = gv[..., :S5_WIDTH] * jax.nn.sigmoid(gv[..., S5_WIDTH:])
    return out.astype(u.dtype), finals


def moe_ffn(h, router_w, router_b, w_gate, w_up, w_down):
    n_tok = h.shape[0]
    logits = jnp.matmul(h, router_w).astype(F32) + router_b.astype(F32)
    probs = jax.nn.softmax(logits, axis=-1)
    grp = probs.reshape(n_tok, N_EXPERT_GROUPS, EXPERTS_PER_GROUP)
    group_score = jnp.sum(lax.top_k(grp, TOP_K)[0], axis=-1)
    best = jnp.argmax(group_score, axis=-1)
    in_group = (jnp.arange(N_EXPERTS) // EXPERTS_PER_GROUP)[None, :] == best[:, None]
    vals, idx = lax.top_k(jnp.where(in_group, probs, -jnp.inf), TOP_K)
    gates = vals / jnp.sum(vals, axis=-1, keepdims=True)
    dense_gate = jnp.sum(jax.nn.one_hot(idx, N_EXPERTS, dtype=F32) * gates[..., None], axis=1).astype(h.dtype)
    out = jnp.zeros_like(h)
    for e in range(N_EXPERTS):
        act = jax.nn.silu(h @ w_gate[e]) * (h @ w_up[e])
        out = out + dense_gate[:, e:e + 1] * (act @ w_down[e])
    return out


def hybrid_layer(x, ctx, mod_lat, mod_ctx, p, router_w, router_b, with_ctx_out):
    sh1, sc1, g1, sh2, sc2, g2 = jnp.split(mod_lat, N_MOD, axis=-1)
    csh1, csc1, cg1, csh2, csc2, cg2 = jnp.split(mod_ctx, N_MOD, axis=-1)
    bsz = x.shape[0]
    h = rms_norm(x, p["g_mix"]) * (1 + sc1) + sh1
    hc = rms_norm(ctx, p["g_mix"]) * (1 + csc1) + csh1
    cols = h @ p["w_in"]
    ccols = hc @ p["w_in"]
    o_b, o_c = SSD_COLS, SSD_COLS + CONF_COLS

    ssd_args = (p["ssd_conv_w"], p["ssd_conv_b"], p["ssd_dt_bias"], p["ssd_a_log"], p["ssd_d"], p["ssd_norm_g"])
    zero_ssd = jnp.zeros((bsz, SSD_HEADS, SSD_HEAD_DIM, SSD_STATE), F32)
    a_ctx, ssd_states = ssd_mixer(ccols[..., :o_b], *ssd_args, [zero_ssd, zero_ssd])
    a_lat, _ = ssd_mixer(cols[..., :o_b], *ssd_args, ssd_states)

    s5_args = (p["s5_lambda_re"], p["s5_lambda_im"], p["s5_log_step"], p["s5_b_re"], p["s5_b_im"],
               p["s5_c_re"], p["s5_c_im"], p["s5_d"], p["s5_glu_w"], p["s5_glu_b"])
    zero_s5 = (jnp.zeros((bsz, S5_GROUPS, S5_STATE), F32), jnp.zeros((bsz, S5_GROUPS, S5_STATE), F32))
    s_ctx, s5_states = s5_mixer(ccols[..., o_c:], *s5_args, [zero_s5, zero_s5])
    s_lat, _ = s5_mixer(cols[..., o_c:], *s5_args, s5_states)

    conf_args = (p["conf_dw_w"], p["conf_dw_b"], p["conf_ln_g"], p["conf_ln_b"], p["conf_pw_w"], p["conf_pw_b"])
    b_lat = conformer_conv(cols[..., o_b:o_c], *conf_args)

    mix = jnp.concatenate([a_lat, b_lat.astype(x.dtype), s_lat], axis=-1)
    x = x + g1 * (mix @ p["w_out"])
    h2 = rms_norm(x, p["g_ffn"]) * (1 + sc2) + sh2
    moe_w = (p["exp_w_gate"], p["exp_w_up"], p["exp_w_down"])
    if with_ctx_out:
        b_ctx = conformer_conv(ccols[..., o_b:o_c], *conf_args)
        cmix = jnp.concatenate([a_ctx, b_ctx.astype(ctx.dtype), s_ctx], axis=-1)
        ctx = ctx + cg1 * (cmix @ p["w_out"])
        hc2 = rms_norm(ctx, p["g_ffn"]) * (1 + csc2) + csh2
        n_lat = h2.shape[0] * h2.shape[1]
        tokens = jnp.concatenate([h2.reshape(-1, D_MODEL), hc2.reshape(-1, D_MODEL)], axis=0)
        ff = moe_ffn(tokens, router_w, router_b, *moe_w)
        x = x + g2 * ff[:n_lat].reshape(x.shape)
        ctx = ctx + cg2 * ff[n_lat:].reshape(ctx.shape)
    else:
        x = x + g2 * moe_ffn(h2.reshape(-1, D_MODEL), router_w, router_b, *moe_w).reshape(x.shape)
    return x, ctx


def setup_inputs(seed: int = 0) -> dict:
    key = jax.random.key(seed)
    ks = iter(jax.random.split(key, 64))
    L = DEPTH

    def nrm(shape, scale):
        return jax.random.normal(next(ks), shape, F32) * scale

    def gain(shape):
        return 1.0 + nrm(shape, 0.01)

    def unif(shape, lo, hi):
        return jax.random.uniform(next(ks), shape, F32, minval=lo, maxval=hi)

    dt0 = jnp.exp(unif((L, 2, SSD_HEADS), math.log(1e-3), math.log(1e-1)))
    s5_n = jnp.arange(S5_STATE, dtype=F32)
    return {
        "x": nrm((BATCH, SEQ, D_MODEL), 1.0),
        "c": nrm((BATCH, D_MODEL), 1.0),
        "ctx": nrm((BATCH, CTX_LEN, D_MODEL), 1.0),
        "c_ctx": nrm((D_MODEL,), 1.0),
        "w_ada": nrm((L, D_MODEL, N_MOD * D_MODEL), 0.5 * D_MODEL ** -0.5),
        "b_ada": nrm((L, N_MOD * D_MODEL), 0.01),
        "g_mix": gain((L, D_MODEL)),
        "w_in": nrm((L, D_MODEL, IN_COLS), D_MODEL ** -0.5),
        "ssd_conv_w": nrm((L, SSD_CONV, SSD_XBC), SSD_CONV ** -0.5),
        "ssd_conv_b": nrm((L, SSD_XBC), 0.01),
        "ssd_dt_bias": dt0 + jnp.log(-jnp.expm1(-dt0)),
        "ssd_a_log": jnp.log(unif((L, 2, SSD_HEADS), 1.0, 16.0)),
        "ssd_d": gain((L, SSD_HEADS)),
        "ssd_norm_g": gain((L, SSD_WIDTH)),
        "conf_dw_w": nrm((L, CONF_KERNEL, CONF_WIDTH), CONF_KERNEL ** -0.5),
        "conf_dw_b": nrm((L, CONF_WIDTH), 0.01),
        "conf_ln_g": gain((L, CONF_WIDTH)),
        "conf_ln_b": nrm((L, CONF_WIDTH), 0.01),
        "conf_pw_w": nrm((L, CONF_WIDTH, CONF_WIDTH), CONF_WIDTH ** -0.5),
        "conf_pw_b": nrm((L, CONF_WIDTH), 0.01),
        "s5_lambda_re": -0.5 + nrm((L, 2, S5_GROUPS, S5_STATE), 0.01),
        "s5_lambda_im": math.pi * s5_n + nrm((L, 2, S5_GROUPS, S5_STATE), 0.01),
        "s5_log_step": unif((L, 2, S5_GROUPS), math.log(1e-3), math.log(1e-1)),
        "s5_b_re": nrm((L, S5_GROUPS, S5_STATE, S5_GROUP_CH), (2 * S5_GROUP_CH) ** -0.5),
        "s5_b_im": nrm((L, S5_GROUPS, S5_STATE, S5_GROUP_CH), (2 * S5_GROUP_CH) ** -0.5),
        "s5_c_re": nrm((L, S5_GROUPS, S5_GROUP_CH, S5_STATE), (2 * S5_STATE) ** -0.5),
        "s5_c_im": nrm((L, S5_GROUPS, S5_GROUP_CH, S5_STATE), (2 * S5_STATE) ** -0.5),
        "s5_d": nrm((L, S5_WIDTH), 1.0),
        "s5_glu_w": nrm((L, S5_WIDTH, 2 * S5_WIDTH), S5_WIDTH ** -0.5),
        "s5_glu_b": nrm((L, 2 * S5_WIDTH), 0.01),
        "w_out": nrm((L, MIX_WIDTH, D_MODEL), MIX_WIDTH ** -0.5),
        "g_ffn": gain((L, D_MODEL)),
        "router_w": nrm((D_MODEL, N_EXPERTS), D_MODEL ** -0.5),
        "router_b": nrm((N_EXPERTS,), 0.01),
        "exp_w_gate": nrm((L, N_EXPERTS, D_MODEL, EXPERT_FF), D_MODEL ** -0.5),
        "exp_w_up": nrm((L, N_EXPERTS, D_MODEL, EXPERT_FF), D_MODEL ** -0.5),
        "exp_w_down": nrm((L, N_EXPERTS, EXPERT_FF, D_MODEL), EXPERT_FF ** -0.5),
        "g_final": gain((D_MODEL,)),
    }


def reference(x, c, ctx, c_ctx, w_ada, b_ada, g_mix, w_in, ssd_conv_w, ssd_conv_b, ssd_dt_bias,
              ssd_a_log, ssd_d, ssd_norm_g, conf_dw_w, conf_dw_b, conf_ln_g, conf_ln_b, conf_pw_w,
              conf_pw_b, s5_lambda_re, s5_lambda_im, s5_log_step, s5_b_re, s5_b_im, s5_c_re, s5_c_im,
              s5_d, s5_glu_w, s5_glu_b, w_out, g_ffn, router_w, router_b, exp_w_gate, exp_w_up,
              exp_w_down, g_final):
    rows = x.shape[1] // GRID_W
    x = x + grid_pos_embed(rows, x.dtype)[None]
    cond_lat = jax.nn.silu(c)
    cond_ctx = jax.nn.silu(c_ctx)
    for l in range(DEPTH):
        mod_lat = (cond_lat @ w_ada[l] + b_ada[l])[:, None, :]
        mod_ctx = (cond_ctx @ w_ada[l] + b_ada[l])[None, None, :]
        p = dict(
            g_mix=g_mix[l], w_in=w_in[l], ssd_conv_w=ssd_conv_w[l], ssd_conv_b=ssd_conv_b[l],
            ssd_dt_bias=ssd_dt_bias[l], ssd_a_log=ssd_a_log[l], ssd_d=ssd_d[l], ssd_norm_g=ssd_norm_g[l],
            conf_dw_w=conf_dw_w[l], conf_dw_b=conf_dw_b[l], conf_ln_g=conf_ln_g[l], conf_ln_b=conf_ln_b[l],
            conf_pw_w=conf_pw_w[l], conf_pw_b=conf_pw_b[l], s5_lambda_re=s5_lambda_re[l],
            s5_lambda_im=s5_lambda_im[l], s5_log_step=s5_log_step[l], s5_b_re=s5_b_re[l],
            s5_b_im=s5_b_im[l], s5_c_re=s5_c_re[l], s5_c_im=s5_c_im[l], s5_d=s5_d[l],
            s5_glu_w=s5_glu_w[l], s5_glu_b=s5_glu_b[l], w_out=w_out[l], g_ffn=g_ffn[l],
            exp_w_gate=exp_w_gate[l], exp_w_up=exp_w_up[l], exp_w_down=exp_w_down[l])
        x, ctx = hybrid_layer(x, ctx, mod_lat, mod_ctx, p, router_w, router_b, l < DEPTH - 1)
    return rms_norm(x, g_final)
```

```python
import functools
import math

import jax
import jax.numpy as jnp
from jax import lax
from jax.experimental import pallas as pl
from jax.experimental.pallas import tpu as pltpu

F32 = jnp.float32
BF16 = jnp.bfloat16
HIGHEST = lax.Precision.HIGHEST

NB = 8
D = 1024
GRID_W = 64
N_MOD = 6
EPS = 1e-6
SSD_W = 512
SSD_P = 64
SSD_H = 8
SSD_G = 2
SSD_N = 128
SSD_K = 5
SSD_XBC = SSD_W + 2 * SSD_G * SSD_N
SSD_CHUNK = 128
DT_PAD = 128
SSD_COLS_P = SSD_W + SSD_XBC + DT_PAD
CONF_W = 256
CONF_K = 31
S5_W = 256
S5_G = 16
S5_P = 64
S5_CH = 16
S5_S = S5_G * S5_P
S5_TL = 64
NE = 16
NGRP = 4
EPG = 4
MOE_TM = 256
IN_COLS_P = SSD_COLS_P + 2 * CONF_W + S5_W
ROW_TM = 512
CH_ROWS = SSD_CHUNK * NB
LANE = 128
VMEM_LIMIT = 56 * 1024 * 1024


def _cparams(sem):
    return pltpu.CompilerParams(dimension_semantics=sem, vmem_limit_bytes=VMEM_LIMIT)


def _const_spec(shape):
    nd = len(shape)
    return pl.BlockSpec(shape, lambda *_: (0,) * nd)


def _silu(v):
    return v * jax.nn.sigmoid(v)


def _mod_kernel(c_ref, w_ref, b_ref, o_ref):
    c = c_ref[...]
    h = _silu(c).astype(BF16)
    o_ref[...] = jnp.dot(h, w_ref[...].astype(BF16), preferred_element_type=F32) + b_ref[...]


def _modulation(cond, w_ada, b_ada):
    depth = w_ada.shape[0]
    nrow = cond.shape[0]
    return pl.pallas_call(
        _mod_kernel,
        out_shape=jax.ShapeDtypeStruct((depth, nrow, N_MOD * D), F32),
        grid=(depth, N_MOD),
        in_specs=[
            pl.BlockSpec((nrow, D), lambda l, j: (0, 0)),
            pl.BlockSpec((None, D, D), lambda l, j: (l, 0, j)),
            pl.BlockSpec((None, 1, D), lambda l, j: (l, 0, j)),
        ],
        out_specs=pl.BlockSpec((None, nrow, D), lambda l, j: (l, 0, j)),
        compiler_params=_cparams(("arbitrary", "arbitrary")),
        name="adaln_mod",
    )(cond, w_ada, b_ada.reshape(depth, 1, N_MOD * D))


def _in_kernel(add_pos, *refs):
    if add_pos:
        x_ref, pos_ref, mod_ref, g_ref, w_ref, ssd_ref, conf_ref, s5_ref, x0_ref = refs
    else:
        x_ref, mod_ref, g_ref, w_ref, ssd_ref, conf_ref, s5_ref = refs
    tm = x_ref.shape[0]
    x = x_ref[...]
    if add_pos:
        x = (x.reshape(tm // NB, NB, D) + pos_ref[...][:, None, :]).reshape(tm, D)
        x0_ref[...] = x
    ms = jnp.mean(x * x, axis=-1, keepdims=True)
    xn = x * lax.rsqrt(ms + EPS) * g_ref[...]
    h = xn.reshape(tm // NB, NB, D) * (1.0 + mod_ref[1])[None] + mod_ref[0][None]
    h = h.reshape(tm, D).astype(BF16)
    ssd_ref[...] = jnp.dot(h, w_ref[:, :SSD_COLS_P], preferred_element_type=F32)
    conf_ref[...] = jnp.dot(h, w_ref[:, SSD_COLS_P:SSD_COLS_P + 2 * CONF_W], preferred_element_type=F32)
    s5_ref[...] = jnp.dot(h, w_ref[:, SSD_COLS_P + 2 * CONF_W:], preferred_element_type=F32)


def _in_proj(x, posz, modtab, g_mix, w_in_p, n_ctx_rows):
    rows = x.shape[0]
    tm = ROW_TM
    nblk = rows // tm
    ctx_blk = n_ctx_rows // tm
    add_pos = posz is not None
    in_specs = [pl.BlockSpec((tm, D), lambda i: (i, 0))]
    args = [x]
    if add_pos:
        in_specs.append(pl.BlockSpec((tm // NB, D), lambda i: (i, 0)))
        args.append(posz)
    in_specs += [
        pl.BlockSpec((None, N_MOD, NB, D), lambda i: (jnp.where(i < ctx_blk, 0, 1), 0, 0, 0)),
        _const_spec((1, D)),
        _const_spec((D, IN_COLS_P)),
    ]
    args += [modtab, g_mix.reshape(1, D), w_in_p]
    out_shape = [
        jax.ShapeDtypeStruct((rows, SSD_COLS_P), F32),
        jax.ShapeDtypeStruct((rows, 2 * CONF_W), F32),
        jax.ShapeDtypeStruct((rows, S5_W), F32),
    ]
    out_specs = [
        pl.BlockSpec((tm, SSD_COLS_P), lambda i: (i, 0)),
        pl.BlockSpec((tm, 2 * CONF_W), lambda i: (i, 0)),
        pl.BlockSpec((tm, S5_W), lambda i: (i, 0)),
    ]
    if add_pos:
        out_shape.append(jax.ShapeDtypeStruct((rows, D), F32))
        out_specs.append(pl.BlockSpec((tm, D), lambda i: (i, 0)))
    return pl.pallas_call(
        functools.partial(_in_kernel, add_pos),
        out_shape=out_shape,
        grid=(nblk,),
        in_specs=in_specs,
        out_specs=out_specs,
        compiler_params=_cparams(("parallel",)),
        name="in_proj",
    )(*args)


def _chunk_of(direction, i, nc_ctx, nc):
    if direction == 0:
        return i
    return jnp.where(i < nc_ctx, nc_ctx - 1 - i, nc - 1 - (i - nc_ctx))


def _has_prev(c, nc_ctx):
    return jnp.logical_and(c != 0, c != nc_ctx)


def _has_next(c, nc_ctx, nc):
    return jnp.logical_and(c != nc_ctx - 1, c != nc - 1)


SSD_HALO = 16


def _ssd_kernel(direction, nc_ctx, nc, *refs):
    if direction == 0:
        (main_ref, prev_ref, next_ref, cw_ref, cb_ref, dtb_ref, alog_ref, dsk_ref,
         o_ref, ext, act, dts, yout, state) = refs
    else:
        (main_ref, prev_ref, next_ref, cw_ref, cb_ref, dtb_ref, alog_ref, ng_ref, yp_ref,
         o_ref, ext, act, dts, yout, state) = refs
    i = pl.program_id(0)
    c = _chunk_of(direction, i, nc_ctx, nc)

    @pl.when(i == 0)
    def _():
        state[...] = jnp.zeros_like(state)

    has_prev = _has_prev(c, nc_ctx)
    has_next = _has_next(c, nc_ctx, nc)
    nblk = SSD_XBC // LANE
    for k in range(nblk):
        lo = SSD_W + k * LANE
        ext[k, SSD_HALO:SSD_HALO + CH_ROWS, :] = main_ref[:, lo:lo + LANE]
        ext[k, 0:SSD_HALO, :] = jnp.where(has_prev, prev_ref[:, lo:lo + LANE], 0.0)
        ext[k, SSD_HALO + CH_ROWS:, :] = jnp.where(has_next, next_ref[:, lo:lo + LANE], 0.0)

    def conv_rb(rb, carry):
        r0 = pl.multiple_of(rb * LANE, LANE)
        for k in range(nblk):
            lo = k * LANE
            acc = jnp.broadcast_to(cb_ref[:, lo:lo + LANE], (LANE, LANE))
            for tap in range(SSD_K):
                acc = acc + ext[k, pl.ds(r0 + NB * tap, LANE), :] * cw_ref[tap:tap + 1, lo:lo + LANE]
            act[k, pl.ds(r0, LANE), :] = _silu(acc)
        return carry

    lax.fori_loop(0, CH_ROWS // LANE, conv_rb, 0)

    raw = main_ref[:, SSD_W + SSD_XBC:] + dtb_ref[...]
    dts[...] = jnp.maximum(raw, 0.0) + jnp.log1p(jnp.exp(-jnp.abs(raw)))
    a_row = -jnp.exp(alog_ref[...])

    tt = lax.broadcasted_iota(jnp.int32, (SSD_CHUNK, SSD_CHUNK), 0)
    ss = lax.broadcasted_iota(jnp.int32, (SSD_CHUNK, SSD_CHUNK), 1)
    mask = (ss <= tt) if direction == 0 else (ss >= tt)
    tmat = mask.astype(F32)
    last = SSD_CHUNK - 1 if direction == 0 else 0
    hpg = SSD_H // SSD_G

    def per_batch(b, carry):
        sl = pl.ds(b, SSD_CHUNK, stride=NB)
        dt_b = dts[sl, :]
        cs = jnp.dot(tmat, dt_b * a_row, precision=HIGHEST, preferred_element_type=F32)
        cs_t = cs.T
        for g in range(SSD_G):
            bg_t = act[SSD_W // LANE + g, sl, :].T.astype(BF16)
            cg = act[SSD_W // LANE + SSD_G + g, sl, :].astype(BF16)
            gmat = jnp.dot(cg, bg_t, preferred_element_type=F32)
            s_in = state[b, g]
            yoff = jnp.dot(cg, s_in.astype(BF16), preferred_element_type=F32)
            xw, dec, ys = [], [], []
            for hh in range(hpg):
                h = g * hpg + hh
                ln = direction * SSD_H + h
                col = cs[:, ln:ln + 1]
                row = cs_t[ln:ln + 1, :]
                decay = jnp.where(mask, jnp.exp(jnp.where(mask, col - row, 0.0)), 0.0)
                xs_h = act[h // 2, sl, :][:, (h % 2) * SSD_P:(h % 2 + 1) * SSD_P]
                xdt = xs_h * dt_b[:, ln:ln + 1]
                y_h = jnp.dot((gmat * decay).astype(BF16), xdt.astype(BF16), preferred_element_type=F32)
                y_h = y_h + yoff[:, hh * SSD_P:(hh + 1) * SSD_P] * jnp.exp(col)
                if direction == 0:
                    y_h = y_h + dsk_ref[:, h * SSD_P:(h + 1) * SSD_P] * xs_h
                tot = cs[last:last + 1, ln:ln + 1]
                xw.append((xdt * jnp.exp(tot - col)).astype(BF16))
                dec.append(jnp.broadcast_to(jnp.exp(tot), (1, SSD_P)))
                ys.append(y_h)
            s_new = jnp.dot(bg_t, jnp.concatenate(xw, axis=1), preferred_element_type=F32)
            state[b, g] = s_in * jnp.concatenate(dec, axis=1) + s_new
            yout[2 * g, sl, :] = jnp.concatenate(ys[0:2], axis=1)
            yout[2 * g + 1, sl, :] = jnp.concatenate(ys[2:4], axis=1)
        return carry

    lax.fori_loop(0, NB, per_batch, 0)

    nyb = SSD_W // LANE
    if direction == 0:
        for k in range(nyb):
            o_ref[:, k * LANE:(k + 1) * LANE] = yout[k]
    else:
        def fin(rb, carry):
            r0 = pl.multiple_of(rb * LANE, LANE)
            y = jnp.concatenate([yout[k, pl.ds(r0, LANE), :] for k in range(nyb)], axis=1)
            y = y + yp_ref[pl.ds(r0, LANE), :]
            y = y * _silu(main_ref[pl.ds(r0, LANE), 0:SSD_W])
            ms = jnp.mean(y * y, axis=-1, keepdims=True)
            o_ref[pl.ds(r0, LANE), :] = y * lax.rsqrt(ms + EPS) * ng_ref[...]
            return carry

        lax.fori_loop(0, CH_ROWS // LANE, fin, 0)


def _ssd_pass(direction, cols, n_ctx_rows, cw, cb, dtb, alog, extra, ypart):
    rows = cols.shape[0]
    nc = rows // CH_ROWS
    nc_ctx = n_ctx_rows // CH_ROWS
    hpc = CH_ROWS // SSD_HALO
    n_halo = rows // SSD_HALO
    chunk = functools.partial(_chunk_of, direction, nc_ctx=nc_ctx, nc=nc)
    in_specs = [
        pl.BlockSpec((CH_ROWS, SSD_COLS_P), lambda i: (chunk(i), 0)),
        pl.BlockSpec((SSD_HALO, SSD_COLS_P), lambda i: (jnp.maximum(chunk(i) * hpc - 1, 0), 0)),
        pl.BlockSpec((SSD_HALO, SSD_COLS_P), lambda i: (jnp.minimum((chunk(i) + 1) * hpc, n_halo - 1), 0)),
        _const_spec((SSD_K, SSD_XBC)),
        _const_spec((1, SSD_XBC)),
        _const_spec((1, DT_PAD)),
        _const_spec((1, DT_PAD)),
        _const_spec((1, SSD_W)),
    ]
    args = [cols, cols, cols, cw, cb, dtb, alog, extra]
    if direction == 1:
        in_specs.append(pl.BlockSpec((CH_ROWS, SSD_W), lambda i: (chunk(i), 0)))
        args.append(ypart)
    return pl.pallas_call(
        functools.partial(_ssd_kernel, direction, nc_ctx, nc),
        out_shape=jax.ShapeDtypeStruct((rows, SSD_W), F32),
        grid=(nc,),
        in_specs=in_specs,
        out_specs=pl.BlockSpec((CH_ROWS, SSD_W), lambda i: (chunk(i), 0)),
        scratch_shapes=[
            pltpu.VMEM((SSD_XBC // LANE, CH_ROWS + 2 * SSD_HALO, LANE), F32),
            pltpu.VMEM((SSD_XBC // LANE, CH_ROWS, LANE), F32),
            pltpu.VMEM((CH_ROWS, DT_PAD), F32),
            pltpu.VMEM((SSD_W // LANE, CH_ROWS, LANE), F32),
            pltpu.VMEM((NB, SSD_G, SSD_N, (SSD_H // SSD_G) * SSD_P), F32),
        ],
        compiler_params=_cparams(("arbitrary",)),
        name="ssd_fwd" if direction == 0 else "ssd_bwd",
    )(*args)


CONF_HALO = 128


def _conf_kernel(first_chunk, nc_ctx, nc, main_ref, prev_ref, next_ref, w_ref, b_ref, lng_ref, lnb_ref,
                 pw_ref, pwb_ref, o_ref, ext):
    c = pl.program_id(0) + first_chunk
    has_prev = _has_prev(c, nc_ctx)
    has_next = _has_next(c, nc_ctx, nc)

    def glu(ref):
        v = ref[...]
        return v[:, :CONF_W] * jax.nn.sigmoid(v[:, CONF_W:])

    um = glu(main_ref)
    up = jnp.where(has_prev, glu(prev_ref), 0.0)
    un = jnp.where(has_next, glu(next_ref), 0.0)
    nblk = CONF_W // LANE
    for k in range(nblk):
        ext[k, 0:CONF_HALO, :] = up[:, k * LANE:(k + 1) * LANE]
        ext[k, CONF_HALO:CONF_HALO + CH_ROWS, :] = um[:, k * LANE:(k + 1) * LANE]
        ext[k, CONF_HALO + CH_ROWS:, :] = un[:, k * LANE:(k + 1) * LANE]
    off = CONF_HALO - NB * (CONF_K // 2)

    def rbody(rb, carry):
        r0 = pl.multiple_of(rb * LANE, LANE)
        accs = []
        for k in range(nblk):
            lo = k * LANE
            acc = jnp.broadcast_to(b_ref[:, lo:lo + LANE], (LANE, LANE))
            for tap in range(CONF_K):
                acc = acc + ext[k, pl.ds(r0 + NB * tap + off, LANE), :] * w_ref[tap:tap + 1, lo:lo + LANE]
            accs.append(acc)
        u = jnp.concatenate(accs, axis=1)
        xc = u - jnp.mean(u, axis=-1, keepdims=True)
        y = xc * lax.rsqrt(jnp.mean(xc * xc, axis=-1, keepdims=True) + EPS) * lng_ref[...] + lnb_ref[...]
        y = _silu(y).astype(BF16)
        o_ref[pl.ds(r0, LANE), :] = jnp.dot(y, pw_ref[...], preferred_element_type=F32) + pwb_ref[...]
        return carry

    lax.fori_loop(0, CH_ROWS // LANE, rbody, 0)


def _conformer(cols, n_ctx_rows, with_ctx, w, b, lng, lnb, pw, pwb):
    rows = cols.shape[0]
    nc = rows // CH_ROWS
    nc_ctx = n_ctx_rows // CH_ROWS
    first = 0 if with_ctx else nc_ctx
    hpc = CH_ROWS // CONF_HALO
    n_halo = rows // CONF_HALO
    return pl.pallas_call(
        functools.partial(_conf_kernel, first, nc_ctx, nc),
        out_shape=jax.ShapeDtypeStruct(((nc - first) * CH_ROWS, CONF_W), F32),
        grid=(nc - first,),
        in_specs=[
            pl.BlockSpec((CH_ROWS, 2 * CONF_W), lambda i: (i + first, 0)),
            pl.BlockSpec((CONF_HALO, 2 * CONF_W), lambda i: (jnp.maximum((i + first) * hpc - 1, 0), 0)),
            pl.BlockSpec((CONF_HALO, 2 * CONF_W), lambda i: (jnp.minimum((i + first + 1) * hpc, n_halo - 1), 0)),
            _const_spec((CONF_K, CONF_W)),
            _const_spec((1, CONF_W)),
            _const_spec((1, CONF_W)),
            _const_spec((1, CONF_W)),
            _const_spec((CONF_W, CONF_W)),
            _const_spec((1, CONF_W)),
        ],
        out_specs=pl.BlockSpec((CH_ROWS, CONF_W), lambda i: (i, 0)),
        scratch_shapes=[pltpu.VMEM((CONF_W // LANE, CH_ROWS + 2 * CONF_HALO, LANE), F32)],
        compiler_params=_cparams(("parallel",)),
        name="conformer",
    )(cols, cols, cols, w, b, lng, lnb, pw, pwb)


def _s5_kernel(direction, *refs):
    if direction == 0:
        u_ref, bmat_ref, ar_ref, ai_ref, cmat_ref, o_ref, hbuf, st = refs
    else:
        (u_ref, bmat_ref, ar_ref, ai_ref, cmat_ref, yf_ref, dsk_ref, gw_ref, gb_ref,
         o_ref, hbuf, st) = refs
    i = pl.program_id(0)

    @pl.when(i == 0)
    def _():
        st[...] = jnp.zeros_like(st)

    u = u_ref[...]
    hbuf[...] = jnp.dot(u.astype(BF16), bmat_ref[...], preferred_element_type=F32)
    ar = jnp.broadcast_to(ar_ref[...], (NB, S5_S))
    ai = jnp.broadcast_to(ai_ref[...], (NB, S5_S))

    def step(j, carry):
        hr, hi = carry
        t = j if direction == 0 else S5_TL - 1 - j
        r0 = pl.multiple_of(t * NB, NB)
        nr = ar * hr - ai * hi + hbuf[pl.ds(r0, NB), 0:S5_S]
        ni = ar * hi + ai * hr + hbuf[pl.ds(r0, NB), S5_S:]
        hbuf[pl.ds(r0, NB), 0:S5_S] = nr
        hbuf[pl.ds(r0, NB), S5_S:] = ni
        return nr, ni

    hr, hi = lax.fori_loop(0, S5_TL, step, (st[0], st[1]), unroll=2)
    st[0] = hr
    st[1] = hi
    y = jnp.dot(hbuf[...].astype(BF16), cmat_ref[...], preferred_element_type=F32)
    if direction == 0:
        o_ref[...] = y
    else:
        v = jax.nn.gelu(y + yf_ref[...] + dsk_ref[...] * u, approximate=True)
        gv = jnp.dot(v.astype(BF16), gw_ref[...], preferred_element_type=F32) + gb_ref[...]
        o_ref[...] = gv[:, :S5_W] * jax.nn.sigmoid(gv[:, S5_W:])


def _s5_pass(direction, u, n_ctx_rows, bmat, ar, ai, cmat, yf, dsk, gw, gb):
    rows = u.shape[0]
    tr = S5_TL * NB
    nblk = rows // tr
    nb_ctx = n_ctx_rows // tr
    blk = functools.partial(_chunk_of, direction, nc_ctx=nb_ctx, nc=nblk)
    in_specs = [
        pl.BlockSpec((tr, S5_W), lambda i: (blk(i), 0)),
        _const_spec((S5_W, 2 * S5_S)),
        _const_spec((1, S5_S)),
        _const_spec((1, S5_S)),
        _const_spec((2 * S5_S, S5_W)),
    ]
    args = [u, bmat, ar, ai, cmat]
    if direction == 1:
        in_specs += [
            pl.BlockSpec((tr, S5_W), lambda i: (blk(i), 0)),
            _const_spec((1, S5_W)),
            _const_spec((S5_W, 2 * S5_W)),
            _const_spec((1, 2 * S5_W)),
        ]
        args += [yf, dsk, gw, gb]
    return pl.pallas_call(
        functools.partial(_s5_kernel, direction),
        out_shape=jax.ShapeDtypeStruct((rows, S5_W), F32),
        grid=(nblk,),
        in_specs=in_specs,
        out_specs=pl.BlockSpec((tr, S5_W), lambda i: (blk(i), 0)),
        scratch_shapes=[pltpu.VMEM((tr, 2 * S5_S), F32), pltpu.VMEM((2, NB, S5_S), F32)],
        compiler_params=_cparams(("arbitrary",)),
        name="s5_fwd" if direction == 0 else "s5_bwd",
    )(*args)


def _s5_operators(lam_re, lam_im, log_step, b_re, b_im, c_re, c_im):
    eye = jnp.eye(S5_G, dtype=F32)
    bmats, ars, ais = [], [], []
    for d in range(2):
        step = jnp.exp(log_step[d])[:, None]
        lr, li = lam_re[d], lam_im[d]
        mag = jnp.exp(lr * step)
        ar, ai = mag * jnp.cos(li * step), mag * jnp.sin(li * step)
        inv_den = 1.0 / (lr * lr + li * li)
        cr = ((ar - 1.0) * lr + ai * li) * inv_den
        ci = (ai * lr - (ar - 1.0) * li) * inv_den
        bbr = cr[..., None] * b_re - ci[..., None] * b_im
        bbi = cr[..., None] * b_im + ci[..., None] * b_re

        def blockdiag(m):
            return jnp.einsum("gph,gk->ghkp", m, eye).reshape(S5_W, S5_S)

        bmats.append(jnp.concatenate([blockdiag(bbr), blockdiag(bbi)], axis=1).astype(BF16))
        ars.append(ar.reshape(1, S5_S))
        ais.append(ai.reshape(1, S5_S))
    cre = jnp.einsum("ghp,gk->gpkh", c_re, eye).reshape(S5_S, S5_W)
    cim = jnp.einsum("ghp,gk->gpkh", c_im, eye).reshape(S5_S, S5_W)
    cmat = jnp.concatenate([cre, -cim], axis=0).astype(BF16)
    return bmats, ars, ais, cmat


def _route_rows(lg):
    m = jnp.max(lg, axis=0, keepdims=True)
    ex = jnp.exp(lg - m)
    probs = ex / jnp.sum(ex, axis=0, keepdims=True)
    p = [probs[e:e + 1, :] for e in range(NE)]
    tops = []
    for q in range(NGRP):
        v = p[q * EPG:(q + 1) * EPG]
        m1 = functools.reduce(jnp.maximum, v)
        i1 = jnp.full_like(m1, float(EPG - 1))
        for j in range(EPG - 2, -1, -1):
            i1 = jnp.where(v[j] == m1, float(j), i1)
        rest = [jnp.where(i1 == float(j), -jnp.inf, v[j]) for j in range(EPG)]
        m2 = functools.reduce(jnp.maximum, rest)
        i2 = jnp.full_like(m2, float(EPG - 1))
        for j in range(EPG - 2, -1, -1):
            i2 = jnp.where(jnp.logical_and(rest[j] == m2, i1 != float(j)), float(j), i2)
        tops.append((m1, i1, m2, i2))
    score = [t[0] + t[2] for t in tops]
    best = functools.reduce(jnp.maximum, score)
    sel = tops[NGRP - 1] + (jnp.full_like(best, float(NGRP - 1)),)
    for q in range(NGRP - 2, -1, -1):
        hit = score[q] == best
        sel = tuple(jnp.where(hit, a, b) for a, b in zip(tops[q] + (jnp.full_like(best, float(q)),), sel))
    m1, i1, m2, i2, grp = sel
    den = m1 + m2
    return grp * EPG + i1, grp * EPG + i2, m1 / den, m2 / den


def _out_kernel(a_ref, b_ref, s_ref, x_ref, mod_ref, g_ref, w_ref, rw_ref, rb_ref, x1_ref, h2_ref, rt_ref):
    tm = x_ref.shape[0]
    mix = jnp.dot(a_ref[...].astype(BF16), w_ref[0:SSD_W, :], preferred_element_type=F32)
    mix = mix + jnp.dot(b_ref[...].astype(BF16), w_ref[SSD_W:SSD_W + CONF_W, :], preferred_element_type=F32)
    mix = mix + jnp.dot(s_ref[...].astype(BF16), w_ref[SSD_W + CONF_W:, :], preferred_element_type=F32)
    x1 = (x_ref[...].reshape(tm // NB, NB, D) + mod_ref[2][None] * mix.reshape(tm // NB, NB, D)).reshape(tm, D)
    x1_ref[...] = x1
    ms = jnp.mean(x1 * x1, axis=-1, keepdims=True)
    xn = x1 * lax.rsqrt(ms + EPS) * g_ref[...]
    h2 = (xn.reshape(tm // NB, NB, D) * (1.0 + mod_ref[4])[None] + mod_ref[3][None]).reshape(tm, D)
    h2_ref[...] = h2.astype(BF16)
    lg = lax.dot_general(rw_ref[...], h2, (((1,), (1,)), ((), ())), precision=HIGHEST,
                         preferred_element_type=F32) + rb_ref[...]
    e1, e2, g1, g2 = _route_rows(lg)
    rt_ref[...] = jnp.concatenate([e1, e2, g1, g2, jnp.zeros((NB - 4, tm), F32)], axis=0)


def _out_proj(a, bconf, s, x, modtab, g_ffn, w_out, rw_t, rb, n_ctx_rows, with_ctx):
    tm = ROW_TM
    ctx_blk = n_ctx_rows // tm
    tot_blk = a.shape[0] // tm
    first = 0 if with_ctx else ctx_blk
    nblk = tot_blk - first
    rows = nblk * tm
    x_off = x.shape[0] // tm - nblk
    b_off = bconf.shape[0] // tm - nblk
    return pl.pallas_call(
        _out_kernel,
        out_shape=[
            jax.ShapeDtypeStruct((rows, D), F32),
            jax.ShapeDtypeStruct((rows, D), BF16),
            jax.ShapeDtypeStruct((NB, rows), F32),
        ],
        grid=(nblk,),
        in_specs=[
            pl.BlockSpec((tm, SSD_W), lambda i: (i + first, 0)),
            pl.BlockSpec((tm, CONF_W), lambda i: (i + b_off, 0)),
            pl.BlockSpec((tm, S5_W), lambda i: (i + first, 0)),
            pl.BlockSpec((tm, D), lambda i: (i + x_off, 0)),
            pl.BlockSpec((None, N_MOD, NB, D), lambda i: (jnp.where(i + first < ctx_blk, 0, 1), 0, 0, 0)),
            _const_spec((1, D)),
            _const_spec((D, D)),
            _const_spec((NE, D)),
            _const_spec((NE, 1)),
        ],
        out_specs=[
            pl.BlockSpec((tm, D), lambda i: (i, 0)),
            pl.BlockSpec((tm, D), lambda i: (i, 0)),
            pl.BlockSpec((NB, tm), lambda i: (0, i)),
        ],
        compiler_params=_cparams(("parallel",)),
        name="out_proj_route",
    )(a, bconf, s, x, modtab, g_ffn.reshape(1, D), w_out, rw_t, rb)


def _moe_kernel(te_ref, tf_ref, nt_ref, h_ref, gate_ref, wg_ref, wu_ref, wd_ref, o_ref, wgb, wub, wdb):
    j = pl.program_id(0)

    @pl.when(jnp.logical_and(tf_ref[j] == 1, j < nt_ref[0]))
    def _():
        wgb[...] = wg_ref[...].astype(BF16)
        wub[...] = wu_ref[...].astype(BF16)
        wdb[...] = wd_ref[...].astype(BF16)

    @pl.when(j < nt_ref[0])
    def _():
        h = h_ref[...]
        act = _silu(jnp.dot(h, wgb[...], preferred_element_type=F32))
        act = act * jnp.dot(h, wub[...], preferred_element_type=F32)
        y = jnp.dot(act.astype(BF16), wdb[...], preferred_element_type=F32)
        o_ref[...] = y * gate_ref[...]

    @pl.when(j >= nt_ref[0])
    def _():
        o_ref[...] = jnp.zeros_like(o_ref)


def _moe_experts(hs, gate_sorted, tile_expert, tile_first, n_tiles_used, w_gate, w_up, w_down):
    p_pad = hs.shape[0]
    nt = p_pad // MOE_TM
    wspec = pl.BlockSpec((None, D, D), lambda j, te, tf, n: (te[j], 0, 0))
    return pl.pallas_call(
        _moe_kernel,
        out_shape=jax.ShapeDtypeStruct((p_pad, D), F32),
        grid_spec=pltpu.PrefetchScalarGridSpec(
            num_scalar_prefetch=3,
            grid=(nt,),
            in_specs=[
                pl.BlockSpec((MOE_TM, D), lambda j, te, tf, n: (j, 0)),
                pl.BlockSpec((MOE_TM, 1), lambda j, te, tf, n: (j, 0)),
                wspec, wspec, wspec,
            ],
            out_specs=pl.BlockSpec((MOE_TM, D), lambda j, te, tf, n: (j, 0)),
            scratch_shapes=[pltpu.VMEM((D, D), BF16)] * 3,
        ),
        compiler_params=_cparams(("arbitrary",)),
        name="moe_experts",
    )(tile_expert, tile_first, n_tiles_used, hs, gate_sorted, w_gate, w_up, w_down)


def _moe_plan(route):
    rows = route.shape[1]
    e = route[0:2].astype(jnp.int32).reshape(-1)
    gates = route[2:4].reshape(-1)
    onehot = (e[:, None] == jnp.arange(NE, dtype=jnp.int32)[None, :]).astype(jnp.int32)
    csum = jnp.cumsum(onehot, axis=0)
    rank = jnp.take_along_axis(csum, e[:, None], axis=1)[:, 0] - 1
    counts = csum[-1]
    padded = ((counts + MOE_TM - 1) // MOE_TM) * MOE_TM
    pend = jnp.cumsum(padded)
    pstart = pend - padded
    dest = pstart[e] + rank
    p_pad = ((2 * rows + NE * (MOE_TM - 1)) // MOE_TM + 1) * MOE_TM
    nt = p_pad // MOE_TM
    tok = jnp.tile(jnp.arange(rows, dtype=jnp.int32), 2)
    row_ids = jnp.zeros((p_pad,), jnp.int32).at[dest].set(tok)
    gate_sorted = jnp.zeros((p_pad,), F32).at[dest].set(gates)
    tile_start = jnp.arange(nt, dtype=jnp.int32) * MOE_TM
    tile_expert = jnp.minimum(jnp.searchsorted(pend, tile_start, side="right"), NE - 1).astype(jnp.int32)
    tile_first = jnp.concatenate([jnp.ones((1,), jnp.int32),
                                  (tile_expert[1:] != tile_expert[:-1]).astype(jnp.int32)])
    n_used = (pend[-1] // MOE_TM).astype(jnp.int32).reshape(1)
    return row_ids, gate_sorted.reshape(p_pad, 1), tile_expert, tile_first, n_used, dest


def _comb_kernel(final, x_ref, ya_ref, yb_ref, mod_ref, g_ref, o_ref):
    tm = x_ref.shape[0]
    ff = (ya_ref[...] + yb_ref[...]).reshape(tm // NB, NB, D)
    x2 = (x_ref[...].reshape(tm // NB, NB, D) + mod_ref[5][None] * ff).reshape(tm, D)
    if final:
        ms = jnp.mean(x2 * x2, axis=-1, keepdims=True)
        x2 = x2 * lax.rsqrt(ms + EPS) * g_ref[...]
    o_ref[...] = x2


def _combine(x1, ya, yb, modtab, g_final, n_ctx_rows, with_ctx, final):
    tm = ROW_TM
    rows = x1.shape[0]
    ctx_blk = n_ctx_rows // tm if with_ctx else 0
    return pl.pallas_call(
        functools.partial(_comb_kernel, final),
        out_shape=jax.ShapeDtypeStruct((rows, D), F32),
        grid=(rows // tm,),
        in_specs=[
            pl.BlockSpec((tm, D), lambda i: (i, 0)),
            pl.BlockSpec((tm, D), lambda i: (i, 0)),
            pl.BlockSpec((tm, D), lambda i: (i, 0)),
            pl.BlockSpec((None, N_MOD, NB, D), lambda i: (jnp.where(i < ctx_blk, 0, 1), 0, 0, 0)),
            _const_spec((1, D)),
        ],
        out_specs=pl.BlockSpec((tm, D), lambda i: (i, 0)),
        compiler_params=_cparams(("parallel",)),
        name="moe_combine",
    )(x1, ya, yb, modtab, g_final.reshape(1, D))


def _grid_pos_embed(rows_n):
    rr, cc = jnp.meshgrid(jnp.arange(rows_n, dtype=F32), jnp.arange(GRID_W, dtype=F32), indexing="ij")
    quarter = D // 4
    inv_freq = jnp.exp(-math.log(10000.0) * jnp.arange(quarter, dtype=F32) / quarter)

    def emb(pos):
        ang = pos.reshape(-1)[:, None] * inv_freq[None, :]
        return jnp.concatenate([jnp.sin(ang), jnp.cos(ang)], axis=-1)

    return jnp.concatenate([emb(rr), emb(cc)], axis=-1)


def _pad_lanes(v, width):
    return jnp.pad(v, ((0, 0), (0, width - v.shape[1])))


def _layer(l, depth, xall, posz, modtab, n_ctx_rows, p, router_w_t, router_b, g_final):
    last = l == depth - 1
    w_in = p["w_in"]
    o_dt = SSD_W + SSD_XBC
    o_b = o_dt + 2 * SSD_H
    w_in_p = jnp.concatenate(
        [w_in[:, :o_b], jnp.zeros((D, DT_PAD - 2 * SSD_H), F32), w_in[:, o_b:]], axis=1).astype(BF16)
    outs = _in_proj(xall, posz, modtab, p["g_mix"], w_in_p, n_ctx_rows)
    if posz is not None:
        ssd_cols, conf_cols, s5_u, xall = outs
    else:
        ssd_cols, conf_cols, s5_u = outs

    cw = p["ssd_conv_w"]
    cb = p["ssd_conv_b"].reshape(1, SSD_XBC)
    dtb = _pad_lanes(p["ssd_dt_bias"].reshape(1, 2 * SSD_H), DT_PAD)
    alog = _pad_lanes(p["ssd_a_log"].reshape(1, 2 * SSD_H), DT_PAD)
    dsk = jnp.repeat(p["ssd_d"], SSD_P).reshape(1, SSD_W)
    ypart = _ssd_pass(0, ssd_cols, n_ctx_rows, cw, cb, dtb, alog, dsk, None)
    a_mix = _ssd_pass(1, ssd_cols, n_ctx_rows, cw, cb, dtb, alog, p["ssd_norm_g"].reshape(1, SSD_W), ypart)

    b_mix = _conformer(conf_cols, n_ctx_rows, not last, p["conf_dw_w"], p["conf_dw_b"].reshape(1, CONF_W),
                       p["conf_ln_g"].reshape(1, CONF_W), p["conf_ln_b"].reshape(1, CONF_W),
                       p["conf_pw_w"].astype(BF16), p["conf_pw_b"].reshape(1, CONF_W))

    bmats, ars, ais, cmat = _s5_operators(p["s5_lambda_re"], p["s5_lambda_im"], p["s5_log_step"],
                                          p["s5_b_re"], p["s5_b_im"], p["s5_c_re"], p["s5_c_im"])
    yf = _s5_pass(0, s5_u, n_ctx_rows, bmats[0], ars[0], ais[0], cmat, None, None, None, None)
    s_mix = _s5_pass(1, s5_u, n_ctx_rows, bmats[1], ars[1], ais[1], cmat, yf,
                     p["s5_d"].reshape(1, S5_W), p["s5_glu_w"].astype(BF16), p["s5_glu_b"].reshape(1, 2 * S5_W))

    x1, h2, route = _out_proj(a_mix, b_mix, s_mix, xall, modtab, p["g_ffn"], p["w_out"].astype(BF16),
                              router_w_t, router_b, n_ctx_rows, not last)
    row_ids, gate_sorted, tile_expert, tile_first, n_used, dest = _moe_plan(route)
    hs = jnp.take(h2, row_ids, axis=0)
    y = _moe_experts(hs, gate_sorted, tile_expert, tile_first, n_used,
                     p["exp_w_gate"], p["exp_w_up"], p["exp_w_down"])
    rows = x1.shape[0]
    ya = jnp.take(y, dest[:rows], axis=0)
    yb = jnp.take(y, dest[rows:], axis=0)
    return _combine(x1, ya, yb, modtab, g_final, n_ctx_rows, not last, last)


def kernel(x, c, ctx, c_ctx, w_ada, b_ada, g_mix, w_in, ssd_conv_w, ssd_conv_b, ssd_dt_bias, ssd_a_log, ssd_d, ssd_norm_g, conf_dw_w, conf_dw_b, conf_ln_g, conf_ln_b, conf_pw_w, conf_pw_b, s5_lambda_re, s5_lambda_im, s5_log_step, s5_b_re, s5_b_im, s5_c_re, s5_c_im, s5_d, s5_glu_w, s5_glu_b, w_out, g_ffn, router_w, router_b, exp_w_gate, exp_w_up, exp_w_down, g_final):
    bsz, seq, _ = x.shape
    ctx_len = ctx.shape[1]
    depth = w_ada.shape[0]
    assert bsz == NB and seq % SSD_CHUNK == 0 and ctx_len % SSD_CHUNK == 0
    n_ctx_rows = ctx_len * NB

    x_tm = jnp.transpose(x, (1, 0, 2)).reshape(seq * NB, D)
    ctx_tm = jnp.transpose(ctx, (1, 0, 2)).reshape(n_ctx_rows, D)
    xall = jnp.concatenate([ctx_tm, x_tm], axis=0)
    posz = jnp.concatenate([jnp.zeros((ctx_len, D), F32), _grid_pos_embed(seq // GRID_W)], axis=0)

    cond = jnp.concatenate([c, c_ctx[None, :], jnp.zeros((2 * NB - bsz - 1, D), F32)], axis=0)
    mod = _modulation(cond, w_ada, b_ada)
    mod = mod.reshape(depth, 2 * NB, N_MOD, D)
    mod_lat = jnp.transpose(mod[:, :NB], (0, 2, 1, 3))
    mod_ctx = jnp.broadcast_to(mod[:, NB][:, :, None, :], (depth, N_MOD, NB, D))
    modtabs = jnp.stack([mod_ctx, mod_lat], axis=1)

    router_w_t = router_w.T
    router_b2 = router_b.reshape(NE, 1)
    for l in range(depth):
        p = dict(
            g_mix=g_mix[l], w_in=w_in[l], ssd_conv_w=ssd_conv_w[l], ssd_conv_b=ssd_conv_b[l],
            ssd_dt_bias=ssd_dt_bias[l], ssd_a_log=ssd_a_log[l], ssd_d=ssd_d[l], ssd_norm_g=ssd_norm_g[l],
            conf_dw_w=conf_dw_w[l], conf_dw_b=conf_dw_b[l], conf_ln_g=conf_ln_g[l], conf_ln_b=conf_ln_b[l],
            conf_pw_w=conf_pw_w[l], conf_pw_b=conf_pw_b[l], s5_lambda_re=s5_lambda_re[l],
            s5_lambda_im=s5_lambda_im[l], s5_log_step=s5_log_step[l], s5_b_re=s5_b_re[l],
            s5_b_im=s5_b_im[l], s5_c_re=s5_c_re[l], s5_c_im=s5_c_im[l], s5_d=s5_d[l],
            s5_glu_w=s5_glu_w[l], s5_glu_b=s5_glu_b[l], w_out=w_out[l], g_ffn=g_ffn[l],
            exp_w_gate=exp_w_gate[l], exp_w_up=exp_w_up[l], exp_w_down=exp_w_down[l])
        xall = _layer(l, depth, xall, posz if l == 0 else None, modtabs[l], n_ctx_rows, p,
                      router_w_t, router_b2, g_final)
    return jnp.transpose(xall.reshape(seq, NB, D), (1, 0, 2))
```

```python
import functools
import math

import jax
import jax.numpy as jnp
from jax import lax
from jax.experimental import pallas as pl
from jax.experimental.pallas import tpu as pltpu

F32 = jnp.float32
BF16 = jnp.bfloat16
I32 = jnp.int32
HIGHEST = lax.Precision.HIGHEST

NB = 8
D = 1024
GRID_W = 64
N_MOD = 6
EPS = 1e-6
LANE = 128
SSD_W = 512
SSD_P = 64
SSD_H = 8
SSD_G = 2
SSD_N = 128
SSD_K = 5
SSD_XBC = SSD_W + 2 * SSD_G * SSD_N
SSD_CHUNK = 128
DT_PAD = LANE
SSD_COLS_P = SSD_W + SSD_XBC + DT_PAD
CONF_W = 256
CONF_K = 31
S5_W = 256
S5_G = 16
S5_P = 64
S5_CH = 16
S5_S = S5_G * S5_P
S5_TL = 64
NE = 16
NGRP = 4
EPG = 4
MOE_TM = 256
IN_COLS_P = SSD_COLS_P + 2 * CONF_W + S5_W
ROW_TM = 512
ROW_TQ = ROW_TM // NB
CH_ROWS = SSD_CHUNK * NB
VMEM_LIMIT = 56 * 1024 * 1024


def _cparams(sem):
    return pltpu.CompilerParams(dimension_semantics=sem, vmem_limit_bytes=VMEM_LIMIT)


def _const_spec(shape):
    nd = len(shape)
    return pl.BlockSpec(shape, lambda *_: (0,) * nd)


def _layer_spec(l, shape):
    nd = len(shape)
    return pl.BlockSpec((None,) + tuple(shape), lambda *_: (l,) + (0,) * nd)


def _silu(v):
    return v * jax.nn.sigmoid(v)


def _mod_kernel(c_ref, w_ref, b_ref, o_ref):
    c = c_ref[...]
    h = _silu(c).astype(BF16)
    o_ref[...] = jnp.dot(h, w_ref[...].astype(BF16), preferred_element_type=F32) + b_ref[...]


def _modulation(cond, w_ada, b_ada):
    depth = w_ada.shape[0]
    nrow = cond.shape[0]
    return pl.pallas_call(
        _mod_kernel,
        out_shape=jax.ShapeDtypeStruct((depth, nrow, N_MOD * D), F32),
        grid=(depth, N_MOD),
        in_specs=[
            pl.BlockSpec((nrow, D), lambda l, j: (0, 0)),
            pl.BlockSpec((None, D, D), lambda l, j: (l, 0, j)),
            pl.BlockSpec((None, 1, D), lambda l, j: (l, 0, j)),
        ],
        out_specs=pl.BlockSpec((None, nrow, D), lambda l, j: (l, 0, j)),
        compiler_params=_cparams(("arbitrary", "arbitrary")),
        name="adaln_mod",
    )(cond, w_ada, b_ada.reshape(depth, 1, N_MOD * D))


def _mod_spec(l, ctx_blk, first=0):
    return pl.BlockSpec((None, None, N_MOD, NB, D),
                        lambda i, *_: (l, jnp.where(i + first < ctx_blk, 0, 1), 0, 0, 0))


def _norm_mod_project(x, mod_ref, g_ref, w_ref, ssd_ref, conf_ref, s5_ref):
    tm = x.shape[0]
    ms = jnp.mean(x * x, axis=-1, keepdims=True)
    xn = x * lax.rsqrt(ms + EPS) * g_ref[...]
    h = xn.reshape(tm // NB, NB, D) * (1.0 + mod_ref[1])[None] + mod_ref[0][None]
    h = h.reshape(tm, D).astype(BF16)
    ssd_ref[...] = jnp.dot(h, w_ref[:, :SSD_COLS_P], preferred_element_type=F32)
    conf_ref[...] = jnp.dot(h, w_ref[:, SSD_COLS_P:SSD_COLS_P + 2 * CONF_W], preferred_element_type=F32)
    s5_ref[...] = jnp.dot(h, w_ref[:, SSD_COLS_P + 2 * CONF_W:], preferred_element_type=F32)


def _in_first_kernel(ctx_blk, ctx_ref, x_ref, pos_ref, mod_ref, g_ref, w_ref,
                     ssd_ref, conf_ref, s5_ref, x0_ref, scr):
    i = pl.program_id(0)
    ncb = D // LANE

    @pl.when(i < ctx_blk)
    def _():
        for b in range(NB):
            for k in range(ncb):
                scr[k, pl.ds(b, ROW_TQ, stride=NB), :] = ctx_ref[b, :, k * LANE:(k + 1) * LANE]

    @pl.when(i >= ctx_blk)
    def _():
        for b in range(NB):
            for k in range(ncb):
                scr[k, pl.ds(b, ROW_TQ, stride=NB), :] = (
                    x_ref[b, :, k * LANE:(k + 1) * LANE] + pos_ref[:, k * LANE:(k + 1) * LANE])

    x = jnp.concatenate([scr[k] for k in range(ncb)], axis=1)
    x0_ref[...] = x
    _norm_mod_project(x, mod_ref, g_ref, w_ref, ssd_ref, conf_ref, s5_ref)


def _in_kernel(x_ref, mod_ref, g_ref, w_ref, ssd_ref, conf_ref, s5_ref):
    _norm_mod_project(x_ref[...], mod_ref, g_ref, w_ref, ssd_ref, conf_ref, s5_ref)


def _in_proj(l, first, x_or_pair, pos, modtabs, g_mix, w_in_p, rows, n_ctx_rows):
    tm = ROW_TM
    nblk = rows // tm
    ctx_blk = n_ctx_rows // tm
    common_specs = [_mod_spec(l, ctx_blk), _layer_spec(l, (1, D)), _layer_spec(l, (D, IN_COLS_P))]
    common_args = [modtabs, g_mix, w_in_p]
    out_shape = [
        jax.ShapeDtypeStruct((rows, SSD_COLS_P), F32),
        jax.ShapeDtypeStruct((rows, 2 * CONF_W), F32),
        jax.ShapeDtypeStruct((rows, S5_W), F32),
    ]
    out_specs = [
        pl.BlockSpec((tm, SSD_COLS_P), lambda i: (i, 0)),
        pl.BlockSpec((tm, 2 * CONF_W), lambda i: (i, 0)),
        pl.BlockSpec((tm, S5_W), lambda i: (i, 0)),
    ]
    if first:
        ctx, x = x_or_pair
        body = functools.partial(_in_first_kernel, ctx_blk)
        in_specs = [
            pl.BlockSpec((NB, ROW_TQ, D), lambda i: (0, jnp.minimum(i, ctx_blk - 1), 0)),
            pl.BlockSpec((NB, ROW_TQ, D), lambda i: (0, jnp.maximum(i - ctx_blk, 0), 0)),
            pl.BlockSpec((ROW_TQ, D), lambda i: (jnp.maximum(i - ctx_blk, 0), 0)),
        ] + common_specs
        args = [ctx, x, pos] + common_args
        out_shape.append(jax.ShapeDtypeStruct((rows, D), F32))
        out_specs.append(pl.BlockSpec((tm, D), lambda i: (i, 0)))
        scratch = [pltpu.VMEM((D // LANE, tm, LANE), F32)]
    else:
        body = _in_kernel
        in_specs = [pl.BlockSpec((tm, D), lambda i: (i, 0))] + common_specs
        args = [x_or_pair] + common_args
        scratch = []
    return pl.pallas_call(
        body,
        out_shape=out_shape,
        grid=(nblk,),
        in_specs=in_specs,
        out_specs=out_specs,
        scratch_shapes=scratch,
        compiler_params=_cparams(("arbitrary",)),
        name="in_proj",
    )(*args)


def _chunk_of(direction, i, nc_ctx, nc):
    if direction == 0:
        return i
    return jnp.where(i < nc_ctx, nc_ctx - 1 - i, nc - 1 - (i - nc_ctx))


def _has_prev(c, nc_ctx):
    return jnp.logical_and(c != 0, c != nc_ctx)


def _has_next(c, nc_ctx, nc):
    return jnp.logical_and(c != nc_ctx - 1, c != nc - 1)


SSD_HALO = 2 * NB


def _ssd_kernel(direction, nc_ctx, nc, *refs):
    if direction == 0:
        (main_ref, prev_ref, next_ref, cw_ref, cb_ref, dtb_ref, alog_ref, dsk_ref,
         o_ref, ext, act, dts, yout, state) = refs
    else:
        (main_ref, prev_ref, next_ref, cw_ref, cb_ref, dtb_ref, alog_ref, ng_ref, yp_ref,
         o_ref, ext, act, dts, yout, state) = refs
    i = pl.program_id(0)
    c = _chunk_of(direction, i, nc_ctx, nc)

    @pl.when(i == 0)
    def _():
        state[...] = jnp.zeros_like(state)

    has_prev = _has_prev(c, nc_ctx)
    has_next = _has_next(c, nc_ctx, nc)
    nblk = SSD_XBC // LANE
    for k in range(nblk):
        lo = SSD_W + k * LANE
        ext[k, SSD_HALO:SSD_HALO + CH_ROWS, :] = main_ref[:, lo:lo + LANE]
        ext[k, 0:SSD_HALO, :] = jnp.where(has_prev, prev_ref[:, lo:lo + LANE], 0.0)
        ext[k, SSD_HALO + CH_ROWS:, :] = jnp.where(has_next, next_ref[:, lo:lo + LANE], 0.0)

    def conv_rb(rb, carry):
        r0 = pl.multiple_of(rb * LANE, LANE)
        for k in range(nblk):
            lo = k * LANE
            acc = jnp.broadcast_to(cb_ref[:, lo:lo + LANE], (LANE, LANE))
            for tap in range(SSD_K):
                acc = acc + ext[k, pl.ds(r0 + NB * tap, LANE), :] * cw_ref[tap:tap + 1, lo:lo + LANE]
            act[k, pl.ds(r0, LANE), :] = _silu(acc)
        return carry

    lax.fori_loop(0, CH_ROWS // LANE, conv_rb, 0)

    raw = main_ref[:, SSD_W + SSD_XBC:] + dtb_ref[...]
    dts[...] = jnp.maximum(raw, 0.0) + jnp.log1p(jnp.exp(-jnp.abs(raw)))
    a_row = -jnp.exp(alog_ref[...])

    tt = lax.broadcasted_iota(I32, (SSD_CHUNK, SSD_CHUNK), 0)
    ss = lax.broadcasted_iota(I32, (SSD_CHUNK, SSD_CHUNK), 1)
    mask = (ss <= tt) if direction == 0 else (ss >= tt)
    tmat = mask.astype(F32)
    last = SSD_CHUNK - 1 if direction == 0 else 0
    hpg = SSD_H // SSD_G

    def per_batch(b, carry):
        sl = pl.ds(b, SSD_CHUNK, stride=NB)
        dt_b = dts[sl, :]
        cs = jnp.dot(tmat, dt_b * a_row, precision=HIGHEST, preferred_element_type=F32)
        cs_t = cs.T
        for g in range(SSD_G):
            bg_t = act[SSD_W // LANE + g, sl, :].T.astype(BF16)
            cg = act[SSD_W // LANE + SSD_G + g, sl, :].astype(BF16)
            gmat = jnp.dot(cg, bg_t, preferred_element_type=F32)
            s_in = state[b, g]
            yoff = jnp.dot(cg, s_in.astype(BF16), preferred_element_type=F32)
            xw, dec, ys = [], [], []
            for hh in range(hpg):
                h = g * hpg + hh
                ln = direction * SSD_H + h
                col = cs[:, ln:ln + 1]
                row = cs_t[ln:ln + 1, :]
                decay = jnp.where(mask, jnp.exp(jnp.where(mask, col - row, 0.0)), 0.0)
                xs_h = act[h // 2, sl, :][:, (h % 2) * SSD_P:(h % 2 + 1) * SSD_P]
                xdt = xs_h * dt_b[:, ln:ln + 1]
                y_h = jnp.dot((gmat * decay).astype(BF16), xdt.astype(BF16), preferred_element_type=F32)
                y_h = y_h + yoff[:, hh * SSD_P:(hh + 1) * SSD_P] * jnp.exp(col)
                if direction == 0:
                    y_h = y_h + dsk_ref[:, h * SSD_P:(h + 1) * SSD_P] * xs_h
                tot = cs[last:last + 1, ln:ln + 1]
                xw.append((xdt * jnp.exp(tot - col)).astype(BF16))
                dec.append(jnp.broadcast_to(jnp.exp(tot), (1, SSD_P)))
                ys.append(y_h)
            s_new = jnp.dot(bg_t, jnp.concatenate(xw, axis=1), preferred_element_type=F32)
            state[b, g] = s_in * jnp.concatenate(dec, axis=1) + s_new
            yout[2 * g, sl, :] = jnp.concatenate(ys[0:2], axis=1)
            yout[2 * g + 1, sl, :] = jnp.concatenate(ys[2:4], axis=1)
        return carry

    lax.fori_loop(0, NB, per_batch, 0)

    nyb = SSD_W // LANE
    if direction == 0:
        for k in range(nyb):
            o_ref[:, k * LANE:(k + 1) * LANE] = yout[k]
    else:
        def fin(rb, carry):
            r0 = pl.multiple_of(rb * LANE, LANE)
            y = jnp.concatenate([yout[k, pl.ds(r0, LANE), :] for k in range(nyb)], axis=1)
            y = y + yp_ref[pl.ds(r0, LANE), :]
            y = y * _silu(main_ref[pl.ds(r0, LANE), 0:SSD_W])
            ms = jnp.mean(y * y, axis=-1, keepdims=True)
            o_ref[pl.ds(r0, LANE), :] = y * lax.rsqrt(ms + EPS) * ng_ref[...]
            return carry

        lax.fori_loop(0, CH_ROWS // LANE, fin, 0)


def _ssd_pass(l, direction, cols, n_ctx_rows, cw, cb, dtb, alog, extra, ypart):
    rows = cols.shape[0]
    nc = rows // CH_ROWS
    nc_ctx = n_ctx_rows // CH_ROWS
    hpc = CH_ROWS // SSD_HALO
    n_halo = rows // SSD_HALO
    chunk = functools.partial(_chunk_of, direction, nc_ctx=nc_ctx, nc=nc)
    in_specs = [
        pl.BlockSpec((CH_ROWS, SSD_COLS_P), lambda i: (chunk(i), 0)),
        pl.BlockSpec((SSD_HALO, SSD_COLS_P), lambda i: (jnp.maximum(chunk(i) * hpc - 1, 0), 0)),
        pl.BlockSpec((SSD_HALO, SSD_COLS_P), lambda i: (jnp.minimum((chunk(i) + 1) * hpc, n_halo - 1), 0)),
        _layer_spec(l, (SSD_K, SSD_XBC)),
        _layer_spec(l, (1, SSD_XBC)),
        _layer_spec(l, (1, DT_PAD)),
        _layer_spec(l, (1, DT_PAD)),
        _layer_spec(l, (1, SSD_W)),
    ]
    args = [cols, cols, cols, cw, cb, dtb, alog, extra]
    if direction == 1:
        in_specs.append(pl.BlockSpec((CH_ROWS, SSD_W), lambda i: (chunk(i), 0)))
        args.append(ypart)
    return pl.pallas_call(
        functools.partial(_ssd_kernel, direction, nc_ctx, nc),
        out_shape=jax.ShapeDtypeStruct((rows, SSD_W), F32),
        grid=(nc,),
        in_specs=in_specs,
        out_specs=pl.BlockSpec((CH_ROWS, SSD_W), lambda i: (chunk(i), 0)),
        scratch_shapes=[
            pltpu.VMEM((SSD_XBC // LANE, CH_ROWS + 2 * SSD_HALO, LANE), F32),
            pltpu.VMEM((SSD_XBC // LANE, CH_ROWS, LANE), F32),
            pltpu.VMEM((CH_ROWS, DT_PAD), F32),
            pltpu.VMEM((SSD_W // LANE, CH_ROWS, LANE), F32),
            pltpu.VMEM((NB, SSD_G, SSD_N, (SSD_H // SSD_G) * SSD_P), F32),
        ],
        compiler_params=_cparams(("arbitrary",)),
        name="ssd_fwd" if direction == 0 else "ssd_bwd",
    )(*args)


CONF_HALO = 128


def _conf_kernel(first_chunk, nc_ctx, nc, main_ref, prev_ref, next_ref, w_ref, b_ref, lng_ref, lnb_ref,
                 pw_ref, pwb_ref, o_ref, ext):
    c = pl.program_id(0) + first_chunk
    has_prev = _has_prev(c, nc_ctx)
    has_next = _has_next(c, nc_ctx, nc)

    def glu(ref):
        v = ref[...]
        return v[:, :CONF_W] * jax.nn.sigmoid(v[:, CONF_W:])

    um = glu(main_ref)
    up = jnp.where(has_prev, glu(prev_ref), 0.0)
    un = jnp.where(has_next, glu(next_ref), 0.0)
    nblk = CONF_W // LANE
    for k in range(nblk):
        ext[k, 0:CONF_HALO, :] = up[:, k * LANE:(k + 1) * LANE]
        ext[k, CONF_HALO:CONF_HALO + CH_ROWS, :] = um[:, k * LANE:(k + 1) * LANE]
        ext[k, CONF_HALO + CH_ROWS:, :] = un[:, k * LANE:(k + 1) * LANE]
    off = CONF_HALO - NB * (CONF_K // 2)

    def rbody(rb, carry):
        r0 = pl.multiple_of(rb * LANE, LANE)
        accs = []
        for k in range(nblk):
            lo = k * LANE
            acc = jnp.broadcast_to(b_ref[:, lo:lo + LANE], (LANE, LANE))
            for tap in range(CONF_K):
                acc = acc + ext[k, pl.ds(r0 + NB * tap + off, LANE), :] * w_ref[tap:tap + 1, lo:lo + LANE]
            accs.append(acc)
        u = jnp.concatenate(accs, axis=1)
        xc = u - jnp.mean(u, axis=-1, keepdims=True)
        y = xc * lax.rsqrt(jnp.mean(xc * xc, axis=-1, keepdims=True) + EPS) * lng_ref[...] + lnb_ref[...]
        y = _silu(y).astype(BF16)
        o_ref[pl.ds(r0, LANE), :] = jnp.dot(y, pw_ref[...], preferred_element_type=F32) + pwb_ref[...]
        return carry

    lax.fori_loop(0, CH_ROWS // LANE, rbody, 0)


def _conformer(l, cols, n_ctx_rows, with_ctx, w, b, lng, lnb, pw, pwb):
    rows = cols.shape[0]
    nc = rows // CH_ROWS
    nc_ctx = n_ctx_rows // CH_ROWS
    first = 0 if with_ctx else nc_ctx
    hpc = CH_ROWS // CONF_HALO
    n_halo = rows // CONF_HALO
    return pl.pallas_call(
        functools.partial(_conf_kernel, first, nc_ctx, nc),
        out_shape=jax.ShapeDtypeStruct(((nc - first) * CH_ROWS, CONF_W), F32),
        grid=(nc - first,),
        in_specs=[
            pl.BlockSpec((CH_ROWS, 2 * CONF_W), lambda i: (i + first, 0)),
            pl.BlockSpec((CONF_HALO, 2 * CONF_W), lambda i: (jnp.maximum((i + first) * hpc - 1, 0), 0)),
            pl.BlockSpec((CONF_HALO, 2 * CONF_W), lambda i: (jnp.minimum((i + first + 1) * hpc, n_halo - 1), 0)),
            _layer_spec(l, (CONF_K, CONF_W)),
            _layer_spec(l, (1, CONF_W)),
            _layer_spec(l, (1, CONF_W)),
            _layer_spec(l, (1, CONF_W)),
            _layer_spec(l, (CONF_W, CONF_W)),
            _layer_spec(l, (1, CONF_W)),
        ],
        out_specs=pl.BlockSpec((CH_ROWS, CONF_W), lambda i: (i, 0)),
        scratch_shapes=[pltpu.VMEM((CONF_W // LANE, CH_ROWS + 2 * CONF_HALO, LANE), F32)],
        compiler_params=_cparams(("arbitrary",)),
        name="conformer",
    )(cols, cols, cols, w, b, lng, lnb, pw, pwb)


def _s5_kernel(direction, *refs):
    if direction == 0:
        u_ref, bmat_ref, ar_ref, ai_ref, cmat_ref, o_ref, hbuf, st = refs
    else:
        (u_ref, bmat_ref, ar_ref, ai_ref, cmat_ref, yf_ref, dsk_ref, gw_ref, gb_ref,
         o_ref, hbuf, st) = refs
    i = pl.program_id(0)

    @pl.when(i == 0)
    def _():
        st[...] = jnp.zeros_like(st)

    u = u_ref[...]
    hbuf[...] = jnp.dot(u.astype(BF16), bmat_ref[...], preferred_element_type=F32)
    ar = jnp.broadcast_to(ar_ref[...], (NB, S5_S))
    ai = jnp.broadcast_to(ai_ref[...], (NB, S5_S))

    def step(j, carry):
        hr, hi = carry
        t = j if direction == 0 else S5_TL - 1 - j
        r0 = pl.multiple_of(t * NB, NB)
        nr = ar * hr - ai * hi + hbuf[pl.ds(r0, NB), 0:S5_S]
        ni = ar * hi + ai * hr + hbuf[pl.ds(r0, NB), S5_S:]
        hbuf[pl.ds(r0, NB), 0:S5_S] = nr
        hbuf[pl.ds(r0, NB), S5_S:] = ni
        return nr, ni

    hr, hi = lax.fori_loop(0, S5_TL, step, (st[0], st[1]), unroll=2)
    st[0] = hr
    st[1] = hi
    y = jnp.dot(hbuf[...].astype(BF16), cmat_ref[...], preferred_element_type=F32)
    if direction == 0:
        o_ref[...] = y
    else:
        v = jax.nn.gelu(y + yf_ref[...] + dsk_ref[...] * u, approximate=True)
        gv = jnp.dot(v.astype(BF16), gw_ref[...], preferred_element_type=F32) + gb_ref[...]
        o_ref[...] = gv[:, :S5_W] * jax.nn.sigmoid(gv[:, S5_W:])


def _s5_pass(l, direction, u, n_ctx_rows, bmat, ar, ai, cmat, yf, dsk, gw, gb):
    rows = u.shape[0]
    tr = S5_TL * NB
    nblk = rows // tr
    nb_ctx = n_ctx_rows // tr
    blk = functools.partial(_chunk_of, direction, nc_ctx=nb_ctx, nc=nblk)

    def dir_spec(shape):
        nd = len(shape)
        return pl.BlockSpec((None, None) + tuple(shape), lambda i: (l, direction) + (0,) * nd)

    in_specs = [
        pl.BlockSpec((tr, S5_W), lambda i: (blk(i), 0)),
        dir_spec((S5_W, 2 * S5_S)),
        dir_spec((1, S5_S)),
        dir_spec((1, S5_S)),
        _layer_spec(l, (2 * S5_S, S5_W)),
    ]
    args = [u, bmat, ar, ai, cmat]
    if direction == 1:
        in_specs += [
            pl.BlockSpec((tr, S5_W), lambda i: (blk(i), 0)),
            _layer_spec(l, (1, S5_W)),
            _layer_spec(l, (S5_W, 2 * S5_W)),
            _layer_spec(l, (1, 2 * S5_W)),
        ]
        args += [yf, dsk, gw, gb]
    return pl.pallas_call(
        functools.partial(_s5_kernel, direction),
        out_shape=jax.ShapeDtypeStruct((rows, S5_W), F32),
        grid=(nblk,),
        in_specs=in_specs,
        out_specs=pl.BlockSpec((tr, S5_W), lambda i: (blk(i), 0)),
        scratch_shapes=[pltpu.VMEM((tr, 2 * S5_S), F32), pltpu.VMEM((2, NB, S5_S), F32)],
        compiler_params=_cparams(("arbitrary",)),
        name="s5_fwd" if direction == 0 else "s5_bwd",
    )(*args)


def _s5_operators(lam_re, lam_im, log_step, b_re, b_im, c_re, c_im):
    depth = lam_re.shape[0]
    eye = jnp.eye(S5_G, dtype=F32)
    step = jnp.exp(log_step)[..., None]
    mag = jnp.exp(lam_re * step)
    ar, ai = mag * jnp.cos(lam_im * step), mag * jnp.sin(lam_im * step)
    inv_den = 1.0 / (lam_re * lam_re + lam_im * lam_im)
    cr = ((ar - 1.0) * lam_re + ai * lam_im) * inv_den
    ci = (ai * lam_re - (ar - 1.0) * lam_im) * inv_den
    bre, bim = b_re[:, None], b_im[:, None]
    bbr = cr[..., None] * bre - ci[..., None] * bim
    bbi = cr[..., None] * bim + ci[..., None] * bre
    bd_r = jnp.einsum("ldgph,gk->ldghkp", bbr, eye).reshape(depth, 2, S5_W, S5_S)
    bd_i = jnp.einsum("ldgph,gk->ldghkp", bbi, eye).reshape(depth, 2, S5_W, S5_S)
    bmat = jnp.concatenate([bd_r, bd_i], axis=-1).astype(BF16)
    cre = jnp.einsum("lghp,gk->lgpkh", c_re, eye).reshape(depth, S5_S, S5_W)
    cim = jnp.einsum("lghp,gk->lgpkh", c_im, eye).reshape(depth, S5_S, S5_W)
    cmat = jnp.concatenate([cre, -cim], axis=1).astype(BF16)
    return bmat, ar.reshape(depth, 2, 1, S5_S), ai.reshape(depth, 2, 1, S5_S), cmat


def _route_rows(lg):
    m = jnp.max(lg, axis=0, keepdims=True)
    ex = jnp.exp(lg - m)
    probs = ex / jnp.sum(ex, axis=0, keepdims=True)
    p = [probs[e:e + 1, :] for e in range(NE)]
    tops = []
    for q in range(NGRP):
        v = p[q * EPG:(q + 1) * EPG]
        m1 = functools.reduce(jnp.maximum, v)
        i1 = jnp.full_like(m1, float(EPG - 1))
        for j in range(EPG - 2, -1, -1):
            i1 = jnp.where(v[j] == m1, float(j), i1)
        rest = [jnp.where(i1 == float(j), -jnp.inf, v[j]) for j in range(EPG)]
        m2 = functools.reduce(jnp.maximum, rest)
        i2 = jnp.full_like(m2, float(EPG - 1))
        for j in range(EPG - 2, -1, -1):
            i2 = jnp.where(jnp.logical_and(rest[j] == m2, i1 != float(j)), float(j), i2)
        tops.append((m1, i1, m2, i2))
    score = [t[0] + t[2] for t in tops]
    best = functools.reduce(jnp.maximum, score)
    sel = tops[NGRP - 1] + (jnp.full_like(best, float(NGRP - 1)),)
    for q in range(NGRP - 2, -1, -1):
        hit = score[q] == best
        sel = tuple(jnp.where(hit, a, b) for a, b in zip(tops[q] + (jnp.full_like(best, float(q)),), sel))
    m1, i1, m2, i2, grp = sel
    den = m1 + m2
    return grp * EPG + i1, grp * EPG + i2, m1 / den, m2 / den


def _out_kernel(a_ref, b_ref, s_ref, x_ref, mod_ref, g_ref, w_ref, rw_ref, rb_ref,
                x1_ref, h2_ref, rt_ref, gc_ref, cnt_ref):
    tm = x_ref.shape[0]
    mix = jnp.dot(a_ref[...].astype(BF16), w_ref[0:SSD_W, :], preferred_element_type=F32)
    mix = mix + jnp.dot(b_ref[...].astype(BF16), w_ref[SSD_W:SSD_W + CONF_W, :], preferred_element_type=F32)
    mix = mix + jnp.dot(s_ref[...].astype(BF16), w_ref[SSD_W + CONF_W:, :], preferred_element_type=F32)
    x1 = (x_ref[...].reshape(tm // NB, NB, D) + mod_ref[2][None] * mix.reshape(tm // NB, NB, D)).reshape(tm, D)
    x1_ref[...] = x1
    ms = jnp.mean(x1 * x1, axis=-1, keepdims=True)
    xn = x1 * lax.rsqrt(ms + EPS) * g_ref[...]
    h2 = (xn.reshape(tm // NB, NB, D) * (1.0 + mod_ref[4])[None] + mod_ref[3][None]).reshape(tm, D)
    h2_ref[...] = h2
    lg = lax.dot_general(rw_ref[...], h2, (((1,), (1,)), ((), ())), precision=HIGHEST,
                         preferred_element_type=F32) + rb_ref[...]
    e1, e2, g1, g2 = _route_rows(lg)
    eid = lax.broadcasted_iota(I32, (NE, tm), 0).astype(F32)
    oh0 = (eid == e1).astype(F32)
    oh1 = (eid == e2).astype(F32)
    t0 = lax.broadcasted_iota(I32, (tm, tm), 0)
    t1 = lax.broadcasted_iota(I32, (tm, tm), 1)
    before = (t0 < t1).astype(BF16)
    pre0 = jnp.dot(oh0.astype(BF16), before, preferred_element_type=F32)
    pre1 = jnp.dot(oh1.astype(BF16), before, preferred_element_type=F32)
    lr0 = jnp.sum(oh0 * pre0, axis=0, keepdims=True)
    lr1 = jnp.sum(oh1 * pre1, axis=0, keepdims=True)
    rt_ref[...] = jnp.concatenate([e1, e2, lr0, lr1, jnp.zeros((NB - 4, tm), F32)], axis=0)
    gates = jnp.concatenate([g1, g2, jnp.zeros((LANE - 2, tm), F32)], axis=0)
    gc_ref[...] = gates.T
    cnt = jnp.concatenate([jnp.sum(oh0, axis=1, keepdims=True), jnp.sum(oh1, axis=1, keepdims=True)], axis=0)
    cnt_ref[...] = jnp.broadcast_to(cnt, (2 * NE, LANE))


def _out_proj(l, a, bconf, s, x, modtabs, g_ffn, w_out, rw_t, rb, n_ctx_rows, with_ctx):
    tm = ROW_TM
    ctx_blk = n_ctx_rows // tm
    tot_blk = a.shape[0] // tm
    first = 0 if with_ctx else ctx_blk
    nblk = tot_blk - first
    rows = nblk * tm
    x_off = x.shape[0] // tm - nblk
    b_off = bconf.shape[0] // tm - nblk
    return pl.pallas_call(
        _out_kernel,
        out_shape=[
            jax.ShapeDtypeStruct((rows, D), F32),
            jax.ShapeDtypeStruct((rows, D), F32),
            jax.ShapeDtypeStruct((NB, rows), F32),
            jax.ShapeDtypeStruct((rows, LANE), F32),
            jax.ShapeDtypeStruct((nblk, 2 * NE, LANE), F32),
        ],
        grid=(nblk,),
        in_specs=[
            pl.BlockSpec((tm, SSD_W), lambda i: (i + first, 0)),
            pl.BlockSpec((tm, CONF_W), lambda i: (i + b_off, 0)),
            pl.BlockSpec((tm, S5_W), lambda i: (i + first, 0)),
            pl.BlockSpec((tm, D), lambda i: (i + x_off, 0)),
            _mod_spec(l, ctx_blk, first),
            _layer_spec(l, (1, D)),
            _layer_spec(l, (D, D)),
            _const_spec((NE, D)),
            _const_spec((NE, 1)),
        ],
        out_specs=[
            pl.BlockSpec((tm, D), lambda i: (i, 0)),
            pl.BlockSpec((tm, D), lambda i: (i, 0)),
            pl.BlockSpec((NB, tm), lambda i: (0, i)),
            pl.BlockSpec((tm, LANE), lambda i: (i, 0)),
            pl.BlockSpec((None, 2 * NE, LANE), lambda i: (i, 0, 0)),
        ],
        compiler_params=_cparams(("arbitrary",)),
        name="out_proj_route",
    )(a, bconf, s, x, modtabs, g_ffn, w_out, rw_t, rb)


def _moe_kernel(it_ref, ie_ref, lo_ref, hi_ref, h_ref, wg_ref, wu_ref, wd_ref, o_ref, wgb, wub, wdb, cur):
    w = pl.program_id(0)
    lo = lo_ref[w]
    hi = hi_ref[w]
    new_tile = jnp.logical_or(w == 0, it_ref[w] != it_ref[jnp.maximum(w - 1, 0)])

    @pl.when(w == 0)
    def _():
        cur[0] = -1

    @pl.when(new_tile)
    def _():
        o_ref[...] = jnp.zeros_like(o_ref)

    @pl.when(hi > lo)
    def _():
        @pl.when(cur[0] != ie_ref[w])
        def _():
            wgb[...] = wg_ref[...].astype(BF16)
            wub[...] = wu_ref[...].astype(BF16)
            wdb[...] = wd_ref[...].astype(BF16)
            cur[0] = ie_ref[w]

        h = h_ref[...].astype(BF16)
        gate = jnp.dot(h, wgb[...], preferred_element_type=F32)
        up = jnp.dot(h, wub[...], preferred_element_type=F32)
        act = (_silu(gate) * up).astype(BF16)
        y = jnp.dot(act, wdb[...], preferred_element_type=F32)
        r = lax.broadcasted_iota(I32, (MOE_TM, 1), 0)
        keep = jnp.logical_and(r >= lo, r < hi)
        o_ref[...] = jnp.where(keep, y, o_ref[...])


def _moe_experts(l, hs, item_tile, item_expert, item_lo, item_hi, w_gate, w_up, w_down):
    n_rows = hs.shape[0]
    n_items = item_tile.shape[0]

    def wspec():
        return pl.BlockSpec((None, None, D, D), lambda w, it, ie, lo, hi: (l, ie[w], 0, 0))

    return pl.pallas_call(
        _moe_kernel,
        out_shape=jax.ShapeDtypeStruct((n_rows, D), F32),
        grid_spec=pltpu.PrefetchScalarGridSpec(
            num_scalar_prefetch=4,
            grid=(n_items,),
            in_specs=[
                pl.BlockSpec((MOE_TM, D), lambda w, it, ie, lo, hi: (it[w], 0)),
                wspec(), wspec(), wspec(),
            ],
            out_specs=pl.BlockSpec((MOE_TM, D), lambda w, it, ie, lo, hi: (it[w], 0)),
            scratch_shapes=[pltpu.VMEM((D, D), BF16)] * 3 + [pltpu.SMEM((1,), I32)],
        ),
        compiler_params=_cparams(("arbitrary",)),
        name="moe_experts",
    )(item_tile, item_expert, item_lo, item_hi, hs, w_gate, w_up, w_down)


def _moe_plan(route, cnt):
    rows = route.shape[1]
    nblk = cnt.shape[0]
    n_flat = 2 * rows
    e = route[0:2].astype(I32)
    lrank = route[2:4].astype(I32)
    counts = cnt[:, :, 0].astype(I32).reshape(nblk, 2, NE)
    per = jnp.transpose(counts, (1, 0, 2)).reshape(2 * nblk, NE)
    before = jnp.cumsum(per, axis=0) - per
    gcount = jnp.sum(per, axis=0)
    gend = jnp.cumsum(gcount)
    gstart = gend - gcount
    base = jnp.transpose((before + gstart[None, :]).reshape(2, nblk, NE), (2, 0, 1))[..., None]
    onehot = e.reshape(1, 2, nblk, ROW_TM) == jnp.arange(NE, dtype=I32).reshape(NE, 1, 1, 1)
    inv = jnp.sum(jnp.where(onehot, base, 0), axis=0).reshape(2, rows) + lrank
    keys = (e * 65536).reshape(-1) + jnp.arange(n_flat, dtype=I32)
    order = lax.sort(keys, is_stable=False) & 0xFFFF
    tok = jnp.where(order >= rows, order - rows, order)
    n_tiles = n_flat // MOE_TM
    tile_start = jnp.arange(n_tiles, dtype=I32) * MOE_TM
    bnd = lax.sort(jnp.concatenate([tile_start, gstart[1:]]), is_stable=False)
    bnd_hi = jnp.concatenate([bnd[1:], jnp.full((1,), n_flat, I32)])
    item_tile = jnp.minimum(bnd // MOE_TM, n_tiles - 1)
    item_expert = jnp.minimum(jnp.sum((gend[None, :] <= bnd[:, None]).astype(I32), axis=1), NE - 1)
    item_lo = bnd - item_tile * MOE_TM
    item_hi = bnd_hi - item_tile * MOE_TM
    return tok, inv, item_tile, item_expert, item_lo, item_hi


def _comb_kernel(final, x_ref, ya_ref, yb_ref, gc_ref, mod_ref, g_ref, o_ref, *scratch):
    tm = x_ref.shape[0]
    gc = gc_ref[...]
    ff = (gc[:, 0:1] * ya_ref[...] + gc[:, 1:2] * yb_ref[...]).reshape(tm // NB, NB, D)
    x2 = (x_ref[...].reshape(tm // NB, NB, D) + mod_ref[5][None] * ff).reshape(tm, D)
    if not final:
        o_ref[...] = x2
        return
    scr, = scratch
    ms = jnp.mean(x2 * x2, axis=-1, keepdims=True)
    x2 = x2 * lax.rsqrt(ms + EPS) * g_ref[...]
    ncb = D // LANE
    for k in range(ncb):
        scr[k] = x2[:, k * LANE:(k + 1) * LANE]
    for b in range(NB):
        for k in range(ncb):
            o_ref[b, :, k * LANE:(k + 1) * LANE] = scr[k, pl.ds(b, ROW_TQ, stride=NB), :]


def _combine(l, x1, ya, yb, gcol, modtabs, g_final, n_ctx_rows, with_ctx, final):
    tm = ROW_TM
    rows = x1.shape[0]
    ctx_blk = n_ctx_rows // tm if with_ctx else 0
    if final:
        out_shape = jax.ShapeDtypeStruct((NB, rows // NB, D), F32)
        out_spec = pl.BlockSpec((NB, ROW_TQ, D), lambda i: (0, i, 0))
        scratch = [pltpu.VMEM((D // LANE, tm, LANE), F32)]
    else:
        out_shape = jax.ShapeDtypeStruct((rows, D), F32)
        out_spec = pl.BlockSpec((tm, D), lambda i: (i, 0))
        scratch = []
    return pl.pallas_call(
        functools.partial(_comb_kernel, final),
        out_shape=out_shape,
        grid=(rows // tm,),
        in_specs=[
            pl.BlockSpec((tm, D), lambda i: (i, 0)),
            pl.BlockSpec((tm, D), lambda i: (i, 0)),
            pl.BlockSpec((tm, D), lambda i: (i, 0)),
            pl.BlockSpec((tm, LANE), lambda i: (i, 0)),
            _mod_spec(l, ctx_blk),
            _const_spec((1, D)),
        ],
        out_specs=out_spec,
        scratch_shapes=scratch,
        compiler_params=_cparams(("arbitrary",)),
        name="moe_combine",
    )(x1, ya, yb, gcol, modtabs, g_final)


def _grid_pos_embed(rows_n):
    rr, cc = jnp.meshgrid(jnp.arange(rows_n, dtype=F32), jnp.arange(GRID_W, dtype=F32), indexing="ij")
    quarter = D // 4
    inv_freq = jnp.exp(-math.log(10000.0) * jnp.arange(quarter, dtype=F32) / quarter)

    def emb(pos):
        ang = pos.reshape(-1)[:, None] * inv_freq[None, :]
        return jnp.concatenate([jnp.sin(ang), jnp.cos(ang)], axis=-1)

    return jnp.concatenate([emb(rr), emb(cc)], axis=-1)


def _take_rows(a, idx):
    return a.at[idx].get(mode="promise_in_bounds")


def _pad_last(v, width):
    return jnp.pad(v, [(0, 0)] * (v.ndim - 1) + [(0, width - v.shape[-1])])


def kernel(x, c, ctx, c_ctx, w_ada, b_ada, g_mix, w_in, ssd_conv_w, ssd_conv_b, ssd_dt_bias, ssd_a_log, ssd_d, ssd_norm_g, conf_dw_w, conf_dw_b, conf_ln_g, conf_ln_b, conf_pw_w, conf_pw_b, s5_lambda_re, s5_lambda_im, s5_log_step, s5_b_re, s5_b_im, s5_c_re, s5_c_im, s5_d, s5_glu_w, s5_glu_b, w_out, g_ffn, router_w, router_b, exp_w_gate, exp_w_up, exp_w_down, g_final):
    bsz, seq, _ = x.shape
    ctx_len = ctx.shape[1]
    depth = w_ada.shape[0]
    assert bsz == NB and seq % SSD_CHUNK == 0 and ctx_len % SSD_CHUNK == 0
    n_ctx_rows = ctx_len * NB
    rows_all = (ctx_len + seq) * NB

    cond = jnp.concatenate([c, c_ctx[None, :], jnp.zeros((2 * NB - bsz - 1, D), F32)], axis=0)
    mod = _modulation(cond, w_ada, b_ada)
    mod = mod.reshape(depth, 2 * NB, N_MOD, D)
    mod_lat = jnp.transpose(mod[:, :NB], (0, 2, 1, 3))
    mod_ctx = jnp.broadcast_to(mod[:, NB][:, :, None, :], (depth, N_MOD, NB, D))
    modtabs = jnp.stack([mod_ctx, mod_lat], axis=1)

    o_b = SSD_W + SSD_XBC + 2 * SSD_H
    w_in_p = jnp.concatenate(
        [w_in[:, :, :o_b], jnp.zeros((depth, D, DT_PAD - 2 * SSD_H), F32), w_in[:, :, o_b:]], axis=2).astype(BF16)
    g_mix3 = g_mix.reshape(depth, 1, D)
    ssd_cb = ssd_conv_b.reshape(depth, 1, SSD_XBC)
    ssd_dtb = _pad_last(ssd_dt_bias.reshape(depth, 1, 2 * SSD_H), DT_PAD)
    ssd_alog = _pad_last(ssd_a_log.reshape(depth, 1, 2 * SSD_H), DT_PAD)
    ssd_dsk = jnp.repeat(ssd_d, SSD_P, axis=1).reshape(depth, 1, SSD_W)
    ssd_ng = ssd_norm_g.reshape(depth, 1, SSD_W)
    conf_b = conf_dw_b.reshape(depth, 1, CONF_W)
    conf_lg = conf_ln_g.reshape(depth, 1, CONF_W)
    conf_lb = conf_ln_b.reshape(depth, 1, CONF_W)
    conf_pw = conf_pw_w.astype(BF16)
    conf_pb = conf_pw_b.reshape(depth, 1, CONF_W)
    s5_bmat, s5_ar, s5_ai, s5_cmat = _s5_operators(s5_lambda_re, s5_lambda_im, s5_log_step,
                                                   s5_b_re, s5_b_im, s5_c_re, s5_c_im)
    s5_dsk = s5_d.reshape(depth, 1, S5_W)
    s5_gw = s5_glu_w.astype(BF16)
    s5_gb = s5_glu_b.reshape(depth, 1, 2 * S5_W)
    w_out_b = w_out.astype(BF16)
    g_ffn3 = g_ffn.reshape(depth, 1, D)
    router_w_t = router_w.T
    router_b2 = router_b.reshape(NE, 1)
    g_final2 = g_final.reshape(1, D)
    pos = _grid_pos_embed(seq // GRID_W)

    xall = None
    for l in range(depth):
        last = l == depth - 1
        if l == 0:
            ssd_cols, conf_cols, s5_u, xall = _in_proj(l, True, (ctx, x), pos, modtabs, g_mix3, w_in_p,
                                                       rows_all, n_ctx_rows)
        else:
            ssd_cols, conf_cols, s5_u = _in_proj(l, False, xall, None, modtabs, g_mix3, w_in_p,
                                                 rows_all, n_ctx_rows)
        ypart = _ssd_pass(l, 0, ssd_cols, n_ctx_rows, ssd_conv_w, ssd_cb, ssd_dtb, ssd_alog, ssd_dsk, None)
        a_mix = _ssd_pass(l, 1, ssd_cols, n_ctx_rows, ssd_conv_w, ssd_cb, ssd_dtb, ssd_alog, ssd_ng, ypart)
        b_mix = _conformer(l, conf_cols, n_ctx_rows, not last, conf_dw_w, conf_b, conf_lg, conf_lb, conf_pw, conf_pb)
        yf = _s5_pass(l, 0, s5_u, n_ctx_rows, s5_bmat, s5_ar, s5_ai, s5_cmat, None, None, None, None)
        s_mix = _s5_pass(l, 1, s5_u, n_ctx_rows, s5_bmat, s5_ar, s5_ai, s5_cmat, yf, s5_dsk, s5_gw, s5_gb)
        x1, h2p, route, gcol, cnt = _out_proj(l, a_mix, b_mix, s_mix, xall, modtabs, g_ffn3, w_out_b,
                                              router_w_t, router_b2, n_ctx_rows, not last)
        tok, inv, item_tile, item_expert, item_lo, item_hi = _moe_plan(route, cnt)
        hs = _take_rows(h2p, tok)
        y = _moe_experts(l, hs, item_tile, item_expert, item_lo, item_hi, exp_w_gate, exp_w_up, exp_w_down)
        ya = _take_rows(y, inv[0])
        yb = _take_rows(y, inv[1])
        xall = _combine(l, x1, ya, yb, gcol, modtabs, g_final2, n_ctx_rows, not last, last)
    return xall
```

```python
import functools
import math

import jax
import jax.numpy as jnp
from jax import lax
from jax.experimental import pallas as pl
from jax.experimental.pallas import tpu as pltpu

F32 = jnp.float32
BF16 = jnp.bfloat16
I32 = jnp.int32
HIGHEST = lax.Precision.HIGHEST

NB = 8
D = 1024
GRID_W = 64
N_MOD = 6
EPS = 1e-6
LANE = 128
SSD_W = 512
SSD_P = 64
SSD_H = 8
SSD_G = 2
SSD_N = 128
SSD_K = 5
SSD_XBC = SSD_W + 2 * SSD_G * SSD_N
SSD_CHUNK = 128
DT_PAD = LANE
SSD_COLS_P = SSD_W + SSD_XBC + DT_PAD
CONF_W = 256
CONF_K = 31
S5_W = 256
S5_G = 16
S5_P = 64
S5_CH = 16
S5_S = S5_G * S5_P
S5_TL = 128
NE = 16
NGRP = 4
EPG = 4
MOE_TM = 256
IN_COLS_P = SSD_COLS_P + 2 * CONF_W + S5_W
ROW_TM = 512
ROW_TQ = ROW_TM // NB
CH_ROWS = SSD_CHUNK * NB
VMEM_LIMIT = 56 * 1024 * 1024


def _cparams(sem):
    return pltpu.CompilerParams(dimension_semantics=sem, vmem_limit_bytes=VMEM_LIMIT)


def _const_spec(shape):
    nd = len(shape)
    return pl.BlockSpec(shape, lambda *_: (0,) * nd)


def _layer_spec(l, shape):
    nd = len(shape)
    return pl.BlockSpec((None,) + tuple(shape), lambda *_: (l,) + (0,) * nd)


def _silu(v):
    return v * jax.nn.sigmoid(v)


def _mod_kernel(c_ref, w_ref, b_ref, o_ref):
    c = c_ref[...]
    h = _silu(c).astype(BF16)
    o_ref[...] = jnp.dot(h, w_ref[...].astype(BF16), preferred_element_type=F32) + b_ref[...]


def _modulation(cond, w_ada, b_ada):
    depth = w_ada.shape[0]
    nrow = cond.shape[0]
    return pl.pallas_call(
        _mod_kernel,
        out_shape=jax.ShapeDtypeStruct((depth, nrow, N_MOD * D), F32),
        grid=(depth, N_MOD),
        in_specs=[
            pl.BlockSpec((nrow, D), lambda l, j: (0, 0)),
            pl.BlockSpec((None, D, D), lambda l, j: (l, 0, j)),
            pl.BlockSpec((None, 1, D), lambda l, j: (l, 0, j)),
        ],
        out_specs=pl.BlockSpec((None, nrow, D), lambda l, j: (l, 0, j)),
        compiler_params=_cparams(("arbitrary", "arbitrary")),
        name="adaln_mod",
    )(cond, w_ada, b_ada.reshape(depth, 1, N_MOD * D))


def _mod_spec(l, ctx_blk, first=0):
    return pl.BlockSpec((None, None, N_MOD, NB, D),
                        lambda i, *_: (l, jnp.where(i + first < ctx_blk, 0, 1), 0, 0, 0))


def _norm_mod_project(x, mod_ref, g_ref, w_ref, ssd_ref, conf_ref, s5_ref):
    tm = x.shape[0]
    ms = jnp.mean(x * x, axis=-1, keepdims=True)
    xn = x * lax.rsqrt(ms + EPS) * g_ref[...]
    h = xn.reshape(tm // NB, NB, D) * (1.0 + mod_ref[1])[None] + mod_ref[0][None]
    h = h.reshape(tm, D).astype(BF16)
    ssd_ref[...] = jnp.dot(h, w_ref[:, :SSD_COLS_P], preferred_element_type=F32)
    conf_ref[...] = jnp.dot(h, w_ref[:, SSD_COLS_P:SSD_COLS_P + 2 * CONF_W], preferred_element_type=F32)
    s5_ref[...] = jnp.dot(h, w_ref[:, SSD_COLS_P + 2 * CONF_W:], preferred_element_type=F32)


def _in_first_kernel(ctx_blk, ctx_ref, x_ref, pos_ref, mod_ref, g_ref, w_ref,
                     ssd_ref, conf_ref, s5_ref, x0_ref, scr):
    i = pl.program_id(0)
    ncb = D // LANE

    @pl.when(i < ctx_blk)
    def _():
        for b in range(NB):
            for k in range(ncb):
                scr[k, pl.ds(b, ROW_TQ, stride=NB), :] = ctx_ref[b, :, k * LANE:(k + 1) * LANE]

    @pl.when(i >= ctx_blk)
    def _():
        for b in range(NB):
            for k in range(ncb):
                scr[k, pl.ds(b, ROW_TQ, stride=NB), :] = (
                    x_ref[b, :, k * LANE:(k + 1) * LANE] + pos_ref[:, k * LANE:(k + 1) * LANE])

    x = jnp.concatenate([scr[k] for k in range(ncb)], axis=1)
    x0_ref[...] = x
    _norm_mod_project(x, mod_ref, g_ref, w_ref, ssd_ref, conf_ref, s5_ref)


def _in_kernel(x_ref, mod_ref, g_ref, w_ref, ssd_ref, conf_ref, s5_ref):
    _norm_mod_project(x_ref[...], mod_ref, g_ref, w_ref, ssd_ref, conf_ref, s5_ref)


def _in_proj(l, first, x_or_pair, pos, modtabs, g_mix, w_in_p, rows, n_ctx_rows):
    tm = ROW_TM
    nblk = rows // tm
    ctx_blk = n_ctx_rows // tm
    common_specs = [_mod_spec(l, ctx_blk), _layer_spec(l, (1, D)), _layer_spec(l, (D, IN_COLS_P))]
    common_args = [modtabs, g_mix, w_in_p]
    out_shape = [
        jax.ShapeDtypeStruct((rows, SSD_COLS_P), F32),
        jax.ShapeDtypeStruct((rows, 2 * CONF_W), F32),
        jax.ShapeDtypeStruct((rows, S5_W), F32),
    ]
    out_specs = [
        pl.BlockSpec((tm, SSD_COLS_P), lambda i: (i, 0)),
        pl.BlockSpec((tm, 2 * CONF_W), lambda i: (i, 0)),
        pl.BlockSpec((tm, S5_W), lambda i: (i, 0)),
    ]
    if first:
        ctx, x = x_or_pair
        body = functools.partial(_in_first_kernel, ctx_blk)
        in_specs = [
            pl.BlockSpec((NB, ROW_TQ, D), lambda i: (0, jnp.minimum(i, ctx_blk - 1), 0)),
            pl.BlockSpec((NB, ROW_TQ, D), lambda i: (0, jnp.maximum(i - ctx_blk, 0), 0)),
            pl.BlockSpec((ROW_TQ, D), lambda i: (jnp.maximum(i - ctx_blk, 0), 0)),
        ] + common_specs
        args = [ctx, x, pos] + common_args
        out_shape.append(jax.ShapeDtypeStruct((rows, D), F32))
        out_specs.append(pl.BlockSpec((tm, D), lambda i: (i, 0)))
        scratch = [pltpu.VMEM((D // LANE, tm, LANE), F32)]
    else:
        body = _in_kernel
        in_specs = [pl.BlockSpec((tm, D), lambda i: (i, 0))] + common_specs
        args = [x_or_pair] + common_args
        scratch = []
    return pl.pallas_call(
        body,
        out_shape=out_shape,
        grid=(nblk,),
        in_specs=in_specs,
        out_specs=out_specs,
        scratch_shapes=scratch,
        compiler_params=_cparams(("arbitrary",)),
        name="in_proj",
    )(*args)


def _chunk_of(direction, i, nc_ctx, nc):
    if direction == 0:
        return i
    return jnp.where(i < nc_ctx, nc_ctx - 1 - i, nc - 1 - (i - nc_ctx))


def _has_prev(c, nc_ctx):
    return jnp.logical_and(c != 0, c != nc_ctx)


def _has_next(c, nc_ctx, nc):
    return jnp.logical_and(c != nc_ctx - 1, c != nc - 1)


SSD_HALO = 2 * NB


def _ssd_kernel(direction, nc_ctx, nc, *refs):
    if direction == 0:
        (main_ref, prev_ref, next_ref, cw_ref, cb_ref, dtb_ref, alog_ref, dsk_ref,
         o_ref, ext, act, dts, yout, state) = refs
    else:
        (main_ref, prev_ref, next_ref, cw_ref, cb_ref, dtb_ref, alog_ref, ng_ref, yp_ref,
         o_ref, ext, act, dts, yout, state) = refs
    i = pl.program_id(0)
    c = _chunk_of(direction, i, nc_ctx, nc)

    @pl.when(i == 0)
    def _():
        state[...] = jnp.zeros_like(state)

    has_prev = _has_prev(c, nc_ctx)
    has_next = _has_next(c, nc_ctx, nc)
    nblk = SSD_XBC // LANE
    for k in range(nblk):
        lo = SSD_W + k * LANE
        ext[k, SSD_HALO:SSD_HALO + CH_ROWS, :] = main_ref[:, lo:lo + LANE]
        ext[k, 0:SSD_HALO, :] = jnp.where(has_prev, prev_ref[:, lo:lo + LANE], 0.0)
        ext[k, SSD_HALO + CH_ROWS:, :] = jnp.where(has_next, next_ref[:, lo:lo + LANE], 0.0)

    def conv_rb(rb, carry):
        r0 = pl.multiple_of(rb * LANE, LANE)
        for k in range(nblk):
            lo = k * LANE
            acc = jnp.broadcast_to(cb_ref[:, lo:lo + LANE], (LANE, LANE))
            for tap in range(SSD_K):
                acc = acc + ext[k, pl.ds(r0 + NB * tap, LANE), :] * cw_ref[tap:tap + 1, lo:lo + LANE]
            act[k, pl.ds(r0, LANE), :] = _silu(acc)
        return carry

    lax.fori_loop(0, CH_ROWS // LANE, conv_rb, 0)

    raw = main_ref[:, SSD_W + SSD_XBC:] + dtb_ref[...]
    dts[...] = jnp.maximum(raw, 0.0) + jnp.log1p(jnp.exp(-jnp.abs(raw)))
    a_row = -jnp.exp(alog_ref[...])

    tt = lax.broadcasted_iota(I32, (SSD_CHUNK, SSD_CHUNK), 0)
    ss = lax.broadcasted_iota(I32, (SSD_CHUNK, SSD_CHUNK), 1)
    mask = (ss <= tt) if direction == 0 else (ss >= tt)
    tmat = mask.astype(F32)
    last = SSD_CHUNK - 1 if direction == 0 else 0
    hpg = SSD_H // SSD_G
    lane_id = lax.broadcasted_iota(I32, (1, LANE), 1)
    half_lo = lane_id < SSD_P
    half_hi = lane_id >= SSD_P

    def per_batch(b, carry):
        sl = pl.ds(b, SSD_CHUNK, stride=NB)
        dt_b = dts[sl, :]
        cs = jnp.dot(tmat, dt_b * a_row, precision=HIGHEST, preferred_element_type=F32)
        cs_t = cs.T
        dt_t = dt_b.T
        for g in range(SSD_G):
            bg_t = act[SSD_W // LANE + g, sl, :].T
            cg = act[SSD_W // LANE + SSD_G + g, sl, :]
            gmat = jnp.dot(cg.astype(BF16), bg_t.astype(BF16), preferred_element_type=F32)
            for pr in range(hpg // 2):
                k = g * (hpg // 2) + pr
                xs_pair = act[k, sl, :]
                s_pair = state[b, k]
                acc_y = jnp.zeros((SSD_CHUNK, LANE), F32)
                acc_s = jnp.zeros((SSD_N, LANE), F32)
                etot = []
                for j in range(2):
                    ln = direction * SSD_H + 2 * k + j
                    own = half_lo if j == 0 else half_hi
                    colb = jnp.broadcast_to(cs[:, ln:ln + 1], (SSD_CHUNK, SSD_CHUNK))
                    row = cs_t[ln:ln + 1, :]
                    dtr = dt_t[ln:ln + 1, :]
                    decay = jnp.where(mask, jnp.exp(jnp.where(mask, colb - row, 0.0)) * dtr, 0.0)
                    lhs = jnp.concatenate([(gmat * decay).astype(BF16),
                                           (cg * jnp.exp(colb)).astype(BF16)], axis=1)
                    xs_own = jnp.where(own, xs_pair, 0.0).astype(BF16)
                    rhs = jnp.concatenate([xs_own, jnp.where(own, s_pair, 0.0).astype(BF16)], axis=0)
                    acc_y = acc_y + jnp.dot(lhs, rhs, preferred_element_type=F32)
                    tot = cs[last:last + 1, ln:ln + 1]
                    wrow = jnp.exp(tot - row) * dtr
                    acc_s = acc_s + jnp.dot((bg_t * wrow).astype(BF16), xs_own, preferred_element_type=F32)
                    etot.append(jnp.exp(tot))
                state[b, k] = s_pair * jnp.where(half_lo, etot[0], etot[1]) + acc_s
                if direction == 0:
                    acc_y = acc_y + dsk_ref[:, k * LANE:(k + 1) * LANE] * xs_pair
                yout[k, sl, :] = acc_y
        return carry

    lax.fori_loop(0, NB, per_batch, 0, unroll=2)

    nyb = SSD_W // LANE
    if direction == 0:
        for k in range(nyb):
            o_ref[:, k * LANE:(k + 1) * LANE] = yout[k]
    else:
        def fin(rb, carry):
            r0 = pl.multiple_of(rb * LANE, LANE)
            y = jnp.concatenate([yout[k, pl.ds(r0, LANE), :] for k in range(nyb)], axis=1)
            y = y + yp_ref[pl.ds(r0, LANE), :]
            y = y * _silu(main_ref[pl.ds(r0, LANE), 0:SSD_W])
            ms = jnp.mean(y * y, axis=-1, keepdims=True)
            o_ref[pl.ds(r0, LANE), :] = y * lax.rsqrt(ms + EPS) * ng_ref[...]
            return carry

        lax.fori_loop(0, CH_ROWS // LANE, fin, 0)


def _ssd_pass(l, direction, cols, n_ctx_rows, cw, cb, dtb, alog, extra, ypart):
    rows = cols.shape[0]
    nc = rows // CH_ROWS
    nc_ctx = n_ctx_rows // CH_ROWS
    hpc = CH_ROWS // SSD_HALO
    n_halo = rows // SSD_HALO
    chunk = functools.partial(_chunk_of, direction, nc_ctx=nc_ctx, nc=nc)
    in_specs = [
        pl.BlockSpec((CH_ROWS, SSD_COLS_P), lambda i: (chunk(i), 0)),
        pl.BlockSpec((SSD_HALO, SSD_COLS_P), lambda i: (jnp.maximum(chunk(i) * hpc - 1, 0), 0)),
        pl.BlockSpec((SSD_HALO, SSD_COLS_P), lambda i: (jnp.minimum((chunk(i) + 1) * hpc, n_halo - 1), 0)),
        _layer_spec(l, (SSD_K, SSD_XBC)),
        _layer_spec(l, (1, SSD_XBC)),
        _layer_spec(l, (1, DT_PAD)),
        _layer_spec(l, (1, DT_PAD)),
        _layer_spec(l, (1, SSD_W)),
    ]
    args = [cols, cols, cols, cw, cb, dtb, alog, extra]
    if direction == 1:
        in_specs.append(pl.BlockSpec((CH_ROWS, SSD_W), lambda i: (chunk(i), 0)))
        args.append(ypart)
    return pl.pallas_call(
        functools.partial(_ssd_kernel, direction, nc_ctx, nc),
        out_shape=jax.ShapeDtypeStruct((rows, SSD_W), F32),
        grid=(nc,),
        in_specs=in_specs,
        out_specs=pl.BlockSpec((CH_ROWS, SSD_W), lambda i: (chunk(i), 0)),
        scratch_shapes=[
            pltpu.VMEM((SSD_XBC // LANE, CH_ROWS + 2 * SSD_HALO, LANE), F32),
            pltpu.VMEM((SSD_XBC // LANE, CH_ROWS, LANE), F32),
            pltpu.VMEM((CH_ROWS, DT_PAD), F32),
            pltpu.VMEM((SSD_W // LANE, CH_ROWS, LANE), F32),
            pltpu.VMEM((NB, SSD_W // LANE, SSD_N, LANE), F32),
        ],
        compiler_params=_cparams(("arbitrary",)),
        name="ssd_fwd" if direction == 0 else "ssd_bwd",
    )(*args)


CONF_HALO = 128


def _conf_kernel(first_chunk, nc_ctx, nc, main_ref, prev_ref, next_ref, w_ref, b_ref, lng_ref, lnb_ref,
                 pw_ref, pwb_ref, o_ref, ext):
    c = pl.program_id(0) + first_chunk
    has_prev = _has_prev(c, nc_ctx)
    has_next = _has_next(c, nc_ctx, nc)

    def glu(ref):
        v = ref[...]
        return v[:, :CONF_W] * jax.nn.sigmoid(v[:, CONF_W:])

    um = glu(main_ref)
    up = jnp.where(has_prev, glu(prev_ref), 0.0)
    un = jnp.where(has_next, glu(next_ref), 0.0)
    nblk = CONF_W // LANE
    for k in range(nblk):
        ext[k, 0:CONF_HALO, :] = up[:, k * LANE:(k + 1) * LANE]
        ext[k, CONF_HALO:CONF_HALO + CH_ROWS, :] = um[:, k * LANE:(k + 1) * LANE]
        ext[k, CONF_HALO + CH_ROWS:, :] = un[:, k * LANE:(k + 1) * LANE]
    off = CONF_HALO - NB * (CONF_K // 2)

    def rbody(rb, carry):
        r0 = pl.multiple_of(rb * LANE, LANE)
        accs = []
        for k in range(nblk):
            lo = k * LANE
            acc = jnp.broadcast_to(b_ref[:, lo:lo + LANE], (LANE, LANE))
            for tap in range(CONF_K):
                acc = acc + ext[k, pl.ds(r0 + NB * tap + off, LANE), :] * w_ref[tap:tap + 1, lo:lo + LANE]
            accs.append(acc)
        u = jnp.concatenate(accs, axis=1)
        xc = u - jnp.mean(u, axis=-1, keepdims=True)
        y = xc * lax.rsqrt(jnp.mean(xc * xc, axis=-1, keepdims=True) + EPS) * lng_ref[...] + lnb_ref[...]
        y = _silu(y).astype(BF16)
        o_ref[pl.ds(r0, LANE), :] = jnp.dot(y, pw_ref[...], preferred_element_type=F32) + pwb_ref[...]
        return carry

    lax.fori_loop(0, CH_ROWS // LANE, rbody, 0)


def _conformer(l, cols, n_ctx_rows, with_ctx, w, b, lng, lnb, pw, pwb):
    rows = cols.shape[0]
    nc = rows // CH_ROWS
    nc_ctx = n_ctx_rows // CH_ROWS
    first = 0 if with_ctx else nc_ctx
    hpc = CH_ROWS // CONF_HALO
    n_halo = rows // CONF_HALO
    return pl.pallas_call(
        functools.partial(_conf_kernel, first, nc_ctx, nc),
        out_shape=jax.ShapeDtypeStruct(((nc - first) * CH_ROWS, CONF_W), F32),
        grid=(nc - first,),
        in_specs=[
            pl.BlockSpec((CH_ROWS, 2 * CONF_W), lambda i: (i + first, 0)),
            pl.BlockSpec((CONF_HALO, 2 * CONF_W), lambda i: (jnp.maximum((i + first) * hpc - 1, 0), 0)),
            pl.BlockSpec((CONF_HALO, 2 * CONF_W), lambda i: (jnp.minimum((i + first + 1) * hpc, n_halo - 1), 0)),
            _layer_spec(l, (CONF_K, CONF_W)),
            _layer_spec(l, (1, CONF_W)),
            _layer_spec(l, (1, CONF_W)),
            _layer_spec(l, (1, CONF_W)),
            _layer_spec(l, (CONF_W, CONF_W)),
            _layer_spec(l, (1, CONF_W)),
        ],
        out_specs=pl.BlockSpec((CH_ROWS, CONF_W), lambda i: (i, 0)),
        scratch_shapes=[pltpu.VMEM((CONF_W // LANE, CH_ROWS + 2 * CONF_HALO, LANE), F32)],
        compiler_params=_cparams(("arbitrary",)),
        name="conformer",
    )(cols, cols, cols, w, b, lng, lnb, pw, pwb)


def _s5_kernel(direction, *refs):
    if direction == 0:
        u_ref, bmat_ref, ar_ref, ai_ref, cmat_ref, o_ref, hbuf, st = refs
    else:
        (u_ref, bmat_ref, ar_ref, ai_ref, cmat_ref, yf_ref, dsk_ref, gw_ref, gb_ref,
         o_ref, hbuf, st) = refs
    i = pl.program_id(0)

    @pl.when(i == 0)
    def _():
        st[...] = jnp.zeros_like(st)

    u = u_ref[...]
    hbuf[...] = jnp.dot(u.astype(BF16), bmat_ref[...], preferred_element_type=F32)
    ar = jnp.broadcast_to(ar_ref[...], (NB, S5_S))
    ai = jnp.broadcast_to(ai_ref[...], (NB, S5_S))

    def step(j, carry):
        hr, hi = carry
        t = j if direction == 0 else S5_TL - 1 - j
        r0 = pl.multiple_of(t * NB, NB)
        nr = ar * hr - ai * hi + hbuf[pl.ds(r0, NB), 0:S5_S]
        ni = ar * hi + ai * hr + hbuf[pl.ds(r0, NB), S5_S:]
        hbuf[pl.ds(r0, NB), 0:S5_S] = nr
        hbuf[pl.ds(r0, NB), S5_S:] = ni
        return nr, ni

    hr, hi = lax.fori_loop(0, S5_TL, step, (st[0], st[1]), unroll=2)
    st[0] = hr
    st[1] = hi
    y = jnp.dot(hbuf[...].astype(BF16), cmat_ref[...], preferred_element_type=F32)
    if direction == 0:
        o_ref[...] = y
    else:
        v = jax.nn.gelu(y + yf_ref[...] + dsk_ref[...] * u, approximate=True)
        gv = jnp.dot(v.astype(BF16), gw_ref[...], preferred_element_type=F32) + gb_ref[...]
        o_ref[...] = gv[:, :S5_W] * jax.nn.sigmoid(gv[:, S5_W:])


def _s5_pass(l, direction, u, n_ctx_rows, bmat, ar, ai, cmat, yf, dsk, gw, gb):
    rows = u.shape[0]
    tr = S5_TL * NB
    nblk = rows // tr
    nb_ctx = n_ctx_rows // tr
    blk = functools.partial(_chunk_of, direction, nc_ctx=nb_ctx, nc=nblk)

    def dir_spec(shape):
        nd = len(shape)
        return pl.BlockSpec((None, None) + tuple(shape), lambda i: (l, direction) + (0,) * nd)

    in_specs = [
        pl.BlockSpec((tr, S5_W), lambda i: (blk(i), 0)),
        dir_spec((S5_W, 2 * S5_S)),
        dir_spec((1, S5_S)),
        dir_spec((1, S5_S)),
        _layer_spec(l, (2 * S5_S, S5_W)),
    ]
    args = [u, bmat, ar, ai, cmat]
    if direction == 1:
        in_specs += [
            pl.BlockSpec((tr, S5_W), lambda i: (blk(i), 0)),
            _layer_spec(l, (1, S5_W)),
            _layer_spec(l, (S5_W, 2 * S5_W)),
            _layer_spec(l, (1, 2 * S5_W)),
        ]
        args += [yf, dsk, gw, gb]
    return pl.pallas_call(
        functools.partial(_s5_kernel, direction),
        out_shape=jax.ShapeDtypeStruct((rows, S5_W), F32),
        grid=(nblk,),
        in_specs=in_specs,
        out_specs=pl.BlockSpec((tr, S5_W), lambda i: (blk(i), 0)),
        scratch_shapes=[pltpu.VMEM((tr, 2 * S5_S), F32), pltpu.VMEM((2, NB, S5_S), F32)],
        compiler_params=_cparams(("arbitrary",)),
        name="s5_fwd" if direction == 0 else "s5_bwd",
    )(*args)


def _s5_operators(lam_re, lam_im, log_step, b_re, b_im, c_re, c_im):
    depth = lam_re.shape[0]
    eye = jnp.eye(S5_G, dtype=F32)
    step = jnp.exp(log_step)[..., None]
    mag = jnp.exp(lam_re * step)
    ar, ai = mag * jnp.cos(lam_im * step), mag * jnp.sin(lam_im * step)
    inv_den = 1.0 / (lam_re * lam_re + lam_im * lam_im)
    cr = ((ar - 1.0) * lam_re + ai * lam_im) * inv_den
    ci = (ai * lam_re - (ar - 1.0) * lam_im) * inv_den
    bre, bim = b_re[:, None], b_im[:, None]
    bbr = cr[..., None] * bre - ci[..., None] * bim
    bbi = cr[..., None] * bim + ci[..., None] * bre
    bd_r = jnp.einsum("ldgph,gk->ldghkp", bbr, eye).reshape(depth, 2, S5_W, S5_S)
    bd_i = jnp.einsum("ldgph,gk->ldghkp", bbi, eye).reshape(depth, 2, S5_W, S5_S)
    bmat = jnp.concatenate([bd_r, bd_i], axis=-1).astype(BF16)
    cre = jnp.einsum("lghp,gk->lgpkh", c_re, eye).reshape(depth, S5_S, S5_W)
    cim = jnp.einsum("lghp,gk->lgpkh", c_im, eye).reshape(depth, S5_S, S5_W)
    cmat = jnp.concatenate([cre, -cim], axis=1).astype(BF16)
    return bmat, ar.reshape(depth, 2, 1, S5_S), ai.reshape(depth, 2, 1, S5_S), cmat


def _route_rows(lg):
    m = jnp.max(lg, axis=0, keepdims=True)
    ex = jnp.exp(lg - m)
    probs = ex / jnp.sum(ex, axis=0, keepdims=True)
    p = [probs[e:e + 1, :] for e in range(NE)]
    tops = []
    for q in range(NGRP):
        v = p[q * EPG:(q + 1) * EPG]
        m1 = functools.reduce(jnp.maximum, v)
        i1 = jnp.full_like(m1, float(EPG - 1))
        for j in range(EPG - 2, -1, -1):
            i1 = jnp.where(v[j] == m1, float(j), i1)
        rest = [jnp.where(i1 == float(j), -jnp.inf, v[j]) for j in range(EPG)]
        m2 = functools.reduce(jnp.maximum, rest)
        i2 = jnp.full_like(m2, float(EPG - 1))
        for j in range(EPG - 2, -1, -1):
            i2 = jnp.where(jnp.logical_and(rest[j] == m2, i1 != float(j)), float(j), i2)
        tops.append((m1, i1, m2, i2))
    score = [t[0] + t[2] for t in tops]
    best = functools.reduce(jnp.maximum, score)
    sel = tops[NGRP - 1] + (jnp.full_like(best, float(NGRP - 1)),)
    for q in range(NGRP - 2, -1, -1):
        hit = score[q] == best
        sel = tuple(jnp.where(hit, a, b) for a, b in zip(tops[q] + (jnp.full_like(best, float(q)),), sel))
    m1, i1, m2, i2, grp = sel
    den = m1 + m2
    return grp * EPG + i1, grp * EPG + i2, m1 / den, m2 / den


def _out_kernel(a_ref, b_ref, s_ref, x_ref, mod_ref, g_ref, w_ref, rw_ref, rb_ref,
                x1_ref, h2_ref, rt_ref, gc_ref, cnt_ref):
    tm = x_ref.shape[0]
    mix = jnp.dot(a_ref[...].astype(BF16), w_ref[0:SSD_W, :], preferred_element_type=F32)
    mix = mix + jnp.dot(b_ref[...].astype(BF16), w_ref[SSD_W:SSD_W + CONF_W, :], preferred_element_type=F32)
    mix = mix + jnp.dot(s_ref[...].astype(BF16), w_ref[SSD_W + CONF_W:, :], preferred_element_type=F32)
    x1 = (x_ref[...].reshape(tm // NB, NB, D) + mod_ref[2][None] * mix.reshape(tm // NB, NB, D)).reshape(tm, D)
    x1_ref[...] = x1
    ms = jnp.mean(x1 * x1, axis=-1, keepdims=True)
    xn = x1 * lax.rsqrt(ms + EPS) * g_ref[...]
    h2 = (xn.reshape(tm // NB, NB, D) * (1.0 + mod_ref[4])[None] + mod_ref[3][None]).reshape(tm, D)
    h2_ref[...] = h2
    lg = lax.dot_general(rw_ref[...], h2, (((1,), (1,)), ((), ())), precision=HIGHEST,
                         preferred_element_type=F32) + rb_ref[...]
    e1, e2, g1, g2 = _route_rows(lg)
    eid = lax.broadcasted_iota(I32, (NE, tm), 0).astype(F32)
    oh0 = (eid == e1).astype(F32)
    oh1 = (eid == e2).astype(F32)
    t0 = lax.broadcasted_iota(I32, (tm, tm), 0)
    t1 = lax.broadcasted_iota(I32, (tm, tm), 1)
    before = (t0 < t1).astype(BF16)
    pre0 = jnp.dot(oh0.astype(BF16), before, preferred_element_type=F32)
    pre1 = jnp.dot(oh1.astype(BF16), before, preferred_element_type=F32)
    lr0 = jnp.sum(oh0 * pre0, axis=0, keepdims=True)
    lr1 = jnp.sum(oh1 * pre1, axis=0, keepdims=True)
    rt_ref[...] = jnp.concatenate([e1, e2, lr0, lr1, jnp.zeros((NB - 4, tm), F32)], axis=0)
    gates = jnp.concatenate([g1, g2, jnp.zeros((LANE - 2, tm), F32)], axis=0)
    gc_ref[...] = gates.T
    cnt = jnp.concatenate([jnp.sum(oh0, axis=1, keepdims=True), jnp.sum(oh1, axis=1, keepdims=True)], axis=0)
    cnt_ref[...] = jnp.broadcast_to(cnt, (2 * NE, LANE))


def _out_proj(l, a, bconf, s, x, modtabs, g_ffn, w_out, rw_t, rb, n_ctx_rows, with_ctx):
    tm = ROW_TM
    ctx_blk = n_ctx_rows // tm
    tot_blk = a.shape[0] // tm
    first = 0 if with_ctx else ctx_blk
    nblk = tot_blk - first
    rows = nblk * tm
    x_off = x.shape[0] // tm - nblk
    b_off = bconf.shape[0] // tm - nblk
    return pl.pallas_call(
        _out_kernel,
        out_shape=[
            jax.ShapeDtypeStruct((rows, D), F32),
            jax.ShapeDtypeStruct((rows, D), F32),
            jax.ShapeDtypeStruct((NB, rows), F32),
            jax.ShapeDtypeStruct((rows, LANE), F32),
            jax.ShapeDtypeStruct((nblk, 2 * NE, LANE), F32),
        ],
        grid=(nblk,),
        in_specs=[
            pl.BlockSpec((tm, SSD_W), lambda i: (i + first, 0)),
            pl.BlockSpec((tm, CONF_W), lambda i: (i + b_off, 0)),
            pl.BlockSpec((tm, S5_W), lambda i: (i + first, 0)),
            pl.BlockSpec((tm, D), lambda i: (i + x_off, 0)),
            _mod_spec(l, ctx_blk, first),
            _layer_spec(l, (1, D)),
            _layer_spec(l, (D, D)),
            _const_spec((NE, D)),
            _const_spec((NE, 1)),
        ],
        out_specs=[
            pl.BlockSpec((tm, D), lambda i: (i, 0)),
            pl.BlockSpec((tm, D), lambda i: (i, 0)),
            pl.BlockSpec((NB, tm), lambda i: (0, i)),
            pl.BlockSpec((tm, LANE), lambda i: (i, 0)),
            pl.BlockSpec((None, 2 * NE, LANE), lambda i: (i, 0, 0)),
        ],
        compiler_params=_cparams(("arbitrary",)),
        name="out_proj_route",
    )(a, bconf, s, x, modtabs, g_ffn, w_out, rw_t, rb)


def _moe_kernel(it_ref, ie_ref, lo_ref, hi_ref, h_ref, wg_ref, wu_ref, wd_ref, o_ref, wgb, wub, wdb, cur):
    w = pl.program_id(0)
    lo = lo_ref[w]
    hi = hi_ref[w]
    new_tile = jnp.logical_or(w == 0, it_ref[w] != it_ref[jnp.maximum(w - 1, 0)])

    @pl.when(w == 0)
    def _():
        cur[0] = -1

    @pl.when(new_tile)
    def _():
        o_ref[...] = jnp.zeros_like(o_ref)

    @pl.when(hi > lo)
    def _():
        @pl.when(cur[0] != ie_ref[w])
        def _():
            wgb[...] = wg_ref[...].astype(BF16)
            wub[...] = wu_ref[...].astype(BF16)
            wdb[...] = wd_ref[...].astype(BF16)
            cur[0] = ie_ref[w]

        h = h_ref[...].astype(BF16)
        gate = jnp.dot(h, wgb[...], preferred_element_type=F32)
        up = jnp.dot(h, wub[...], preferred_element_type=F32)
        act = (_silu(gate) * up).astype(BF16)
        y = jnp.dot(act, wdb[...], preferred_element_type=F32)
        r = lax.broadcasted_iota(I32, (MOE_TM, 1), 0)
        keep = jnp.logical_and(r >= lo, r < hi)
        o_ref[...] = jnp.where(keep, y, o_ref[...])


def _moe_experts(l, hs, item_tile, item_expert, item_lo, item_hi, w_gate, w_up, w_down):
    n_rows = hs.shape[0]
    n_items = item_tile.shape[0]

    def wspec():
        return pl.BlockSpec((None, None, D, D), lambda w, it, ie, lo, hi: (l, ie[w], 0, 0))

    return pl.pallas_call(
        _moe_kernel,
        out_shape=jax.ShapeDtypeStruct((n_rows, D), F32),
        grid_spec=pltpu.PrefetchScalarGridSpec(
            num_scalar_prefetch=4,
            grid=(n_items,),
            in_specs=[
                pl.BlockSpec((MOE_TM, D), lambda w, it, ie, lo, hi: (it[w], 0)),
                wspec(), wspec(), wspec(),
            ],
            out_specs=pl.BlockSpec((MOE_TM, D), lambda w, it, ie, lo, hi: (it[w], 0)),
            scratch_shapes=[pltpu.VMEM((D, D), BF16)] * 3 + [pltpu.SMEM((1,), I32)],
        ),
        compiler_params=_cparams(("arbitrary",)),
        name="moe_experts",
    )(item_tile, item_expert, item_lo, item_hi, hs, w_gate, w_up, w_down)


def _moe_plan(route, cnt):
    rows = route.shape[1]
    nblk = cnt.shape[0]
    n_flat = 2 * rows
    e = route[0:2].astype(I32)
    lrank = route[2:4].astype(I32)
    counts = cnt[:, :, 0].astype(I32).reshape(nblk, 2, NE)
    per = jnp.transpose(counts, (1, 0, 2)).reshape(2 * nblk, NE)
    before = jnp.cumsum(per, axis=0) - per
    gcount = jnp.sum(per, axis=0)
    gend = jnp.cumsum(gcount)
    gstart = gend - gcount
    base = jnp.transpose((before + gstart[None, :]).reshape(2, nblk, NE), (2, 0, 1))[..., None]
    onehot = e.reshape(1, 2, nblk, ROW_TM) == jnp.arange(NE, dtype=I32).reshape(NE, 1, 1, 1)
    inv = jnp.sum(jnp.where(onehot, base, 0), axis=0).reshape(2, rows) + lrank
    keys = (e * 65536).reshape(-1) + jnp.arange(n_flat, dtype=I32)
    order = lax.sort(keys, is_stable=False) & 0xFFFF
    tok = jnp.where(order >= rows, order - rows, order)
    n_tiles = n_flat // MOE_TM
    tile_start = jnp.arange(n_tiles, dtype=I32) * MOE_TM
    bnd = lax.sort(jnp.concatenate([tile_start, gstart[1:]]), is_stable=False)
    bnd_hi = jnp.concatenate([bnd[1:], jnp.full((1,), n_flat, I32)])
    item_tile = jnp.minimum(bnd // MOE_TM, n_tiles - 1)
    item_expert = jnp.minimum(jnp.sum((gend[None, :] <= bnd[:, None]).astype(I32), axis=1), NE - 1)
    item_lo = bnd - item_tile * MOE_TM
    item_hi = bnd_hi - item_tile * MOE_TM
    return tok, inv, item_tile, item_expert, item_lo, item_hi


def _comb_kernel(final, x_ref, ya_ref, yb_ref, gc_ref, mod_ref, g_ref, o_ref, *scratch):
    tm = x_ref.shape[0]
    gc = gc_ref[...]
    ff = (gc[:, 0:1] * ya_ref[...] + gc[:, 1:2] * yb_ref[...]).reshape(tm // NB, NB, D)
    x2 = (x_ref[...].reshape(tm // NB, NB, D) + mod_ref[5][None] * ff).reshape(tm, D)
    if not final:
        o_ref[...] = x2
        return
    scr, = scratch
    ms = jnp.mean(x2 * x2, axis=-1, keepdims=True)
    x2 = x2 * lax.rsqrt(ms + EPS) * g_ref[...]
    ncb = D // LANE
    for k in range(ncb):
        scr[k] = x2[:, k * LANE:(k + 1) * LANE]
    for b in range(NB):
        for k in range(ncb):
            o_ref[b, :, k * LANE:(k + 1) * LANE] = scr[k, pl.ds(b, ROW_TQ, stride=NB), :]


def _combine(l, x1, ya, yb, gcol, modtabs, g_final, n_ctx_rows, with_ctx, final):
    tm = ROW_TM
    rows = x1.shape[0]
    ctx_blk = n_ctx_rows // tm if with_ctx else 0
    if final:
        out_shape = jax.ShapeDtypeStruct((NB, rows // NB, D), F32)
        out_spec = pl.BlockSpec((NB, ROW_TQ, D), lambda i: (0, i, 0))
        scratch = [pltpu.VMEM((D // LANE, tm, LANE), F32)]
    else:
        out_shape = jax.ShapeDtypeStruct((rows, D), F32)
        out_spec = pl.BlockSpec((tm, D), lambda i: (i, 0))
        scratch = []
    return pl.pallas_call(
        functools.partial(_comb_kernel, final),
        out_shape=out_shape,
        grid=(rows // tm,),
        in_specs=[
            pl.BlockSpec((tm, D), lambda i: (i, 0)),
            pl.BlockSpec((tm, D), lambda i: (i, 0)),
            pl.BlockSpec((tm, D), lambda i: (i, 0)),
            pl.BlockSpec((tm, LANE), lambda i: (i, 0)),
            _mod_spec(l, ctx_blk),
            _const_spec((1, D)),
        ],
        out_specs=out_spec,
        scratch_shapes=scratch,
        compiler_params=_cparams(("arbitrary",)),
        name="moe_combine",
    )(x1, ya, yb, gcol, modtabs, g_final)


def _grid_pos_embed(rows_n):
    rr, cc = jnp.meshgrid(jnp.arange(rows_n, dtype=F32), jnp.arange(GRID_W, dtype=F32), indexing="ij")
    quarter = D // 4
    inv_freq = jnp.exp(-math.log(10000.0) * jnp.arange(quarter, dtype=F32) / quarter)

    def emb(pos):
        ang = pos.reshape(-1)[:, None] * inv_freq[None, :]
        return jnp.concatenate([jnp.sin(ang), jnp.cos(ang)], axis=-1)

    return jnp.concatenate([emb(rr), emb(cc)], axis=-1)


def _take_rows(a, idx):
    return a.at[idx].get(mode="promise_in_bounds")


def _pad_last(v, width):
    return jnp.pad(v, [(0, 0)] * (v.ndim - 1) + [(0, width - v.shape[-1])])


def kernel(x, c, ctx, c_ctx, w_ada, b_ada, g_mix, w_in, ssd_conv_w, ssd_conv_b, ssd_dt_bias, ssd_a_log, ssd_d, ssd_norm_g, conf_dw_w, conf_dw_b, conf_ln_g, conf_ln_b, conf_pw_w, conf_pw_b, s5_lambda_re, s5_lambda_im, s5_log_step, s5_b_re, s5_b_im, s5_c_re, s5_c_im, s5_d, s5_glu_w, s5_glu_b, w_out, g_ffn, router_w, router_b, exp_w_gate, exp_w_up, exp_w_down, g_final):
    bsz, seq, _ = x.shape
    ctx_len = ctx.shape[1]
    depth = w_ada.shape[0]
    assert bsz == NB and seq % SSD_CHUNK == 0 and ctx_len % SSD_CHUNK == 0
    n_ctx_rows = ctx_len * NB
    rows_all = (ctx_len + seq) * NB

    cond = jnp.concatenate([c, c_ctx[None, :], jnp.zeros((2 * NB - bsz - 1, D), F32)], axis=0)
    mod = _modulation(cond, w_ada, b_ada)
    mod = mod.reshape(depth, 2 * NB, N_MOD, D)
    mod_lat = jnp.transpose(mod[:, :NB], (0, 2, 1, 3))
    mod_ctx = jnp.broadcast_to(mod[:, NB][:, :, None, :], (depth, N_MOD, NB, D))
    modtabs = jnp.stack([mod_ctx, mod_lat], axis=1)

    o_b = SSD_W + SSD_XBC + 2 * SSD_H
    w_in_p = jnp.concatenate(
        [w_in[:, :, :o_b], jnp.zeros((depth, D, DT_PAD - 2 * SSD_H), F32), w_in[:, :, o_b:]], axis=2).astype(BF16)
    g_mix3 = g_mix.reshape(depth, 1, D)
    ssd_cb = ssd_conv_b.reshape(depth, 1, SSD_XBC)
    ssd_dtb = _pad_last(ssd_dt_bias.reshape(depth, 1, 2 * SSD_H), DT_PAD)
    ssd_alog = _pad_last(ssd_a_log.reshape(depth, 1, 2 * SSD_H), DT_PAD)
    ssd_dsk = jnp.repeat(ssd_d, SSD_P, axis=1).reshape(depth, 1, SSD_W)
    ssd_ng = ssd_norm_g.reshape(depth, 1, SSD_W)
    conf_b = conf_dw_b.reshape(depth, 1, CONF_W)
    conf_lg = conf_ln_g.reshape(depth, 1, CONF_W)
    conf_lb = conf_ln_b.reshape(depth, 1, CONF_W)
    conf_pw = conf_pw_w.astype(BF16)
    conf_pb = conf_pw_b.reshape(depth, 1, CONF_W)
    s5_bmat, s5_ar, s5_ai, s5_cmat = _s5_operators(s5_lambda_re, s5_lambda_im, s5_log_step,
                                                   s5_b_re, s5_b_im, s5_c_re, s5_c_im)
    s5_dsk = s5_d.reshape(depth, 1, S5_W)
    s5_gw = s5_glu_w.astype(BF16)
    s5_gb = s5_glu_b.reshape(depth, 1, 2 * S5_W)
    w_out_b = w_out.astype(BF16)
    g_ffn3 = g_ffn.reshape(depth, 1, D)
    router_w_t = router_w.T
    router_b2 = router_b.reshape(NE, 1)
    g_final2 = g_final.reshape(1, D)
    pos = _grid_pos_embed(seq // GRID_W)

    xall = None
    for l in range(depth):
        last = l == depth - 1
        if l == 0:
            ssd_cols, conf_cols, s5_u, xall = _in_proj(l, True, (ctx, x), pos, modtabs, g_mix3, w_in_p,
                                                       rows_all, n_ctx_rows)
        else:
            ssd_cols, conf_cols, s5_u = _in_proj(l, False, xall, None, modtabs, g_mix3, w_in_p,
                                                 rows_all, n_ctx_rows)
        ypart = _ssd_pass(l, 0, ssd_cols, n_ctx_rows, ssd_conv_w, ssd_cb, ssd_dtb, ssd_alog, ssd_dsk, None)
        a_mix = _ssd_pass(l, 1, ssd_cols, n_ctx_rows, ssd_conv_w, ssd_cb, ssd_dtb, ssd_alog, ssd_ng, ypart)
        b_mix = _conformer(l, conf_cols, n_ctx_rows, not last, conf_dw_w, conf_b, conf_lg, conf_lb, conf_pw, conf_pb)
        yf = _s5_pass(l, 0, s5_u, n_ctx_rows, s5_bmat, s5_ar, s5_ai, s5_cmat, None, None, None, None)
        s_mix = _s5_pass(l, 1, s5_u, n_ctx_rows, s5_bmat, s5_ar, s5_ai, s5_cmat, yf, s5_dsk, s5_gw, s5_gb)
        x1, h2p, route, gcol, cnt = _out_proj(l, a_mix, b_mix, s_mix, xall, modtabs, g_ffn3, w_out_b,
                                              router_w_t, router_b2, n_ctx_rows, not last)
        tok, inv, item_tile, item_expert, item_lo, item_hi = _moe_plan(route, cnt)
        hs = _take_rows(h2p, tok)
        y = _moe_experts(l, hs, item_tile, item_expert, item_lo, item_hi, exp_w_gate, exp_w_up, exp_w_down)
        ya = _take_rows(y, inv[0])
        yb = _take_rows(y, inv[1])
        xall = _combine(l, x1, ya, yb, gcol, modtabs, g_final2, n_ctx_rows, not last, last)
    return xall
```

```python
import functools
import math

import jax
import jax.numpy as jnp
from jax import lax
from jax.experimental import pallas as pl
from jax.experimental.pallas import tpu as pltpu

F32 = jnp.float32
BF16 = jnp.bfloat16
I32 = jnp.int32
HIGHEST = lax.Precision.HIGHEST

NB = 8
D = 1024
GRID_W = 64
N_MOD = 6
EPS = 1e-6
LANE = 128
SSD_W = 512
SSD_P = 64
SSD_H = 8
SSD_G = 2
SSD_N = 128
SSD_K = 5
SSD_XBC = SSD_W + 2 * SSD_G * SSD_N
SSD_CHUNK = 128
DT_PAD = LANE
SSD_COLS_P = SSD_W + SSD_XBC + DT_PAD
CONF_W = 256
CONF_K = 31
S5_W = 256
S5_G = 16
S5_P = 64
S5_CH = 16
S5_S = S5_G * S5_P
S5_TL = 128
NE = 16
NGRP = 4
EPG = 4
MOE_TM = 256
IN_COLS_P = SSD_COLS_P + 2 * CONF_W + S5_W
ROW_TM = 512
ROW_TQ = ROW_TM // NB
CH_ROWS = SSD_CHUNK * NB
VMEM_LIMIT = 56 * 1024 * 1024


def _cparams(sem):
    return pltpu.CompilerParams(dimension_semantics=sem, vmem_limit_bytes=VMEM_LIMIT)


def _const_spec(shape):
    nd = len(shape)
    return pl.BlockSpec(shape, lambda *_: (0,) * nd)


def _layer_spec(l, shape):
    nd = len(shape)
    return pl.BlockSpec((None,) + tuple(shape), lambda *_: (l,) + (0,) * nd)


def _silu(v):
    return v * jax.nn.sigmoid(v)


def _mod_kernel(c_ref, w_ref, b_ref, o_ref):
    c = c_ref[...]
    h = _silu(c).astype(BF16)
    o_ref[...] = jnp.dot(h, w_ref[...].astype(BF16), preferred_element_type=F32) + b_ref[...]


def _modulation(cond, w_ada, b_ada):
    depth = w_ada.shape[0]
    nrow = cond.shape[0]
    return pl.pallas_call(
        _mod_kernel,
        out_shape=jax.ShapeDtypeStruct((depth, nrow, N_MOD * D), F32),
        grid=(depth, N_MOD),
        in_specs=[
            pl.BlockSpec((nrow, D), lambda l, j: (0, 0)),
            pl.BlockSpec((None, D, D), lambda l, j: (l, 0, j)),
            pl.BlockSpec((None, 1, D), lambda l, j: (l, 0, j)),
        ],
        out_specs=pl.BlockSpec((None, nrow, D), lambda l, j: (l, 0, j)),
        compiler_params=_cparams(("arbitrary", "arbitrary")),
        name="adaln_mod",
    )(cond, w_ada, b_ada.reshape(depth, 1, N_MOD * D))


def _mod_spec(l, ctx_blk, first=0):
    return pl.BlockSpec((None, None, N_MOD, NB, D),
                        lambda i, *_: (l, jnp.where(i + first < ctx_blk, 0, 1), 0, 0, 0))


def _norm_mod_project(x, mod_ref, g_ref, w_ref, ssd_ref, conf_ref, s5_ref):
    tm = x.shape[0]
    ms = jnp.mean(x * x, axis=-1, keepdims=True)
    xn = x * lax.rsqrt(ms + EPS) * g_ref[...]
    h = xn.reshape(tm // NB, NB, D) * (1.0 + mod_ref[1])[None] + mod_ref[0][None]
    h = h.reshape(tm, D).astype(BF16)
    ssd_ref[...] = jnp.dot(h, w_ref[:, :SSD_COLS_P], preferred_element_type=F32)
    conf_ref[...] = jnp.dot(h, w_ref[:, SSD_COLS_P:SSD_COLS_P + 2 * CONF_W], preferred_element_type=F32)
    s5_ref[...] = jnp.dot(h, w_ref[:, SSD_COLS_P + 2 * CONF_W:], preferred_element_type=F32)


def _in_first_kernel(ctx_blk, ctx_ref, x_ref, pos_ref, mod_ref, g_ref, w_ref,
                     ssd_ref, conf_ref, s5_ref, x0_ref, scr):
    i = pl.program_id(0)
    ncb = D // LANE

    @pl.when(i < ctx_blk)
    def _():
        for b in range(NB):
            for k in range(ncb):
                scr[k, pl.ds(b, ROW_TQ, stride=NB), :] = ctx_ref[b, :, k * LANE:(k + 1) * LANE]

    @pl.when(i >= ctx_blk)
    def _():
        for b in range(NB):
            for k in range(ncb):
                scr[k, pl.ds(b, ROW_TQ, stride=NB), :] = (
                    x_ref[b, :, k * LANE:(k + 1) * LANE] + pos_ref[:, k * LANE:(k + 1) * LANE])

    x = jnp.concatenate([scr[k] for k in range(ncb)], axis=1)
    x0_ref[...] = x
    _norm_mod_project(x, mod_ref, g_ref, w_ref, ssd_ref, conf_ref, s5_ref)


def _in_kernel(x_ref, mod_ref, g_ref, w_ref, ssd_ref, conf_ref, s5_ref):
    _norm_mod_project(x_ref[...], mod_ref, g_ref, w_ref, ssd_ref, conf_ref, s5_ref)


def _in_proj(l, first, x_or_pair, pos, modtabs, g_mix, w_in_p, rows, n_ctx_rows):
    tm = ROW_TM
    nblk = rows // tm
    ctx_blk = n_ctx_rows // tm
    common_specs = [_mod_spec(l, ctx_blk), _layer_spec(l, (1, D)), _layer_spec(l, (D, IN_COLS_P))]
    common_args = [modtabs, g_mix, w_in_p]
    out_shape = [
        jax.ShapeDtypeStruct((rows, SSD_COLS_P), F32),
        jax.ShapeDtypeStruct((rows, 2 * CONF_W), F32),
        jax.ShapeDtypeStruct((rows, S5_W), F32),
    ]
    out_specs = [
        pl.BlockSpec((tm, SSD_COLS_P), lambda i: (i, 0)),
        pl.BlockSpec((tm, 2 * CONF_W), lambda i: (i, 0)),
        pl.BlockSpec((tm, S5_W), lambda i: (i, 0)),
    ]
    if first:
        ctx, x = x_or_pair
        body = functools.partial(_in_first_kernel, ctx_blk)
        in_specs = [
            pl.BlockSpec((NB, ROW_TQ, D), lambda i: (0, jnp.minimum(i, ctx_blk - 1), 0)),
            pl.BlockSpec((NB, ROW_TQ, D), lambda i: (0, jnp.maximum(i - ctx_blk, 0), 0)),
            pl.BlockSpec((ROW_TQ, D), lambda i: (jnp.maximum(i - ctx_blk, 0), 0)),
        ] + common_specs
        args = [ctx, x, pos] + common_args
        out_shape.append(jax.ShapeDtypeStruct((rows, D), F32))
        out_specs.append(pl.BlockSpec((tm, D), lambda i: (i, 0)))
        scratch = [pltpu.VMEM((D // LANE, tm, LANE), F32)]
    else:
        body = _in_kernel
        in_specs = [pl.BlockSpec((tm, D), lambda i: (i, 0))] + common_specs
        args = [x_or_pair] + common_args
        scratch = []
    return pl.pallas_call(
        body,
        out_shape=out_shape,
        grid=(nblk,),
        in_specs=in_specs,
        out_specs=out_specs,
        scratch_shapes=scratch,
        compiler_params=_cparams(("arbitrary",)),
        name="in_proj",
    )(*args)


def _chunk_of(direction, i, nc_ctx, nc):
    if direction == 0:
        return i
    return jnp.where(i < nc_ctx, nc_ctx - 1 - i, nc - 1 - (i - nc_ctx))


def _has_prev(c, nc_ctx):
    return jnp.logical_and(c != 0, c != nc_ctx)


def _has_next(c, nc_ctx, nc):
    return jnp.logical_and(c != nc_ctx - 1, c != nc - 1)


SSD_HALO = 2 * NB


def _ssd_kernel(direction, nc_ctx, nc, *refs):
    if direction == 0:
        (main_ref, prev_ref, next_ref, cw_ref, cb_ref, dtb_ref, alog_ref, dsk_ref,
         o_ref, ext, act, dts, yout, state) = refs
    else:
        (main_ref, prev_ref, next_ref, cw_ref, cb_ref, dtb_ref, alog_ref, ng_ref, yp_ref,
         o_ref, ext, act, dts, yout, state) = refs
    i = pl.program_id(0)
    c = _chunk_of(direction, i, nc_ctx, nc)

    @pl.when(i == 0)
    def _():
        state[...] = jnp.zeros_like(state)

    has_prev = _has_prev(c, nc_ctx)
    has_next = _has_next(c, nc_ctx, nc)
    nblk = SSD_XBC // LANE
    for k in range(nblk):
        lo = SSD_W + k * LANE
        ext[k, SSD_HALO:SSD_HALO + CH_ROWS, :] = main_ref[:, lo:lo + LANE]
        ext[k, 0:SSD_HALO, :] = jnp.where(has_prev, prev_ref[:, lo:lo + LANE], 0.0)
        ext[k, SSD_HALO + CH_ROWS:, :] = jnp.where(has_next, next_ref[:, lo:lo + LANE], 0.0)

    def conv_rb(rb, carry):
        r0 = pl.multiple_of(rb * LANE, LANE)
        for k in range(nblk):
            lo = k * LANE
            acc = jnp.broadcast_to(cb_ref[:, lo:lo + LANE], (LANE, LANE))
            for tap in range(SSD_K):
                acc = acc + ext[k, pl.ds(r0 + NB * tap, LANE), :] * cw_ref[tap:tap + 1, lo:lo + LANE]
            act[k, pl.ds(r0, LANE), :] = _silu(acc)
        return carry

    lax.fori_loop(0, CH_ROWS // LANE, conv_rb, 0)

    raw = main_ref[:, SSD_W + SSD_XBC:] + dtb_ref[...]
    dts[...] = jnp.maximum(raw, 0.0) + jnp.log1p(jnp.exp(-jnp.abs(raw)))
    a_row = -jnp.exp(alog_ref[...])

    tt = lax.broadcasted_iota(I32, (SSD_CHUNK, SSD_CHUNK), 0)
    ss = lax.broadcasted_iota(I32, (SSD_CHUNK, SSD_CHUNK), 1)
    mask = (ss <= tt) if direction == 0 else (ss >= tt)
    tmat = mask.astype(F32)
    last = SSD_CHUNK - 1 if direction == 0 else 0
    hpg = SSD_H // SSD_G
    lane_id = lax.broadcasted_iota(I32, (1, LANE), 1)
    half_lo = lane_id < SSD_P
    half_hi = lane_id >= SSD_P

    def per_batch(b, carry):
        sl = pl.ds(b, SSD_CHUNK, stride=NB)
        dt_b = dts[sl, :]
        cs = jnp.dot(tmat, dt_b * a_row, precision=HIGHEST, preferred_element_type=F32)
        cs_t = cs.T
        dt_t = dt_b.T
        for g in range(SSD_G):
            bg_t = act[SSD_W // LANE + g, sl, :].T
            cg = act[SSD_W // LANE + SSD_G + g, sl, :]
            gmat = jnp.dot(cg.astype(BF16), bg_t.astype(BF16), preferred_element_type=F32)
            for pr in range(hpg // 2):
                k = g * (hpg // 2) + pr
                xs_pair = act[k, sl, :]
                s_pair = state[b, k]
                acc_y = jnp.zeros((SSD_CHUNK, LANE), F32)
                acc_s = jnp.zeros((SSD_N, LANE), F32)
                etot = []
                for j in range(2):
                    ln = direction * SSD_H + 2 * k + j
                    own = half_lo if j == 0 else half_hi
                    colb = jnp.broadcast_to(cs[:, ln:ln + 1], (SSD_CHUNK, SSD_CHUNK))
                    row = cs_t[ln:ln + 1, :]
                    dtr = dt_t[ln:ln + 1, :]
                    decay = jnp.where(mask, jnp.exp(jnp.where(mask, colb - row, 0.0)) * dtr, 0.0)
                    lhs = jnp.concatenate([(gmat * decay).astype(BF16),
                                           (cg * jnp.exp(colb)).astype(BF16)], axis=1)
                    xs_own = jnp.where(own, xs_pair, 0.0).astype(BF16)
                    rhs = jnp.concatenate([xs_own, jnp.where(own, s_pair, 0.0).astype(BF16)], axis=0)
                    acc_y = acc_y + jnp.dot(lhs, rhs, preferred_element_type=F32)
                    tot = cs[last:last + 1, ln:ln + 1]
                    wrow = jnp.exp(tot - row) * dtr
                    acc_s = acc_s + jnp.dot((bg_t * wrow).astype(BF16), xs_own, preferred_element_type=F32)
                    etot.append(jnp.exp(tot))
                state[b, k] = s_pair * jnp.where(half_lo, etot[0], etot[1]) + acc_s
                if direction == 0:
                    acc_y = acc_y + dsk_ref[:, k * LANE:(k + 1) * LANE] * xs_pair
                yout[k, sl, :] = acc_y
        return carry

    lax.fori_loop(0, NB, per_batch, 0, unroll=2)

    nyb = SSD_W // LANE
    if direction == 0:
        for k in range(nyb):
            o_ref[:, k * LANE:(k + 1) * LANE] = yout[k]
    else:
        def fin(rb, carry):
            r0 = pl.multiple_of(rb * LANE, LANE)
            y = jnp.concatenate([yout[k, pl.ds(r0, LANE), :] for k in range(nyb)], axis=1)
            y = y + yp_ref[pl.ds(r0, LANE), :]
            y = y * _silu(main_ref[pl.ds(r0, LANE), 0:SSD_W])
            ms = jnp.mean(y * y, axis=-1, keepdims=True)
            o_ref[pl.ds(r0, LANE), :] = y * lax.rsqrt(ms + EPS) * ng_ref[...]
            return carry

        lax.fori_loop(0, CH_ROWS // LANE, fin, 0)


def _ssd_pass(l, direction, cols, n_ctx_rows, cw, cb, dtb, alog, extra, ypart):
    rows = cols.shape[0]
    nc = rows // CH_ROWS
    nc_ctx = n_ctx_rows // CH_ROWS
    hpc = CH_ROWS // SSD_HALO
    n_halo = rows // SSD_HALO
    chunk = functools.partial(_chunk_of, direction, nc_ctx=nc_ctx, nc=nc)
    in_specs = [
        pl.BlockSpec((CH_ROWS, SSD_COLS_P), lambda i: (chunk(i), 0)),
        pl.BlockSpec((SSD_HALO, SSD_COLS_P), lambda i: (jnp.maximum(chunk(i) * hpc - 1, 0), 0)),
        pl.BlockSpec((SSD_HALO, SSD_COLS_P), lambda i: (jnp.minimum((chunk(i) + 1) * hpc, n_halo - 1), 0)),
        _layer_spec(l, (SSD_K, SSD_XBC)),
        _layer_spec(l, (1, SSD_XBC)),
        _layer_spec(l, (1, DT_PAD)),
        _layer_spec(l, (1, DT_PAD)),
        _layer_spec(l, (1, SSD_W)),
    ]
    args = [cols, cols, cols, cw, cb, dtb, alog, extra]
    if direction == 1:
        in_specs.append(pl.BlockSpec((CH_ROWS, SSD_W), lambda i: (chunk(i), 0)))
        args.append(ypart)
    return pl.pallas_call(
        functools.partial(_ssd_kernel, direction, nc_ctx, nc),
        out_shape=jax.ShapeDtypeStruct((rows, SSD_W), F32),
        grid=(nc,),
        in_specs=in_specs,
        out_specs=pl.BlockSpec((CH_ROWS, SSD_W), lambda i: (chunk(i), 0)),
        scratch_shapes=[
            pltpu.VMEM((SSD_XBC // LANE, CH_ROWS + 2 * SSD_HALO, LANE), F32),
            pltpu.VMEM((SSD_XBC // LANE, CH_ROWS, LANE), F32),
            pltpu.VMEM((CH_ROWS, DT_PAD), F32),
            pltpu.VMEM((SSD_W // LANE, CH_ROWS, LANE), F32),
            pltpu.VMEM((NB, SSD_W // LANE, SSD_N, LANE), F32),
        ],
        compiler_params=_cparams(("arbitrary",)),
        name="ssd_fwd" if direction == 0 else "ssd_bwd",
    )(*args)


CONF_HALO = 128


def _conf_kernel(first_chunk, nc_ctx, nc, main_ref, prev_ref, next_ref, w_ref, b_ref, lng_ref, lnb_ref,
                 pw_ref, pwb_ref, o_ref, ext):
    c = pl.program_id(0) + first_chunk
    has_prev = _has_prev(c, nc_ctx)
    has_next = _has_next(c, nc_ctx, nc)

    def glu(ref):
        v = ref[...]
        return v[:, :CONF_W] * jax.nn.sigmoid(v[:, CONF_W:])

    um = glu(main_ref)
    up = jnp.where(has_prev, glu(prev_ref), 0.0)
    un = jnp.where(has_next, glu(next_ref), 0.0)
    nblk = CONF_W // LANE
    for k in range(nblk):
        ext[k, 0:CONF_HALO, :] = up[:, k * LANE:(k + 1) * LANE]
        ext[k, CONF_HALO:CONF_HALO + CH_ROWS, :] = um[:, k * LANE:(k + 1) * LANE]
        ext[k, CONF_HALO + CH_ROWS:, :] = un[:, k * LANE:(k + 1) * LANE]
    off = CONF_HALO - NB * (CONF_K // 2)

    def rbody(rb, carry):
        r0 = pl.multiple_of(rb * LANE, LANE)
        accs = []
        for k in range(nblk):
            lo = k * LANE
            acc = jnp.broadcast_to(b_ref[:, lo:lo + LANE], (LANE, LANE))
            for tap in range(CONF_K):
                acc = acc + ext[k, pl.ds(r0 + NB * tap + off, LANE), :] * w_ref[tap:tap + 1, lo:lo + LANE]
            accs.append(acc)
        u = jnp.concatenate(accs, axis=1)
        xc = u - jnp.mean(u, axis=-1, keepdims=True)
        y = xc * lax.rsqrt(jnp.mean(xc * xc, axis=-1, keepdims=True) + EPS) * lng_ref[...] + lnb_ref[...]
        y = _silu(y).astype(BF16)
        o_ref[pl.ds(r0, LANE), :] = jnp.dot(y, pw_ref[...], preferred_element_type=F32) + pwb_ref[...]
        return carry

    lax.fori_loop(0, CH_ROWS // LANE, rbody, 0)


def _conformer(l, cols, n_ctx_rows, with_ctx, w, b, lng, lnb, pw, pwb):
    rows = cols.shape[0]
    nc = rows // CH_ROWS
    nc_ctx = n_ctx_rows // CH_ROWS
    first = 0 if with_ctx else nc_ctx
    hpc = CH_ROWS // CONF_HALO
    n_halo = rows // CONF_HALO
    return pl.pallas_call(
        functools.partial(_conf_kernel, first, nc_ctx, nc),
        out_shape=jax.ShapeDtypeStruct(((nc - first) * CH_ROWS, CONF_W), F32),
        grid=(nc - first,),
        in_specs=[
            pl.BlockSpec((CH_ROWS, 2 * CONF_W), lambda i: (i + first, 0)),
            pl.BlockSpec((CONF_HALO, 2 * CONF_W), lambda i: (jnp.maximum((i + first) * hpc - 1, 0), 0)),
            pl.BlockSpec((CONF_HALO, 2 * CONF_W), lambda i: (jnp.minimum((i + first + 1) * hpc, n_halo - 1), 0)),
            _layer_spec(l, (CONF_K, CONF_W)),
            _layer_spec(l, (1, CONF_W)),
            _layer_spec(l, (1, CONF_W)),
            _layer_spec(l, (1, CONF_W)),
            _layer_spec(l, (CONF_W, CONF_W)),
            _layer_spec(l, (1, CONF_W)),
        ],
        out_specs=pl.BlockSpec((CH_ROWS, CONF_W), lambda i: (i, 0)),
        scratch_shapes=[pltpu.VMEM((CONF_W // LANE, CH_ROWS + 2 * CONF_HALO, LANE), F32)],
        compiler_params=_cparams(("arbitrary",)),
        name="conformer",
    )(cols, cols, cols, w, b, lng, lnb, pw, pwb)


def _s5_kernel(direction, *refs):
    if direction == 0:
        u_ref, bmat_ref, ar_ref, ai_ref, cmat_ref, o_ref, hbuf, st = refs
    else:
        (u_ref, bmat_ref, ar_ref, ai_ref, cmat_ref, yf_ref, dsk_ref, gw_ref, gb_ref,
         o_ref, hbuf, st) = refs
    i = pl.program_id(0)

    @pl.when(i == 0)
    def _():
        st[...] = jnp.zeros_like(st)

    u = u_ref[...]
    hbuf[...] = jnp.dot(u.astype(BF16), bmat_ref[...], preferred_element_type=F32)
    ar = jnp.broadcast_to(ar_ref[...], (NB, S5_S))
    ai = jnp.broadcast_to(ai_ref[...], (NB, S5_S))

    def step(j, carry):
        hr, hi = carry
        t = j if direction == 0 else S5_TL - 1 - j
        r0 = pl.multiple_of(t * NB, NB)
        nr = ar * hr - ai * hi + hbuf[pl.ds(r0, NB), 0:S5_S]
        ni = ar * hi + ai * hr + hbuf[pl.ds(r0, NB), S5_S:]
        hbuf[pl.ds(r0, NB), 0:S5_S] = nr
        hbuf[pl.ds(r0, NB), S5_S:] = ni
        return nr, ni

    hr, hi = lax.fori_loop(0, S5_TL, step, (st[0], st[1]), unroll=2)
    st[0] = hr
    st[1] = hi
    y = jnp.dot(hbuf[...].astype(BF16), cmat_ref[...], preferred_element_type=F32)
    if direction == 0:
        o_ref[...] = y
    else:
        v = jax.nn.gelu(y + yf_ref[...] + dsk_ref[...] * u, approximate=True)
        gv = jnp.dot(v.astype(BF16), gw_ref[...], preferred_element_type=F32) + gb_ref[...]
        o_ref[...] = gv[:, :S5_W] * jax.nn.sigmoid(gv[:, S5_W:])


def _s5_pass(l, direction, u, n_ctx_rows, bmat, ar, ai, cmat, yf, dsk, gw, gb):
    rows = u.shape[0]
    tr = S5_TL * NB
    nblk = rows // tr
    nb_ctx = n_ctx_rows // tr
    blk = functools.partial(_chunk_of, direction, nc_ctx=nb_ctx, nc=nblk)

    def dir_spec(shape):
        nd = len(shape)
        return pl.BlockSpec((None, None) + tuple(shape), lambda i: (l, direction) + (0,) * nd)

    in_specs = [
        pl.BlockSpec((tr, S5_W), lambda i: (blk(i), 0)),
        dir_spec((S5_W, 2 * S5_S)),
        dir_spec((1, S5_S)),
        dir_spec((1, S5_S)),
        _layer_spec(l, (2 * S5_S, S5_W)),
    ]
    args = [u, bmat, ar, ai, cmat]
    if direction == 1:
        in_specs += [
            pl.BlockSpec((tr, S5_W), lambda i: (blk(i), 0)),
            _layer_spec(l, (1, S5_W)),
            _layer_spec(l, (S5_W, 2 * S5_W)),
            _layer_spec(l, (1, 2 * S5_W)),
        ]
        args += [yf, dsk, gw, gb]
    return pl.pallas_call(
        functools.partial(_s5_kernel, direction),
        out_shape=jax.ShapeDtypeStruct((rows, S5_W), F32),
        grid=(nblk,),
        in_specs=in_specs,
        out_specs=pl.BlockSpec((tr, S5_W), lambda i: (blk(i), 0)),
        scratch_shapes=[pltpu.VMEM((tr, 2 * S5_S), F32), pltpu.VMEM((2, NB, S5_S), F32)],
        compiler_params=_cparams(("arbitrary",)),
        name="s5_fwd" if direction == 0 else "s5_bwd",
    )(*args)


def _s5_operators(lam_re, lam_im, log_step, b_re, b_im, c_re, c_im):
    depth = lam_re.shape[0]
    eye = jnp.eye(S5_G, dtype=F32)
    step = jnp.exp(log_step)[..., None]
    mag = jnp.exp(lam_re * step)
    ar, ai = mag * jnp.cos(lam_im * step), mag * jnp.sin(lam_im * step)
    inv_den = 1.0 / (lam_re * lam_re + lam_im * lam_im)
    cr = ((ar - 1.0) * lam_re + ai * lam_im) * inv_den
    ci = (ai * lam_re - (ar - 1.0) * lam_im) * inv_den
    bre, bim = b_re[:, None], b_im[:, None]
    bbr = cr[..., None] * bre - ci[..., None] * bim
    bbi = cr[..., None] * bim + ci[..., None] * bre
    bd_r = jnp.einsum("ldgph,gk->ldghkp", bbr, eye).reshape(depth, 2, S5_W, S5_S)
    bd_i = jnp.einsum("ldgph,gk->ldghkp", bbi, eye).reshape(depth, 2, S5_W, S5_S)
    bmat = jnp.concatenate([bd_r, bd_i], axis=-1).astype(BF16)
    cre = jnp.einsum("lghp,gk->lgpkh", c_re, eye).reshape(depth, S5_S, S5_W)
    cim = jnp.einsum("lghp,gk->lgpkh", c_im, eye).reshape(depth, S5_S, S5_W)
    cmat = jnp.concatenate([cre, -cim], axis=1).astype(BF16)
    return bmat, ar.reshape(depth, 2, 1, S5_S), ai.reshape(depth, 2, 1, S5_S), cmat


def _route_rows(lg):
    m = jnp.max(lg, axis=0, keepdims=True)
    ex = jnp.exp(lg - m)
    probs = ex / jnp.sum(ex, axis=0, keepdims=True)
    p = [probs[e:e + 1, :] for e in range(NE)]
    tops = []
    for q in range(NGRP):
        v = p[q * EPG:(q + 1) * EPG]
        m1 = functools.reduce(jnp.maximum, v)
        i1 = jnp.full_like(m1, float(EPG - 1))
        for j in range(EPG - 2, -1, -1):
            i1 = jnp.where(v[j] == m1, float(j), i1)
        rest = [jnp.where(i1 == float(j), -jnp.inf, v[j]) for j in range(EPG)]
        m2 = functools.reduce(jnp.maximum, rest)
        i2 = jnp.full_like(m2, float(EPG - 1))
        for j in range(EPG - 2, -1, -1):
            i2 = jnp.where(jnp.logical_and(rest[j] == m2, i1 != float(j)), float(j), i2)
        tops.append((m1, i1, m2, i2))
    score = [t[0] + t[2] for t in tops]
    best = functools.reduce(jnp.maximum, score)
    sel = tops[NGRP - 1] + (jnp.full_like(best, float(NGRP - 1)),)
    for q in range(NGRP - 2, -1, -1):
        hit = score[q] == best
        sel = tuple(jnp.where(hit, a, b) for a, b in zip(tops[q] + (jnp.full_like(best, float(q)),), sel))
    m1, i1, m2, i2, grp = sel
    den = m1 + m2
    return grp * EPG + i1, grp * EPG + i2, m1 / den, m2 / den


def _out_kernel(a_ref, b_ref, s_ref, x_ref, mod_ref, g_ref, w_ref, rw_ref, rb_ref,
                x1_ref, h2_ref, rt_ref, gc_ref, cnt_ref):
    tm = x_ref.shape[0]
    mix = jnp.dot(a_ref[...].astype(BF16), w_ref[0:SSD_W, :], preferred_element_type=F32)
    mix = mix + jnp.dot(b_ref[...].astype(BF16), w_ref[SSD_W:SSD_W + CONF_W, :], preferred_element_type=F32)
    mix = mix + jnp.dot(s_ref[...].astype(BF16), w_ref[SSD_W + CONF_W:, :], preferred_element_type=F32)
    x1 = (x_ref[...].reshape(tm // NB, NB, D) + mod_ref[2][None] * mix.reshape(tm // NB, NB, D)).reshape(tm, D)
    x1_ref[...] = x1
    ms = jnp.mean(x1 * x1, axis=-1, keepdims=True)
    xn = x1 * lax.rsqrt(ms + EPS) * g_ref[...]
    h2 = (xn.reshape(tm // NB, NB, D) * (1.0 + mod_ref[4])[None] + mod_ref[3][None]).reshape(tm, D)
    h2_ref[...] = h2
    lg = lax.dot_general(rw_ref[...], h2, (((1,), (1,)), ((), ())), precision=HIGHEST,
                         preferred_element_type=F32) + rb_ref[...]
    e1, e2, g1, g2 = _route_rows(lg)
    eid = lax.broadcasted_iota(I32, (NE, tm), 0).astype(F32)
    oh0 = (eid == e1).astype(F32)
    oh1 = (eid == e2).astype(F32)
    t0 = lax.broadcasted_iota(I32, (tm, tm), 0)
    t1 = lax.broadcasted_iota(I32, (tm, tm), 1)
    before = (t0 < t1).astype(BF16)
    pre0 = jnp.dot(oh0.astype(BF16), before, preferred_element_type=F32)
    pre1 = jnp.dot(oh1.astype(BF16), before, preferred_element_type=F32)
    lr0 = jnp.sum(oh0 * pre0, axis=0, keepdims=True)
    lr1 = jnp.sum(oh1 * pre1, axis=0, keepdims=True)
    rt_ref[...] = jnp.concatenate([e1, e2, lr0, lr1, jnp.zeros((NB - 4, tm), F32)], axis=0)
    gates = jnp.concatenate([g1, g2, jnp.zeros((LANE - 2, tm), F32)], axis=0)
    gc_ref[...] = gates.T
    cnt = jnp.concatenate([jnp.sum(oh0, axis=1, keepdims=True), jnp.sum(oh1, axis=1, keepdims=True)], axis=0)
    cnt_ref[...] = jnp.broadcast_to(cnt, (2 * NE, LANE))


def _out_proj(l, a, bconf, s, x, modtabs, g_ffn, w_out, rw_t, rb, n_ctx_rows, conf_first, blk0, nblk):
    tm = ROW_TM
    ctx_blk = n_ctx_rows // tm
    rows = nblk * tm
    return pl.pallas_call(
        _out_kernel,
        out_shape=[
            jax.ShapeDtypeStruct((rows, D), F32),
            jax.ShapeDtypeStruct((rows, D), F32),
            jax.ShapeDtypeStruct((NB, rows), F32),
            jax.ShapeDtypeStruct((rows, LANE), F32),
            jax.ShapeDtypeStruct((nblk, 2 * NE, LANE), F32),
        ],
        grid=(nblk,),
        in_specs=[
            pl.BlockSpec((tm, SSD_W), lambda i: (i + blk0, 0)),
            pl.BlockSpec((tm, CONF_W), lambda i: (i + blk0 - conf_first, 0)),
            pl.BlockSpec((tm, S5_W), lambda i: (i + blk0, 0)),
            pl.BlockSpec((tm, D), lambda i: (i + blk0, 0)),
            _mod_spec(l, ctx_blk, blk0),
            _layer_spec(l, (1, D)),
            _layer_spec(l, (D, D)),
            _const_spec((NE, D)),
            _const_spec((NE, 1)),
        ],
        out_specs=[
            pl.BlockSpec((tm, D), lambda i: (i, 0)),
            pl.BlockSpec((tm, D), lambda i: (i, 0)),
            pl.BlockSpec((NB, tm), lambda i: (0, i)),
            pl.BlockSpec((tm, LANE), lambda i: (i, 0)),
            pl.BlockSpec((None, 2 * NE, LANE), lambda i: (i, 0, 0)),
        ],
        compiler_params=_cparams(("arbitrary",)),
        name="out_proj_route",
    )(a, bconf, s, x, modtabs, g_ffn, w_out, rw_t, rb)


def _moe_kernel(it_ref, ie_ref, lo_ref, hi_ref, h_ref, wg_ref, wu_ref, wd_ref, o_ref, wgb, wub, wdb, cur):
    w = pl.program_id(0)
    lo = lo_ref[w]
    hi = hi_ref[w]
    new_tile = jnp.logical_or(w == 0, it_ref[w] != it_ref[jnp.maximum(w - 1, 0)])

    @pl.when(w == 0)
    def _():
        cur[0] = -1

    @pl.when(new_tile)
    def _():
        o_ref[...] = jnp.zeros_like(o_ref)

    @pl.when(hi > lo)
    def _():
        @pl.when(cur[0] != ie_ref[w])
        def _():
            wgb[...] = wg_ref[...].astype(BF16)
            wub[...] = wu_ref[...].astype(BF16)
            wdb[...] = wd_ref[...].astype(BF16)
            cur[0] = ie_ref[w]

        h = h_ref[...].astype(BF16)
        gate = jnp.dot(h, wgb[...], preferred_element_type=F32)
        up = jnp.dot(h, wub[...], preferred_element_type=F32)
        act = (_silu(gate) * up).astype(BF16)
        y = jnp.dot(act, wdb[...], preferred_element_type=F32)
        r = lax.broadcasted_iota(I32, (MOE_TM, 1), 0)
        keep = jnp.logical_and(r >= lo, r < hi)
        o_ref[...] = jnp.where(keep, y, o_ref[...])


def _moe_experts(l, hs, item_tile, item_expert, item_lo, item_hi, w_gate, w_up, w_down):
    n_rows = hs.shape[0]
    n_items = item_tile.shape[0]

    def wspec():
        return pl.BlockSpec((None, None, D, D), lambda w, it, ie, lo, hi: (l, ie[w], 0, 0))

    return pl.pallas_call(
        _moe_kernel,
        out_shape=jax.ShapeDtypeStruct((n_rows, D), F32),
        grid_spec=pltpu.PrefetchScalarGridSpec(
            num_scalar_prefetch=4,
            grid=(n_items,),
            in_specs=[
                pl.BlockSpec((MOE_TM, D), lambda w, it, ie, lo, hi: (it[w], 0)),
                wspec(), wspec(), wspec(),
            ],
            out_specs=pl.BlockSpec((MOE_TM, D), lambda w, it, ie, lo, hi: (it[w], 0)),
            scratch_shapes=[pltpu.VMEM((D, D), BF16)] * 3 + [pltpu.SMEM((1,), I32)],
        ),
        compiler_params=_cparams(("arbitrary",)),
        name="moe_experts",
    )(item_tile, item_expert, item_lo, item_hi, hs, w_gate, w_up, w_down)


def _moe_plan(route, cnt):
    rows = route.shape[1]
    nblk = cnt.shape[0]
    n_flat = 2 * rows
    e = route[0:2].astype(I32)
    lrank = route[2:4].astype(I32)
    counts = cnt[:, :, 0].astype(I32).reshape(nblk, 2, NE)
    per = jnp.transpose(counts, (1, 0, 2)).reshape(2 * nblk, NE)
    before = jnp.cumsum(per, axis=0) - per
    gcount = jnp.sum(per, axis=0)
    gend = jnp.cumsum(gcount)
    gstart = gend - gcount
    base = jnp.transpose((before + gstart[None, :]).reshape(2, nblk, NE), (2, 0, 1))[..., None]
    onehot = e.reshape(1, 2, nblk, ROW_TM) == jnp.arange(NE, dtype=I32).reshape(NE, 1, 1, 1)
    inv = jnp.sum(jnp.where(onehot, base, 0), axis=0).reshape(2, rows) + lrank
    keys = (e * 65536).reshape(-1) + jnp.arange(n_flat, dtype=I32)
    order = lax.sort(keys, is_stable=False) & 0xFFFF
    tok = jnp.where(order >= rows, order - rows, order)
    n_tiles = n_flat // MOE_TM
    tile_start = jnp.arange(n_tiles, dtype=I32) * MOE_TM
    bnd = lax.sort(jnp.concatenate([tile_start, gstart[1:]]), is_stable=False)
    bnd_hi = jnp.concatenate([bnd[1:], jnp.full((1,), n_flat, I32)])
    item_tile = jnp.minimum(bnd // MOE_TM, n_tiles - 1)
    item_expert = jnp.minimum(jnp.sum((gend[None, :] <= bnd[:, None]).astype(I32), axis=1), NE - 1)
    item_lo = bnd - item_tile * MOE_TM
    item_hi = bnd_hi - item_tile * MOE_TM
    return tok, inv, item_tile, item_expert, item_lo, item_hi


def _comb_kernel(final, has_prev, x_ref, ya_ref, yb_ref, gc_ref, mod_ref, g_ref, *rest):
    rest = rest[1:] if has_prev else rest
    o_ref, scratch = rest[0], rest[1:]
    tm = x_ref.shape[0]
    gc = gc_ref[...]
    ff = (gc[:, 0:1] * ya_ref[...] + gc[:, 1:2] * yb_ref[...]).reshape(tm // NB, NB, D)
    x2 = (x_ref[...].reshape(tm // NB, NB, D) + mod_ref[5][None] * ff).reshape(tm, D)
    if not final:
        o_ref[...] = x2
        return
    scr, = scratch
    ms = jnp.mean(x2 * x2, axis=-1, keepdims=True)
    x2 = x2 * lax.rsqrt(ms + EPS) * g_ref[...]
    ncb = D // LANE
    for k in range(ncb):
        scr[k] = x2[:, k * LANE:(k + 1) * LANE]
    for b in range(NB):
        for k in range(ncb):
            o_ref[b, :, k * LANE:(k + 1) * LANE] = scr[k, pl.ds(b, ROW_TQ, stride=NB), :]


def _combine(l, x1, ya, yb, gcol, modtabs, g_final, ctx_blk, first, blk0, nblk, out_blocks, final, prev):
    tm = ROW_TM
    if final:
        out_shape = jax.ShapeDtypeStruct((NB, out_blocks * ROW_TQ, D), F32)
        out_spec = pl.BlockSpec((NB, ROW_TQ, D), lambda i: (0, i + blk0, 0))
        scratch = [pltpu.VMEM((D // LANE, tm, LANE), F32)]
    else:
        out_shape = jax.ShapeDtypeStruct((out_blocks * tm, D), F32)
        out_spec = pl.BlockSpec((tm, D), lambda i: (i + blk0, 0))
        scratch = []
    in_specs = [
        pl.BlockSpec((tm, D), lambda i: (i + blk0, 0)),
        pl.BlockSpec((tm, D), lambda i: (i, 0)),
        pl.BlockSpec((tm, D), lambda i: (i, 0)),
        pl.BlockSpec((tm, LANE), lambda i: (i + blk0, 0)),
        _mod_spec(l, ctx_blk, first + blk0),
        _const_spec((1, D)),
    ]
    args = [x1, ya, yb, gcol, modtabs, g_final]
    aliases = {}
    if prev is not None:
        in_specs.append(pl.BlockSpec(memory_space=pl.ANY))
        args.append(prev)
        aliases = {len(args) - 1: 0}
    return pl.pallas_call(
        functools.partial(_comb_kernel, final, prev is not None),
        out_shape=out_shape,
        grid=(nblk,),
        in_specs=in_specs,
        out_specs=out_spec,
        scratch_shapes=scratch,
        input_output_aliases=aliases,
        compiler_params=_cparams(("arbitrary",)),
        name="moe_combine",
    )(*args)


def _grid_pos_embed(rows_n):
    rr, cc = jnp.meshgrid(jnp.arange(rows_n, dtype=F32), jnp.arange(GRID_W, dtype=F32), indexing="ij")
    quarter = D // 4
    inv_freq = jnp.exp(-math.log(10000.0) * jnp.arange(quarter, dtype=F32) / quarter)

    def emb(pos):
        ang = pos.reshape(-1)[:, None] * inv_freq[None, :]
        return jnp.concatenate([jnp.sin(ang), jnp.cos(ang)], axis=-1)

    return jnp.concatenate([emb(rr), emb(cc)], axis=-1)


def _take_rows(a, idx):
    return a.at[idx].get(mode="promise_in_bounds")


def _pad_last(v, width):
    return jnp.pad(v, [(0, 0)] * (v.ndim - 1) + [(0, width - v.shape[-1])])


def kernel(x, c, ctx, c_ctx, w_ada, b_ada, g_mix, w_in, ssd_conv_w, ssd_conv_b, ssd_dt_bias, ssd_a_log, ssd_d, ssd_norm_g, conf_dw_w, conf_dw_b, conf_ln_g, conf_ln_b, conf_pw_w, conf_pw_b, s5_lambda_re, s5_lambda_im, s5_log_step, s5_b_re, s5_b_im, s5_c_re, s5_c_im, s5_d, s5_glu_w, s5_glu_b, w_out, g_ffn, router_w, router_b, exp_w_gate, exp_w_up, exp_w_down, g_final):
    bsz, seq, _ = x.shape
    ctx_len = ctx.shape[1]
    depth = w_ada.shape[0]
    assert bsz == NB and seq % SSD_CHUNK == 0 and ctx_len % SSD_CHUNK == 0
    n_ctx_rows = ctx_len * NB
    rows_all = (ctx_len + seq) * NB

    cond = jnp.concatenate([c, c_ctx[None, :], jnp.zeros((2 * NB - bsz - 1, D), F32)], axis=0)
    mod = _modulation(cond, w_ada, b_ada)
    mod = mod.reshape(depth, 2 * NB, N_MOD, D)
    mod_lat = jnp.transpose(mod[:, :NB], (0, 2, 1, 3))
    mod_ctx = jnp.broadcast_to(mod[:, NB][:, :, None, :], (depth, N_MOD, NB, D))
    modtabs = jnp.stack([mod_ctx, mod_lat], axis=1)

    o_b = SSD_W + SSD_XBC + 2 * SSD_H
    w_in_p = jnp.concatenate(
        [w_in[:, :, :o_b], jnp.zeros((depth, D, DT_PAD - 2 * SSD_H), F32), w_in[:, :, o_b:]], axis=2).astype(BF16)
    g_mix3 = g_mix.reshape(depth, 1, D)
    ssd_cb = ssd_conv_b.reshape(depth, 1, SSD_XBC)
    ssd_dtb = _pad_last(ssd_dt_bias.reshape(depth, 1, 2 * SSD_H), DT_PAD)
    ssd_alog = _pad_last(ssd_a_log.reshape(depth, 1, 2 * SSD_H), DT_PAD)
    ssd_dsk = jnp.repeat(ssd_d, SSD_P, axis=1).reshape(depth, 1, SSD_W)
    ssd_ng = ssd_norm_g.reshape(depth, 1, SSD_W)
    conf_b = conf_dw_b.reshape(depth, 1, CONF_W)
    conf_lg = conf_ln_g.reshape(depth, 1, CONF_W)
    conf_lb = conf_ln_b.reshape(depth, 1, CONF_W)
    conf_pw = conf_pw_w.astype(BF16)
    conf_pb = conf_pw_b.reshape(depth, 1, CONF_W)
    s5_bmat, s5_ar, s5_ai, s5_cmat = _s5_operators(s5_lambda_re, s5_lambda_im, s5_log_step,
                                                   s5_b_re, s5_b_im, s5_c_re, s5_c_im)
    s5_dsk = s5_d.reshape(depth, 1, S5_W)
    s5_gw = s5_glu_w.astype(BF16)
    s5_gb = s5_glu_b.reshape(depth, 1, 2 * S5_W)
    w_out_b = w_out.astype(BF16)
    g_ffn3 = g_ffn.reshape(depth, 1, D)
    router_w_t = router_w.T
    router_b2 = router_b.reshape(NE, 1)
    g_final2 = g_final.reshape(1, D)
    pos = _grid_pos_embed(seq // GRID_W)

    xall = None
    for l in range(depth):
        last = l == depth - 1
        if l == 0:
            ssd_cols, conf_cols, s5_u, xall = _in_proj(l, True, (ctx, x), pos, modtabs, g_mix3, w_in_p,
                                                       rows_all, n_ctx_rows)
        else:
            ssd_cols, conf_cols, s5_u = _in_proj(l, False, xall, None, modtabs, g_mix3, w_in_p,
                                                 rows_all, n_ctx_rows)
        ypart = _ssd_pass(l, 0, ssd_cols, n_ctx_rows, ssd_conv_w, ssd_cb, ssd_dtb, ssd_alog, ssd_dsk, None)
        a_mix = _ssd_pass(l, 1, ssd_cols, n_ctx_rows, ssd_conv_w, ssd_cb, ssd_dtb, ssd_alog, ssd_ng, ypart)
        b_mix = _conformer(l, conf_cols, n_ctx_rows, not last, conf_dw_w, conf_b, conf_lg, conf_lb, conf_pw, conf_pb)
        yf = _s5_pass(l, 0, s5_u, n_ctx_rows, s5_bmat, s5_ar, s5_ai, s5_cmat, None, None, None, None)
        s_mix = _s5_pass(l, 1, s5_u, n_ctx_rows, s5_bmat, s5_ar, s5_ai, s5_cmat, yf, s5_dsk, s5_gw, s5_gb)
        ctx_blk = n_ctx_rows // ROW_TM
        first = ctx_blk if last else 0
        n_out = rows_all // ROW_TM - first
        x1, h2, route, gcol, cnt = _out_proj(l, a_mix, b_mix, s_mix, xall, modtabs, g_ffn3, w_out_b,
                                             router_w_t, router_b2, n_ctx_rows, first, first, n_out)
        halves = [(0, n_out // 2), (n_out // 2, n_out - n_out // 2)]
        plans = [_moe_plan(route[:, b0 * ROW_TM:(b0 + nb) * ROW_TM], cnt[b0:b0 + nb]) for b0, nb in halves]
        hs = [_take_rows(h2, p[0] + b0 * ROW_TM) for (b0, _), p in zip(halves, plans)]
        ys = [_moe_experts(l, h, *p[2:], exp_w_gate, exp_w_up, exp_w_down) for h, p in zip(hs, plans)]
        out = None
        for (b0, nb), p, y in zip(halves, plans, ys):
            ya = _take_rows(y, p[1][0])
            yb = _take_rows(y, p[1][1])
            out = _combine(l, x1, ya, yb, gcol, modtabs, g_final2, ctx_blk, first, b0, nb, n_out, last, out)
        xall = out
    return xall
```

```python
import functools
import math

import jax
import jax.numpy as jnp
from jax import lax
from jax.experimental import pallas as pl
from jax.experimental.pallas import tpu as pltpu

F32 = jnp.float32
BF16 = jnp.bfloat16
I32 = jnp.int32
HIGHEST = lax.Precision.HIGHEST

NB = 8
D = 1024
GRID_W = 64
N_MOD = 6
EPS = 1e-6
LANE = 128
SSD_W = 512
SSD_P = 64
SSD_H = 8
SSD_G = 2
SSD_N = 128
SSD_K = 5
SSD_XBC = SSD_W + 2 * SSD_G * SSD_N
SSD_CHUNK = 128
DT_PAD = LANE
SSD_COLS_P = SSD_W + SSD_XBC + DT_PAD
CONF_W = 256
CONF_K = 31
S5_W = 256
S5_G = 16
S5_P = 64
S5_CH = 16
S5_S = S5_G * S5_P
S5_TL = 128
NE = 16
NGRP = 4
EPG = 4
MOE_TM = 256
IN_COLS_P = SSD_COLS_P + 2 * CONF_W + S5_W
ROW_TM = 512
ROW_TQ = ROW_TM // NB
CH_ROWS = SSD_CHUNK * NB
VMEM_LIMIT = 56 * 1024 * 1024


def _cparams(sem):
    return pltpu.CompilerParams(dimension_semantics=sem, vmem_limit_bytes=VMEM_LIMIT)


def _const_spec(shape):
    nd = len(shape)
    return pl.BlockSpec(shape, lambda *_: (0,) * nd)


def _layer_spec(l, shape):
    nd = len(shape)
    return pl.BlockSpec((None,) + tuple(shape), lambda *_: (l,) + (0,) * nd)


def _silu(v):
    return v * jax.nn.sigmoid(v)


def _mod_kernel(c_ref, w_ref, b_ref, o_ref):
    c = c_ref[...]
    h = _silu(c).astype(BF16)
    o_ref[...] = jnp.dot(h, w_ref[...].astype(BF16), preferred_element_type=F32) + b_ref[...]


def _modulation(cond, w_ada, b_ada):
    depth = w_ada.shape[0]
    nrow = cond.shape[0]
    return pl.pallas_call(
        _mod_kernel,
        out_shape=jax.ShapeDtypeStruct((depth, nrow, N_MOD * D), F32),
        grid=(depth, N_MOD),
        in_specs=[
            pl.BlockSpec((nrow, D), lambda l, j: (0, 0)),
            pl.BlockSpec((None, D, D), lambda l, j: (l, 0, j)),
            pl.BlockSpec((None, 1, D), lambda l, j: (l, 0, j)),
        ],
        out_specs=pl.BlockSpec((None, nrow, D), lambda l, j: (l, 0, j)),
        compiler_params=_cparams(("arbitrary", "arbitrary")),
        name="adaln_mod",
    )(cond, w_ada, b_ada.reshape(depth, 1, N_MOD * D))


def _mod_spec(l, ctx_blk, first=0):
    return pl.BlockSpec((None, None, N_MOD, NB, D),
                        lambda i, *_: (l, jnp.where(i + first < ctx_blk, 0, 1), 0, 0, 0))


def _norm_mod_project(x, mod_ref, g_ref, w_ref, ssd_ref, conf_ref, s5_ref):
    tm = x.shape[0]
    ms = jnp.mean(x * x, axis=-1, keepdims=True)
    xn = x * lax.rsqrt(ms + EPS) * g_ref[...]
    h = xn.reshape(tm // NB, NB, D) * (1.0 + mod_ref[1])[None] + mod_ref[0][None]
    h = h.reshape(tm, D).astype(BF16)
    ssd_ref[...] = jnp.dot(h, w_ref[:, :SSD_COLS_P], preferred_element_type=F32)
    conf_ref[...] = jnp.dot(h, w_ref[:, SSD_COLS_P:SSD_COLS_P + 2 * CONF_W], preferred_element_type=F32)
    s5_ref[...] = jnp.dot(h, w_ref[:, SSD_COLS_P + 2 * CONF_W:], preferred_element_type=F32)


def _in_first_kernel(ctx_blk, ctx_ref, x_ref, pos_ref, mod_ref, g_ref, w_ref,
                     ssd_ref, conf_ref, s5_ref, x0_ref, scr):
    i = pl.program_id(0)
    ncb = D // LANE

    @pl.when(i < ctx_blk)
    def _():
        for b in range(NB):
            for k in range(ncb):
                scr[k, pl.ds(b, ROW_TQ, stride=NB), :] = ctx_ref[b, :, k * LANE:(k + 1) * LANE]

    @pl.when(i >= ctx_blk)
    def _():
        for b in range(NB):
            for k in range(ncb):
                scr[k, pl.ds(b, ROW_TQ, stride=NB), :] = (
                    x_ref[b, :, k * LANE:(k + 1) * LANE] + pos_ref[:, k * LANE:(k + 1) * LANE])

    x = jnp.concatenate([scr[k] for k in range(ncb)], axis=1)
    x0_ref[...] = x
    _norm_mod_project(x, mod_ref, g_ref, w_ref, ssd_ref, conf_ref, s5_ref)


def _in_kernel(x_ref, mod_ref, g_ref, w_ref, ssd_ref, conf_ref, s5_ref):
    _norm_mod_project(x_ref[...], mod_ref, g_ref, w_ref, ssd_ref, conf_ref, s5_ref)


def _in_proj(l, first, x_or_pair, pos, modtabs, g_mix, w_in_p, rows, n_ctx_rows):
    tm = ROW_TM
    nblk = rows // tm
    ctx_blk = n_ctx_rows // tm
    common_specs = [_mod_spec(l, ctx_blk), _layer_spec(l, (1, D)), _layer_spec(l, (D, IN_COLS_P))]
    common_args = [modtabs, g_mix, w_in_p]
    out_shape = [
        jax.ShapeDtypeStruct((rows, SSD_COLS_P), F32),
        jax.ShapeDtypeStruct((rows, 2 * CONF_W), F32),
        jax.ShapeDtypeStruct((rows, S5_W), F32),
    ]
    out_specs = [
        pl.BlockSpec((tm, SSD_COLS_P), lambda i: (i, 0)),
        pl.BlockSpec((tm, 2 * CONF_W), lambda i: (i, 0)),
        pl.BlockSpec((tm, S5_W), lambda i: (i, 0)),
    ]
    if first:
        ctx, x = x_or_pair
        body = functools.partial(_in_first_kernel, ctx_blk)
        in_specs = [
            pl.BlockSpec((NB, ROW_TQ, D), lambda i: (0, jnp.minimum(i, ctx_blk - 1), 0)),
            pl.BlockSpec((NB, ROW_TQ, D), lambda i: (0, jnp.maximum(i - ctx_blk, 0), 0)),
            pl.BlockSpec((ROW_TQ, D), lambda i: (jnp.maximum(i - ctx_blk, 0), 0)),
        ] + common_specs
        args = [ctx, x, pos] + common_args
        out_shape.append(jax.ShapeDtypeStruct((rows, D), F32))
        out_specs.append(pl.BlockSpec((tm, D), lambda i: (i, 0)))
        scratch = [pltpu.VMEM((D // LANE, tm, LANE), F32)]
    else:
        body = _in_kernel
        in_specs = [pl.BlockSpec((tm, D), lambda i: (i, 0))] + common_specs
        args = [x_or_pair] + common_args
        scratch = []
    return pl.pallas_call(
        body,
        out_shape=out_shape,
        grid=(nblk,),
        in_specs=in_specs,
        out_specs=out_specs,
        scratch_shapes=scratch,
        compiler_params=_cparams(("arbitrary",)),
        name="in_proj",
    )(*args)


def _chunk_of(direction, i, nc_ctx, nc):
    if direction == 0:
        return i
    return jnp.where(i < nc_ctx, nc_ctx - 1 - i, nc - 1 - (i - nc_ctx))


def _has_prev(c, nc_ctx):
    return jnp.logical_and(c != 0, c != nc_ctx)


def _has_next(c, nc_ctx, nc):
    return jnp.logical_and(c != nc_ctx - 1, c != nc - 1)


SSD_HALO = 2 * NB


def _ssd_kernel(direction, nc_ctx, nc, *refs):
    if direction == 0:
        (main_ref, prev_ref, next_ref, cw_ref, cb_ref, dtb_ref, alog_ref, dsk_ref,
         o_ref, ext, act, dts, yout, state) = refs
    else:
        (main_ref, prev_ref, next_ref, cw_ref, cb_ref, dtb_ref, alog_ref, ng_ref, yp_ref,
         o_ref, ext, act, dts, yout, state) = refs
    i = pl.program_id(0)
    c = _chunk_of(direction, i, nc_ctx, nc)

    @pl.when(i == 0)
    def _():
        state[...] = jnp.zeros_like(state)

    has_prev = _has_prev(c, nc_ctx)
    has_next = _has_next(c, nc_ctx, nc)
    nblk = SSD_XBC // LANE
    for k in range(nblk):
        lo = SSD_W + k * LANE
        ext[k, SSD_HALO:SSD_HALO + CH_ROWS, :] = main_ref[:, lo:lo + LANE]
        ext[k, 0:SSD_HALO, :] = jnp.where(has_prev, prev_ref[:, lo:lo + LANE], 0.0)
        ext[k, SSD_HALO + CH_ROWS:, :] = jnp.where(has_next, next_ref[:, lo:lo + LANE], 0.0)

    def conv_rb(rb, carry):
        r0 = pl.multiple_of(rb * LANE, LANE)
        for k in range(nblk):
            lo = k * LANE
            acc = jnp.broadcast_to(cb_ref[:, lo:lo + LANE], (LANE, LANE))
            for tap in range(SSD_K):
                acc = acc + ext[k, pl.ds(r0 + NB * tap, LANE), :] * cw_ref[tap:tap + 1, lo:lo + LANE]
            act[k, pl.ds(r0, LANE), :] = _silu(acc)
        return carry

    lax.fori_loop(0, CH_ROWS // LANE, conv_rb, 0)

    raw = main_ref[:, SSD_W + SSD_XBC:] + dtb_ref[...]
    dts[...] = jnp.maximum(raw, 0.0) + jnp.log1p(jnp.exp(-jnp.abs(raw)))
    a_row = -jnp.exp(alog_ref[...])

    tt = lax.broadcasted_iota(I32, (SSD_CHUNK, SSD_CHUNK), 0)
    ss = lax.broadcasted_iota(I32, (SSD_CHUNK, SSD_CHUNK), 1)
    mask = (ss <= tt) if direction == 0 else (ss >= tt)
    tmat = mask.astype(F32)
    last = SSD_CHUNK - 1 if direction == 0 else 0
    hpg = SSD_H // SSD_G
    lane_id = lax.broadcasted_iota(I32, (1, LANE), 1)
    half_lo = lane_id < SSD_P
    half_hi = lane_id >= SSD_P

    def per_batch(b, carry):
        sl = pl.ds(b, SSD_CHUNK, stride=NB)
        dt_b = dts[sl, :]
        cs = jnp.dot(tmat, dt_b * a_row, precision=HIGHEST, preferred_element_type=F32)
        cs_t = cs.T
        dt_t = dt_b.T
        for g in range(SSD_G):
            bg_t = act[SSD_W // LANE + g, sl, :].T
            cg = act[SSD_W // LANE + SSD_G + g, sl, :]
            gmat = jnp.dot(cg.astype(BF16), bg_t.astype(BF16), preferred_element_type=F32)
            for pr in range(hpg // 2):
                k = g * (hpg // 2) + pr
                xs_pair = act[k, sl, :]
                s_pair = state[b, k]
                acc_y = jnp.zeros((SSD_CHUNK, LANE), F32)
                acc_s = jnp.zeros((SSD_N, LANE), F32)
                etot = []
                for j in range(2):
                    ln = direction * SSD_H + 2 * k + j
                    own = half_lo if j == 0 else half_hi
                    colb = jnp.broadcast_to(cs[:, ln:ln + 1], (SSD_CHUNK, SSD_CHUNK))
                    row = cs_t[ln:ln + 1, :]
                    dtr = dt_t[ln:ln + 1, :]
                    decay = jnp.where(mask, jnp.exp(jnp.where(mask, colb - row, 0.0)) * dtr, 0.0)
                    lhs = jnp.concatenate([(gmat * decay).astype(BF16),
                                           (cg * jnp.exp(colb)).astype(BF16)], axis=1)
                    xs_own = jnp.where(own, xs_pair, 0.0).astype(BF16)
                    rhs = jnp.concatenate([xs_own, jnp.where(own, s_pair, 0.0).astype(BF16)], axis=0)
                    acc_y = acc_y + jnp.dot(lhs, rhs, preferred_element_type=F32)
                    tot = cs[last:last + 1, ln:ln + 1]
                    wrow = jnp.exp(tot - row) * dtr
                    acc_s = acc_s + jnp.dot((bg_t * wrow).astype(BF16), xs_own, preferred_element_type=F32)
                    etot.append(jnp.exp(tot))
                state[b, k] = s_pair * jnp.where(half_lo, etot[0], etot[1]) + acc_s
                if direction == 0:
                    acc_y = acc_y + dsk_ref[:, k * LANE:(k + 1) * LANE] * xs_pair
                yout[k, sl, :] = acc_y
        return carry

    lax.fori_loop(0, NB, per_batch, 0, unroll=2)

    nyb = SSD_W // LANE
    if direction == 0:
        for k in range(nyb):
            o_ref[:, k * LANE:(k + 1) * LANE] = yout[k]
    else:
        def fin(rb, carry):
            r0 = pl.multiple_of(rb * LANE, LANE)
            y = jnp.concatenate([yout[k, pl.ds(r0, LANE), :] for k in range(nyb)], axis=1)
            y = y + yp_ref[pl.ds(r0, LANE), :]
            y = y * _silu(main_ref[pl.ds(r0, LANE), 0:SSD_W])
            ms = jnp.mean(y * y, axis=-1, keepdims=True)
            o_ref[pl.ds(r0, LANE), :] = y * lax.rsqrt(ms + EPS) * ng_ref[...]
            return carry

        lax.fori_loop(0, CH_ROWS // LANE, fin, 0)


def _ssd_pass(l, direction, cols, n_ctx_rows, cw, cb, dtb, alog, extra, ypart):
    rows = cols.shape[0]
    nc = rows // CH_ROWS
    nc_ctx = n_ctx_rows // CH_ROWS
    hpc = CH_ROWS // SSD_HALO
    n_halo = rows // SSD_HALO
    chunk = functools.partial(_chunk_of, direction, nc_ctx=nc_ctx, nc=nc)
    in_specs = [
        pl.BlockSpec((CH_ROWS, SSD_COLS_P), lambda i: (chunk(i), 0)),
        pl.BlockSpec((SSD_HALO, SSD_COLS_P), lambda i: (jnp.maximum(chunk(i) * hpc - 1, 0), 0)),
        pl.BlockSpec((SSD_HALO, SSD_COLS_P), lambda i: (jnp.minimum((chunk(i) + 1) * hpc, n_halo - 1), 0)),
        _layer_spec(l, (SSD_K, SSD_XBC)),
        _layer_spec(l, (1, SSD_XBC)),
        _layer_spec(l, (1, DT_PAD)),
        _layer_spec(l, (1, DT_PAD)),
        _layer_spec(l, (1, SSD_W)),
    ]
    args = [cols, cols, cols, cw, cb, dtb, alog, extra]
    if direction == 1:
        in_specs.append(pl.BlockSpec((CH_ROWS, SSD_W), lambda i: (chunk(i), 0)))
        args.append(ypart)
    return pl.pallas_call(
        functools.partial(_ssd_kernel, direction, nc_ctx, nc),
        out_shape=jax.ShapeDtypeStruct((rows, SSD_W), F32),
        grid=(nc,),
        in_specs=in_specs,
        out_specs=pl.BlockSpec((CH_ROWS, SSD_W), lambda i: (chunk(i), 0)),
        scratch_shapes=[
            pltpu.VMEM((SSD_XBC // LANE, CH_ROWS + 2 * SSD_HALO, LANE), F32),
            pltpu.VMEM((SSD_XBC // LANE, CH_ROWS, LANE), F32),
            pltpu.VMEM((CH_ROWS, DT_PAD), F32),
            pltpu.VMEM((SSD_W // LANE, CH_ROWS, LANE), F32),
            pltpu.VMEM((NB, SSD_W // LANE, SSD_N, LANE), F32),
        ],
        compiler_params=_cparams(("arbitrary",)),
        name="ssd_fwd" if direction == 0 else "ssd_bwd",
    )(*args)


CONF_HALO = 128


def _conf_kernel(first_chunk, nc_ctx, nc, main_ref, prev_ref, next_ref, w_ref, b_ref, lng_ref, lnb_ref,
                 pw_ref, pwb_ref, o_ref, ext):
    c = pl.program_id(0) + first_chunk
    has_prev = _has_prev(c, nc_ctx)
    has_next = _has_next(c, nc_ctx, nc)

    def glu(ref):
        v = ref[...]
        return v[:, :CONF_W] * jax.nn.sigmoid(v[:, CONF_W:])

    um = glu(main_ref)
    up = jnp.where(has_prev, glu(prev_ref), 0.0)
    un = jnp.where(has_next, glu(next_ref), 0.0)
    nblk = CONF_W // LANE
    for k in range(nblk):
        ext[k, 0:CONF_HALO, :] = up[:, k * LANE:(k + 1) * LANE]
        ext[k, CONF_HALO:CONF_HALO + CH_ROWS, :] = um[:, k * LANE:(k + 1) * LANE]
        ext[k, CONF_HALO + CH_ROWS:, :] = un[:, k * LANE:(k + 1) * LANE]
    off = CONF_HALO - NB * (CONF_K // 2)

    def rbody(rb, carry):
        r0 = pl.multiple_of(rb * LANE, LANE)
        accs = []
        for k in range(nblk):
            lo = k * LANE
            acc = jnp.broadcast_to(b_ref[:, lo:lo + LANE], (LANE, LANE))
            for tap in range(CONF_K):
                acc = acc + ext[k, pl.ds(r0 + NB * tap + off, LANE), :] * w_ref[tap:tap + 1, lo:lo + LANE]
            accs.append(acc)
        u = jnp.concatenate(accs, axis=1)
        xc = u - jnp.mean(u, axis=-1, keepdims=True)
        y = xc * lax.rsqrt(jnp.mean(xc * xc, axis=-1, keepdims=True) + EPS) * lng_ref[...] + lnb_ref[...]
        y = _silu(y).astype(BF16)
        o_ref[pl.ds(r0, LANE), :] = jnp.dot(y, pw_ref[...], preferred_element_type=F32) + pwb_ref[...]
        return carry

    lax.fori_loop(0, CH_ROWS // LANE, rbody, 0)


def _conformer(l, cols, n_ctx_rows, with_ctx, w, b, lng, lnb, pw, pwb):
    rows = cols.shape[0]
    nc = rows // CH_ROWS
    nc_ctx = n_ctx_rows // CH_ROWS
    first = 0 if with_ctx else nc_ctx
    hpc = CH_ROWS // CONF_HALO
    n_halo = rows // CONF_HALO
    return pl.pallas_call(
        functools.partial(_conf_kernel, first, nc_ctx, nc),
        out_shape=jax.ShapeDtypeStruct(((nc - first) * CH_ROWS, CONF_W), F32),
        grid=(nc - first,),
        in_specs=[
            pl.BlockSpec((CH_ROWS, 2 * CONF_W), lambda i: (i + first, 0)),
            pl.BlockSpec((CONF_HALO, 2 * CONF_W), lambda i: (jnp.maximum((i + first) * hpc - 1, 0), 0)),
            pl.BlockSpec((CONF_HALO, 2 * CONF_W), lambda i: (jnp.minimum((i + first + 1) * hpc, n_halo - 1), 0)),
            _layer_spec(l, (CONF_K, CONF_W)),
            _layer_spec(l, (1, CONF_W)),
            _layer_spec(l, (1, CONF_W)),
            _layer_spec(l, (1, CONF_W)),
            _layer_spec(l, (CONF_W, CONF_W)),
            _layer_spec(l, (1, CONF_W)),
        ],
        out_specs=pl.BlockSpec((CH_ROWS, CONF_W), lambda i: (i, 0)),
        scratch_shapes=[pltpu.VMEM((CONF_W // LANE, CH_ROWS + 2 * CONF_HALO, LANE), F32)],
        compiler_params=_cparams(("arbitrary",)),
        name="conformer",
    )(cols, cols, cols, w, b, lng, lnb, pw, pwb)


def _s5_kernel(direction, *refs):
    if direction == 0:
        u_ref, bmat_ref, ar_ref, ai_ref, cmat_ref, o_ref, hbuf, st = refs
    else:
        (u_ref, bmat_ref, ar_ref, ai_ref, cmat_ref, yf_ref, dsk_ref, gw_ref, gb_ref,
         o_ref, hbuf, st) = refs
    i = pl.program_id(0)

    @pl.when(i == 0)
    def _():
        st[...] = jnp.zeros_like(st)

    u = u_ref[...]
    rb = 2 * LANE
    for k in range(u.shape[0] // rb):
        hbuf[k * rb:(k + 1) * rb, :] = jnp.dot(u[k * rb:(k + 1) * rb, :].astype(BF16), bmat_ref[...],
                                                preferred_element_type=F32)
    ar = jnp.broadcast_to(ar_ref[...], (NB, S5_S))
    ai = jnp.broadcast_to(ai_ref[...], (NB, S5_S))

    def step(j, carry):
        hr, hi = carry
        t = j if direction == 0 else S5_TL - 1 - j
        r0 = pl.multiple_of(t * NB, NB)
        nr = ar * hr - ai * hi + hbuf[pl.ds(r0, NB), 0:S5_S]
        ni = ar * hi + ai * hr + hbuf[pl.ds(r0, NB), S5_S:]
        hbuf[pl.ds(r0, NB), 0:S5_S] = nr
        hbuf[pl.ds(r0, NB), S5_S:] = ni
        return nr, ni

    hr, hi = lax.fori_loop(0, S5_TL, step, (st[0], st[1]), unroll=2)
    st[0] = hr
    st[1] = hi
    for k in range(u.shape[0] // rb):
        rows = slice(k * rb, (k + 1) * rb)
        o_ref[rows, :] = jnp.dot(hbuf[rows, :].astype(BF16), cmat_ref[...], preferred_element_type=F32)
    if direction == 1:
        for k in range(u.shape[0] // rb):
            rows = slice(k * rb, (k + 1) * rb)
            v = jax.nn.gelu(o_ref[rows, :] + yf_ref[rows, :] + dsk_ref[...] * u[rows, :], approximate=True)
            gv = jnp.dot(v.astype(BF16), gw_ref[...], preferred_element_type=F32) + gb_ref[...]
            o_ref[rows, :] = gv[:, :S5_W] * jax.nn.sigmoid(gv[:, S5_W:])


def _s5_pass(l, direction, u, n_ctx_rows, bmat, ar, ai, cmat, yf, dsk, gw, gb):
    rows = u.shape[0]
    tr = S5_TL * NB
    nblk = rows // tr
    nb_ctx = n_ctx_rows // tr
    blk = functools.partial(_chunk_of, direction, nc_ctx=nb_ctx, nc=nblk)

    def dir_spec(shape):
        nd = len(shape)
        return pl.BlockSpec((None, None) + tuple(shape), lambda i: (l, direction) + (0,) * nd)

    in_specs = [
        pl.BlockSpec((tr, S5_W), lambda i: (blk(i), 0)),
        dir_spec((S5_W, 2 * S5_S)),
        dir_spec((1, S5_S)),
        dir_spec((1, S5_S)),
        _layer_spec(l, (2 * S5_S, S5_W)),
    ]
    args = [u, bmat, ar, ai, cmat]
    if direction == 1:
        in_specs += [
            pl.BlockSpec((tr, S5_W), lambda i: (blk(i), 0)),
            _layer_spec(l, (1, S5_W)),
            _layer_spec(l, (S5_W, 2 * S5_W)),
            _layer_spec(l, (1, 2 * S5_W)),
        ]
        args += [yf, dsk, gw, gb]
    return pl.pallas_call(
        functools.partial(_s5_kernel, direction),
        out_shape=jax.ShapeDtypeStruct((rows, S5_W), F32),
        grid=(nblk,),
        in_specs=in_specs,
        out_specs=pl.BlockSpec((tr, S5_W), lambda i: (blk(i), 0)),
        scratch_shapes=[pltpu.VMEM((tr, 2 * S5_S), F32), pltpu.VMEM((2, NB, S5_S), F32)],
        compiler_params=_cparams(("arbitrary",)),
        name="s5_fwd" if direction == 0 else "s5_bwd",
    )(*args)


def _s5_operators(lam_re, lam_im, log_step, b_re, b_im, c_re, c_im):
    depth = lam_re.shape[0]
    eye = jnp.eye(S5_G, dtype=F32)
    step = jnp.exp(log_step)[..., None]
    mag = jnp.exp(lam_re * step)
    ar, ai = mag * jnp.cos(lam_im * step), mag * jnp.sin(lam_im * step)
    inv_den = 1.0 / (lam_re * lam_re + lam_im * lam_im)
    cr = ((ar - 1.0) * lam_re + ai * lam_im) * inv_den
    ci = (ai * lam_re - (ar - 1.0) * lam_im) * inv_den
    bre, bim = b_re[:, None], b_im[:, None]
    bbr = cr[..., None] * bre - ci[..., None] * bim
    bbi = cr[..., None] * bim + ci[..., None] * bre
    bd_r = jnp.einsum("ldgph,gk->ldghkp", bbr, eye).reshape(depth, 2, S5_W, S5_S)
    bd_i = jnp.einsum("ldgph,gk->ldghkp", bbi, eye).reshape(depth, 2, S5_W, S5_S)
    bmat = jnp.concatenate([bd_r, bd_i], axis=-1).astype(BF16)
    cre = jnp.einsum("lghp,gk->lgpkh", c_re, eye).reshape(depth, S5_S, S5_W)
    cim = jnp.einsum("lghp,gk->lgpkh", c_im, eye).reshape(depth, S5_S, S5_W)
    cmat = jnp.concatenate([cre, -cim], axis=1).astype(BF16)
    return bmat, ar.reshape(depth, 2, 1, S5_S), ai.reshape(depth, 2, 1, S5_S), cmat


def _route_rows(lg):
    m = jnp.max(lg, axis=0, keepdims=True)
    ex = jnp.exp(lg - m)
    probs = ex / jnp.sum(ex, axis=0, keepdims=True)
    p = [probs[e:e + 1, :] for e in range(NE)]
    tops = []
    for q in range(NGRP):
        v = p[q * EPG:(q + 1) * EPG]
        m1 = functools.reduce(jnp.maximum, v)
        i1 = jnp.full_like(m1, float(EPG - 1))
        for j in range(EPG - 2, -1, -1):
            i1 = jnp.where(v[j] == m1, float(j), i1)
        rest = [jnp.where(i1 == float(j), -jnp.inf, v[j]) for j in range(EPG)]
        m2 = functools.reduce(jnp.maximum, rest)
        i2 = jnp.full_like(m2, float(EPG - 1))
        for j in range(EPG - 2, -1, -1):
            i2 = jnp.where(jnp.logical_and(rest[j] == m2, i1 != float(j)), float(j), i2)
        tops.append((m1, i1, m2, i2))
    score = [t[0] + t[2] for t in tops]
    best = functools.reduce(jnp.maximum, score)
    sel = tops[NGRP - 1] + (jnp.full_like(best, float(NGRP - 1)),)
    for q in range(NGRP - 2, -1, -1):
        hit = score[q] == best
        sel = tuple(jnp.where(hit, a, b) for a, b in zip(tops[q] + (jnp.full_like(best, float(q)),), sel))
    m1, i1, m2, i2, grp = sel
    den = m1 + m2
    return grp * EPG + i1, grp * EPG + i2, m1 / den, m2 / den


def _out_kernel(a_ref, b_ref, s_ref, x_ref, mod_ref, g_ref, w_ref, rw_ref, rb_ref,
                x1_ref, h2_ref, rt_ref, gc_ref, cnt_ref):
    tm = x_ref.shape[0]
    rb = 2 * LANE
    for k in range(tm // rb):
        rows = slice(k * rb, (k + 1) * rb)
        mix = jnp.dot(a_ref[rows, :].astype(BF16), w_ref[0:SSD_W, :], preferred_element_type=F32)
        mix = mix + jnp.dot(b_ref[rows, :].astype(BF16), w_ref[SSD_W:SSD_W + CONF_W, :],
                            preferred_element_type=F32)
        mix = mix + jnp.dot(s_ref[rows, :].astype(BF16), w_ref[SSD_W + CONF_W:, :], preferred_element_type=F32)
        x1 = (x_ref[rows, :].reshape(rb // NB, NB, D) + mod_ref[2][None] * mix.reshape(rb // NB, NB, D))
        x1 = x1.reshape(rb, D)
        x1_ref[rows, :] = x1
        ms = jnp.mean(x1 * x1, axis=-1, keepdims=True)
        xn = x1 * lax.rsqrt(ms + EPS) * g_ref[...]
        h2_ref[rows, :] = (xn.reshape(rb // NB, NB, D) * (1.0 + mod_ref[4])[None] + mod_ref[3][None]).reshape(rb, D)
    lg = lax.dot_general(rw_ref[...], h2_ref[...], (((1,), (1,)), ((), ())), precision=HIGHEST,
                         preferred_element_type=F32) + rb_ref[...]
    e1, e2, g1, g2 = _route_rows(lg)
    eid = lax.broadcasted_iota(I32, (NE, tm), 0).astype(F32)
    oh0 = (eid == e1).astype(F32)
    oh1 = (eid == e2).astype(F32)
    t0 = lax.broadcasted_iota(I32, (tm, tm), 0)
    t1 = lax.broadcasted_iota(I32, (tm, tm), 1)
    before = (t0 < t1).astype(BF16)
    pre0 = jnp.dot(oh0.astype(BF16), before, preferred_element_type=F32)
    pre1 = jnp.dot(oh1.astype(BF16), before, preferred_element_type=F32)
    lr0 = jnp.sum(oh0 * pre0, axis=0, keepdims=True)
    lr1 = jnp.sum(oh1 * pre1, axis=0, keepdims=True)
    rt_ref[...] = jnp.concatenate([e1, e2, lr0, lr1, jnp.zeros((NB - 4, tm), F32)], axis=0)
    gates = jnp.concatenate([g1, g2, jnp.zeros((LANE - 2, tm), F32)], axis=0)
    gc_ref[...] = gates.T
    cnt = jnp.concatenate([jnp.sum(oh0, axis=1, keepdims=True), jnp.sum(oh1, axis=1, keepdims=True)], axis=0)
    cnt_ref[...] = jnp.broadcast_to(cnt, (2 * NE, LANE))


def _out_proj(l, a, bconf, s, x, modtabs, g_ffn, w_out, rw_t, rb, n_ctx_rows, conf_first, blk0, nblk):
    tm = ROW_TM
    ctx_blk = n_ctx_rows // tm
    rows = nblk * tm
    return pl.pallas_call(
        _out_kernel,
        out_shape=[
            jax.ShapeDtypeStruct((rows, D), F32),
            jax.ShapeDtypeStruct((rows, D), F32),
            jax.ShapeDtypeStruct((NB, rows), F32),
            jax.ShapeDtypeStruct((rows, LANE), F32),
            jax.ShapeDtypeStruct((nblk, 2 * NE, LANE), F32),
        ],
        grid=(nblk,),
        in_specs=[
            pl.BlockSpec((tm, SSD_W), lambda i: (i + blk0, 0)),
            pl.BlockSpec((tm, CONF_W), lambda i: (i + blk0 - conf_first, 0)),
            pl.BlockSpec((tm, S5_W), lambda i: (i + blk0, 0)),
            pl.BlockSpec((tm, D), lambda i: (i + blk0, 0)),
            _mod_spec(l, ctx_blk, blk0),
            _layer_spec(l, (1, D)),
            _layer_spec(l, (D, D)),
            _const_spec((NE, D)),
            _const_spec((NE, 1)),
        ],
        out_specs=[
            pl.BlockSpec((tm, D), lambda i: (i, 0)),
            pl.BlockSpec((tm, D), lambda i: (i, 0)),
            pl.BlockSpec((NB, tm), lambda i: (0, i)),
            pl.BlockSpec((tm, LANE), lambda i: (i, 0)),
            pl.BlockSpec((None, 2 * NE, LANE), lambda i: (i, 0, 0)),
        ],
        compiler_params=_cparams(("arbitrary",)),
        name="out_proj_route",
    )(a, bconf, s, x, modtabs, g_ffn, w_out, rw_t, rb)


def _moe_kernel(has_prev, it_ref, ie_ref, lo_ref, hi_ref, h_ref, wg_ref, wu_ref, wd_ref, *rest):
    rest = rest[1:] if has_prev else rest
    o_ref, wgb, wub, wdb, cur = rest
    w = pl.program_id(0)
    lo = lo_ref[w]
    hi = hi_ref[w]
    new_tile = jnp.logical_or(w == 0, it_ref[w] != it_ref[jnp.maximum(w - 1, 0)])

    @pl.when(w == 0)
    def _():
        cur[0] = -1

    @pl.when(new_tile)
    def _():
        o_ref[...] = jnp.zeros_like(o_ref)

    @pl.when(hi > lo)
    def _():
        @pl.when(cur[0] != ie_ref[w])
        def _():
            wgb[...] = wg_ref[...].astype(BF16)
            wub[...] = wu_ref[...].astype(BF16)
            wdb[...] = wd_ref[...].astype(BF16)
            cur[0] = ie_ref[w]

        h = h_ref[...].astype(BF16)
        gate = jnp.dot(h, wgb[...], preferred_element_type=F32)
        up = jnp.dot(h, wub[...], preferred_element_type=F32)
        act = (_silu(gate) * up).astype(BF16)
        y = jnp.dot(act, wdb[...], preferred_element_type=F32)
        r = lax.broadcasted_iota(I32, (MOE_TM, 1), 0)
        keep = jnp.logical_and(r >= lo, r < hi)
        o_ref[...] = jnp.where(keep, y, o_ref[...])


def _moe_experts(l, hs, items, tile0, n_tiles_all, w_gate, w_up, w_down, prev):
    item_tile, item_expert, item_lo, item_hi = items
    n_items = item_tile.shape[0]
    n_prefetch = 4

    def wspec():
        return pl.BlockSpec((None, None, D, D), lambda w, it, ie, lo, hi: (l, ie[w], 0, 0))

    in_specs = [pl.BlockSpec((MOE_TM, D), lambda w, it, ie, lo, hi: (it[w] - tile0, 0)), wspec(), wspec(), wspec()]
    args = [item_tile, item_expert, item_lo, item_hi, hs, w_gate, w_up, w_down]
    aliases = {}
    if prev is not None:
        in_specs.append(pl.BlockSpec(memory_space=pl.ANY))
        args.append(prev)
        aliases = {len(args) - 1: 0}
    return pl.pallas_call(
        functools.partial(_moe_kernel, prev is not None),
        out_shape=jax.ShapeDtypeStruct((n_tiles_all * MOE_TM, D), F32),
        grid_spec=pltpu.PrefetchScalarGridSpec(
            num_scalar_prefetch=n_prefetch,
            grid=(n_items,),
            in_specs=in_specs,
            out_specs=pl.BlockSpec((MOE_TM, D), lambda w, it, ie, lo, hi: (it[w], 0)),
            scratch_shapes=[pltpu.VMEM((D, D), BF16)] * 3 + [pltpu.SMEM((1,), I32)],
        ),
        input_output_aliases=aliases,
        compiler_params=_cparams(("arbitrary",)),
        name="moe_experts",
    )(*args)


def _moe_plan(route, cnt, tile_parts):
    rows = route.shape[1]
    nblk = cnt.shape[0]
    n_flat = 2 * rows
    e = route[0:2].astype(I32)
    lrank = route[2:4].astype(I32)
    counts = cnt[:, :, 0].astype(I32).reshape(nblk, 2, NE)
    per = jnp.transpose(counts, (1, 0, 2)).reshape(2 * nblk, NE)
    before = jnp.cumsum(per, axis=0) - per
    gcount = jnp.sum(per, axis=0)
    gend = jnp.cumsum(gcount)
    gstart = gend - gcount
    base = jnp.transpose((before + gstart[None, :]).reshape(2, nblk, NE), (2, 0, 1))[..., None]
    onehot = e.reshape(1, 2, nblk, ROW_TM) == jnp.arange(NE, dtype=I32).reshape(NE, 1, 1, 1)
    inv = jnp.sum(jnp.where(onehot, base, 0), axis=0).reshape(2, rows) + lrank
    keys = (e * 65536).reshape(-1) + jnp.arange(n_flat, dtype=I32)
    order = lax.sort(keys, is_stable=False) & 0xFFFF
    tok = jnp.where(order >= rows, order - rows, order)
    items = []
    for t0, tn in tile_parts:
        row_lo, row_hi = t0 * MOE_TM, (t0 + tn) * MOE_TM
        tile_start = (t0 + jnp.arange(tn, dtype=I32)) * MOE_TM
        bnd = lax.sort(jnp.concatenate([tile_start, jnp.clip(gstart[1:], row_lo, row_hi)]), is_stable=False)
        bnd_hi = jnp.concatenate([bnd[1:], jnp.full((1,), row_hi, I32)])
        item_tile = jnp.minimum(bnd // MOE_TM, t0 + tn - 1)
        first_row = jnp.minimum(bnd, row_hi - 1)[:, None]
        item_expert = jnp.minimum(jnp.sum((gend[None, :] <= first_row).astype(I32), axis=1), NE - 1)
        items.append((item_tile, item_expert, bnd - item_tile * MOE_TM, bnd_hi - item_tile * MOE_TM))
    return tok, inv, items


def _comb_kernel(final, has_prev, x_ref, ya_ref, yb_ref, gc_ref, mod_ref, g_ref, *rest):
    rest = rest[1:] if has_prev else rest
    o_ref, scratch = rest[0], rest[1:]
    tm = x_ref.shape[0]
    gc = gc_ref[...]
    ff = (gc[:, 0:1] * ya_ref[...] + gc[:, 1:2] * yb_ref[...]).reshape(tm // NB, NB, D)
    x2 = (x_ref[...].reshape(tm // NB, NB, D) + mod_ref[5][None] * ff).reshape(tm, D)
    if not final:
        o_ref[...] = x2
        return
    scr, = scratch
    ms = jnp.mean(x2 * x2, axis=-1, keepdims=True)
    x2 = x2 * lax.rsqrt(ms + EPS) * g_ref[...]
    ncb = D // LANE
    for k in range(ncb):
        scr[k] = x2[:, k * LANE:(k + 1) * LANE]
    for b in range(NB):
        for k in range(ncb):
            o_ref[b, :, k * LANE:(k + 1) * LANE] = scr[k, pl.ds(b, ROW_TQ, stride=NB), :]


def _combine(l, x1, ya, yb, gcol, modtabs, g_final, ctx_blk, first, blk0, nblk, out_blocks, final, prev):
    tm = ROW_TM
    if final:
        out_shape = jax.ShapeDtypeStruct((NB, out_blocks * ROW_TQ, D), F32)
        out_spec = pl.BlockSpec((NB, ROW_TQ, D), lambda i: (0, i + blk0, 0))
        scratch = [pltpu.VMEM((D // LANE, tm, LANE), F32)]
    else:
        out_shape = jax.ShapeDtypeStruct((out_blocks * tm, D), F32)
        out_spec = pl.BlockSpec((tm, D), lambda i: (i + blk0, 0))
        scratch = []
    in_specs = [
        pl.BlockSpec((tm, D), lambda i: (i + blk0, 0)),
        pl.BlockSpec((tm, D), lambda i: (i, 0)),
        pl.BlockSpec((tm, D), lambda i: (i, 0)),
        pl.BlockSpec((tm, LANE), lambda i: (i + blk0, 0)),
        _mod_spec(l, ctx_blk, first + blk0),
        _const_spec((1, D)),
    ]
    args = [x1, ya, yb, gcol, modtabs, g_final]
    aliases = {}
    if prev is not None:
        in_specs.append(pl.BlockSpec(memory_space=pl.ANY))
        args.append(prev)
        aliases = {len(args) - 1: 0}
    return pl.pallas_call(
        functools.partial(_comb_kernel, final, prev is not None),
        out_shape=out_shape,
        grid=(nblk,),
        in_specs=in_specs,
        out_specs=out_spec,
        scratch_shapes=scratch,
        input_output_aliases=aliases,
        compiler_params=_cparams(("arbitrary",)),
        name="moe_combine",
    )(*args)


def _grid_pos_embed(rows_n):
    rr, cc = jnp.meshgrid(jnp.arange(rows_n, dtype=F32), jnp.arange(GRID_W, dtype=F32), indexing="ij")
    quarter = D // 4
    inv_freq = jnp.exp(-math.log(10000.0) * jnp.arange(quarter, dtype=F32) / quarter)

    def emb(pos):
        ang = pos.reshape(-1)[:, None] * inv_freq[None, :]
        return jnp.concatenate([jnp.sin(ang), jnp.cos(ang)], axis=-1)

    return jnp.concatenate([emb(rr), emb(cc)], axis=-1)


def _take_rows(a, idx):
    return a.at[idx].get(mode="promise_in_bounds")


def _pad_last(v, width):
    return jnp.pad(v, [(0, 0)] * (v.ndim - 1) + [(0, width - v.shape[-1])])


def kernel(x, c, ctx, c_ctx, w_ada, b_ada, g_mix, w_in, ssd_conv_w, ssd_conv_b, ssd_dt_bias, ssd_a_log, ssd_d, ssd_norm_g, conf_dw_w, conf_dw_b, conf_ln_g, conf_ln_b, conf_pw_w, conf_pw_b, s5_lambda_re, s5_lambda_im, s5_log_step, s5_b_re, s5_b_im, s5_c_re, s5_c_im, s5_d, s5_glu_w, s5_glu_b, w_out, g_ffn, router_w, router_b, exp_w_gate, exp_w_up, exp_w_down, g_final):
    bsz, seq, _ = x.shape
    ctx_len = ctx.shape[1]
    depth = w_ada.shape[0]
    assert bsz == NB and seq % SSD_CHUNK == 0 and ctx_len % SSD_CHUNK == 0
    n_ctx_rows = ctx_len * NB
    rows_all = (ctx_len + seq) * NB

    cond = jnp.concatenate([c, c_ctx[None, :], jnp.zeros((2 * NB - bsz - 1, D), F32)], axis=0)
    mod = _modulation(cond, w_ada, b_ada)
    mod = mod.reshape(depth, 2 * NB, N_MOD, D)
    mod_lat = jnp.transpose(mod[:, :NB], (0, 2, 1, 3))
    mod_ctx = jnp.broadcast_to(mod[:, NB][:, :, None, :], (depth, N_MOD, NB, D))
    modtabs = jnp.stack([mod_ctx, mod_lat], axis=1)

    o_b = SSD_W + SSD_XBC + 2 * SSD_H
    w_in_p = jnp.concatenate(
        [w_in[:, :, :o_b], jnp.zeros((depth, D, DT_PAD - 2 * SSD_H), F32), w_in[:, :, o_b:]], axis=2).astype(BF16)
    g_mix3 = g_mix.reshape(depth, 1, D)
    ssd_cb = ssd_conv_b.reshape(depth, 1, SSD_XBC)
    ssd_dtb = _pad_last(ssd_dt_bias.reshape(depth, 1, 2 * SSD_H), DT_PAD)
    ssd_alog = _pad_last(ssd_a_log.reshape(depth, 1, 2 * SSD_H), DT_PAD)
    ssd_dsk = jnp.repeat(ssd_d, SSD_P, axis=1).reshape(depth, 1, SSD_W)
    ssd_ng = ssd_norm_g.reshape(depth, 1, SSD_W)
    conf_b = conf_dw_b.reshape(depth, 1, CONF_W)
    conf_lg = conf_ln_g.reshape(depth, 1, CONF_W)
    conf_lb = conf_ln_b.reshape(depth, 1, CONF_W)
    conf_pw = conf_pw_w.astype(BF16)
    conf_pb = conf_pw_b.reshape(depth, 1, CONF_W)
    s5_bmat, s5_ar, s5_ai, s5_cmat = _s5_operators(s5_lambda_re, s5_lambda_im, s5_log_step,
                                                   s5_b_re, s5_b_im, s5_c_re, s5_c_im)
    s5_dsk = s5_d.reshape(depth, 1, S5_W)
    s5_gw = s5_glu_w.astype(BF16)
    s5_gb = s5_glu_b.reshape(depth, 1, 2 * S5_W)
    w_out_b = w_out.astype(BF16)
    g_ffn3 = g_ffn.reshape(depth, 1, D)
    router_w_t = router_w.T
    router_b2 = router_b.reshape(NE, 1)
    g_final2 = g_final.reshape(1, D)
    pos = _grid_pos_embed(seq // GRID_W)

    xall = None
    for l in range(depth):
        last = l == depth - 1
        if l == 0:
            ssd_cols, conf_cols, s5_u, xall = _in_proj(l, True, (ctx, x), pos, modtabs, g_mix3, w_in_p,
                                                       rows_all, n_ctx_rows)
        else:
            ssd_cols, conf_cols, s5_u = _in_proj(l, False, xall, None, modtabs, g_mix3, w_in_p,
                                                 rows_all, n_ctx_rows)
        ypart = _ssd_pass(l, 0, ssd_cols, n_ctx_rows, ssd_conv_w, ssd_cb, ssd_dtb, ssd_alog, ssd_dsk, None)
        a_mix = _ssd_pass(l, 1, ssd_cols, n_ctx_rows, ssd_conv_w, ssd_cb, ssd_dtb, ssd_alog, ssd_ng, ypart)
        b_mix = _conformer(l, conf_cols, n_ctx_rows, not last, conf_dw_w, conf_b, conf_lg, conf_lb, conf_pw, conf_pb)
        yf = _s5_pass(l, 0, s5_u, n_ctx_rows, s5_bmat, s5_ar, s5_ai, s5_cmat, None, None, None, None)
        s_mix = _s5_pass(l, 1, s5_u, n_ctx_rows, s5_bmat, s5_ar, s5_ai, s5_cmat, yf, s5_dsk, s5_gw, s5_gb)
        ctx_blk = n_ctx_rows // ROW_TM
        first = ctx_blk if last else 0
        n_out = rows_all // ROW_TM - first
        x1, h2, route, gcol, cnt = _out_proj(l, a_mix, b_mix, s_mix, xall, modtabs, g_ffn3, w_out_b,
                                             router_w_t, router_b2, n_ctx_rows, first, first, n_out)
        n_tiles = 2 * n_out * ROW_TM // MOE_TM
        tile_parts = [(0, n_tiles // 2), (n_tiles // 2, n_tiles - n_tiles // 2)]
        tok, inv, items = _moe_plan(route, cnt, tile_parts)
        hs = [_take_rows(h2, tok[t0 * MOE_TM:(t0 + tn) * MOE_TM]) for t0, tn in tile_parts]
        y = None
        for (t0, tn), h, it in zip(tile_parts, hs, items):
            y = _moe_experts(l, h, it, t0, n_tiles, exp_w_gate, exp_w_up, exp_w_down, y)
        ya = _take_rows(y, inv[0])
        yb = _take_rows(y, inv[1])
        xall = _combine(l, x1, ya, yb, gcol, modtabs, g_final2, ctx_blk, first, 0, n_out, n_out, last, None)
    return xall
```

```python
import functools
import math

import jax
import jax.numpy as jnp
from jax import lax
from jax.experimental import pallas as pl
from jax.experimental.pallas import tpu as pltpu

F32 = jnp.float32
BF16 = jnp.bfloat16
I32 = jnp.int32
HIGHEST = lax.Precision.HIGHEST

NB = 8
D = 1024
GRID_W = 64
N_MOD = 6
EPS = 1e-6
LANE = 128
SSD_W = 512
SSD_P = 64
SSD_H = 8
SSD_G = 2
SSD_N = 128
SSD_K = 5
SSD_XBC = SSD_W + 2 * SSD_G * SSD_N
SSD_CHUNK = 128
DT_PAD = LANE
SSD_COLS_P = SSD_W + SSD_XBC + DT_PAD
CONF_W = 256
CONF_K = 31
S5_W = 256
S5_G = 16
S5_P = 64
S5_CH = 16
S5_S = S5_G * S5_P
S5_TL = 128
NE = 16
NGRP = 4
EPG = 4
MOE_TM = 256
IN_COLS_P = SSD_COLS_P + 2 * CONF_W + S5_W
ROW_TM = 512
ROW_TQ = ROW_TM // NB
CH_ROWS = SSD_CHUNK * NB
VMEM_LIMIT = 56 * 1024 * 1024


def _cparams(sem):
    return pltpu.CompilerParams(dimension_semantics=sem, vmem_limit_bytes=VMEM_LIMIT)


def _const_spec(shape):
    nd = len(shape)
    return pl.BlockSpec(shape, lambda *_: (0,) * nd)


def _layer_spec(l, shape):
    nd = len(shape)
    return pl.BlockSpec((None,) + tuple(shape), lambda *_: (l,) + (0,) * nd)


def _silu(v):
    return v * jax.nn.sigmoid(v)


def _mod_kernel(c_ref, w_ref, b_ref, o_ref):
    c = c_ref[...]
    h = _silu(c).astype(BF16)
    o_ref[...] = jnp.dot(h, w_ref[...].astype(BF16), preferred_element_type=F32) + b_ref[...]


def _modulation(cond, w_ada, b_ada):
    depth = w_ada.shape[0]
    nrow = cond.shape[0]
    return pl.pallas_call(
        _mod_kernel,
        out_shape=jax.ShapeDtypeStruct((depth, nrow, N_MOD * D), F32),
        grid=(depth, N_MOD),
        in_specs=[
            pl.BlockSpec((nrow, D), lambda l, j: (0, 0)),
            pl.BlockSpec((None, D, D), lambda l, j: (l, 0, j)),
            pl.BlockSpec((None, 1, D), lambda l, j: (l, 0, j)),
        ],
        out_specs=pl.BlockSpec((None, nrow, D), lambda l, j: (l, 0, j)),
        compiler_params=_cparams(("arbitrary", "arbitrary")),
        name="adaln_mod",
    )(cond, w_ada, b_ada.reshape(depth, 1, N_MOD * D))


def _mod_spec(l, ctx_blk, first=0):
    return pl.BlockSpec((None, None, N_MOD, NB, D),
                        lambda i, *_: (l, jnp.where(i + first < ctx_blk, 0, 1), 0, 0, 0))


def _norm_mod_project(x, mod_ref, g_ref, w_ref, ssd_ref, conf_ref, s5_ref):
    tm = x.shape[0]
    ms = jnp.mean(x * x, axis=-1, keepdims=True)
    xn = x * lax.rsqrt(ms + EPS) * g_ref[...]
    h = xn.reshape(tm // NB, NB, D) * (1.0 + mod_ref[1])[None] + mod_ref[0][None]
    h = h.reshape(tm, D).astype(BF16)
    ssd_ref[...] = jnp.dot(h, w_ref[:, :SSD_COLS_P], preferred_element_type=F32)
    conf_ref[...] = jnp.dot(h, w_ref[:, SSD_COLS_P:SSD_COLS_P + 2 * CONF_W], preferred_element_type=F32)
    s5_ref[...] = jnp.dot(h, w_ref[:, SSD_COLS_P + 2 * CONF_W:], preferred_element_type=F32)


def _in_first_kernel(ctx_blk, ctx_ref, x_ref, pos_ref, mod_ref, g_ref, w_ref,
                     ssd_ref, conf_ref, s5_ref, x0_ref, scr):
    i = pl.program_id(0)
    ncb = D // LANE

    @pl.when(i < ctx_blk)
    def _():
        for b in range(NB):
            for k in range(ncb):
                scr[k, pl.ds(b, ROW_TQ, stride=NB), :] = ctx_ref[b, :, k * LANE:(k + 1) * LANE]

    @pl.when(i >= ctx_blk)
    def _():
        for b in range(NB):
            for k in range(ncb):
                scr[k, pl.ds(b, ROW_TQ, stride=NB), :] = (
                    x_ref[b, :, k * LANE:(k + 1) * LANE] + pos_ref[:, k * LANE:(k + 1) * LANE])

    x = jnp.concatenate([scr[k] for k in range(ncb)], axis=1)
    x0_ref[...] = x
    _norm_mod_project(x, mod_ref, g_ref, w_ref, ssd_ref, conf_ref, s5_ref)


def _in_next_kernel(x1_ref, ya_ref, yb_ref, gc_ref, modp_ref, mod_ref, g_ref, w_ref,
                    ssd_ref, conf_ref, s5_ref, x_ref):
    tm = x1_ref.shape[0]
    gc = gc_ref[...]
    ff = (gc[:, 0:1] * ya_ref[...] + gc[:, 1:2] * yb_ref[...]).reshape(tm // NB, NB, D)
    x = (x1_ref[...].reshape(tm // NB, NB, D) + modp_ref[5][None] * ff).reshape(tm, D)
    x_ref[...] = x
    _norm_mod_project(x, mod_ref, g_ref, w_ref, ssd_ref, conf_ref, s5_ref)


def _in_proj(l, first, x_or_pair, pos, modtabs, g_mix, w_in_p, rows, n_ctx_rows):
    tm = ROW_TM
    nblk = rows // tm
    ctx_blk = n_ctx_rows // tm
    common_specs = [_mod_spec(l, ctx_blk), _layer_spec(l, (1, D)), _layer_spec(l, (D, IN_COLS_P))]
    common_args = [modtabs, g_mix, w_in_p]
    out_shape = [
        jax.ShapeDtypeStruct((rows, SSD_COLS_P), F32),
        jax.ShapeDtypeStruct((rows, 2 * CONF_W), F32),
        jax.ShapeDtypeStruct((rows, S5_W), F32),
    ]
    out_specs = [
        pl.BlockSpec((tm, SSD_COLS_P), lambda i: (i, 0)),
        pl.BlockSpec((tm, 2 * CONF_W), lambda i: (i, 0)),
        pl.BlockSpec((tm, S5_W), lambda i: (i, 0)),
    ]
    if first:
        ctx, x = x_or_pair
        body = functools.partial(_in_first_kernel, ctx_blk)
        in_specs = [
            pl.BlockSpec((NB, ROW_TQ, D), lambda i: (0, jnp.minimum(i, ctx_blk - 1), 0)),
            pl.BlockSpec((NB, ROW_TQ, D), lambda i: (0, jnp.maximum(i - ctx_blk, 0), 0)),
            pl.BlockSpec((ROW_TQ, D), lambda i: (jnp.maximum(i - ctx_blk, 0), 0)),
        ] + common_specs
        args = [ctx, x, pos] + common_args
        scratch = [pltpu.VMEM((D // LANE, tm, LANE), F32)]
    else:
        x1, ya, yb, gcol = x_or_pair
        body = _in_next_kernel
        in_specs = [
            pl.BlockSpec((tm, D), lambda i: (i, 0)),
            pl.BlockSpec((tm, D), lambda i: (i, 0)),
            pl.BlockSpec((tm, D), lambda i: (i, 0)),
            pl.BlockSpec((tm, LANE), lambda i: (i, 0)),
            _mod_spec(l - 1, ctx_blk),
        ] + common_specs
        args = [x1, ya, yb, gcol, modtabs] + common_args
        scratch = []
    out_shape.append(jax.ShapeDtypeStruct((rows, D), F32))
    out_specs.append(pl.BlockSpec((tm, D), lambda i: (i, 0)))
    return pl.pallas_call(
        body,
        out_shape=out_shape,
        grid=(nblk,),
        in_specs=in_specs,
        out_specs=out_specs,
        scratch_shapes=scratch,
        compiler_params=_cparams(("arbitrary",)),
        name="in_proj",
    )(*args)


def _chunk_of(direction, i, nc_ctx, nc):
    if direction == 0:
        return i
    return jnp.where(i < nc_ctx, nc_ctx - 1 - i, nc - 1 - (i - nc_ctx))


def _has_prev(c, nc_ctx):
    return jnp.logical_and(c != 0, c != nc_ctx)


def _has_next(c, nc_ctx, nc):
    return jnp.logical_and(c != nc_ctx - 1, c != nc - 1)


SSD_HALO = 2 * NB


def _ssd_kernel(direction, nc_ctx, nc, *refs):
    if direction == 0:
        (main_ref, prev_ref, next_ref, cw_ref, cb_ref, dtb_ref, alog_ref, dsk_ref,
         o_ref, ext, act, dts, yout, state, tabs) = refs
    else:
        (main_ref, prev_ref, next_ref, cw_ref, cb_ref, dtb_ref, alog_ref, ng_ref, yp_ref,
         o_ref, ext, act, dts, yout, state, tabs) = refs
    i = pl.program_id(0)
    c = _chunk_of(direction, i, nc_ctx, nc)

    @pl.when(i == 0)
    def _():
        state[...] = jnp.zeros_like(state)

    has_prev = _has_prev(c, nc_ctx)
    has_next = _has_next(c, nc_ctx, nc)
    nblk = SSD_XBC // LANE
    n_rb = CH_ROWS // LANE
    for k in range(nblk):
        lo = SSD_W + k * LANE
        ext[0, k, 0:SSD_HALO, :] = jnp.where(has_prev, prev_ref[:, lo:lo + LANE], 0.0)
        ext[0, k, SSD_HALO:, :] = main_ref[0:LANE + SSD_HALO, lo:lo + LANE]
        ext[1, k, 0:LANE + SSD_HALO, :] = main_ref[CH_ROWS - LANE - SSD_HALO:, lo:lo + LANE]
        ext[1, k, LANE + SSD_HALO:, :] = jnp.where(has_next, next_ref[:, lo:lo + LANE], 0.0)

    def conv_block(tap_rows, r0):
        for k in range(nblk):
            lo = k * LANE
            acc = jnp.broadcast_to(cb_ref[:, lo:lo + LANE], (LANE, LANE))
            for tap in range(SSD_K):
                acc = acc + tap_rows(k, tap) * cw_ref[tap:tap + 1, lo:lo + LANE]
            act[k, pl.ds(r0, LANE), :] = _silu(acc)

    def conv_rb(rb, carry):
        r0 = pl.multiple_of(rb * LANE, LANE)
        conv_block(lambda k, tap: main_ref[pl.ds(r0 + NB * tap - SSD_HALO, LANE),
                                           SSD_W + k * LANE:SSD_W + (k + 1) * LANE], r0)
        return carry

    conv_block(lambda k, tap: ext[0, k, NB * tap:NB * tap + LANE, :], 0)
    lax.fori_loop(1, n_rb - 1, conv_rb, 0)
    conv_block(lambda k, tap: ext[1, k, NB * tap:NB * tap + LANE, :], CH_ROWS - LANE)

    raw = main_ref[:, SSD_W + SSD_XBC:] + dtb_ref[...]
    dts[...] = jnp.maximum(raw, 0.0) + jnp.log1p(jnp.exp(-jnp.abs(raw)))
    a_row = -jnp.exp(alog_ref[...])

    tt = lax.broadcasted_iota(I32, (SSD_CHUNK, SSD_CHUNK), 0)
    ss = lax.broadcasted_iota(I32, (SSD_CHUNK, SSD_CHUNK), 1)
    mask = (ss <= tt) if direction == 0 else (ss >= tt)
    tmat = mask.astype(F32)
    last = SSD_CHUNK - 1 if direction == 0 else 0
    hpg = SSD_H // SSD_G
    lane_id = lax.broadcasted_iota(I32, (1, LANE), 1)
    half_lo = lane_id < SSD_P

    def decay_tables(b, carry):
        dt_b = dts[pl.ds(b, SSD_CHUNK, stride=NB), :]
        cs = jnp.dot(tmat, dt_b * a_row, precision=HIGHEST, preferred_element_type=F32)
        tabs[b, 0] = cs
        tabs[b, 1] = cs.T
        tabs[b, 2] = dt_b.T
        return carry

    lax.fori_loop(0, NB, decay_tables, 0, unroll=True)

    def per_batch(b, carry):
        sl = pl.ds(b, SSD_CHUNK, stride=NB)
        cs = tabs[b, 0]
        cs_t = tabs[b, 1]
        dt_t = tabs[b, 2]
        for g in range(SSD_G):
            bg_t = act[SSD_W // LANE + g, sl, :].T
            cg = act[SSD_W // LANE + SSD_G + g, sl, :]
            gmat = jnp.dot(cg.astype(BF16), bg_t.astype(BF16), preferred_element_type=F32)
            for pr in range(hpg // 2):
                k = g * (hpg // 2) + pr
                xs_pair = act[k, sl, :]
                s_pair = state[b, k]
                xs_b = xs_pair.astype(BF16)
                rhs = jnp.concatenate([xs_b, s_pair.astype(BF16)], axis=0)
                ys, ss, etot = [], [], []
                for j in range(2):
                    ln = direction * SSD_H + 2 * k + j
                    colb = jnp.broadcast_to(cs[:, ln:ln + 1], (SSD_CHUNK, SSD_CHUNK))
                    row = cs_t[ln:ln + 1, :]
                    dtr = dt_t[ln:ln + 1, :]
                    decay = jnp.where(mask, jnp.exp(jnp.where(mask, colb - row, 0.0)) * dtr, 0.0)
                    lhs = jnp.concatenate([(gmat * decay).astype(BF16),
                                           (cg * jnp.exp(colb)).astype(BF16)], axis=1)
                    ys.append(jnp.dot(lhs, rhs, preferred_element_type=F32))
                    tot = cs[last:last + 1, ln:ln + 1]
                    wrow = jnp.exp(tot - row) * dtr
                    ss.append(jnp.dot((bg_t * wrow).astype(BF16), xs_b, preferred_element_type=F32))
                    etot.append(jnp.exp(tot))
                acc_y = jnp.where(half_lo, ys[0], ys[1])
                state[b, k] = s_pair * jnp.where(half_lo, etot[0], etot[1]) + jnp.where(half_lo, ss[0], ss[1])
                if direction == 0:
                    acc_y = acc_y + dsk_ref[:, k * LANE:(k + 1) * LANE] * xs_pair
                yout[k, sl, :] = acc_y
        return carry

    lax.fori_loop(0, NB, per_batch, 0, unroll=2)

    nyb = SSD_W // LANE
    if direction == 0:
        for k in range(nyb):
            o_ref[:, k * LANE:(k + 1) * LANE] = yout[k]
    else:
        def fin(rb, carry):
            r0 = pl.multiple_of(rb * LANE, LANE)
            y = jnp.concatenate([yout[k, pl.ds(r0, LANE), :] for k in range(nyb)], axis=1)
            y = y + yp_ref[pl.ds(r0, LANE), :]
            y = y * _silu(main_ref[pl.ds(r0, LANE), 0:SSD_W])
            ms = jnp.mean(y * y, axis=-1, keepdims=True)
            o_ref[pl.ds(r0, LANE), :] = y * lax.rsqrt(ms + EPS) * ng_ref[...]
            return carry

        lax.fori_loop(0, CH_ROWS // LANE, fin, 0)


def _ssd_pass(l, direction, cols, n_ctx_rows, cw, cb, dtb, alog, extra, ypart):
    rows = cols.shape[0]
    nc = rows // CH_ROWS
    nc_ctx = n_ctx_rows // CH_ROWS
    hpc = CH_ROWS // SSD_HALO
    n_halo = rows // SSD_HALO
    chunk = functools.partial(_chunk_of, direction, nc_ctx=nc_ctx, nc=nc)
    in_specs = [
        pl.BlockSpec((CH_ROWS, SSD_COLS_P), lambda i: (chunk(i), 0)),
        pl.BlockSpec((SSD_HALO, SSD_COLS_P), lambda i: (jnp.maximum(chunk(i) * hpc - 1, 0), 0)),
        pl.BlockSpec((SSD_HALO, SSD_COLS_P), lambda i: (jnp.minimum((chunk(i) + 1) * hpc, n_halo - 1), 0)),
        _layer_spec(l, (SSD_K, SSD_XBC)),
        _layer_spec(l, (1, SSD_XBC)),
        _layer_spec(l, (1, DT_PAD)),
        _layer_spec(l, (1, DT_PAD)),
        _layer_spec(l, (1, SSD_W)),
    ]
    args = [cols, cols, cols, cw, cb, dtb, alog, extra]
    if direction == 1:
        in_specs.append(pl.BlockSpec((CH_ROWS, SSD_W), lambda i: (chunk(i), 0)))
        args.append(ypart)
    return pl.pallas_call(
        functools.partial(_ssd_kernel, direction, nc_ctx, nc),
        out_shape=jax.ShapeDtypeStruct((rows, SSD_W), F32),
        grid=(nc,),
        in_specs=in_specs,
        out_specs=pl.BlockSpec((CH_ROWS, SSD_W), lambda i: (chunk(i), 0)),
        scratch_shapes=[
            pltpu.VMEM((2, SSD_XBC // LANE, LANE + 2 * SSD_HALO, LANE), F32),
            pltpu.VMEM((SSD_XBC // LANE, CH_ROWS, LANE), F32),
            pltpu.VMEM((CH_ROWS, DT_PAD), F32),
            pltpu.VMEM((SSD_W // LANE, CH_ROWS, LANE), F32),
            pltpu.VMEM((NB, SSD_W // LANE, SSD_N, LANE), F32),
            pltpu.VMEM((NB, 3, SSD_CHUNK, DT_PAD), F32),
        ],
        compiler_params=_cparams(("arbitrary",)),
        name="ssd_fwd" if direction == 0 else "ssd_bwd",
    )(*args)


CONF_HALO = 128


def _conf_kernel(first_chunk, nc_ctx, nc, main_ref, prev_ref, next_ref, w_ref, b_ref, lng_ref, lnb_ref,
                 pw_ref, pwb_ref, o_ref, ext):
    c = pl.program_id(0) + first_chunk
    has_prev = _has_prev(c, nc_ctx)
    has_next = _has_next(c, nc_ctx, nc)

    def glu(ref):
        v = ref[...]
        return v[:, :CONF_W] * jax.nn.sigmoid(v[:, CONF_W:])

    um = glu(main_ref)
    up = jnp.where(has_prev, glu(prev_ref), 0.0)
    un = jnp.where(has_next, glu(next_ref), 0.0)
    nblk = CONF_W // LANE
    for k in range(nblk):
        ext[k, 0:CONF_HALO, :] = up[:, k * LANE:(k + 1) * LANE]
        ext[k, CONF_HALO:CONF_HALO + CH_ROWS, :] = um[:, k * LANE:(k + 1) * LANE]
        ext[k, CONF_HALO + CH_ROWS:, :] = un[:, k * LANE:(k + 1) * LANE]
    off = CONF_HALO - NB * (CONF_K // 2)

    def rbody(rb, carry):
        r0 = pl.multiple_of(rb * LANE, LANE)
        accs = []
        for k in range(nblk):
            lo = k * LANE
            acc = jnp.broadcast_to(b_ref[:, lo:lo + LANE], (LANE, LANE))
            for tap in range(CONF_K):
                acc = acc + ext[k, pl.ds(r0 + NB * tap + off, LANE), :] * w_ref[tap:tap + 1, lo:lo + LANE]
            accs.append(acc)
        u = jnp.concatenate(accs, axis=1)
        xc = u - jnp.mean(u, axis=-1, keepdims=True)
        y = xc * lax.rsqrt(jnp.mean(xc * xc, axis=-1, keepdims=True) + EPS) * lng_ref[...] + lnb_ref[...]
        y = _silu(y).astype(BF16)
        o_ref[pl.ds(r0, LANE), :] = jnp.dot(y, pw_ref[...], preferred_element_type=F32) + pwb_ref[...]
        return carry

    lax.fori_loop(0, CH_ROWS // LANE, rbody, 0)


def _conformer(l, cols, n_ctx_rows, with_ctx, w, b, lng, lnb, pw, pwb):
    rows = cols.shape[0]
    nc = rows // CH_ROWS
    nc_ctx = n_ctx_rows // CH_ROWS
    first = 0 if with_ctx else nc_ctx
    hpc = CH_ROWS // CONF_HALO
    n_halo = rows // CONF_HALO
    return pl.pallas_call(
        functools.partial(_conf_kernel, first, nc_ctx, nc),
        out_shape=jax.ShapeDtypeStruct(((nc - first) * CH_ROWS, CONF_W), F32),
        grid=(nc - first,),
        in_specs=[
            pl.BlockSpec((CH_ROWS, 2 * CONF_W), lambda i: (i + first, 0)),
            pl.BlockSpec((CONF_HALO, 2 * CONF_W), lambda i: (jnp.maximum((i + first) * hpc - 1, 0), 0)),
            pl.BlockSpec((CONF_HALO, 2 * CONF_W), lambda i: (jnp.minimum((i + first + 1) * hpc, n_halo - 1), 0)),
            _layer_spec(l, (CONF_K, CONF_W)),
            _layer_spec(l, (1, CONF_W)),
            _layer_spec(l, (1, CONF_W)),
            _layer_spec(l, (1, CONF_W)),
            _layer_spec(l, (CONF_W, CONF_W)),
            _layer_spec(l, (1, CONF_W)),
        ],
        out_specs=pl.BlockSpec((CH_ROWS, CONF_W), lambda i: (i, 0)),
        scratch_shapes=[pltpu.VMEM((CONF_W // LANE, CH_ROWS + 2 * CONF_HALO, LANE), F32)],
        compiler_params=_cparams(("arbitrary",)),
        name="conformer",
    )(cols, cols, cols, w, b, lng, lnb, pw, pwb)


def _s5_kernel(direction, *refs):
    if direction == 0:
        u_ref, bmat_ref, ar_ref, ai_ref, cmat_ref, o_ref, hbuf, st = refs
    else:
        (u_ref, bmat_ref, ar_ref, ai_ref, cmat_ref, yf_ref, dsk_ref, gw_ref, gb_ref,
         o_ref, hbuf, st) = refs
    i = pl.program_id(0)

    @pl.when(i == 0)
    def _():
        st[...] = jnp.zeros_like(st)

    u = u_ref[...]
    rb = 2 * LANE
    for k in range(u.shape[0] // rb):
        hbuf[k * rb:(k + 1) * rb, :] = jnp.dot(u[k * rb:(k + 1) * rb, :].astype(BF16), bmat_ref[...],
                                                preferred_element_type=F32)
    ar = jnp.broadcast_to(ar_ref[...], (NB, S5_S))
    ai = jnp.broadcast_to(ai_ref[...], (NB, S5_S))

    def step(j, carry):
        hr, hi = carry
        t = j if direction == 0 else S5_TL - 1 - j
        r0 = pl.multiple_of(t * NB, NB)
        nr = ar * hr - ai * hi + hbuf[pl.ds(r0, NB), 0:S5_S]
        ni = ar * hi + ai * hr + hbuf[pl.ds(r0, NB), S5_S:]
        hbuf[pl.ds(r0, NB), 0:S5_S] = nr
        hbuf[pl.ds(r0, NB), S5_S:] = ni
        return nr, ni

    hr, hi = lax.fori_loop(0, S5_TL, step, (st[0], st[1]), unroll=2)
    st[0] = hr
    st[1] = hi
    for k in range(u.shape[0] // rb):
        rows = slice(k * rb, (k + 1) * rb)
        o_ref[rows, :] = jnp.dot(hbuf[rows, :].astype(BF16), cmat_ref[...], preferred_element_type=F32)
    if direction == 1:
        for k in range(u.shape[0] // rb):
            rows = slice(k * rb, (k + 1) * rb)
            v = jax.nn.gelu(o_ref[rows, :] + yf_ref[rows, :] + dsk_ref[...] * u[rows, :], approximate=True)
            gv = jnp.dot(v.astype(BF16), gw_ref[...], preferred_element_type=F32) + gb_ref[...]
            o_ref[rows, :] = gv[:, :S5_W] * jax.nn.sigmoid(gv[:, S5_W:])


def _s5_pass(l, direction, u, n_ctx_rows, bmat, ar, ai, cmat, yf, dsk, gw, gb):
    rows = u.shape[0]
    tr = S5_TL * NB
    nblk = rows // tr
    nb_ctx = n_ctx_rows // tr
    blk = functools.partial(_chunk_of, direction, nc_ctx=nb_ctx, nc=nblk)

    def dir_spec(shape):
        nd = len(shape)
        return pl.BlockSpec((None, None) + tuple(shape), lambda i: (l, direction) + (0,) * nd)

    in_specs = [
        pl.BlockSpec((tr, S5_W), lambda i: (blk(i), 0)),
        dir_spec((S5_W, 2 * S5_S)),
        dir_spec((1, S5_S)),
        dir_spec((1, S5_S)),
        _layer_spec(l, (2 * S5_S, S5_W)),
    ]
    args = [u, bmat, ar, ai, cmat]
    if direction == 1:
        in_specs += [
            pl.BlockSpec((tr, S5_W), lambda i: (blk(i), 0)),
            _layer_spec(l, (1, S5_W)),
            _layer_spec(l, (S5_W, 2 * S5_W)),
            _layer_spec(l, (1, 2 * S5_W)),
        ]
        args += [yf, dsk, gw, gb]
    return pl.pallas_call(
        functools.partial(_s5_kernel, direction),
        out_shape=jax.ShapeDtypeStruct((rows, S5_W), F32),
        grid=(nblk,),
        in_specs=in_specs,
        out_specs=pl.BlockSpec((tr, S5_W), lambda i: (blk(i), 0)),
        scratch_shapes=[pltpu.VMEM((tr, 2 * S5_S), F32), pltpu.VMEM((2, NB, S5_S), F32)],
        compiler_params=_cparams(("arbitrary",)),
        name="s5_fwd" if direction == 0 else "s5_bwd",
    )(*args)


def _s5_operators(lam_re, lam_im, log_step, b_re, b_im, c_re, c_im):
    depth = lam_re.shape[0]
    eye = jnp.eye(S5_G, dtype=F32)
    step = jnp.exp(log_step)[..., None]
    mag = jnp.exp(lam_re * step)
    ar, ai = mag * jnp.cos(lam_im * step), mag * jnp.sin(lam_im * step)
    inv_den = 1.0 / (lam_re * lam_re + lam_im * lam_im)
    cr = ((ar - 1.0) * lam_re + ai * lam_im) * inv_den
    ci = (ai * lam_re - (ar - 1.0) * lam_im) * inv_den
    bre, bim = b_re[:, None], b_im[:, None]
    bbr = cr[..., None] * bre - ci[..., None] * bim
    bbi = cr[..., None] * bim + ci[..., None] * bre
    bd_r = jnp.einsum("ldgph,gk->ldghkp", bbr, eye).reshape(depth, 2, S5_W, S5_S)
    bd_i = jnp.einsum("ldgph,gk->ldghkp", bbi, eye).reshape(depth, 2, S5_W, S5_S)
    bmat = jnp.concatenate([bd_r, bd_i], axis=-1).astype(BF16)
    cre = jnp.einsum("lghp,gk->lgpkh", c_re, eye).reshape(depth, S5_S, S5_W)
    cim = jnp.einsum("lghp,gk->lgpkh", c_im, eye).reshape(depth, S5_S, S5_W)
    cmat = jnp.concatenate([cre, -cim], axis=1).astype(BF16)
    return bmat, ar.reshape(depth, 2, 1, S5_S), ai.reshape(depth, 2, 1, S5_S), cmat


def _route_rows(lg):
    m = jnp.max(lg, axis=0, keepdims=True)
    ex = jnp.exp(lg - m)
    probs = ex / jnp.sum(ex, axis=0, keepdims=True)
    p = [probs[e:e + 1, :] for e in range(NE)]
    tops = []
    for q in range(NGRP):
        v = p[q * EPG:(q + 1) * EPG]
        m1 = functools.reduce(jnp.maximum, v)
        i1 = jnp.full_like(m1, float(EPG - 1))
        for j in range(EPG - 2, -1, -1):
            i1 = jnp.where(v[j] == m1, float(j), i1)
        rest = [jnp.where(i1 == float(j), -jnp.inf, v[j]) for j in range(EPG)]
        m2 = functools.reduce(jnp.maximum, rest)
        i2 = jnp.full_like(m2, float(EPG - 1))
        for j in range(EPG - 2, -1, -1):
            i2 = jnp.where(jnp.logical_and(rest[j] == m2, i1 != float(j)), float(j), i2)
        tops.append((m1, i1, m2, i2))
    score = [t[0] + t[2] for t in tops]
    best = functools.reduce(jnp.maximum, score)
    sel = tops[NGRP - 1] + (jnp.full_like(best, float(NGRP - 1)),)
    for q in range(NGRP - 2, -1, -1):
        hit = score[q] == best
        sel = tuple(jnp.where(hit, a, b) for a, b in zip(tops[q] + (jnp.full_like(best, float(q)),), sel))
    m1, i1, m2, i2, grp = sel
    den = m1 + m2
    return grp * EPG + i1, grp * EPG + i2, m1 / den, m2 / den


def _out_kernel(a_ref, b_ref, s_ref, x_ref, mod_ref, g_ref, w_ref, rw_ref, rb_ref,
                x1_ref, h2_ref, rt_ref, gc_ref, cnt_ref):
    tm = x_ref.shape[0]
    rb = 2 * LANE
    for k in range(tm // rb):
        rows = slice(k * rb, (k + 1) * rb)
        mix = jnp.dot(a_ref[rows, :].astype(BF16), w_ref[0:SSD_W, :], preferred_element_type=F32)
        mix = mix + jnp.dot(b_ref[rows, :].astype(BF16), w_ref[SSD_W:SSD_W + CONF_W, :],
                            preferred_element_type=F32)
        mix = mix + jnp.dot(s_ref[rows, :].astype(BF16), w_ref[SSD_W + CONF_W:, :], preferred_element_type=F32)
        x1 = (x_ref[rows, :].reshape(rb // NB, NB, D) + mod_ref[2][None] * mix.reshape(rb // NB, NB, D))
        x1 = x1.reshape(rb, D)
        x1_ref[rows, :] = x1
        ms = jnp.mean(x1 * x1, axis=-1, keepdims=True)
        xn = x1 * lax.rsqrt(ms + EPS) * g_ref[...]
        h2_ref[rows, :] = (xn.reshape(rb // NB, NB, D) * (1.0 + mod_ref[4])[None] + mod_ref[3][None]).reshape(rb, D)
    lg = lax.dot_general(rw_ref[...], h2_ref[...], (((1,), (1,)), ((), ())), precision=HIGHEST,
                         preferred_element_type=F32) + rb_ref[...]
    e1, e2, g1, g2 = _route_rows(lg)
    eid = lax.broadcasted_iota(I32, (NE, tm), 0).astype(F32)
    oh0 = (eid == e1).astype(F32)
    oh1 = (eid == e2).astype(F32)
    t0 = lax.broadcasted_iota(I32, (tm, tm), 0)
    t1 = lax.broadcasted_iota(I32, (tm, tm), 1)
    before = (t0 < t1).astype(BF16)
    pre0 = jnp.dot(oh0.astype(BF16), before, preferred_element_type=F32)
    pre1 = jnp.dot(oh1.astype(BF16), before, preferred_element_type=F32)
    lr0 = jnp.sum(oh0 * pre0, axis=0, keepdims=True)
    lr1 = jnp.sum(oh1 * pre1, axis=0, keepdims=True)
    rt_ref[...] = jnp.concatenate([e1, e2, lr0, lr1, jnp.zeros((NB - 4, tm), F32)], axis=0)
    gates = jnp.concatenate([g1, g2, jnp.zeros((LANE - 2, tm), F32)], axis=0)
    gc_ref[...] = gates.T
    cnt = jnp.concatenate([jnp.sum(oh0, axis=1, keepdims=True), jnp.sum(oh1, axis=1, keepdims=True)], axis=0)
    cnt_ref[...] = jnp.broadcast_to(cnt, (2 * NE, LANE))


def _out_proj(l, a, bconf, s, x, modtabs, g_ffn, w_out, rw_t, rb, n_ctx_rows, conf_first, blk0, nblk):
    tm = ROW_TM
    ctx_blk = n_ctx_rows // tm
    rows = nblk * tm
    return pl.pallas_call(
        _out_kernel,
        out_shape=[
            jax.ShapeDtypeStruct((rows, D), F32),
            jax.ShapeDtypeStruct((rows, D), F32),
            jax.ShapeDtypeStruct((NB, rows), F32),
            jax.ShapeDtypeStruct((rows, LANE), F32),
            jax.ShapeDtypeStruct((nblk, 2 * NE, LANE), F32),
        ],
        grid=(nblk,),
        in_specs=[
            pl.BlockSpec((tm, SSD_W), lambda i: (i + blk0, 0)),
            pl.BlockSpec((tm, CONF_W), lambda i: (i + blk0 - conf_first, 0)),
            pl.BlockSpec((tm, S5_W), lambda i: (i + blk0, 0)),
            pl.BlockSpec((tm, D), lambda i: (i + blk0, 0)),
            _mod_spec(l, ctx_blk, blk0),
            _layer_spec(l, (1, D)),
            _layer_spec(l, (D, D)),
            _const_spec((NE, D)),
            _const_spec((NE, 1)),
        ],
        out_specs=[
            pl.BlockSpec((tm, D), lambda i: (i, 0)),
            pl.BlockSpec((tm, D), lambda i: (i, 0)),
            pl.BlockSpec((NB, tm), lambda i: (0, i)),
            pl.BlockSpec((tm, LANE), lambda i: (i, 0)),
            pl.BlockSpec((None, 2 * NE, LANE), lambda i: (i, 0, 0)),
        ],
        compiler_params=_cparams(("arbitrary",)),
        name="out_proj_route",
    )(a, bconf, s, x, modtabs, g_ffn, w_out, rw_t, rb)


def _moe_kernel(has_prev, it_ref, ie_ref, lo_ref, hi_ref, h_ref, wg_ref, wu_ref, wd_ref, *rest):
    rest = rest[1:] if has_prev else rest
    o_ref, wgb, wub, wdb, cur = rest
    w = pl.program_id(0)
    lo = lo_ref[w]
    hi = hi_ref[w]
    new_tile = jnp.logical_or(w == 0, it_ref[w] != it_ref[jnp.maximum(w - 1, 0)])

    @pl.when(w == 0)
    def _():
        cur[0] = -1

    @pl.when(new_tile)
    def _():
        o_ref[...] = jnp.zeros_like(o_ref)

    @pl.when(hi > lo)
    def _():
        @pl.when(cur[0] != ie_ref[w])
        def _():
            wgb[...] = wg_ref[...].astype(BF16)
            wub[...] = wu_ref[...].astype(BF16)
            wdb[...] = wd_ref[...].astype(BF16)
            cur[0] = ie_ref[w]

        h = h_ref[...].astype(BF16)
        gate = jnp.dot(h, wgb[...], preferred_element_type=F32)
        up = jnp.dot(h, wub[...], preferred_element_type=F32)
        act = (_silu(gate) * up).astype(BF16)
        y = jnp.dot(act, wdb[...], preferred_element_type=F32)
        r = lax.broadcasted_iota(I32, (MOE_TM, 1), 0)
        keep = jnp.logical_and(r >= lo, r < hi)
        o_ref[...] = jnp.where(keep, y, o_ref[...])


def _moe_experts(l, hs, items, tile0, n_tiles_all, w_gate, w_up, w_down, prev):
    item_tile, item_expert, item_lo, item_hi = items
    n_items = item_tile.shape[0]
    n_prefetch = 4

    def wspec():
        return pl.BlockSpec((None, None, D, D), lambda w, it, ie, lo, hi: (l, ie[w], 0, 0))

    in_specs = [pl.BlockSpec((MOE_TM, D), lambda w, it, ie, lo, hi: (it[w] - tile0, 0)), wspec(), wspec(), wspec()]
    args = [item_tile, item_expert, item_lo, item_hi, hs, w_gate, w_up, w_down]
    aliases = {}
    if prev is not None:
        in_specs.append(pl.BlockSpec(memory_space=pl.ANY))
        args.append(prev)
        aliases = {len(args) - 1: 0}
    return pl.pallas_call(
        functools.partial(_moe_kernel, prev is not None),
        out_shape=jax.ShapeDtypeStruct((n_tiles_all * MOE_TM, D), F32),
        grid_spec=pltpu.PrefetchScalarGridSpec(
            num_scalar_prefetch=n_prefetch,
            grid=(n_items,),
            in_specs=in_specs,
            out_specs=pl.BlockSpec((MOE_TM, D), lambda w, it, ie, lo, hi: (it[w], 0)),
            scratch_shapes=[pltpu.VMEM((D, D), BF16)] * 3 + [pltpu.SMEM((1,), I32)],
        ),
        input_output_aliases=aliases,
        compiler_params=_cparams(("arbitrary",)),
        name="moe_experts",
    )(*args)


def _moe_plan(route, cnt, tile_parts):
    rows = route.shape[1]
    nblk = cnt.shape[0]
    n_flat = 2 * rows
    e = route[0:2].astype(I32)
    lrank = route[2:4].astype(I32)
    counts = cnt[:, :, 0].astype(I32).reshape(nblk, 2, NE)
    per = jnp.transpose(counts, (1, 0, 2)).reshape(2 * nblk, NE)
    before = jnp.cumsum(per, axis=0) - per
    gcount = jnp.sum(per, axis=0)
    gend = jnp.cumsum(gcount)
    gstart = gend - gcount
    base = jnp.transpose((before + gstart[None, :]).reshape(2, nblk, NE), (2, 0, 1))[..., None]
    onehot = e.reshape(1, 2, nblk, ROW_TM) == jnp.arange(NE, dtype=I32).reshape(NE, 1, 1, 1)
    inv = jnp.sum(jnp.where(onehot, base, 0), axis=0).reshape(2, rows) + lrank
    keys = (e * 65536).reshape(-1) + jnp.arange(n_flat, dtype=I32)
    order = lax.sort(keys, is_stable=False) & 0xFFFF
    tok = jnp.where(order >= rows, order - rows, order)
    items = []
    for t0, tn in tile_parts:
        row_lo, row_hi = t0 * MOE_TM, (t0 + tn) * MOE_TM
        tile_start = (t0 + jnp.arange(tn, dtype=I32)) * MOE_TM
        bnd = lax.sort(jnp.concatenate([tile_start, jnp.clip(gstart[1:], row_lo, row_hi)]), is_stable=False)
        bnd_hi = jnp.concatenate([bnd[1:], jnp.full((1,), row_hi, I32)])
        item_tile = jnp.minimum(bnd // MOE_TM, t0 + tn - 1)
        first_row = jnp.minimum(bnd, row_hi - 1)[:, None]
        item_expert = jnp.minimum(jnp.sum((gend[None, :] <= first_row).astype(I32), axis=1), NE - 1)
        items.append((item_tile, item_expert, bnd - item_tile * MOE_TM, bnd_hi - item_tile * MOE_TM))
    return tok, inv, items


def _final_kernel(x_ref, ya_ref, yb_ref, gc_ref, mod_ref, g_ref, o_ref, scr):
    tm = x_ref.shape[0]
    gc = gc_ref[...]
    ff = (gc[:, 0:1] * ya_ref[...] + gc[:, 1:2] * yb_ref[...]).reshape(tm // NB, NB, D)
    x2 = (x_ref[...].reshape(tm // NB, NB, D) + mod_ref[5][None] * ff).reshape(tm, D)
    ms = jnp.mean(x2 * x2, axis=-1, keepdims=True)
    x2 = x2 * lax.rsqrt(ms + EPS) * g_ref[...]
    ncb = D // LANE
    for k in range(ncb):
        scr[k] = x2[:, k * LANE:(k + 1) * LANE]
    for b in range(NB):
        for k in range(ncb):
            o_ref[b, :, k * LANE:(k + 1) * LANE] = scr[k, pl.ds(b, ROW_TQ, stride=NB), :]


def _final_combine(l, x1, ya, yb, gcol, modtabs, g_final):
    tm = ROW_TM
    rows = x1.shape[0]
    row_spec = pl.BlockSpec((tm, D), lambda i: (i, 0))
    return pl.pallas_call(
        _final_kernel,
        out_shape=jax.ShapeDtypeStruct((NB, rows // NB, D), F32),
        grid=(rows // tm,),
        in_specs=[row_spec, row_spec, row_spec, pl.BlockSpec((tm, LANE), lambda i: (i, 0)),
                  _mod_spec(l, 0), _const_spec((1, D))],
        out_specs=pl.BlockSpec((NB, ROW_TQ, D), lambda i: (0, i, 0)),
        scratch_shapes=[pltpu.VMEM((D // LANE, tm, LANE), F32)],
        compiler_params=_cparams(("arbitrary",)),
        name="moe_combine_final",
    )(x1, ya, yb, gcol, modtabs, g_final)


def _grid_pos_embed(rows_n):
    rr, cc = jnp.meshgrid(jnp.arange(rows_n, dtype=F32), jnp.arange(GRID_W, dtype=F32), indexing="ij")
    quarter = D // 4
    inv_freq = jnp.exp(-math.log(10000.0) * jnp.arange(quarter, dtype=F32) / quarter)

    def emb(pos):
        ang = pos.reshape(-1)[:, None] * inv_freq[None, :]
        return jnp.concatenate([jnp.sin(ang), jnp.cos(ang)], axis=-1)

    return jnp.concatenate([emb(rr), emb(cc)], axis=-1)


def _take_rows(a, idx):
    return a.at[idx].get(mode="promise_in_bounds")


def _pad_last(v, width):
    return jnp.pad(v, [(0, 0)] * (v.ndim - 1) + [(0, width - v.shape[-1])])


def kernel(x, c, ctx, c_ctx, w_ada, b_ada, g_mix, w_in, ssd_conv_w, ssd_conv_b, ssd_dt_bias, ssd_a_log, ssd_d, ssd_norm_g, conf_dw_w, conf_dw_b, conf_ln_g, conf_ln_b, conf_pw_w, conf_pw_b, s5_lambda_re, s5_lambda_im, s5_log_step, s5_b_re, s5_b_im, s5_c_re, s5_c_im, s5_d, s5_glu_w, s5_glu_b, w_out, g_ffn, router_w, router_b, exp_w_gate, exp_w_up, exp_w_down, g_final):
    bsz, seq, _ = x.shape
    ctx_len = ctx.shape[1]
    depth = w_ada.shape[0]
    assert bsz == NB and seq % SSD_CHUNK == 0 and ctx_len % SSD_CHUNK == 0
    n_ctx_rows = ctx_len * NB
    rows_all = (ctx_len + seq) * NB

    cond = jnp.concatenate([c, c_ctx[None, :], jnp.zeros((2 * NB - bsz - 1, D), F32)], axis=0)
    mod = _modulation(cond, w_ada, b_ada)
    mod = mod.reshape(depth, 2 * NB, N_MOD, D)
    mod_lat = jnp.transpose(mod[:, :NB], (0, 2, 1, 3))
    mod_ctx = jnp.broadcast_to(mod[:, NB][:, :, None, :], (depth, N_MOD, NB, D))
    modtabs = jnp.stack([mod_ctx, mod_lat], axis=1)

    o_b = SSD_W + SSD_XBC + 2 * SSD_H
    w_in_p = jnp.concatenate(
        [w_in[:, :, :o_b], jnp.zeros((depth, D, DT_PAD - 2 * SSD_H), F32), w_in[:, :, o_b:]], axis=2).astype(BF16)
    g_mix3 = g_mix.reshape(depth, 1, D)
    ssd_cb = ssd_conv_b.reshape(depth, 1, SSD_XBC)
    ssd_dtb = _pad_last(ssd_dt_bias.reshape(depth, 1, 2 * SSD_H), DT_PAD)
    ssd_alog = _pad_last(ssd_a_log.reshape(depth, 1, 2 * SSD_H), DT_PAD)
    ssd_dsk = jnp.repeat(ssd_d, SSD_P, axis=1).reshape(depth, 1, SSD_W)
    ssd_ng = ssd_norm_g.reshape(depth, 1, SSD_W)
    conf_b = conf_dw_b.reshape(depth, 1, CONF_W)
    conf_lg = conf_ln_g.reshape(depth, 1, CONF_W)
    conf_lb = conf_ln_b.reshape(depth, 1, CONF_W)
    conf_pw = conf_pw_w.astype(BF16)
    conf_pb = conf_pw_b.reshape(depth, 1, CONF_W)
    s5_bmat, s5_ar, s5_ai, s5_cmat = _s5_operators(s5_lambda_re, s5_lambda_im, s5_log_step,
                                                   s5_b_re, s5_b_im, s5_c_re, s5_c_im)
    s5_dsk = s5_d.reshape(depth, 1, S5_W)
    s5_gw = s5_glu_w.astype(BF16)
    s5_gb = s5_glu_b.reshape(depth, 1, 2 * S5_W)
    w_out_b = w_out.astype(BF16)
    g_ffn3 = g_ffn.reshape(depth, 1, D)
    router_w_t = router_w.T
    router_b2 = router_b.reshape(NE, 1)
    g_final2 = g_final.reshape(1, D)
    pos = _grid_pos_embed(seq // GRID_W)

    pending = (ctx, x)
    for l in range(depth):
        last = l == depth - 1
        ssd_cols, conf_cols, s5_u, xall = _in_proj(l, l == 0, pending, pos, modtabs, g_mix3, w_in_p,
                                                   rows_all, n_ctx_rows)
        ypart = _ssd_pass(l, 0, ssd_cols, n_ctx_rows, ssd_conv_w, ssd_cb, ssd_dtb, ssd_alog, ssd_dsk, None)
        a_mix = _ssd_pass(l, 1, ssd_cols, n_ctx_rows, ssd_conv_w, ssd_cb, ssd_dtb, ssd_alog, ssd_ng, ypart)
        b_mix = _conformer(l, conf_cols, n_ctx_rows, not last, conf_dw_w, conf_b, conf_lg, conf_lb, conf_pw, conf_pb)
        yf = _s5_pass(l, 0, s5_u, n_ctx_rows, s5_bmat, s5_ar, s5_ai, s5_cmat, None, None, None, None)
        s_mix = _s5_pass(l, 1, s5_u, n_ctx_rows, s5_bmat, s5_ar, s5_ai, s5_cmat, yf, s5_dsk, s5_gw, s5_gb)
        ctx_blk = n_ctx_rows // ROW_TM
        first = ctx_blk if last else 0
        n_out = rows_all // ROW_TM - first
        x1, h2, route, gcol, cnt = _out_proj(l, a_mix, b_mix, s_mix, xall, modtabs, g_ffn3, w_out_b,
                                             router_w_t, router_b2, n_ctx_rows, first, first, n_out)
        n_tiles = 2 * n_out * ROW_TM // MOE_TM
        tile_parts = [(0, n_tiles // 2), (n_tiles // 2, n_tiles - n_tiles // 2)]
        tok, inv, items = _moe_plan(route, cnt, tile_parts)
        hs = [_take_rows(h2, tok[t0 * MOE_TM:(t0 + tn) * MOE_TM]) for t0, tn in tile_parts]
        y = None
        for (t0, tn), h, it in zip(tile_parts, hs, items):
            y = _moe_experts(l, h, it, t0, n_tiles, exp_w_gate, exp_w_up, exp_w_down, y)
        ya = _take_rows(y, inv[0])
        yb = _take_rows(y, inv[1])
        pending = (x1, ya, yb, gcol)
    return _final_combine(depth - 1, *pending, modtabs, g_final2)
```

```python
import functools
import math

import jax
import jax.numpy as jnp
from jax import lax
from jax.experimental import pallas as pl
from jax.experimental.pallas import tpu as pltpu

F32 = jnp.float32
BF16 = jnp.bfloat16
I32 = jnp.int32
HIGHEST = lax.Precision.HIGHEST

NB = 8
D = 1024
GRID_W = 64
N_MOD = 6
EPS = 1e-6
LANE = 128
SSD_W = 512
SSD_P = 64
SSD_H = 8
SSD_G = 2
SSD_N = 128
SSD_K = 5
SSD_XBC = SSD_W + 2 * SSD_G * SSD_N
SSD_CHUNK = 128
DT_PAD = LANE
SSD_COLS_P = SSD_W + SSD_XBC + DT_PAD
CONF_W = 256
CONF_K = 31
S5_W = 256
S5_G = 16
S5_P = 64
S5_CH = 16
S5_S = S5_G * S5_P
S5_TL = 128
NE = 16
NGRP = 4
EPG = 4
MOE_TM = 256
IN_COLS_P = SSD_COLS_P + 2 * CONF_W + S5_W
ROW_TM = 512
ROW_TQ = ROW_TM // NB
CH_ROWS = SSD_CHUNK * NB
VMEM_LIMIT = 56 * 1024 * 1024


def _cparams(sem):
    return pltpu.CompilerParams(dimension_semantics=sem, vmem_limit_bytes=VMEM_LIMIT)


def _const_spec(shape):
    nd = len(shape)
    return pl.BlockSpec(shape, lambda *_: (0,) * nd)


def _layer_spec(l, shape):
    nd = len(shape)
    return pl.BlockSpec((None,) + tuple(shape), lambda *_: (l,) + (0,) * nd)


def _silu(v):
    return v * jax.nn.sigmoid(v)


def _mod_kernel(c_ref, w_ref, b_ref, o_ref):
    c = c_ref[...]
    h = _silu(c).astype(BF16)
    o_ref[...] = jnp.dot(h, w_ref[...].astype(BF16), preferred_element_type=F32) + b_ref[...]


def _modulation(cond, w_ada, b_ada):
    depth = w_ada.shape[0]
    nrow = cond.shape[0]
    return pl.pallas_call(
        _mod_kernel,
        out_shape=jax.ShapeDtypeStruct((depth, nrow, N_MOD * D), F32),
        grid=(depth, N_MOD),
        in_specs=[
            pl.BlockSpec((nrow, D), lambda l, j: (0, 0)),
            pl.BlockSpec((None, D, D), lambda l, j: (l, 0, j)),
            pl.BlockSpec((None, 1, D), lambda l, j: (l, 0, j)),
        ],
        out_specs=pl.BlockSpec((None, nrow, D), lambda l, j: (l, 0, j)),
        compiler_params=_cparams(("arbitrary", "arbitrary")),
        name="adaln_mod",
    )(cond, w_ada, b_ada.reshape(depth, 1, N_MOD * D))


def _mod_spec(l, ctx_blk, first=0):
    return pl.BlockSpec((None, None, N_MOD, NB, D),
                        lambda i, *_: (l, jnp.where(i + first < ctx_blk, 0, 1), 0, 0, 0))


def _norm_mod_project(x, mod_ref, g_ref, w_ref, ssd_ref, conf_ref, s5_ref):
    tm = x.shape[0]
    ms = jnp.mean(x * x, axis=-1, keepdims=True)
    xn = x * lax.rsqrt(ms + EPS) * g_ref[...]
    h = xn.reshape(tm // NB, NB, D) * (1.0 + mod_ref[1])[None] + mod_ref[0][None]
    h = h.reshape(tm, D).astype(BF16)
    ssd_ref[...] = jnp.dot(h, w_ref[:, :SSD_COLS_P], preferred_element_type=F32)
    conf_ref[...] = jnp.dot(h, w_ref[:, SSD_COLS_P:SSD_COLS_P + 2 * CONF_W], preferred_element_type=F32)
    s5_ref[...] = jnp.dot(h, w_ref[:, SSD_COLS_P + 2 * CONF_W:], preferred_element_type=F32)


def _in_first_kernel(ctx_blk, ctx_ref, x_ref, pos_ref, mod_ref, g_ref, w_ref,
                     ssd_ref, conf_ref, s5_ref, x0_ref, scr):
    i = pl.program_id(0)
    ncb = D // LANE

    @pl.when(i < ctx_blk)
    def _():
        for b in range(NB):
            for k in range(ncb):
                scr[k, pl.ds(b, ROW_TQ, stride=NB), :] = ctx_ref[b, :, k * LANE:(k + 1) * LANE]

    @pl.when(i >= ctx_blk)
    def _():
        for b in range(NB):
            for k in range(ncb):
                scr[k, pl.ds(b, ROW_TQ, stride=NB), :] = (
                    x_ref[b, :, k * LANE:(k + 1) * LANE] + pos_ref[:, k * LANE:(k + 1) * LANE])

    x = jnp.concatenate([scr[k] for k in range(ncb)], axis=1)
    x0_ref[...] = x
    _norm_mod_project(x, mod_ref, g_ref, w_ref, ssd_ref, conf_ref, s5_ref)


def _in_next_kernel(x1_ref, ya_ref, yb_ref, gc_ref, modp_ref, mod_ref, g_ref, w_ref,
                    ssd_ref, conf_ref, s5_ref, x_ref):
    tm = x1_ref.shape[0]
    gc = gc_ref[...]
    ff = (gc[:, 0:1] * ya_ref[...] + gc[:, 1:2] * yb_ref[...]).reshape(tm // NB, NB, D)
    x = (x1_ref[...].reshape(tm // NB, NB, D) + modp_ref[5][None] * ff).reshape(tm, D)
    x_ref[...] = x
    _norm_mod_project(x, mod_ref, g_ref, w_ref, ssd_ref, conf_ref, s5_ref)


def _in_proj(l, first, x_or_pair, pos, modtabs, g_mix, w_in_p, rows, n_ctx_rows):
    tm = ROW_TM
    nblk = rows // tm
    ctx_blk = n_ctx_rows // tm
    common_specs = [_mod_spec(l, ctx_blk), _layer_spec(l, (1, D)), _layer_spec(l, (D, IN_COLS_P))]
    common_args = [modtabs, g_mix, w_in_p]
    out_shape = [
        jax.ShapeDtypeStruct((rows, SSD_COLS_P), F32),
        jax.ShapeDtypeStruct((rows, 2 * CONF_W), F32),
        jax.ShapeDtypeStruct((rows, S5_W), F32),
    ]
    out_specs = [
        pl.BlockSpec((tm, SSD_COLS_P), lambda i: (i, 0)),
        pl.BlockSpec((tm, 2 * CONF_W), lambda i: (i, 0)),
        pl.BlockSpec((tm, S5_W), lambda i: (i, 0)),
    ]
    if first:
        ctx, x = x_or_pair
        body = functools.partial(_in_first_kernel, ctx_blk)
        in_specs = [
            pl.BlockSpec((NB, ROW_TQ, D), lambda i: (0, jnp.minimum(i, ctx_blk - 1), 0)),
            pl.BlockSpec((NB, ROW_TQ, D), lambda i: (0, jnp.maximum(i - ctx_blk, 0), 0)),
            pl.BlockSpec((ROW_TQ, D), lambda i: (jnp.maximum(i - ctx_blk, 0), 0)),
        ] + common_specs
        args = [ctx, x, pos] + common_args
        scratch = [pltpu.VMEM((D // LANE, tm, LANE), F32)]
    else:
        x1, ya, yb, gcol = x_or_pair
        body = _in_next_kernel
        in_specs = [
            pl.BlockSpec((tm, D), lambda i: (i, 0)),
            pl.BlockSpec((tm, D), lambda i: (i, 0)),
            pl.BlockSpec((tm, D), lambda i: (i, 0)),
            pl.BlockSpec((tm, LANE), lambda i: (i, 0)),
            _mod_spec(l - 1, ctx_blk),
        ] + common_specs
        args = [x1, ya, yb, gcol, modtabs] + common_args
        scratch = []
    out_shape.append(jax.ShapeDtypeStruct((rows, D), F32))
    out_specs.append(pl.BlockSpec((tm, D), lambda i: (i, 0)))
    return pl.pallas_call(
        body,
        out_shape=out_shape,
        grid=(nblk,),
        in_specs=in_specs,
        out_specs=out_specs,
        scratch_shapes=scratch,
        compiler_params=_cparams(("arbitrary",)),
        name="in_proj",
    )(*args)


def _chunk_of(direction, i, nc_ctx, nc):
    if direction == 0:
        return i
    return jnp.where(i < nc_ctx, nc_ctx - 1 - i, nc - 1 - (i - nc_ctx))


def _has_prev(c, nc_ctx):
    return jnp.logical_and(c != 0, c != nc_ctx)


def _has_next(c, nc_ctx, nc):
    return jnp.logical_and(c != nc_ctx - 1, c != nc - 1)


SSD_HALO = 2 * NB


def _ssd_kernel(direction, nc_ctx, nc, *refs):
    if direction == 0:
        (main_ref, prev_ref, next_ref, cw_ref, cb_ref, dtb_ref, alog_ref, dsk_ref,
         o_ref, ext, act, dts, yout, state, tabs) = refs
    else:
        (main_ref, prev_ref, next_ref, cw_ref, cb_ref, dtb_ref, alog_ref, ng_ref, yp_ref,
         o_ref, ext, act, dts, yout, state, tabs) = refs
    i = pl.program_id(0)
    c = _chunk_of(direction, i, nc_ctx, nc)

    @pl.when(i == 0)
    def _():
        state[...] = jnp.zeros_like(state)

    has_prev = _has_prev(c, nc_ctx)
    has_next = _has_next(c, nc_ctx, nc)
    nblk = SSD_XBC // LANE
    n_rb = CH_ROWS // LANE
    for k in range(nblk):
        lo = SSD_W + k * LANE
        ext[0, k, 0:SSD_HALO, :] = jnp.where(has_prev, prev_ref[:, lo:lo + LANE], 0.0)
        ext[0, k, SSD_HALO:, :] = main_ref[0:LANE + SSD_HALO, lo:lo + LANE]
        ext[1, k, 0:LANE + SSD_HALO, :] = main_ref[CH_ROWS - LANE - SSD_HALO:, lo:lo + LANE]
        ext[1, k, LANE + SSD_HALO:, :] = jnp.where(has_next, next_ref[:, lo:lo + LANE], 0.0)

    def conv_block(tap_rows, r0):
        for k in range(nblk):
            lo = k * LANE
            acc = jnp.broadcast_to(cb_ref[:, lo:lo + LANE], (LANE, LANE))
            for tap in range(SSD_K):
                acc = acc + tap_rows(k, tap) * cw_ref[tap:tap + 1, lo:lo + LANE]
            act[k, pl.ds(r0, LANE), :] = _silu(acc)

    def conv_rb(rb, carry):
        r0 = pl.multiple_of(rb * LANE, LANE)
        conv_block(lambda k, tap: main_ref[pl.ds(r0 + NB * tap - SSD_HALO, LANE),
                                           SSD_W + k * LANE:SSD_W + (k + 1) * LANE], r0)
        return carry

    conv_block(lambda k, tap: ext[0, k, NB * tap:NB * tap + LANE, :], 0)
    lax.fori_loop(1, n_rb - 1, conv_rb, 0)
    conv_block(lambda k, tap: ext[1, k, NB * tap:NB * tap + LANE, :], CH_ROWS - LANE)

    raw = main_ref[:, SSD_W + SSD_XBC:] + dtb_ref[...]
    dts[...] = jnp.maximum(raw, 0.0) + jnp.log1p(jnp.exp(-jnp.abs(raw)))
    a_row = -jnp.exp(alog_ref[...])

    tt = lax.broadcasted_iota(I32, (SSD_CHUNK, SSD_CHUNK), 0)
    ss = lax.broadcasted_iota(I32, (SSD_CHUNK, SSD_CHUNK), 1)
    mask = (ss <= tt) if direction == 0 else (ss >= tt)
    tmat = mask.astype(F32)
    last = SSD_CHUNK - 1 if direction == 0 else 0
    hpg = SSD_H // SSD_G
    lane_id = lax.broadcasted_iota(I32, (1, LANE), 1)
    half_lo = lane_id < SSD_P

    def decay_tables(b, carry):
        dt_b = dts[pl.ds(b, SSD_CHUNK, stride=NB), :]
        cs = jnp.dot(tmat, dt_b * a_row, precision=HIGHEST, preferred_element_type=F32)
        tabs[b, 0] = cs
        tabs[b, 1] = cs.T
        tabs[b, 2] = dt_b.T
        return carry

    lax.fori_loop(0, NB, decay_tables, 0, unroll=True)

    def per_batch(b, carry):
        sl = pl.ds(b, SSD_CHUNK, stride=NB)
        cs = tabs[b, 0]
        cs_t = tabs[b, 1]
        dt_t = tabs[b, 2]
        for g in range(SSD_G):
            bg_t = act[SSD_W // LANE + g, sl, :].T
            cg = act[SSD_W // LANE + SSD_G + g, sl, :]
            gmat = jnp.dot(cg.astype(BF16), bg_t.astype(BF16), preferred_element_type=F32)
            for pr in range(hpg // 2):
                k = g * (hpg // 2) + pr
                xs_pair = act[k, sl, :]
                s_pair = state[b, k]
                xs_b = xs_pair.astype(BF16)
                rhs = jnp.concatenate([xs_b, s_pair.astype(BF16)], axis=0)
                ys, ss, etot = [], [], []
                for j in range(2):
                    ln = direction * SSD_H + 2 * k + j
                    colb = jnp.broadcast_to(cs[:, ln:ln + 1], (SSD_CHUNK, SSD_CHUNK))
                    row = cs_t[ln:ln + 1, :]
                    dtr = dt_t[ln:ln + 1, :]
                    decay = jnp.where(mask, jnp.exp(jnp.where(mask, colb - row, 0.0)) * dtr, 0.0)
                    lhs = jnp.concatenate([(gmat * decay).astype(BF16),
                                           (cg * jnp.exp(colb)).astype(BF16)], axis=1)
                    ys.append(jnp.dot(lhs, rhs, preferred_element_type=F32))
                    tot = cs[last:last + 1, ln:ln + 1]
                    wrow = jnp.exp(tot - row) * dtr
                    ss.append(jnp.dot((bg_t * wrow).astype(BF16), xs_b, preferred_element_type=F32))
                    etot.append(jnp.exp(tot))
                acc_y = jnp.where(half_lo, ys[0], ys[1])
                state[b, k] = s_pair * jnp.where(half_lo, etot[0], etot[1]) + jnp.where(half_lo, ss[0], ss[1])
                if direction == 0:
                    acc_y = acc_y + dsk_ref[:, k * LANE:(k + 1) * LANE] * xs_pair
                yout[k, sl, :] = acc_y
        return carry

    lax.fori_loop(0, NB, per_batch, 0, unroll=2)

    nyb = SSD_W // LANE
    if direction == 0:
        for k in range(nyb):
            o_ref[:, k * LANE:(k + 1) * LANE] = yout[k]
    else:
        def fin(rb, carry):
            r0 = pl.multiple_of(rb * LANE, LANE)
            y = jnp.concatenate([yout[k, pl.ds(r0, LANE), :] for k in range(nyb)], axis=1)
            y = y + yp_ref[pl.ds(r0, LANE), :]
            y = y * _silu(main_ref[pl.ds(r0, LANE), 0:SSD_W])
            ms = jnp.mean(y * y, axis=-1, keepdims=True)
            o_ref[pl.ds(r0, LANE), :] = y * lax.rsqrt(ms + EPS) * ng_ref[...]
            return carry

        lax.fori_loop(0, CH_ROWS // LANE, fin, 0)


def _ssd_pass(l, direction, cols, n_ctx_rows, cw, cb, dtb, alog, extra, ypart):
    rows = cols.shape[0]
    nc = rows // CH_ROWS
    nc_ctx = n_ctx_rows // CH_ROWS
    hpc = CH_ROWS // SSD_HALO
    n_halo = rows // SSD_HALO
    chunk = functools.partial(_chunk_of, direction, nc_ctx=nc_ctx, nc=nc)
    in_specs = [
        pl.BlockSpec((CH_ROWS, SSD_COLS_P), lambda i: (chunk(i), 0)),
        pl.BlockSpec((SSD_HALO, SSD_COLS_P), lambda i: (jnp.maximum(chunk(i) * hpc - 1, 0), 0)),
        pl.BlockSpec((SSD_HALO, SSD_COLS_P), lambda i: (jnp.minimum((chunk(i) + 1) * hpc, n_halo - 1), 0)),
        _layer_spec(l, (SSD_K, SSD_XBC)),
        _layer_spec(l, (1, SSD_XBC)),
        _layer_spec(l, (1, DT_PAD)),
        _layer_spec(l, (1, DT_PAD)),
        _layer_spec(l, (1, SSD_W)),
    ]
    args = [cols, cols, cols, cw, cb, dtb, alog, extra]
    if direction == 1:
        in_specs.append(pl.BlockSpec((CH_ROWS, SSD_W), lambda i: (chunk(i), 0)))
        args.append(ypart)
    return pl.pallas_call(
        functools.partial(_ssd_kernel, direction, nc_ctx, nc),
        out_shape=jax.ShapeDtypeStruct((rows, SSD_W), F32),
        grid=(nc,),
        in_specs=in_specs,
        out_specs=pl.BlockSpec((CH_ROWS, SSD_W), lambda i: (chunk(i), 0)),
        scratch_shapes=[
            pltpu.VMEM((2, SSD_XBC // LANE, LANE + 2 * SSD_HALO, LANE), F32),
            pltpu.VMEM((SSD_XBC // LANE, CH_ROWS, LANE), F32),
            pltpu.VMEM((CH_ROWS, DT_PAD), F32),
            pltpu.VMEM((SSD_W // LANE, CH_ROWS, LANE), F32),
            pltpu.VMEM((NB, SSD_W // LANE, SSD_N, LANE), F32),
            pltpu.VMEM((NB, 3, SSD_CHUNK, DT_PAD), F32),
        ],
        compiler_params=_cparams(("arbitrary",)),
        name="ssd_fwd" if direction == 0 else "ssd_bwd",
    )(*args)


CONF_HALO = 128


def _conf_kernel(first_chunk, nc_ctx, nc, main_ref, prev_ref, next_ref, w_ref, b_ref, lng_ref, lnb_ref,
                 pw_ref, pwb_ref, o_ref, ext):
    c = pl.program_id(0) + first_chunk
    has_prev = _has_prev(c, nc_ctx)
    has_next = _has_next(c, nc_ctx, nc)

    def glu(ref):
        v = ref[...]
        return v[:, :CONF_W] * jax.nn.sigmoid(v[:, CONF_W:])

    um = glu(main_ref)
    up = jnp.where(has_prev, glu(prev_ref), 0.0)
    un = jnp.where(has_next, glu(next_ref), 0.0)
    nblk = CONF_W // LANE
    for k in range(nblk):
        ext[k, 0:CONF_HALO, :] = up[:, k * LANE:(k + 1) * LANE]
        ext[k, CONF_HALO:CONF_HALO + CH_ROWS, :] = um[:, k * LANE:(k + 1) * LANE]
        ext[k, CONF_HALO + CH_ROWS:, :] = un[:, k * LANE:(k + 1) * LANE]
    off = CONF_HALO - NB * (CONF_K // 2)

    def rbody(rb, carry):
        r0 = pl.multiple_of(rb * LANE, LANE)
        accs = []
        for k in range(nblk):
            lo = k * LANE
            acc = jnp.broadcast_to(b_ref[:, lo:lo + LANE], (LANE, LANE))
            for tap in range(CONF_K):
                acc = acc + ext[k, pl.ds(r0 + NB * tap + off, LANE), :] * w_ref[tap:tap + 1, lo:lo + LANE]
            accs.append(acc)
        u = jnp.concatenate(accs, axis=1)
        xc = u - jnp.mean(u, axis=-1, keepdims=True)
        y = xc * lax.rsqrt(jnp.mean(xc * xc, axis=-1, keepdims=True) + EPS) * lng_ref[...] + lnb_ref[...]
        y = _silu(y).astype(BF16)
        o_ref[pl.ds(r0, LANE), :] = jnp.dot(y, pw_ref[...], preferred_element_type=F32) + pwb_ref[...]
        return carry

    lax.fori_loop(0, CH_ROWS // LANE, rbody, 0)


def _conformer(l, cols, n_ctx_rows, with_ctx, w, b, lng, lnb, pw, pwb):
    rows = cols.shape[0]
    nc = rows // CH_ROWS
    nc_ctx = n_ctx_rows // CH_ROWS
    first = 0 if with_ctx else nc_ctx
    hpc = CH_ROWS // CONF_HALO
    n_halo = rows // CONF_HALO
    return pl.pallas_call(
        functools.partial(_conf_kernel, first, nc_ctx, nc),
        out_shape=jax.ShapeDtypeStruct(((nc - first) * CH_ROWS, CONF_W), F32),
        grid=(nc - first,),
        in_specs=[
            pl.BlockSpec((CH_ROWS, 2 * CONF_W), lambda i: (i + first, 0)),
            pl.BlockSpec((CONF_HALO, 2 * CONF_W), lambda i: (jnp.maximum((i + first) * hpc - 1, 0), 0)),
            pl.BlockSpec((CONF_HALO, 2 * CONF_W), lambda i: (jnp.minimum((i + first + 1) * hpc, n_halo - 1), 0)),
            _layer_spec(l, (CONF_K, CONF_W)),
            _layer_spec(l, (1, CONF_W)),
            _layer_spec(l, (1, CONF_W)),
            _layer_spec(l, (1, CONF_W)),
            _layer_spec(l, (CONF_W, CONF_W)),
            _layer_spec(l, (1, CONF_W)),
        ],
        out_specs=pl.BlockSpec((CH_ROWS, CONF_W), lambda i: (i, 0)),
        scratch_shapes=[pltpu.VMEM((CONF_W // LANE, CH_ROWS + 2 * CONF_HALO, LANE), F32)],
        compiler_params=_cparams(("arbitrary",)),
        name="conformer",
    )(cols, cols, cols, w, b, lng, lnb, pw, pwb)


def _s5_kernel(direction, *refs):
    if direction == 0:
        u_ref, bmat_ref, ar_ref, ai_ref, cmat_ref, o_ref, hbuf, st = refs
    else:
        (u_ref, bmat_ref, ar_ref, ai_ref, cmat_ref, yf_ref, dsk_ref, gw_ref, gb_ref,
         o_ref, hbuf, st) = refs
    i = pl.program_id(0)

    @pl.when(i == 0)
    def _():
        st[...] = jnp.zeros_like(st)

    u = u_ref[...]
    rb = 2 * LANE
    for k in range(u.shape[0] // rb):
        hbuf[k * rb:(k + 1) * rb, :] = jnp.dot(u[k * rb:(k + 1) * rb, :].astype(BF16), bmat_ref[...],
                                                preferred_element_type=F32)
    ar = jnp.broadcast_to(ar_ref[...], (NB, S5_S))
    ai = jnp.broadcast_to(ai_ref[...], (NB, S5_S))

    def step(j, carry):
        hr, hi = carry
        t = j if direction == 0 else S5_TL - 1 - j
        r0 = pl.multiple_of(t * NB, NB)
        nr = ar * hr - ai * hi + hbuf[pl.ds(r0, NB), 0:S5_S]
        ni = ar * hi + ai * hr + hbuf[pl.ds(r0, NB), S5_S:]
        hbuf[pl.ds(r0, NB), 0:S5_S] = nr
        hbuf[pl.ds(r0, NB), S5_S:] = ni
        return nr, ni

    hr, hi = lax.fori_loop(0, S5_TL, step, (st[0], st[1]), unroll=2)
    st[0] = hr
    st[1] = hi
    for k in range(u.shape[0] // rb):
        rows = slice(k * rb, (k + 1) * rb)
        o_ref[rows, :] = jnp.dot(hbuf[rows, :].astype(BF16), cmat_ref[...], preferred_element_type=F32)
    if direction == 1:
        for k in range(u.shape[0] // rb):
            rows = slice(k * rb, (k + 1) * rb)
            v = jax.nn.gelu(o_ref[rows, :] + yf_ref[rows, :] + dsk_ref[...] * u[rows, :], approximate=True)
            gv = jnp.dot(v.astype(BF16), gw_ref[...], preferred_element_type=F32) + gb_ref[...]
            o_ref[rows, :] = gv[:, :S5_W] * jax.nn.sigmoid(gv[:, S5_W:])


def _s5_pass(l, direction, u, n_ctx_rows, bmat, ar, ai, cmat, yf, dsk, gw, gb):
    rows = u.shape[0]
    tr = S5_TL * NB
    nblk = rows // tr
    nb_ctx = n_ctx_rows // tr
    blk = functools.partial(_chunk_of, direction, nc_ctx=nb_ctx, nc=nblk)

    def dir_spec(shape):
        nd = len(shape)
        return pl.BlockSpec((None, None) + tuple(shape), lambda i: (l, direction) + (0,) * nd)

    in_specs = [
        pl.BlockSpec((tr, S5_W), lambda i: (blk(i), 0)),
        dir_spec((S5_W, 2 * S5_S)),
        dir_spec((1, S5_S)),
        dir_spec((1, S5_S)),
        _layer_spec(l, (2 * S5_S, S5_W)),
    ]
    args = [u, bmat, ar, ai, cmat]
    if direction == 1:
        in_specs += [
            pl.BlockSpec((tr, S5_W), lambda i: (blk(i), 0)),
            _layer_spec(l, (1, S5_W)),
            _layer_spec(l, (S5_W, 2 * S5_W)),
            _layer_spec(l, (1, 2 * S5_W)),
        ]
        args += [yf, dsk, gw, gb]
    return pl.pallas_call(
        functools.partial(_s5_kernel, direction),
        out_shape=jax.ShapeDtypeStruct((rows, S5_W), F32),
        grid=(nblk,),
        in_specs=in_specs,
        out_specs=pl.BlockSpec((tr, S5_W), lambda i: (blk(i), 0)),
        scratch_shapes=[pltpu.VMEM((tr, 2 * S5_S), F32), pltpu.VMEM((2, NB, S5_S), F32)],
        compiler_params=_cparams(("arbitrary",)),
        name="s5_fwd" if direction == 0 else "s5_bwd",
    )(*args)


def _s5_operators(lam_re, lam_im, log_step, b_re, b_im, c_re, c_im):
    depth = lam_re.shape[0]
    eye = jnp.eye(S5_G, dtype=F32)
    step = jnp.exp(log_step)[..., None]
    mag = jnp.exp(lam_re * step)
    ar, ai = mag * jnp.cos(lam_im * step), mag * jnp.sin(lam_im * step)
    inv_den = 1.0 / (lam_re * lam_re + lam_im * lam_im)
    cr = ((ar - 1.0) * lam_re + ai * lam_im) * inv_den
    ci = (ai * lam_re - (ar - 1.0) * lam_im) * inv_den
    bre, bim = b_re[:, None], b_im[:, None]
    bbr = cr[..., None] * bre - ci[..., None] * bim
    bbi = cr[..., None] * bim + ci[..., None] * bre
    bd_r = jnp.einsum("ldgph,gk->ldghkp", bbr, eye).reshape(depth, 2, S5_W, S5_S)
    bd_i = jnp.einsum("ldgph,gk->ldghkp", bbi, eye).reshape(depth, 2, S5_W, S5_S)
    bmat = jnp.concatenate([bd_r, bd_i], axis=-1).astype(BF16)
    cre = jnp.einsum("lghp,gk->lgpkh", c_re, eye).reshape(depth, S5_S, S5_W)
    cim = jnp.einsum("lghp,gk->lgpkh", c_im, eye).reshape(depth, S5_S, S5_W)
    cmat = jnp.concatenate([cre, -cim], axis=1).astype(BF16)
    return bmat, ar.reshape(depth, 2, 1, S5_S), ai.reshape(depth, 2, 1, S5_S), cmat


def _route_rows(lg):
    m = jnp.max(lg, axis=0, keepdims=True)
    ex = jnp.exp(lg - m)
    probs = ex / jnp.sum(ex, axis=0, keepdims=True)
    p = [probs[e:e + 1, :] for e in range(NE)]
    tops = []
    for q in range(NGRP):
        v = p[q * EPG:(q + 1) * EPG]
        m1 = functools.reduce(jnp.maximum, v)
        i1 = jnp.full_like(m1, float(EPG - 1))
        for j in range(EPG - 2, -1, -1):
            i1 = jnp.where(v[j] == m1, float(j), i1)
        rest = [jnp.where(i1 == float(j), -jnp.inf, v[j]) for j in range(EPG)]
        m2 = functools.reduce(jnp.maximum, rest)
        i2 = jnp.full_like(m2, float(EPG - 1))
        for j in range(EPG - 2, -1, -1):
            i2 = jnp.where(jnp.logical_and(rest[j] == m2, i1 != float(j)), float(j), i2)
        tops.append((m1, i1, m2, i2))
    score = [t[0] + t[2] for t in tops]
    best = functools.reduce(jnp.maximum, score)
    sel = tops[NGRP - 1] + (jnp.full_like(best, float(NGRP - 1)),)
    for q in range(NGRP - 2, -1, -1):
        hit = score[q] == best
        sel = tuple(jnp.where(hit, a, b) for a, b in zip(tops[q] + (jnp.full_like(best, float(q)),), sel))
    m1, i1, m2, i2, grp = sel
    den = m1 + m2
    return grp * EPG + i1, grp * EPG + i2, m1 / den, m2 / den


def _out_kernel(a_ref, b_ref, s_ref, x_ref, mod_ref, g_ref, w_ref, rw_ref, rb_ref,
                x1_ref, h2_ref, rt_ref, gc_ref, cnt_ref):
    tm = x_ref.shape[0]
    rb = 2 * LANE
    for k in range(tm // rb):
        rows = slice(k * rb, (k + 1) * rb)
        mix = jnp.dot(a_ref[rows, :].astype(BF16), w_ref[0:SSD_W, :], preferred_element_type=F32)
        mix = mix + jnp.dot(b_ref[rows, :].astype(BF16), w_ref[SSD_W:SSD_W + CONF_W, :],
                            preferred_element_type=F32)
        mix = mix + jnp.dot(s_ref[rows, :].astype(BF16), w_ref[SSD_W + CONF_W:, :], preferred_element_type=F32)
        x1 = (x_ref[rows, :].reshape(rb // NB, NB, D) + mod_ref[2][None] * mix.reshape(rb // NB, NB, D))
        x1 = x1.reshape(rb, D)
        x1_ref[rows, :] = x1
        ms = jnp.mean(x1 * x1, axis=-1, keepdims=True)
        xn = x1 * lax.rsqrt(ms + EPS) * g_ref[...]
        h2_ref[rows, :] = (xn.reshape(rb // NB, NB, D) * (1.0 + mod_ref[4])[None] + mod_ref[3][None]).reshape(rb, D)
    lg = lax.dot_general(rw_ref[...], h2_ref[...], (((1,), (1,)), ((), ())), precision=HIGHEST,
                         preferred_element_type=F32) + rb_ref[...]
    e1, e2, g1, g2 = _route_rows(lg)
    eid = lax.broadcasted_iota(I32, (NE, tm), 0).astype(F32)
    oh0 = (eid == e1).astype(F32)
    oh1 = (eid == e2).astype(F32)
    t0 = lax.broadcasted_iota(I32, (tm, tm), 0)
    t1 = lax.broadcasted_iota(I32, (tm, tm), 1)
    before = (t0 < t1).astype(BF16)
    pre0 = jnp.dot(oh0.astype(BF16), before, preferred_element_type=F32)
    pre1 = jnp.dot(oh1.astype(BF16), before, preferred_element_type=F32)
    lr0 = jnp.sum(oh0 * pre0, axis=0, keepdims=True)
    lr1 = jnp.sum(oh1 * pre1, axis=0, keepdims=True)
    rt_ref[...] = jnp.concatenate([e1, e2, lr0, lr1, jnp.zeros((NB - 4, tm), F32)], axis=0)
    gates = jnp.concatenate([g1, g2, jnp.zeros((LANE - 2, tm), F32)], axis=0)
    gc_ref[...] = gates.T
    cnt = jnp.concatenate([jnp.sum(oh0, axis=1, keepdims=True), jnp.sum(oh1, axis=1, keepdims=True)], axis=0)
    cnt_ref[...] = jnp.broadcast_to(cnt, (2 * NE, LANE))


def _out_proj(l, a, bconf, s, x, modtabs, g_ffn, w_out, rw_t, rb, n_ctx_rows, conf_first, blk0, nblk):
    tm = ROW_TM
    ctx_blk = n_ctx_rows // tm
    rows = nblk * tm
    return pl.pallas_call(
        _out_kernel,
        out_shape=[
            jax.ShapeDtypeStruct((rows, D), F32),
            jax.ShapeDtypeStruct((rows, D), F32),
            jax.ShapeDtypeStruct((NB, rows), F32),
            jax.ShapeDtypeStruct((rows, LANE), F32),
            jax.ShapeDtypeStruct((nblk, 2 * NE, LANE), F32),
        ],
        grid=(nblk,),
        in_specs=[
            pl.BlockSpec((tm, SSD_W), lambda i: (i + blk0, 0)),
            pl.BlockSpec((tm, CONF_W), lambda i: (i + blk0 - conf_first, 0)),
            pl.BlockSpec((tm, S5_W), lambda i: (i + blk0, 0)),
            pl.BlockSpec((tm, D), lambda i: (i + blk0, 0)),
            _mod_spec(l, ctx_blk, blk0),
            _layer_spec(l, (1, D)),
            _layer_spec(l, (D, D)),
            _const_spec((NE, D)),
            _const_spec((NE, 1)),
        ],
        out_specs=[
            pl.BlockSpec((tm, D), lambda i: (i, 0)),
            pl.BlockSpec((tm, D), lambda i: (i, 0)),
            pl.BlockSpec((NB, tm), lambda i: (0, i)),
            pl.BlockSpec((tm, LANE), lambda i: (i, 0)),
            pl.BlockSpec((None, 2 * NE, LANE), lambda i: (i, 0, 0)),
        ],
        compiler_params=_cparams(("arbitrary",)),
        name="out_proj_route",
    )(a, bconf, s, x, modtabs, g_ffn, w_out, rw_t, rb)


def _moe_kernel(second, it_ref, ie_ref, lo_ref, hi_ref, h_ref, wg_ref, wu_ref, wd_ref, *rest):
    rest = rest[1:] if second else rest
    out_ref, wgb, wub, wdb, cur = rest
    o_ref = out_ref if second else out_ref.at[0]
    w = pl.program_id(0)
    lo = lo_ref[w]
    hi = hi_ref[w]
    new_tile = jnp.logical_or(w == 0, it_ref[w] != it_ref[jnp.maximum(w - 1, 0)])

    @pl.when(w == 0)
    def _():
        cur[0] = -1

    @pl.when(new_tile)
    def _():
        out_ref[...] = jnp.zeros_like(out_ref)

    @pl.when(hi > lo)
    def _():
        @pl.when(cur[0] != ie_ref[w])
        def _():
            wgb[...] = wg_ref[...].astype(BF16)
            wub[...] = wu_ref[...].astype(BF16)
            wdb[...] = wd_ref[...].astype(BF16)
            cur[0] = ie_ref[w]

        h = h_ref[...].astype(BF16)
        gate = jnp.dot(h, wgb[...], preferred_element_type=F32)
        up = jnp.dot(h, wub[...], preferred_element_type=F32)
        act = (_silu(gate) * up).astype(BF16)
        y = jnp.dot(act, wdb[...], preferred_element_type=F32)
        r = lax.broadcasted_iota(I32, (MOE_TM, 1), 0)
        keep = jnp.logical_and(r >= lo, r < hi)
        o_ref[...] = jnp.where(keep, y, o_ref[...])


def _moe_experts(l, hs, items, tile0, w_gate, w_up, w_down, prev):
    item_tile, item_expert, item_lo, item_hi = items
    n_items = item_tile.shape[0]
    n_prefetch = 4
    second = prev is not None

    def wspec():
        return pl.BlockSpec((None, None, D, D), lambda w, it, ie, lo, hi: (l, ie[w], 0, 0))

    in_specs = [pl.BlockSpec((MOE_TM, D), lambda w, it, ie, lo, hi: (it[w] - tile0, 0)), wspec(), wspec(), wspec()]
    args = [item_tile, item_expert, item_lo, item_hi, hs, w_gate, w_up, w_down]
    aliases = {}
    if second:
        in_specs.append(pl.BlockSpec(memory_space=pl.ANY))
        args.append(prev)
        aliases = {len(args) - 1: 0}
        out_spec = pl.BlockSpec((None, MOE_TM, D), lambda w, it, ie, lo, hi: (1, it[w] - tile0, 0))
    else:
        out_spec = pl.BlockSpec((2, MOE_TM, D), lambda w, it, ie, lo, hi: (0, it[w] - tile0, 0))
    return pl.pallas_call(
        functools.partial(_moe_kernel, second),
        out_shape=jax.ShapeDtypeStruct((2, hs.shape[0], D), F32),
        grid_spec=pltpu.PrefetchScalarGridSpec(
            num_scalar_prefetch=n_prefetch,
            grid=(n_items,),
            in_specs=in_specs,
            out_specs=out_spec,
            scratch_shapes=[pltpu.VMEM((D, D), BF16)] * 3 + [pltpu.SMEM((1,), I32)],
        ),
        input_output_aliases=aliases,
        compiler_params=_cparams(("arbitrary",)),
        name="moe_experts",
    )(*args)


def _moe_plan(route, cnt, tile_parts):
    rows = route.shape[1]
    nblk = cnt.shape[0]
    n_flat = 2 * rows
    e = route[0:2].astype(I32)
    lrank = route[2:4].astype(I32)
    counts = cnt[:, :, 0].astype(I32).reshape(nblk, 2, NE)
    per = jnp.transpose(counts, (1, 0, 2)).reshape(2 * nblk, NE)
    before = jnp.cumsum(per, axis=0) - per
    gcount = jnp.sum(per, axis=0)
    gend = jnp.cumsum(gcount)
    gstart = gend - gcount
    base = jnp.transpose((before + gstart[None, :]).reshape(2, nblk, NE), (2, 0, 1))[..., None]
    onehot = e.reshape(1, 2, nblk, ROW_TM) == jnp.arange(NE, dtype=I32).reshape(NE, 1, 1, 1)
    inv = jnp.sum(jnp.where(onehot, base, 0), axis=0).reshape(2, rows) + lrank
    keys = (e * 65536).reshape(-1) + jnp.arange(n_flat, dtype=I32)
    order = lax.sort(keys, is_stable=False) & 0xFFFF
    tok = jnp.where(order >= rows, order - rows, order)
    items = []
    for t0, tn in tile_parts:
        row_lo, row_hi = t0 * MOE_TM, (t0 + tn) * MOE_TM
        tile_start = (t0 + jnp.arange(tn, dtype=I32)) * MOE_TM
        bnd = lax.sort(jnp.concatenate([tile_start, jnp.clip(gstart[1:], row_lo, row_hi)]), is_stable=False)
        bnd_hi = jnp.concatenate([bnd[1:], jnp.full((1,), row_hi, I32)])
        item_tile = jnp.minimum(bnd // MOE_TM, t0 + tn - 1)
        first_row = jnp.minimum(bnd, row_hi - 1)[:, None]
        item_expert = jnp.minimum(jnp.sum((gend[None, :] <= first_row).astype(I32), axis=1), NE - 1)
        items.append((item_tile, item_expert, bnd - item_tile * MOE_TM, bnd_hi - item_tile * MOE_TM))
    return tok, inv, items


def _final_kernel(x_ref, ya_ref, yb_ref, gc_ref, mod_ref, g_ref, o_ref, scr):
    tm = x_ref.shape[0]
    gc = gc_ref[...]
    ff = (gc[:, 0:1] * ya_ref[...] + gc[:, 1:2] * yb_ref[...]).reshape(tm // NB, NB, D)
    x2 = (x_ref[...].reshape(tm // NB, NB, D) + mod_ref[5][None] * ff).reshape(tm, D)
    ms = jnp.mean(x2 * x2, axis=-1, keepdims=True)
    x2 = x2 * lax.rsqrt(ms + EPS) * g_ref[...]
    ncb = D // LANE
    for k in range(ncb):
        scr[k] = x2[:, k * LANE:(k + 1) * LANE]
    for b in range(NB):
        for k in range(ncb):
            o_ref[b, :, k * LANE:(k + 1) * LANE] = scr[k, pl.ds(b, ROW_TQ, stride=NB), :]


def _final_combine(l, x1, ya, yb, gcol, modtabs, g_final):
    tm = ROW_TM
    rows = x1.shape[0]
    row_spec = pl.BlockSpec((tm, D), lambda i: (i, 0))
    return pl.pallas_call(
        _final_kernel,
        out_shape=jax.ShapeDtypeStruct((NB, rows // NB, D), F32),
        grid=(rows // tm,),
        in_specs=[row_spec, row_spec, row_spec, pl.BlockSpec((tm, LANE), lambda i: (i, 0)),
                  _mod_spec(l, 0), _const_spec((1, D))],
        out_specs=pl.BlockSpec((NB, ROW_TQ, D), lambda i: (0, i, 0)),
        scratch_shapes=[pltpu.VMEM((D // LANE, tm, LANE), F32)],
        compiler_params=_cparams(("arbitrary",)),
        name="moe_combine_final",
    )(x1, ya, yb, gcol, modtabs, g_final)


def _grid_pos_embed(rows_n):
    rr, cc = jnp.meshgrid(jnp.arange(rows_n, dtype=F32), jnp.arange(GRID_W, dtype=F32), indexing="ij")
    quarter = D // 4
    inv_freq = jnp.exp(-math.log(10000.0) * jnp.arange(quarter, dtype=F32) / quarter)

    def emb(pos):
        ang = pos.reshape(-1)[:, None] * inv_freq[None, :]
        return jnp.concatenate([jnp.sin(ang), jnp.cos(ang)], axis=-1)

    return jnp.concatenate([emb(rr), emb(cc)], axis=-1)


def _take_rows(a, idx):
    return a.at[idx].get(mode="promise_in_bounds")


def _pad_last(v, width):
    return jnp.pad(v, [(0, 0)] * (v.ndim - 1) + [(0, width - v.shape[-1])])


def kernel(x, c, ctx, c_ctx, w_ada, b_ada, g_mix, w_in, ssd_conv_w, ssd_conv_b, ssd_dt_bias, ssd_a_log, ssd_d, ssd_norm_g, conf_dw_w, conf_dw_b, conf_ln_g, conf_ln_b, conf_pw_w, conf_pw_b, s5_lambda_re, s5_lambda_im, s5_log_step, s5_b_re, s5_b_im, s5_c_re, s5_c_im, s5_d, s5_glu_w, s5_glu_b, w_out, g_ffn, router_w, router_b, exp_w_gate, exp_w_up, exp_w_down, g_final):
    bsz, seq, _ = x.shape
    ctx_len = ctx.shape[1]
    depth = w_ada.shape[0]
    assert bsz == NB and seq % SSD_CHUNK == 0 and ctx_len % SSD_CHUNK == 0
    n_ctx_rows = ctx_len * NB
    rows_all = (ctx_len + seq) * NB

    cond = jnp.concatenate([c, c_ctx[None, :], jnp.zeros((2 * NB - bsz - 1, D), F32)], axis=0)
    mod = _modulation(cond, w_ada, b_ada)
    mod = mod.reshape(depth, 2 * NB, N_MOD, D)
    mod_lat = jnp.transpose(mod[:, :NB], (0, 2, 1, 3))
    mod_ctx = jnp.broadcast_to(mod[:, NB][:, :, None, :], (depth, N_MOD, NB, D))
    modtabs = jnp.stack([mod_ctx, mod_lat], axis=1)

    o_b = SSD_W + SSD_XBC + 2 * SSD_H
    w_in_p = jnp.concatenate(
        [w_in[:, :, :o_b], jnp.zeros((depth, D, DT_PAD - 2 * SSD_H), F32), w_in[:, :, o_b:]], axis=2).astype(BF16)
    g_mix3 = g_mix.reshape(depth, 1, D)
    ssd_cb = ssd_conv_b.reshape(depth, 1, SSD_XBC)
    ssd_dtb = _pad_last(ssd_dt_bias.reshape(depth, 1, 2 * SSD_H), DT_PAD)
    ssd_alog = _pad_last(ssd_a_log.reshape(depth, 1, 2 * SSD_H), DT_PAD)
    ssd_dsk = jnp.repeat(ssd_d, SSD_P, axis=1).reshape(depth, 1, SSD_W)
    ssd_ng = ssd_norm_g.reshape(depth, 1, SSD_W)
    conf_b = conf_dw_b.reshape(depth, 1, CONF_W)
    conf_lg = conf_ln_g.reshape(depth, 1, CONF_W)
    conf_lb = conf_ln_b.reshape(depth, 1, CONF_W)
    conf_pw = conf_pw_w.astype(BF16)
    conf_pb = conf_pw_b.reshape(depth, 1, CONF_W)
    s5_bmat, s5_ar, s5_ai, s5_cmat = _s5_operators(s5_lambda_re, s5_lambda_im, s5_log_step,
                                                   s5_b_re, s5_b_im, s5_c_re, s5_c_im)
    s5_dsk = s5_d.reshape(depth, 1, S5_W)
    s5_gw = s5_glu_w.astype(BF16)
    s5_gb = s5_glu_b.reshape(depth, 1, 2 * S5_W)
    w_out_b = w_out.astype(BF16)
    g_ffn3 = g_ffn.reshape(depth, 1, D)
    router_w_t = router_w.T
    router_b2 = router_b.reshape(NE, 1)
    g_final2 = g_final.reshape(1, D)
    pos = _grid_pos_embed(seq // GRID_W)

    pending = (ctx, x)
    for l in range(depth):
        last = l == depth - 1
        ssd_cols, conf_cols, s5_u, xall = _in_proj(l, l == 0, pending, pos, modtabs, g_mix3, w_in_p,
                                                   rows_all, n_ctx_rows)
        ypart = _ssd_pass(l, 0, ssd_cols, n_ctx_rows, ssd_conv_w, ssd_cb, ssd_dtb, ssd_alog, ssd_dsk, None)
        a_mix = _ssd_pass(l, 1, ssd_cols, n_ctx_rows, ssd_conv_w, ssd_cb, ssd_dtb, ssd_alog, ssd_ng, ypart)
        b_mix = _conformer(l, conf_cols, n_ctx_rows, not last, conf_dw_w, conf_b, conf_lg, conf_lb, conf_pw, conf_pb)
        yf = _s5_pass(l, 0, s5_u, n_ctx_rows, s5_bmat, s5_ar, s5_ai, s5_cmat, None, None, None, None)
        s_mix = _s5_pass(l, 1, s5_u, n_ctx_rows, s5_bmat, s5_ar, s5_ai, s5_cmat, yf, s5_dsk, s5_gw, s5_gb)
        ctx_blk = n_ctx_rows // ROW_TM
        first = ctx_blk if last else 0
        n_out = rows_all // ROW_TM - first
        x1, h2, route, gcol, cnt = _out_proj(l, a_mix, b_mix, s_mix, xall, modtabs, g_ffn3, w_out_b,
                                             router_w_t, router_b2, n_ctx_rows, first, first, n_out)
        n_tiles = 2 * n_out * ROW_TM // MOE_TM
        assert n_tiles % 2 == 0
        tile_parts = [(0, n_tiles // 2), (n_tiles // 2, n_tiles // 2)]
        tok, inv, items = _moe_plan(route, cnt, tile_parts)
        hs = [_take_rows(h2, tok[t0 * MOE_TM:(t0 + tn) * MOE_TM]) for t0, tn in tile_parts]
        y = None
        for (t0, tn), h, it in zip(tile_parts, hs, items):
            y = _moe_experts(l, h, it, t0, exp_w_gate, exp_w_up, exp_w_down, y)
        y = y.reshape(n_tiles * MOE_TM, D)
        ya = _take_rows(y, inv[0])
        yb = _take_rows(y, inv[1])
        pending = (x1, ya, yb, gcol)
    return _final_combine(depth - 1, *pending, modtabs, g_final2)
```

```python
import functools
import math

import jax
import jax.numpy as jnp
from jax import lax
from jax.experimental import pallas as pl
from jax.experimental.pallas import tpu as pltpu

F32 = jnp.float32
BF16 = jnp.bfloat16
I32 = jnp.int32
HIGHEST = lax.Precision.HIGHEST

NB = 8
D = 1024
GRID_W = 64
N_MOD = 6
EPS = 1e-6
LANE = 128
SSD_W = 512
SSD_P = 64
SSD_H = 8
SSD_G = 2
SSD_N = 128
SSD_K = 5
SSD_XBC = SSD_W + 2 * SSD_G * SSD_N
SSD_CHUNK = 128
DT_PAD = LANE
SSD_COLS_P = SSD_W + SSD_XBC + DT_PAD
CONF_W = 256
CONF_K = 31
S5_W = 256
S5_G = 16
S5_P = 64
S5_CH = 16
S5_S = S5_G * S5_P
S5_TL = 128
NE = 16
NGRP = 4
EPG = 4
MOE_TM = 256
IN_COLS_P = SSD_COLS_P + 2 * CONF_W + S5_W
ROW_TM = 512
ROW_TQ = ROW_TM // NB
CH_ROWS = SSD_CHUNK * NB
VMEM_LIMIT = 56 * 1024 * 1024


def _cparams(sem):
    return pltpu.CompilerParams(dimension_semantics=sem, vmem_limit_bytes=VMEM_LIMIT)


def _const_spec(shape):
    nd = len(shape)
    return pl.BlockSpec(shape, lambda *_: (0,) * nd)


def _layer_spec(l, shape):
    nd = len(shape)
    return pl.BlockSpec((None,) + tuple(shape), lambda *_: (l,) + (0,) * nd)


def _silu(v):
    return v * jax.nn.sigmoid(v)


def _mod_kernel(c_ref, w_ref, b_ref, o_ref):
    c = c_ref[...]
    h = _silu(c).astype(BF16)
    o_ref[...] = jnp.dot(h, w_ref[...].astype(BF16), preferred_element_type=F32) + b_ref[...]


def _modulation(cond, w_ada, b_ada):
    depth = w_ada.shape[0]
    nrow = cond.shape[0]
    return pl.pallas_call(
        _mod_kernel,
        out_shape=jax.ShapeDtypeStruct((depth, nrow, N_MOD * D), F32),
        grid=(depth, N_MOD),
        in_specs=[
            pl.BlockSpec((nrow, D), lambda l, j: (0, 0)),
            pl.BlockSpec((None, D, D), lambda l, j: (l, 0, j)),
            pl.BlockSpec((None, 1, D), lambda l, j: (l, 0, j)),
        ],
        out_specs=pl.BlockSpec((None, nrow, D), lambda l, j: (l, 0, j)),
        compiler_params=_cparams(("arbitrary", "arbitrary")),
        name="adaln_mod",
    )(cond, w_ada, b_ada.reshape(depth, 1, N_MOD * D))


def _mod_spec(l, ctx_blk, first=0):
    return pl.BlockSpec((None, None, N_MOD, NB, D),
                        lambda i, *_: (l, jnp.where(i + first < ctx_blk, 0, 1), 0, 0, 0))


def _norm_mod_project(x, mod_ref, g_ref, w_ref, ssd_ref, conf_ref, s5_ref):
    tm = x.shape[0]
    ms = jnp.mean(x * x, axis=-1, keepdims=True)
    xn = x * lax.rsqrt(ms + EPS) * g_ref[...]
    h = xn.reshape(tm // NB, NB, D) * (1.0 + mod_ref[1])[None] + mod_ref[0][None]
    h = h.reshape(tm, D).astype(BF16)
    ssd_ref[...] = jnp.dot(h, w_ref[:, :SSD_COLS_P], preferred_element_type=F32)
    conf_ref[...] = jnp.dot(h, w_ref[:, SSD_COLS_P:SSD_COLS_P + 2 * CONF_W], preferred_element_type=F32)
    s5_ref[...] = jnp.dot(h, w_ref[:, SSD_COLS_P + 2 * CONF_W:], preferred_element_type=F32)


def _in_first_kernel(ctx_blk, ctx_ref, x_ref, pos_ref, mod_ref, g_ref, w_ref,
                     ssd_ref, conf_ref, s5_ref, x0_ref, scr):
    i = pl.program_id(0)
    ncb = D // LANE

    @pl.when(i < ctx_blk)
    def _():
        for b in range(NB):
            for k in range(ncb):
                scr[k, pl.ds(b, ROW_TQ, stride=NB), :] = ctx_ref[b, :, k * LANE:(k + 1) * LANE]

    @pl.when(i >= ctx_blk)
    def _():
        for b in range(NB):
            for k in range(ncb):
                scr[k, pl.ds(b, ROW_TQ, stride=NB), :] = (
                    x_ref[b, :, k * LANE:(k + 1) * LANE] + pos_ref[:, k * LANE:(k + 1) * LANE])

    x = jnp.concatenate([scr[k] for k in range(ncb)], axis=1)
    x0_ref[...] = x
    _norm_mod_project(x, mod_ref, g_ref, w_ref, ssd_ref, conf_ref, s5_ref)


def _in_next_kernel(x1_ref, ya_ref, yb_ref, gc_ref, modp_ref, mod_ref, g_ref, w_ref,
                    ssd_ref, conf_ref, s5_ref, x_ref):
    tm = x1_ref.shape[0]
    gc = gc_ref[...]
    ff = (gc[:, 0:1] * ya_ref[...] + gc[:, 1:2] * yb_ref[...]).reshape(tm // NB, NB, D)
    x = (x1_ref[...].reshape(tm // NB, NB, D) + modp_ref[5][None] * ff).reshape(tm, D)
    x_ref[...] = x
    _norm_mod_project(x, mod_ref, g_ref, w_ref, ssd_ref, conf_ref, s5_ref)


def _in_proj(l, first, x_or_pair, pos, modtabs, g_mix, w_in_p, rows, n_ctx_rows):
    tm = ROW_TM
    nblk = rows // tm
    ctx_blk = n_ctx_rows // tm
    common_specs = [_mod_spec(l, ctx_blk), _layer_spec(l, (1, D)), _layer_spec(l, (D, IN_COLS_P))]
    common_args = [modtabs, g_mix, w_in_p]
    out_shape = [
        jax.ShapeDtypeStruct((rows, SSD_COLS_P), F32),
        jax.ShapeDtypeStruct((rows, 2 * CONF_W), F32),
        jax.ShapeDtypeStruct((rows, S5_W), F32),
    ]
    out_specs = [
        pl.BlockSpec((tm, SSD_COLS_P), lambda i: (i, 0)),
        pl.BlockSpec((tm, 2 * CONF_W), lambda i: (i, 0)),
        pl.BlockSpec((tm, S5_W), lambda i: (i, 0)),
    ]
    if first:
        ctx, x = x_or_pair
        body = functools.partial(_in_first_kernel, ctx_blk)
        in_specs = [
            pl.BlockSpec((NB, ROW_TQ, D), lambda i: (0, jnp.minimum(i, ctx_blk - 1), 0)),
            pl.BlockSpec((NB, ROW_TQ, D), lambda i: (0, jnp.maximum(i - ctx_blk, 0), 0)),
            pl.BlockSpec((ROW_TQ, D), lambda i: (jnp.maximum(i - ctx_blk, 0), 0)),
        ] + common_specs
        args = [ctx, x, pos] + common_args
        scratch = [pltpu.VMEM((D // LANE, tm, LANE), F32)]
    else:
        x1, ya, yb, gcol = x_or_pair
        body = _in_next_kernel
        in_specs = [
            pl.BlockSpec((tm, D), lambda i: (i, 0)),
            pl.BlockSpec((tm, D), lambda i: (i, 0)),
            pl.BlockSpec((tm, D), lambda i: (i, 0)),
            pl.BlockSpec((tm, LANE), lambda i: (i, 0)),
            _mod_spec(l - 1, ctx_blk),
        ] + common_specs
        args = [x1, ya, yb, gcol, modtabs] + common_args
        scratch = []
    out_shape.append(jax.ShapeDtypeStruct((rows, D), F32))
    out_specs.append(pl.BlockSpec((tm, D), lambda i: (i, 0)))
    return pl.pallas_call(
        body,
        out_shape=out_shape,
        grid=(nblk,),
        in_specs=in_specs,
        out_specs=out_specs,
        scratch_shapes=scratch,
        compiler_params=_cparams(("arbitrary",)),
        name="in_proj",
    )(*args)


def _chunk_of(direction, i, nc_ctx, nc):
    if direction == 0:
        return i
    return jnp.where(i < nc_ctx, nc_ctx - 1 - i, nc - 1 - (i - nc_ctx))


def _has_prev(c, nc_ctx):
    return jnp.logical_and(c != 0, c != nc_ctx)


def _has_next(c, nc_ctx, nc):
    return jnp.logical_and(c != nc_ctx - 1, c != nc - 1)


SSD_HALO = 2 * NB


def _ssd_kernel(direction, nc_ctx, nc, *refs):
    if direction == 0:
        (main_ref, prev_ref, next_ref, cw_ref, cb_ref, dtb_ref, alog_ref, dsk_ref,
         o_ref, ext, act, dts, yout, state, tabs) = refs
    else:
        (main_ref, prev_ref, next_ref, cw_ref, cb_ref, dtb_ref, alog_ref, ng_ref, yp_ref,
         o_ref, ext, act, dts, yout, state, tabs) = refs
    i = pl.program_id(0)
    c = _chunk_of(direction, i, nc_ctx, nc)

    @pl.when(i == 0)
    def _():
        state[...] = jnp.zeros_like(state)

    has_prev = _has_prev(c, nc_ctx)
    has_next = _has_next(c, nc_ctx, nc)
    nblk = SSD_XBC // LANE
    n_rb = CH_ROWS // LANE
    for k in range(nblk):
        lo = SSD_W + k * LANE
        ext[0, k, 0:SSD_HALO, :] = jnp.where(has_prev, prev_ref[:, lo:lo + LANE], 0.0)
        ext[0, k, SSD_HALO:, :] = main_ref[0:LANE + SSD_HALO, lo:lo + LANE]
        ext[1, k, 0:LANE + SSD_HALO, :] = main_ref[CH_ROWS - LANE - SSD_HALO:, lo:lo + LANE]
        ext[1, k, LANE + SSD_HALO:, :] = jnp.where(has_next, next_ref[:, lo:lo + LANE], 0.0)

    def conv_block(tap_rows, r0):
        for k in range(nblk):
            lo = k * LANE
            acc = jnp.broadcast_to(cb_ref[:, lo:lo + LANE], (LANE, LANE))
            for tap in range(SSD_K):
                acc = acc + tap_rows(k, tap) * cw_ref[tap:tap + 1, lo:lo + LANE]
            act[k, pl.ds(r0, LANE), :] = _silu(acc)

    def conv_rb(rb, carry):
        r0 = pl.multiple_of(rb * LANE, LANE)
        conv_block(lambda k, tap: main_ref[pl.ds(r0 + NB * tap - SSD_HALO, LANE),
                                           SSD_W + k * LANE:SSD_W + (k + 1) * LANE], r0)
        return carry

    conv_block(lambda k, tap: ext[0, k, NB * tap:NB * tap + LANE, :], 0)
    lax.fori_loop(1, n_rb - 1, conv_rb, 0)
    conv_block(lambda k, tap: ext[1, k, NB * tap:NB * tap + LANE, :], CH_ROWS - LANE)

    raw = main_ref[:, SSD_W + SSD_XBC:] + dtb_ref[...]
    dts[...] = jnp.maximum(raw, 0.0) + jnp.log1p(jnp.exp(-jnp.abs(raw)))
    a_row = -jnp.exp(alog_ref[...])

    tt = lax.broadcasted_iota(I32, (SSD_CHUNK, SSD_CHUNK), 0)
    ss = lax.broadcasted_iota(I32, (SSD_CHUNK, SSD_CHUNK), 1)
    mask = (ss <= tt) if direction == 0 else (ss >= tt)
    tmat = mask.astype(F32)
    last = SSD_CHUNK - 1 if direction == 0 else 0
    hpg = SSD_H // SSD_G
    lane_id = lax.broadcasted_iota(I32, (1, LANE), 1)
    half_lo = lane_id < SSD_P

    def decay_tables(b, carry):
        dt_b = dts[pl.ds(b, SSD_CHUNK, stride=NB), :]
        cs = jnp.dot(tmat, dt_b * a_row, precision=HIGHEST, preferred_element_type=F32)
        tabs[b, 0] = cs
        tabs[b, 1] = cs.T
        tabs[b, 2] = dt_b.T
        return carry

    lax.fori_loop(0, NB, decay_tables, 0, unroll=True)

    def per_batch(b, carry):
        sl = pl.ds(b, SSD_CHUNK, stride=NB)
        cs = tabs[b, 0]
        cs_t = tabs[b, 1]
        dt_t = tabs[b, 2]
        for g in range(SSD_G):
            bg_t = act[SSD_W // LANE + g, sl, :].T
            cg = act[SSD_W // LANE + SSD_G + g, sl, :]
            gmat = jnp.dot(cg.astype(BF16), bg_t.astype(BF16), preferred_element_type=F32)
            for pr in range(hpg // 2):
                k = g * (hpg // 2) + pr
                xs_pair = act[k, sl, :]
                s_pair = state[b, k]
                xs_b = xs_pair.astype(BF16)
                rhs = jnp.concatenate([xs_b, s_pair.astype(BF16)], axis=0)
                ys, ss, etot = [], [], []
                for j in range(2):
                    ln = direction * SSD_H + 2 * k + j
                    colb = jnp.broadcast_to(cs[:, ln:ln + 1], (SSD_CHUNK, SSD_CHUNK))
                    row = cs_t[ln:ln + 1, :]
                    dtr = dt_t[ln:ln + 1, :]
                    decay = jnp.where(mask, jnp.exp(jnp.where(mask, colb - row, 0.0)) * dtr, 0.0)
                    lhs = jnp.concatenate([(gmat * decay).astype(BF16),
                                           (cg * jnp.exp(colb)).astype(BF16)], axis=1)
                    ys.append(jnp.dot(lhs, rhs, preferred_element_type=F32))
                    tot = cs[last:last + 1, ln:ln + 1]
                    wrow = jnp.exp(tot - row) * dtr
                    ss.append(jnp.dot((bg_t * wrow).astype(BF16), xs_b, preferred_element_type=F32))
                    etot.append(jnp.exp(tot))
                acc_y = jnp.where(half_lo, ys[0], ys[1])
                state[b, k] = s_pair * jnp.where(half_lo, etot[0], etot[1]) + jnp.where(half_lo, ss[0], ss[1])
                if direction == 0:
                    acc_y = acc_y + dsk_ref[:, k * LANE:(k + 1) * LANE] * xs_pair
                yout[k, sl, :] = acc_y
        return carry

    lax.fori_loop(0, NB, per_batch, 0, unroll=2)

    nyb = SSD_W // LANE
    if direction == 0:
        for k in range(nyb):
            o_ref[:, k * LANE:(k + 1) * LANE] = yout[k]
    else:
        def fin(rb, carry):
            r0 = pl.multiple_of(rb * LANE, LANE)
            y = jnp.concatenate([yout[k, pl.ds(r0, LANE), :] for k in range(nyb)], axis=1)
            y = y + yp_ref[pl.ds(r0, LANE), :]
            y = y * _silu(main_ref[pl.ds(r0, LANE), 0:SSD_W])
            ms = jnp.mean(y * y, axis=-1, keepdims=True)
            o_ref[pl.ds(r0, LANE), :] = y * lax.rsqrt(ms + EPS) * ng_ref[...]
            return carry

        lax.fori_loop(0, CH_ROWS // LANE, fin, 0, unroll=4)


def _ssd_pass(l, direction, cols, n_ctx_rows, cw, cb, dtb, alog, extra, ypart):
    rows = cols.shape[0]
    nc = rows // CH_ROWS
    nc_ctx = n_ctx_rows // CH_ROWS
    hpc = CH_ROWS // SSD_HALO
    n_halo = rows // SSD_HALO
    chunk = functools.partial(_chunk_of, direction, nc_ctx=nc_ctx, nc=nc)
    in_specs = [
        pl.BlockSpec((CH_ROWS, SSD_COLS_P), lambda i: (chunk(i), 0)),
        pl.BlockSpec((SSD_HALO, SSD_COLS_P), lambda i: (jnp.maximum(chunk(i) * hpc - 1, 0), 0)),
        pl.BlockSpec((SSD_HALO, SSD_COLS_P), lambda i: (jnp.minimum((chunk(i) + 1) * hpc, n_halo - 1), 0)),
        _layer_spec(l, (SSD_K, SSD_XBC)),
        _layer_spec(l, (1, SSD_XBC)),
        _layer_spec(l, (1, DT_PAD)),
        _layer_spec(l, (1, DT_PAD)),
        _layer_spec(l, (1, SSD_W)),
    ]
    args = [cols, cols, cols, cw, cb, dtb, alog, extra]
    if direction == 1:
        in_specs.append(pl.BlockSpec((CH_ROWS, SSD_W), lambda i: (chunk(i), 0)))
        args.append(ypart)
    return pl.pallas_call(
        functools.partial(_ssd_kernel, direction, nc_ctx, nc),
        out_shape=jax.ShapeDtypeStruct((rows, SSD_W), F32),
        grid=(nc,),
        in_specs=in_specs,
        out_specs=pl.BlockSpec((CH_ROWS, SSD_W), lambda i: (chunk(i), 0)),
        scratch_shapes=[
            pltpu.VMEM((2, SSD_XBC // LANE, LANE + 2 * SSD_HALO, LANE), F32),
            pltpu.VMEM((SSD_XBC // LANE, CH_ROWS, LANE), F32),
            pltpu.VMEM((CH_ROWS, DT_PAD), F32),
            pltpu.VMEM((SSD_W // LANE, CH_ROWS, LANE), F32),
            pltpu.VMEM((NB, SSD_W // LANE, SSD_N, LANE), F32),
            pltpu.VMEM((NB, 3, SSD_CHUNK, DT_PAD), F32),
        ],
        compiler_params=_cparams(("arbitrary",)),
        name="ssd_fwd" if direction == 0 else "ssd_bwd",
    )(*args)


CONF_HALO = 128
CONF_RB = 256


def _conf_kernel(first_chunk, nc_ctx, nc, main_ref, prev_ref, next_ref, w_ref, b_ref, lng_ref, lnb_ref,
                 pw_ref, pwb_ref, o_ref, ext, cbuf):
    c = pl.program_id(0) + first_chunk
    has_prev = _has_prev(c, nc_ctx)
    has_next = _has_next(c, nc_ctx, nc)

    def glu(ref):
        v = ref[...]
        return v[:, :CONF_W] * jax.nn.sigmoid(v[:, CONF_W:])

    um = glu(main_ref)
    up = jnp.where(has_prev, glu(prev_ref), 0.0)
    un = jnp.where(has_next, glu(next_ref), 0.0)
    nblk = CONF_W // LANE
    for k in range(nblk):
        ext[k, 0:CONF_HALO, :] = up[:, k * LANE:(k + 1) * LANE]
        ext[k, CONF_HALO:CONF_HALO + CH_ROWS, :] = um[:, k * LANE:(k + 1) * LANE]
        ext[k, CONF_HALO + CH_ROWS:, :] = un[:, k * LANE:(k + 1) * LANE]
    off = CONF_HALO - NB * (CONF_K // 2)

    def conv_rb(rb, carry):
        r0 = pl.multiple_of(rb * LANE, LANE)
        for k in range(nblk):
            lo = k * LANE
            acc = jnp.broadcast_to(b_ref[:, lo:lo + LANE], (LANE, LANE))
            for tap in range(CONF_K):
                acc = acc + ext[k, pl.ds(r0 + NB * tap + off, LANE), :] * w_ref[tap:tap + 1, lo:lo + LANE]
            cbuf[pl.ds(r0, LANE), lo:lo + LANE] = acc
        return carry

    lax.fori_loop(0, CH_ROWS // LANE, conv_rb, 0)

    def norm_rb(rb, carry):
        r0 = pl.multiple_of(rb * CONF_RB, CONF_RB)
        u = cbuf[pl.ds(r0, CONF_RB), :]
        xc = u - jnp.mean(u, axis=-1, keepdims=True)
        y = xc * lax.rsqrt(jnp.mean(xc * xc, axis=-1, keepdims=True) + EPS) * lng_ref[...] + lnb_ref[...]
        y = _silu(y).astype(BF16)
        o_ref[pl.ds(r0, CONF_RB), :] = jnp.dot(y, pw_ref[...], preferred_element_type=F32) + pwb_ref[...]
        return carry

    lax.fori_loop(0, CH_ROWS // CONF_RB, norm_rb, 0, unroll=True)


def _conformer(l, cols, n_ctx_rows, with_ctx, w, b, lng, lnb, pw, pwb):
    rows = cols.shape[0]
    nc = rows // CH_ROWS
    nc_ctx = n_ctx_rows // CH_ROWS
    first = 0 if with_ctx else nc_ctx
    hpc = CH_ROWS // CONF_HALO
    n_halo = rows // CONF_HALO
    return pl.pallas_call(
        functools.partial(_conf_kernel, first, nc_ctx, nc),
        out_shape=jax.ShapeDtypeStruct(((nc - first) * CH_ROWS, CONF_W), F32),
        grid=(nc - first,),
        in_specs=[
            pl.BlockSpec((CH_ROWS, 2 * CONF_W), lambda i: (i + first, 0)),
            pl.BlockSpec((CONF_HALO, 2 * CONF_W), lambda i: (jnp.maximum((i + first) * hpc - 1, 0), 0)),
            pl.BlockSpec((CONF_HALO, 2 * CONF_W), lambda i: (jnp.minimum((i + first + 1) * hpc, n_halo - 1), 0)),
            _layer_spec(l, (CONF_K, CONF_W)),
            _layer_spec(l, (1, CONF_W)),
            _layer_spec(l, (1, CONF_W)),
            _layer_spec(l, (1, CONF_W)),
            _layer_spec(l, (CONF_W, CONF_W)),
            _layer_spec(l, (1, CONF_W)),
        ],
        out_specs=pl.BlockSpec((CH_ROWS, CONF_W), lambda i: (i, 0)),
        scratch_shapes=[pltpu.VMEM((CONF_W // LANE, CH_ROWS + 2 * CONF_HALO, LANE), F32),
                        pltpu.VMEM((CH_ROWS, CONF_W), F32)],
        compiler_params=_cparams(("arbitrary",)),
        name="conformer",
    )(cols, cols, cols, w, b, lng, lnb, pw, pwb)


def _s5_kernel(direction, *refs):
    if direction == 0:
        u_ref, bmat_ref, ar_ref, ai_ref, cmat_ref, o_ref, hbuf, st = refs
    else:
        (u_ref, bmat_ref, ar_ref, ai_ref, cmat_ref, yf_ref, dsk_ref, gw_ref, gb_ref,
         o_ref, hbuf, st) = refs
    i = pl.program_id(0)

    @pl.when(i == 0)
    def _():
        st[...] = jnp.zeros_like(st)

    u = u_ref[...]
    rb = 2 * LANE
    for k in range(u.shape[0] // rb):
        hbuf[k * rb:(k + 1) * rb, :] = jnp.dot(u[k * rb:(k + 1) * rb, :].astype(BF16), bmat_ref[...],
                                                preferred_element_type=F32)
    ar = jnp.broadcast_to(ar_ref[...], (NB, S5_S))
    ai = jnp.broadcast_to(ai_ref[...], (NB, S5_S))

    def step(j, carry):
        hr, hi = carry
        t = j if direction == 0 else S5_TL - 1 - j
        r0 = pl.multiple_of(t * NB, NB)
        nr = ar * hr - ai * hi + hbuf[pl.ds(r0, NB), 0:S5_S]
        ni = ar * hi + ai * hr + hbuf[pl.ds(r0, NB), S5_S:]
        hbuf[pl.ds(r0, NB), 0:S5_S] = nr
        hbuf[pl.ds(r0, NB), S5_S:] = ni
        return nr, ni

    hr, hi = lax.fori_loop(0, S5_TL, step, (st[0], st[1]), unroll=2)
    st[0] = hr
    st[1] = hi
    for k in range(u.shape[0] // rb):
        rows = slice(k * rb, (k + 1) * rb)
        o_ref[rows, :] = jnp.dot(hbuf[rows, :].astype(BF16), cmat_ref[...], preferred_element_type=F32)
    if direction == 1:
        for k in range(u.shape[0] // rb):
            rows = slice(k * rb, (k + 1) * rb)
            v = jax.nn.gelu(o_ref[rows, :] + yf_ref[rows, :] + dsk_ref[...] * u[rows, :], approximate=True)
            gv = jnp.dot(v.astype(BF16), gw_ref[...], preferred_element_type=F32) + gb_ref[...]
            o_ref[rows, :] = gv[:, :S5_W] * jax.nn.sigmoid(gv[:, S5_W:])


def _s5_pass(l, direction, u, n_ctx_rows, bmat, ar, ai, cmat, yf, dsk, gw, gb):
    rows = u.shape[0]
    tr = S5_TL * NB
    nblk = rows // tr
    nb_ctx = n_ctx_rows // tr
    blk = functools.partial(_chunk_of, direction, nc_ctx=nb_ctx, nc=nblk)

    def dir_spec(shape):
        nd = len(shape)
        return pl.BlockSpec((None, None) + tuple(shape), lambda i: (l, direction) + (0,) * nd)

    in_specs = [
        pl.BlockSpec((tr, S5_W), lambda i: (blk(i), 0)),
        dir_spec((S5_W, 2 * S5_S)),
        dir_spec((1, S5_S)),
        dir_spec((1, S5_S)),
        _layer_spec(l, (2 * S5_S, S5_W)),
    ]
    args = [u, bmat, ar, ai, cmat]
    if direction == 1:
        in_specs += [
            pl.BlockSpec((tr, S5_W), lambda i: (blk(i), 0)),
            _layer_spec(l, (1, S5_W)),
            _layer_spec(l, (S5_W, 2 * S5_W)),
            _layer_spec(l, (1, 2 * S5_W)),
        ]
        args += [yf, dsk, gw, gb]
    return pl.pallas_call(
        functools.partial(_s5_kernel, direction),
        out_shape=jax.ShapeDtypeStruct((rows, S5_W), F32),
        grid=(nblk,),
        in_specs=in_specs,
        out_specs=pl.BlockSpec((tr, S5_W), lambda i: (blk(i), 0)),
        scratch_shapes=[pltpu.VMEM((tr, 2 * S5_S), F32), pltpu.VMEM((2, NB, S5_S), F32)],
        compiler_params=_cparams(("arbitrary",)),
        name="s5_fwd" if direction == 0 else "s5_bwd",
    )(*args)


def _s5_operators(lam_re, lam_im, log_step, b_re, b_im, c_re, c_im):
    depth = lam_re.shape[0]
    eye = jnp.eye(S5_G, dtype=F32)
    step = jnp.exp(log_step)[..., None]
    mag = jnp.exp(lam_re * step)
    ar, ai = mag * jnp.cos(lam_im * step), mag * jnp.sin(lam_im * step)
    inv_den = 1.0 / (lam_re * lam_re + lam_im * lam_im)
    cr = ((ar - 1.0) * lam_re + ai * lam_im) * inv_den
    ci = (ai * lam_re - (ar - 1.0) * lam_im) * inv_den
    bre, bim = b_re[:, None], b_im[:, None]
    bbr = cr[..., None] * bre - ci[..., None] * bim
    bbi = cr[..., None] * bim + ci[..., None] * bre
    bd_r = jnp.einsum("ldgph,gk->ldghkp", bbr, eye).reshape(depth, 2, S5_W, S5_S)
    bd_i = jnp.einsum("ldgph,gk->ldghkp", bbi, eye).reshape(depth, 2, S5_W, S5_S)
    bmat = jnp.concatenate([bd_r, bd_i], axis=-1).astype(BF16)
    cre = jnp.einsum("lghp,gk->lgpkh", c_re, eye).reshape(depth, S5_S, S5_W)
    cim = jnp.einsum("lghp,gk->lgpkh", c_im, eye).reshape(depth, S5_S, S5_W)
    cmat = jnp.concatenate([cre, -cim], axis=1).astype(BF16)
    return bmat, ar.reshape(depth, 2, 1, S5_S), ai.reshape(depth, 2, 1, S5_S), cmat


def _route_rows(lg):
    m = jnp.max(lg, axis=0, keepdims=True)
    ex = jnp.exp(lg - m)
    probs = ex / jnp.sum(ex, axis=0, keepdims=True)
    p = [probs[e:e + 1, :] for e in range(NE)]
    tops = []
    for q in range(NGRP):
        v = p[q * EPG:(q + 1) * EPG]
        m1 = functools.reduce(jnp.maximum, v)
        i1 = jnp.full_like(m1, float(EPG - 1))
        for j in range(EPG - 2, -1, -1):
            i1 = jnp.where(v[j] == m1, float(j), i1)
        rest = [jnp.where(i1 == float(j), -jnp.inf, v[j]) for j in range(EPG)]
        m2 = functools.reduce(jnp.maximum, rest)
        i2 = jnp.full_like(m2, float(EPG - 1))
        for j in range(EPG - 2, -1, -1):
            i2 = jnp.where(jnp.logical_and(rest[j] == m2, i1 != float(j)), float(j), i2)
        tops.append((m1, i1, m2, i2))
    score = [t[0] + t[2] for t in tops]
    best = functools.reduce(jnp.maximum, score)
    sel = tops[NGRP - 1] + (jnp.full_like(best, float(NGRP - 1)),)
    for q in range(NGRP - 2, -1, -1):
        hit = score[q] == best
        sel = tuple(jnp.where(hit, a, b) for a, b in zip(tops[q] + (jnp.full_like(best, float(q)),), sel))
    m1, i1, m2, i2, grp = sel
    den = m1 + m2
    return grp * EPG + i1, grp * EPG + i2, m1 / den, m2 / den


def _out_kernel(a_ref, b_ref, s_ref, x_ref, mod_ref, g_ref, w_ref, rw_ref, rb_ref,
                x1_ref, h2_ref, rt_ref, gc_ref, cnt_ref):
    tm = x_ref.shape[0]
    rb = 2 * LANE
    for k in range(tm // rb):
        rows = slice(k * rb, (k + 1) * rb)
        mix = jnp.dot(a_ref[rows, :].astype(BF16), w_ref[0:SSD_W, :], preferred_element_type=F32)
        mix = mix + jnp.dot(b_ref[rows, :].astype(BF16), w_ref[SSD_W:SSD_W + CONF_W, :],
                            preferred_element_type=F32)
        mix = mix + jnp.dot(s_ref[rows, :].astype(BF16), w_ref[SSD_W + CONF_W:, :], preferred_element_type=F32)
        x1 = (x_ref[rows, :].reshape(rb // NB, NB, D) + mod_ref[2][None] * mix.reshape(rb // NB, NB, D))
        x1 = x1.reshape(rb, D)
        x1_ref[rows, :] = x1
        ms = jnp.mean(x1 * x1, axis=-1, keepdims=True)
        xn = x1 * lax.rsqrt(ms + EPS) * g_ref[...]
        h2_ref[rows, :] = (xn.reshape(rb // NB, NB, D) * (1.0 + mod_ref[4])[None] + mod_ref[3][None]).reshape(rb, D)
    lg = lax.dot_general(rw_ref[...], h2_ref[...], (((1,), (1,)), ((), ())), precision=HIGHEST,
                         preferred_element_type=F32) + rb_ref[...]
    e1, e2, g1, g2 = _route_rows(lg)
    eid = lax.broadcasted_iota(I32, (NE, tm), 0).astype(F32)
    oh0 = (eid == e1).astype(F32)
    oh1 = (eid == e2).astype(F32)
    t0 = lax.broadcasted_iota(I32, (tm, tm), 0)
    t1 = lax.broadcasted_iota(I32, (tm, tm), 1)
    before = (t0 < t1).astype(BF16)
    pre0 = jnp.dot(oh0.astype(BF16), before, preferred_element_type=F32)
    pre1 = jnp.dot(oh1.astype(BF16), before, preferred_element_type=F32)
    lr0 = jnp.sum(oh0 * pre0, axis=0, keepdims=True)
    lr1 = jnp.sum(oh1 * pre1, axis=0, keepdims=True)
    rt_ref[...] = jnp.concatenate([e1, e2, lr0, lr1, jnp.zeros((NB - 4, tm), F32)], axis=0)
    gates = jnp.concatenate([g1, g2, jnp.zeros((LANE - 2, tm), F32)], axis=0)
    gc_ref[...] = gates.T
    cnt = jnp.concatenate([jnp.sum(oh0, axis=1, keepdims=True), jnp.sum(oh1, axis=1, keepdims=True)], axis=0)
    cnt_ref[...] = jnp.broadcast_to(cnt, (2 * NE, LANE))


def _out_proj(l, a, bconf, s, x, modtabs, g_ffn, w_out, rw_t, rb, n_ctx_rows, conf_first, blk0, nblk):
    tm = ROW_TM
    ctx_blk = n_ctx_rows // tm
    rows = nblk * tm
    return pl.pallas_call(
        _out_kernel,
        out_shape=[
            jax.ShapeDtypeStruct((rows, D), F32),
            jax.ShapeDtypeStruct((rows, D), F32),
            jax.ShapeDtypeStruct((NB, rows), F32),
            jax.ShapeDtypeStruct((rows, LANE), F32),
            jax.ShapeDtypeStruct((nblk, 2 * NE, LANE), F32),
        ],
        grid=(nblk,),
        in_specs=[
            pl.BlockSpec((tm, SSD_W), lambda i: (i + blk0, 0)),
            pl.BlockSpec((tm, CONF_W), lambda i: (i + blk0 - conf_first, 0)),
            pl.BlockSpec((tm, S5_W), lambda i: (i + blk0, 0)),
            pl.BlockSpec((tm, D), lambda i: (i + blk0, 0)),
            _mod_spec(l, ctx_blk, blk0),
            _layer_spec(l, (1, D)),
            _layer_spec(l, (D, D)),
            _const_spec((NE, D)),
            _const_spec((NE, 1)),
        ],
        out_specs=[
            pl.BlockSpec((tm, D), lambda i: (i, 0)),
            pl.BlockSpec((tm, D), lambda i: (i, 0)),
            pl.BlockSpec((NB, tm), lambda i: (0, i)),
            pl.BlockSpec((tm, LANE), lambda i: (i, 0)),
            pl.BlockSpec((None, 2 * NE, LANE), lambda i: (i, 0, 0)),
        ],
        compiler_params=_cparams(("arbitrary",)),
        name="out_proj_route",
    )(a, bconf, s, x, modtabs, g_ffn, w_out, rw_t, rb)


def _moe_kernel(second, it_ref, ie_ref, lo_ref, hi_ref, h_ref, wg_ref, wu_ref, wd_ref, *rest):
    rest = rest[1:] if second else rest
    out_ref, wgb, wub, wdb, cur = rest
    o_ref = out_ref if second else out_ref.at[0]
    w = pl.program_id(0)
    lo = lo_ref[w]
    hi = hi_ref[w]
    new_tile = jnp.logical_or(w == 0, it_ref[w] != it_ref[jnp.maximum(w - 1, 0)])

    @pl.when(w == 0)
    def _():
        cur[0] = -1

    @pl.when(new_tile)
    def _():
        out_ref[...] = jnp.zeros_like(out_ref)

    @pl.when(hi > lo)
    def _():
        @pl.when(cur[0] != ie_ref[w])
        def _():
            wgb[...] = wg_ref[...].astype(BF16)
            wub[...] = wu_ref[...].astype(BF16)
            wdb[...] = wd_ref[...].astype(BF16)
            cur[0] = ie_ref[w]

        h = h_ref[...].astype(BF16)
        gate = jnp.dot(h, wgb[...], preferred_element_type=F32)
        up = jnp.dot(h, wub[...], preferred_element_type=F32)
        act = (_silu(gate) * up).astype(BF16)
        y = jnp.dot(act, wdb[...], preferred_element_type=F32)
        r = lax.broadcasted_iota(I32, (MOE_TM, 1), 0)
        keep = jnp.logical_and(r >= lo, r < hi)
        o_ref[...] = jnp.where(keep, y, o_ref[...])


def _moe_experts(l, hs, items, tile0, w_gate, w_up, w_down, prev):
    item_tile, item_expert, item_lo, item_hi = items
    n_items = item_tile.shape[0]
    n_prefetch = 4
    second = prev is not None

    def wspec():
        return pl.BlockSpec((None, None, D, D), lambda w, it, ie, lo, hi: (l, ie[w], 0, 0))

    in_specs = [pl.BlockSpec((MOE_TM, D), lambda w, it, ie, lo, hi: (it[w] - tile0, 0)), wspec(), wspec(), wspec()]
    args = [item_tile, item_expert, item_lo, item_hi, hs, w_gate, w_up, w_down]
    aliases = {}
    if second:
        in_specs.append(pl.BlockSpec(memory_space=pl.ANY))
        args.append(prev)
        aliases = {len(args) - 1: 0}
        out_spec = pl.BlockSpec((None, MOE_TM, D), lambda w, it, ie, lo, hi: (1, it[w] - tile0, 0))
    else:
        out_spec = pl.BlockSpec((2, MOE_TM, D), lambda w, it, ie, lo, hi: (0, it[w] - tile0, 0))
    return pl.pallas_call(
        functools.partial(_moe_kernel, second),
        out_shape=jax.ShapeDtypeStruct((2, hs.shape[0], D), F32),
        grid_spec=pltpu.PrefetchScalarGridSpec(
            num_scalar_prefetch=n_prefetch,
            grid=(n_items,),
            in_specs=in_specs,
            out_specs=out_spec,
            scratch_shapes=[pltpu.VMEM((D, D), BF16)] * 3 + [pltpu.SMEM((1,), I32)],
        ),
        input_output_aliases=aliases,
        compiler_params=_cparams(("arbitrary",)),
        name="moe_experts",
    )(*args)


def _moe_plan(route, cnt, tile_parts):
    rows = route.shape[1]
    nblk = cnt.shape[0]
    n_flat = 2 * rows
    e = route[0:2].astype(I32)
    lrank = route[2:4].astype(I32)
    counts = cnt[:, :, 0].astype(I32).reshape(nblk, 2, NE)
    per = jnp.transpose(counts, (1, 0, 2)).reshape(2 * nblk, NE)
    before = jnp.cumsum(per, axis=0) - per
    gcount = jnp.sum(per, axis=0)
    gend = jnp.cumsum(gcount)
    gstart = gend - gcount
    base = jnp.transpose((before + gstart[None, :]).reshape(2, nblk, NE), (2, 0, 1))[..., None]
    onehot = e.reshape(1, 2, nblk, ROW_TM) == jnp.arange(NE, dtype=I32).reshape(NE, 1, 1, 1)
    inv = jnp.sum(jnp.where(onehot, base, 0), axis=0).reshape(2, rows) + lrank
    keys = (e * 65536).reshape(-1) + jnp.arange(n_flat, dtype=I32)
    order = lax.sort(keys, is_stable=False) & 0xFFFF
    tok = jnp.where(order >= rows, order - rows, order)
    items = []
    for t0, tn in tile_parts:
        row_lo, row_hi = t0 * MOE_TM, (t0 + tn) * MOE_TM
        tile_start = (t0 + jnp.arange(tn, dtype=I32)) * MOE_TM
        bnd = lax.sort(jnp.concatenate([tile_start, jnp.clip(gstart[1:], row_lo, row_hi)]), is_stable=False)
        bnd_hi = jnp.concatenate([bnd[1:], jnp.full((1,), row_hi, I32)])
        item_tile = jnp.minimum(bnd // MOE_TM, t0 + tn - 1)
        first_row = jnp.minimum(bnd, row_hi - 1)[:, None]
        item_expert = jnp.minimum(jnp.sum((gend[None, :] <= first_row).astype(I32), axis=1), NE - 1)
        items.append((item_tile, item_expert, bnd - item_tile * MOE_TM, bnd_hi - item_tile * MOE_TM))
    return tok, inv, items


def _final_kernel(x_ref, ya_ref, yb_ref, gc_ref, mod_ref, g_ref, o_ref, scr):
    tm = x_ref.shape[0]
    gc = gc_ref[...]
    ff = (gc[:, 0:1] * ya_ref[...] + gc[:, 1:2] * yb_ref[...]).reshape(tm // NB, NB, D)
    x2 = (x_ref[...].reshape(tm // NB, NB, D) + mod_ref[5][None] * ff).reshape(tm, D)
    ms = jnp.mean(x2 * x2, axis=-1, keepdims=True)
    x2 = x2 * lax.rsqrt(ms + EPS) * g_ref[...]
    ncb = D // LANE
    for k in range(ncb):
        scr[k] = x2[:, k * LANE:(k + 1) * LANE]
    for b in range(NB):
        for k in range(ncb):
            o_ref[b, :, k * LANE:(k + 1) * LANE] = scr[k, pl.ds(b, ROW_TQ, stride=NB), :]


def _final_combine(l, x1, ya, yb, gcol, modtabs, g_final):
    tm = ROW_TM
    rows = x1.shape[0]
    row_spec = pl.BlockSpec((tm, D), lambda i: (i, 0))
    return pl.pallas_call(
        _final_kernel,
        out_shape=jax.ShapeDtypeStruct((NB, rows // NB, D), F32),
        grid=(rows // tm,),
        in_specs=[row_spec, row_spec, row_spec, pl.BlockSpec((tm, LANE), lambda i: (i, 0)),
                  _mod_spec(l, 0), _const_spec((1, D))],
        out_specs=pl.BlockSpec((NB, ROW_TQ, D), lambda i: (0, i, 0)),
        scratch_shapes=[pltpu.VMEM((D // LANE, tm, LANE), F32)],
        compiler_params=_cparams(("arbitrary",)),
        name="moe_combine_final",
    )(x1, ya, yb, gcol, modtabs, g_final)


def _grid_pos_embed(rows_n):
    rr, cc = jnp.meshgrid(jnp.arange(rows_n, dtype=F32), jnp.arange(GRID_W, dtype=F32), indexing="ij")
    quarter = D // 4
    inv_freq = jnp.exp(-math.log(10000.0) * jnp.arange(quarter, dtype=F32) / quarter)

    def emb(pos):
        ang = pos.reshape(-1)[:, None] * inv_freq[None, :]
        return jnp.concatenate([jnp.sin(ang), jnp.cos(ang)], axis=-1)

    return jnp.concatenate([emb(rr), emb(cc)], axis=-1)


def _take_rows(a, idx):
    return a.at[idx].get(mode="promise_in_bounds")


def _pad_last(v, width):
    return jnp.pad(v, [(0, 0)] * (v.ndim - 1) + [(0, width - v.shape[-1])])


def kernel(x, c, ctx, c_ctx, w_ada, b_ada, g_mix, w_in, ssd_conv_w, ssd_conv_b, ssd_dt_bias, ssd_a_log, ssd_d, ssd_norm_g, conf_dw_w, conf_dw_b, conf_ln_g, conf_ln_b, conf_pw_w, conf_pw_b, s5_lambda_re, s5_lambda_im, s5_log_step, s5_b_re, s5_b_im, s5_c_re, s5_c_im, s5_d, s5_glu_w, s5_glu_b, w_out, g_ffn, router_w, router_b, exp_w_gate, exp_w_up, exp_w_down, g_final):
    bsz, seq, _ = x.shape
    ctx_len = ctx.shape[1]
    depth = w_ada.shape[0]
    assert bsz == NB and seq % SSD_CHUNK == 0 and ctx_len % SSD_CHUNK == 0
    n_ctx_rows = ctx_len * NB
    rows_all = (ctx_len + seq) * NB

    cond = jnp.concatenate([c, c_ctx[None, :], jnp.zeros((2 * NB - bsz - 1, D), F32)], axis=0)
    mod = _modulation(cond, w_ada, b_ada)
    mod = mod.reshape(depth, 2 * NB, N_MOD, D)
    mod_lat = jnp.transpose(mod[:, :NB], (0, 2, 1, 3))
    mod_ctx = jnp.broadcast_to(mod[:, NB][:, :, None, :], (depth, N_MOD, NB, D))
    modtabs = jnp.stack([mod_ctx, mod_lat], axis=1)

    o_b = SSD_W + SSD_XBC + 2 * SSD_H
    w_in_p = jnp.concatenate(
        [w_in[:, :, :o_b], jnp.zeros((depth, D, DT_PAD - 2 * SSD_H), F32), w_in[:, :, o_b:]], axis=2).astype(BF16)
    g_mix3 = g_mix.reshape(depth, 1, D)
    ssd_cb = ssd_conv_b.reshape(depth, 1, SSD_XBC)
    ssd_dtb = _pad_last(ssd_dt_bias.reshape(depth, 1, 2 * SSD_H), DT_PAD)
    ssd_alog = _pad_last(ssd_a_log.reshape(depth, 1, 2 * SSD_H), DT_PAD)
    ssd_dsk = jnp.repeat(ssd_d, SSD_P, axis=1).reshape(depth, 1, SSD_W)
    ssd_ng = ssd_norm_g.reshape(depth, 1, SSD_W)
    conf_b = conf_dw_b.reshape(depth, 1, CONF_W)
    conf_lg = conf_ln_g.reshape(depth, 1, CONF_W)
    conf_lb = conf_ln_b.reshape(depth, 1, CONF_W)
    conf_pw = conf_pw_w.astype(BF16)
    conf_pb = conf_pw_b.reshape(depth, 1, CONF_W)
    s5_bmat, s5_ar, s5_ai, s5_cmat = _s5_operators(s5_lambda_re, s5_lambda_im, s5_log_step,
                                                   s5_b_re, s5_b_im, s5_c_re, s5_c_im)
    s5_dsk = s5_d.reshape(depth, 1, S5_W)
    s5_gw = s5_glu_w.astype(BF16)
    s5_gb = s5_glu_b.reshape(depth, 1, 2 * S5_W)
    w_out_b = w_out.astype(BF16)
    g_ffn3 = g_ffn.reshape(depth, 1, D)
    router_w_t = router_w.T
    router_b2 = router_b.reshape(NE, 1)
    g_final2 = g_final.reshape(1, D)
    pos = _grid_pos_embed(seq // GRID_W)

    pending = (ctx, x)
    for l in range(depth):
        last = l == depth - 1
        ssd_cols, conf_cols, s5_u, xall = _in_proj(l, l == 0, pending, pos, modtabs, g_mix3, w_in_p,
                                                   rows_all, n_ctx_rows)
        ypart = _ssd_pass(l, 0, ssd_cols, n_ctx_rows, ssd_conv_w, ssd_cb, ssd_dtb, ssd_alog, ssd_dsk, None)
        a_mix = _ssd_pass(l, 1, ssd_cols, n_ctx_rows, ssd_conv_w, ssd_cb, ssd_dtb, ssd_alog, ssd_ng, ypart)
        b_mix = _conformer(l, conf_cols, n_ctx_rows, not last, conf_dw_w, conf_b, conf_lg, conf_lb, conf_pw, conf_pb)
        yf = _s5_pass(l, 0, s5_u, n_ctx_rows, s5_bmat, s5_ar, s5_ai, s5_cmat, None, None, None, None)
        s_mix = _s5_pass(l, 1, s5_u, n_ctx_rows, s5_bmat, s5_ar, s5_ai, s5_cmat, yf, s5_dsk, s5_gw, s5_gb)
        ctx_blk = n_ctx_rows // ROW_TM
        first = ctx_blk if last else 0
        n_out = rows_all // ROW_TM - first
        x1, h2, route, gcol, cnt = _out_proj(l, a_mix, b_mix, s_mix, xall, modtabs, g_ffn3, w_out_b,
                                             router_w_t, router_b2, n_ctx_rows, first, first, n_out)
        n_tiles = 2 * n_out * ROW_TM // MOE_TM
        assert n_tiles % 2 == 0
        tile_parts = [(0, n_tiles // 2), (n_tiles // 2, n_tiles // 2)]
        tok, inv, items = _moe_plan(route, cnt, tile_parts)
        hs = [_take_rows(h2, tok[t0 * MOE_TM:(t0 + tn) * MOE_TM]) for t0, tn in tile_parts]
        y = None
        for (t0, tn), h, it in zip(tile_parts, hs, items):
            y = _moe_experts(l, h, it, t0, exp_w_gate, exp_w_up, exp_w_down, y)
        y = y.reshape(n_tiles * MOE_TM, D)
        ya = _take_rows(y, inv[0])
        yb = _take_rows(y, inv[1])
        pending = (x1, ya, yb, gcol)
    return _final_combine(depth - 1, *pending, modtabs, g_final2)
```

```python
import functools
import math

import jax
import jax.numpy as jnp
from jax import lax
from jax.experimental import pallas as pl
from jax.experimental.pallas import tpu as pltpu

F32 = jnp.float32
BF16 = jnp.bfloat16
I32 = jnp.int32
HIGHEST = lax.Precision.HIGHEST

NB = 8
D = 1024
GRID_W = 64
N_MOD = 6
EPS = 1e-6
LANE = 128
SSD_W = 512
SSD_P = 64
SSD_H = 8
SSD_G = 2
SSD_N = 128
SSD_K = 5
SSD_XBC = SSD_W + 2 * SSD_G * SSD_N
SSD_CHUNK = 128
DT_PAD = LANE
SSD_COLS_P = SSD_W + SSD_XBC + DT_PAD
CONF_W = 256
CONF_K = 31
S5_W = 256
S5_G = 16
S5_P = 64
S5_CH = 16
S5_S = S5_G * S5_P
S5_TL = 128
NE = 16
NGRP = 4
EPG = 4
MOE_TM = 256
IN_COLS_P = SSD_COLS_P + 2 * CONF_W + S5_W
ROW_TM = 512
ROW_TQ = ROW_TM // NB
CH_ROWS = SSD_CHUNK * NB
VMEM_LIMIT = 56 * 1024 * 1024


def _cparams(sem):
    return pltpu.CompilerParams(dimension_semantics=sem, vmem_limit_bytes=VMEM_LIMIT)


def _const_spec(shape):
    nd = len(shape)
    return pl.BlockSpec(shape, lambda *_: (0,) * nd)


def _layer_spec(l, shape):
    nd = len(shape)
    return pl.BlockSpec((None,) + tuple(shape), lambda *_: (l,) + (0,) * nd)


def _silu(v):
    return v * jax.nn.sigmoid(v)


def _mod_kernel(c_ref, w_ref, b_ref, o_ref):
    c = c_ref[...]
    h = _silu(c).astype(BF16)
    o_ref[...] = jnp.dot(h, w_ref[...].astype(BF16), preferred_element_type=F32) + b_ref[...]


def _modulation(cond, w_ada, b_ada):
    depth = w_ada.shape[0]
    nrow = cond.shape[0]
    return pl.pallas_call(
        _mod_kernel,
        out_shape=jax.ShapeDtypeStruct((depth, nrow, N_MOD * D), F32),
        grid=(depth, N_MOD),
        in_specs=[
            pl.BlockSpec((nrow, D), lambda l, j: (0, 0)),
            pl.BlockSpec((None, D, D), lambda l, j: (l, 0, j)),
            pl.BlockSpec((None, 1, D), lambda l, j: (l, 0, j)),
        ],
        out_specs=pl.BlockSpec((None, nrow, D), lambda l, j: (l, 0, j)),
        compiler_params=_cparams(("arbitrary", "arbitrary")),
        name="adaln_mod",
    )(cond, w_ada, b_ada.reshape(depth, 1, N_MOD * D))


def _mod_spec(l, ctx_blk, first=0):
    return pl.BlockSpec((None, None, N_MOD, NB, D),
                        lambda i, *_: (l, jnp.where(i + first < ctx_blk, 0, 1), 0, 0, 0))


def _norm_mod_project(x, mod_ref, g_ref, w_ref, ssd_ref, conf_ref, s5_ref):
    tm = x.shape[0]
    ms = jnp.mean(x * x, axis=-1, keepdims=True)
    xn = x * lax.rsqrt(ms + EPS) * g_ref[...]
    h = xn.reshape(tm // NB, NB, D) * (1.0 + mod_ref[1])[None] + mod_ref[0][None]
    h = h.reshape(tm, D).astype(BF16)
    ssd_ref[...] = jnp.dot(h, w_ref[:, :SSD_COLS_P], preferred_element_type=F32)
    conf_ref[...] = jnp.dot(h, w_ref[:, SSD_COLS_P:SSD_COLS_P + 2 * CONF_W], preferred_element_type=F32)
    s5_ref[...] = jnp.dot(h, w_ref[:, SSD_COLS_P + 2 * CONF_W:], preferred_element_type=F32)


def _in_first_kernel(ctx_blk, ctx_ref, x_ref, pos_ref, mod_ref, g_ref, w_ref,
                     ssd_ref, conf_ref, s5_ref, x0_ref, scr):
    i = pl.program_id(0)
    ncb = D // LANE

    @pl.when(i < ctx_blk)
    def _():
        for b in range(NB):
            for k in range(ncb):
                scr[k, pl.ds(b, ROW_TQ, stride=NB), :] = ctx_ref[b, :, k * LANE:(k + 1) * LANE]

    @pl.when(i >= ctx_blk)
    def _():
        for b in range(NB):
            for k in range(ncb):
                scr[k, pl.ds(b, ROW_TQ, stride=NB), :] = (
                    x_ref[b, :, k * LANE:(k + 1) * LANE] + pos_ref[:, k * LANE:(k + 1) * LANE])

    x = jnp.concatenate([scr[k] for k in range(ncb)], axis=1)
    x0_ref[...] = x
    _norm_mod_project(x, mod_ref, g_ref, w_ref, ssd_ref, conf_ref, s5_ref)


def _in_next_kernel(x1_ref, ya_ref, yb_ref, gc_ref, modp_ref, mod_ref, g_ref, w_ref,
                    ssd_ref, conf_ref, s5_ref, x_ref):
    tm = x1_ref.shape[0]
    gc = gc_ref[...]
    ff = (gc[:, 0:1] * ya_ref[...] + gc[:, 1:2] * yb_ref[...]).reshape(tm // NB, NB, D)
    x = (x1_ref[...].reshape(tm // NB, NB, D) + modp_ref[5][None] * ff).reshape(tm, D)
    x_ref[...] = x
    _norm_mod_project(x, mod_ref, g_ref, w_ref, ssd_ref, conf_ref, s5_ref)


def _in_proj(l, first, x_or_pair, pos, modtabs, g_mix, w_in_p, rows, n_ctx_rows):
    tm = ROW_TM
    nblk = rows // tm
    ctx_blk = n_ctx_rows // tm
    common_specs = [_mod_spec(l, ctx_blk), _layer_spec(l, (1, D)), _layer_spec(l, (D, IN_COLS_P))]
    common_args = [modtabs, g_mix, w_in_p]
    out_shape = [
        jax.ShapeDtypeStruct((rows, SSD_COLS_P), F32),
        jax.ShapeDtypeStruct((rows, 2 * CONF_W), F32),
        jax.ShapeDtypeStruct((rows, S5_W), F32),
    ]
    out_specs = [
        pl.BlockSpec((tm, SSD_COLS_P), lambda i: (i, 0)),
        pl.BlockSpec((tm, 2 * CONF_W), lambda i: (i, 0)),
        pl.BlockSpec((tm, S5_W), lambda i: (i, 0)),
    ]
    if first:
        ctx, x = x_or_pair
        body = functools.partial(_in_first_kernel, ctx_blk)
        in_specs = [
            pl.BlockSpec((NB, ROW_TQ, D), lambda i: (0, jnp.minimum(i, ctx_blk - 1), 0)),
            pl.BlockSpec((NB, ROW_TQ, D), lambda i: (0, jnp.maximum(i - ctx_blk, 0), 0)),
            pl.BlockSpec((ROW_TQ, D), lambda i: (jnp.maximum(i - ctx_blk, 0), 0)),
        ] + common_specs
        args = [ctx, x, pos] + common_args
        scratch = [pltpu.VMEM((D // LANE, tm, LANE), F32)]
    else:
        x1, ya, yb, gcol = x_or_pair
        body = _in_next_kernel
        in_specs = [
            pl.BlockSpec((tm, D), lambda i: (i, 0)),
            pl.BlockSpec((tm, D), lambda i: (i, 0)),
            pl.BlockSpec((tm, D), lambda i: (i, 0)),
            pl.BlockSpec((tm, LANE), lambda i: (i, 0)),
            _mod_spec(l - 1, ctx_blk),
        ] + common_specs
        args = [x1, ya, yb, gcol, modtabs] + common_args
        scratch = []
    out_shape.append(jax.ShapeDtypeStruct((rows, D), F32))
    out_specs.append(pl.BlockSpec((tm, D), lambda i: (i, 0)))
    return pl.pallas_call(
        body,
        out_shape=out_shape,
        grid=(nblk,),
        in_specs=in_specs,
        out_specs=out_specs,
        scratch_shapes=scratch,
        compiler_params=_cparams(("arbitrary",)),
        name="in_proj",
    )(*args)


def _chunk_of(direction, i, nc_ctx, nc):
    if direction == 0:
        return i
    return jnp.where(i < nc_ctx, nc_ctx - 1 - i, nc - 1 - (i - nc_ctx))


def _has_prev(c, nc_ctx):
    return jnp.logical_and(c != 0, c != nc_ctx)


def _has_next(c, nc_ctx, nc):
    return jnp.logical_and(c != nc_ctx - 1, c != nc - 1)


SSD_HALO = 2 * NB


def _ssd_kernel(direction, nc_ctx, nc, *refs):
    if direction == 0:
        (main_ref, prev_ref, next_ref, cw_ref, cb_ref, dtb_ref, alog_ref, dsk_ref,
         o_ref, ext, act, dts, yout, state, tabs) = refs
    else:
        (main_ref, prev_ref, next_ref, cw_ref, cb_ref, dtb_ref, alog_ref, ng_ref, yp_ref,
         o_ref, ext, act, dts, yout, state, tabs) = refs
    i = pl.program_id(0)
    c = _chunk_of(direction, i, nc_ctx, nc)

    @pl.when(i == 0)
    def _():
        state[...] = jnp.zeros_like(state)

    has_prev = _has_prev(c, nc_ctx)
    has_next = _has_next(c, nc_ctx, nc)
    nblk = SSD_XBC // LANE
    n_rb = CH_ROWS // LANE
    for k in range(nblk):
        lo = SSD_W + k * LANE
        ext[0, k, 0:SSD_HALO, :] = jnp.where(has_prev, prev_ref[:, lo:lo + LANE], 0.0)
        ext[0, k, SSD_HALO:, :] = main_ref[0:LANE + SSD_HALO, lo:lo + LANE]
        ext[1, k, 0:LANE + SSD_HALO, :] = main_ref[CH_ROWS - LANE - SSD_HALO:, lo:lo + LANE]
        ext[1, k, LANE + SSD_HALO:, :] = jnp.where(has_next, next_ref[:, lo:lo + LANE], 0.0)

    def conv_block(tap_rows, r0):
        for k in range(nblk):
            lo = k * LANE
            acc = jnp.broadcast_to(cb_ref[:, lo:lo + LANE], (LANE, LANE))
            for tap in range(SSD_K):
                acc = acc + tap_rows(k, tap) * cw_ref[tap:tap + 1, lo:lo + LANE]
            act[k, pl.ds(r0, LANE), :] = _silu(acc)

    def conv_rb(rb, carry):
        r0 = pl.multiple_of(rb * LANE, LANE)
        conv_block(lambda k, tap: main_ref[pl.ds(r0 + NB * tap - SSD_HALO, LANE),
                                           SSD_W + k * LANE:SSD_W + (k + 1) * LANE], r0)
        return carry

    conv_block(lambda k, tap: ext[0, k, NB * tap:NB * tap + LANE, :], 0)
    lax.fori_loop(1, n_rb - 1, conv_rb, 0)
    conv_block(lambda k, tap: ext[1, k, NB * tap:NB * tap + LANE, :], CH_ROWS - LANE)

    raw = main_ref[:, SSD_W + SSD_XBC:] + dtb_ref[...]
    dts[...] = jnp.maximum(raw, 0.0) + jnp.log1p(jnp.exp(-jnp.abs(raw)))
    a_row = -jnp.exp(alog_ref[...])

    tt = lax.broadcasted_iota(I32, (SSD_CHUNK, SSD_CHUNK), 0)
    ss = lax.broadcasted_iota(I32, (SSD_CHUNK, SSD_CHUNK), 1)
    mask = (ss <= tt) if direction == 0 else (ss >= tt)
    tmat = mask.astype(F32)
    last = SSD_CHUNK - 1 if direction == 0 else 0
    hpg = SSD_H // SSD_G
    lane_id = lax.broadcasted_iota(I32, (1, LANE), 1)
    half_lo = lane_id < SSD_P

    def decay_tables(b, carry):
        dt_b = dts[pl.ds(b, SSD_CHUNK, stride=NB), :]
        cs = jnp.dot(tmat, dt_b * a_row, precision=HIGHEST, preferred_element_type=F32)
        tabs[b, 0] = cs
        tabs[b, 1] = cs.T
        tabs[b, 2] = dt_b.T
        return carry

    lax.fori_loop(0, NB, decay_tables, 0, unroll=True)

    def per_batch(b, carry):
        sl = pl.ds(b, SSD_CHUNK, stride=NB)
        cs = tabs[b, 0]
        cs_t = tabs[b, 1]
        dt_t = tabs[b, 2]
        for g in range(SSD_G):
            bg_t = act[SSD_W // LANE + g, sl, :].T
            cg = act[SSD_W // LANE + SSD_G + g, sl, :]
            gmat = jnp.dot(cg.astype(BF16), bg_t.astype(BF16), preferred_element_type=F32)
            for pr in range(hpg // 2):
                k = g * (hpg // 2) + pr
                xs_pair = act[k, sl, :]
                s_pair = state[b, k]
                xs_b = xs_pair.astype(BF16)
                rhs = jnp.concatenate([xs_b, s_pair.astype(BF16)], axis=0)
                ys, ss, etot = [], [], []
                for j in range(2):
                    ln = direction * SSD_H + 2 * k + j
                    colb = jnp.broadcast_to(cs[:, ln:ln + 1], (SSD_CHUNK, SSD_CHUNK))
                    row = cs_t[ln:ln + 1, :]
                    dtr = dt_t[ln:ln + 1, :]
                    decay = jnp.where(mask, jnp.exp(jnp.where(mask, colb - row, 0.0)) * dtr, 0.0)
                    lhs = jnp.concatenate([(gmat * decay).astype(BF16),
                                           (cg * jnp.exp(colb)).astype(BF16)], axis=1)
                    ys.append(jnp.dot(lhs, rhs, preferred_element_type=F32))
                    tot = cs[last:last + 1, ln:ln + 1]
                    wrow = jnp.exp(tot - row) * dtr
                    ss.append(jnp.dot((bg_t * wrow).astype(BF16), xs_b, preferred_element_type=F32))
                    etot.append(jnp.exp(tot))
                acc_y = jnp.where(half_lo, ys[0], ys[1])
                state[b, k] = s_pair * jnp.where(half_lo, etot[0], etot[1]) + jnp.where(half_lo, ss[0], ss[1])
                if direction == 0:
                    acc_y = acc_y + dsk_ref[:, k * LANE:(k + 1) * LANE] * xs_pair
                yout[k, sl, :] = acc_y
        return carry

    lax.fori_loop(0, NB, per_batch, 0, unroll=2)

    nyb = SSD_W // LANE
    if direction == 0:
        for k in range(nyb):
            o_ref[:, k * LANE:(k + 1) * LANE] = yout[k]
    else:
        def fin(rb, carry):
            r0 = pl.multiple_of(rb * LANE, LANE)
            y = jnp.concatenate([yout[k, pl.ds(r0, LANE), :] for k in range(nyb)], axis=1)
            y = y + yp_ref[pl.ds(r0, LANE), :]
            y = y * _silu(main_ref[pl.ds(r0, LANE), 0:SSD_W])
            ms = jnp.mean(y * y, axis=-1, keepdims=True)
            o_ref[pl.ds(r0, LANE), :] = y * lax.rsqrt(ms + EPS) * ng_ref[...]
            return carry

        lax.fori_loop(0, CH_ROWS // LANE, fin, 0, unroll=4)


def _ssd_pass(l, direction, cols, n_ctx_rows, cw, cb, dtb, alog, extra, ypart):
    rows = cols.shape[0]
    nc = rows // CH_ROWS
    nc_ctx = n_ctx_rows // CH_ROWS
    hpc = CH_ROWS // SSD_HALO
    n_halo = rows // SSD_HALO
    chunk = functools.partial(_chunk_of, direction, nc_ctx=nc_ctx, nc=nc)
    in_specs = [
        pl.BlockSpec((CH_ROWS, SSD_COLS_P), lambda i: (chunk(i), 0)),
        pl.BlockSpec((SSD_HALO, SSD_COLS_P), lambda i: (jnp.maximum(chunk(i) * hpc - 1, 0), 0)),
        pl.BlockSpec((SSD_HALO, SSD_COLS_P), lambda i: (jnp.minimum((chunk(i) + 1) * hpc, n_halo - 1), 0)),
        _layer_spec(l, (SSD_K, SSD_XBC)),
        _layer_spec(l, (1, SSD_XBC)),
        _layer_spec(l, (1, DT_PAD)),
        _layer_spec(l, (1, DT_PAD)),
        _layer_spec(l, (1, SSD_W)),
    ]
    args = [cols, cols, cols, cw, cb, dtb, alog, extra]
    if direction == 1:
        in_specs.append(pl.BlockSpec((CH_ROWS, SSD_W), lambda i: (chunk(i), 0)))
        args.append(ypart)
    return pl.pallas_call(
        functools.partial(_ssd_kernel, direction, nc_ctx, nc),
        out_shape=jax.ShapeDtypeStruct((rows, SSD_W), F32),
        grid=(nc,),
        in_specs=in_specs,
        out_specs=pl.BlockSpec((CH_ROWS, SSD_W), lambda i: (chunk(i), 0)),
        scratch_shapes=[
            pltpu.VMEM((2, SSD_XBC // LANE, LANE + 2 * SSD_HALO, LANE), F32),
            pltpu.VMEM((SSD_XBC // LANE, CH_ROWS, LANE), F32),
            pltpu.VMEM((CH_ROWS, DT_PAD), F32),
            pltpu.VMEM((SSD_W // LANE, CH_ROWS, LANE), F32),
            pltpu.VMEM((NB, SSD_W // LANE, SSD_N, LANE), F32),
            pltpu.VMEM((NB, 3, SSD_CHUNK, DT_PAD), F32),
        ],
        compiler_params=_cparams(("arbitrary",)),
        name="ssd_fwd" if direction == 0 else "ssd_bwd",
    )(*args)


CONF_HALO = 128
CONF_RB = 256


def _conf_kernel(first_chunk, nc_ctx, nc, main_ref, prev_ref, next_ref, w_ref, b_ref, lng_ref, lnb_ref,
                 pw_ref, pwb_ref, o_ref, ext, cbuf):
    c = pl.program_id(0) + first_chunk
    has_prev = _has_prev(c, nc_ctx)
    has_next = _has_next(c, nc_ctx, nc)

    def glu(ref):
        v = ref[...]
        return v[:, :CONF_W] * jax.nn.sigmoid(v[:, CONF_W:])

    um = glu(main_ref)
    up = jnp.where(has_prev, glu(prev_ref), 0.0)
    un = jnp.where(has_next, glu(next_ref), 0.0)
    nblk = CONF_W // LANE
    for k in range(nblk):
        ext[k, 0:CONF_HALO, :] = up[:, k * LANE:(k + 1) * LANE]
        ext[k, CONF_HALO:CONF_HALO + CH_ROWS, :] = um[:, k * LANE:(k + 1) * LANE]
        ext[k, CONF_HALO + CH_ROWS:, :] = un[:, k * LANE:(k + 1) * LANE]
    off = CONF_HALO - NB * (CONF_K // 2)

    def conv_rb(rb, carry):
        r0 = pl.multiple_of(rb * LANE, LANE)
        for k in range(nblk):
            lo = k * LANE
            acc = jnp.broadcast_to(b_ref[:, lo:lo + LANE], (LANE, LANE))
            for tap in range(CONF_K):
                acc = acc + ext[k, pl.ds(r0 + NB * tap + off, LANE), :] * w_ref[tap:tap + 1, lo:lo + LANE]
            cbuf[pl.ds(r0, LANE), lo:lo + LANE] = acc
        return carry

    lax.fori_loop(0, CH_ROWS // LANE, conv_rb, 0)

    def norm_rb(rb, carry):
        r0 = pl.multiple_of(rb * CONF_RB, CONF_RB)
        u = cbuf[pl.ds(r0, CONF_RB), :]
        xc = u - jnp.mean(u, axis=-1, keepdims=True)
        y = xc * lax.rsqrt(jnp.mean(xc * xc, axis=-1, keepdims=True) + EPS) * lng_ref[...] + lnb_ref[...]
        y = _silu(y).astype(BF16)
        o_ref[pl.ds(r0, CONF_RB), :] = jnp.dot(y, pw_ref[...], preferred_element_type=F32) + pwb_ref[...]
        return carry

    lax.fori_loop(0, CH_ROWS // CONF_RB, norm_rb, 0, unroll=True)


def _conformer(l, cols, n_ctx_rows, with_ctx, w, b, lng, lnb, pw, pwb):
    rows = cols.shape[0]
    nc = rows // CH_ROWS
    nc_ctx = n_ctx_rows // CH_ROWS
    first = 0 if with_ctx else nc_ctx
    hpc = CH_ROWS // CONF_HALO
    n_halo = rows // CONF_HALO
    return pl.pallas_call(
        functools.partial(_conf_kernel, first, nc_ctx, nc),
        out_shape=jax.ShapeDtypeStruct(((nc - first) * CH_ROWS, CONF_W), F32),
        grid=(nc - first,),
        in_specs=[
            pl.BlockSpec((CH_ROWS, 2 * CONF_W), lambda i: (i + first, 0)),
            pl.BlockSpec((CONF_HALO, 2 * CONF_W), lambda i: (jnp.maximum((i + first) * hpc - 1, 0), 0)),
            pl.BlockSpec((CONF_HALO, 2 * CONF_W), lambda i: (jnp.minimum((i + first + 1) * hpc, n_halo - 1), 0)),
            _layer_spec(l, (CONF_K, CONF_W)),
            _layer_spec(l, (1, CONF_W)),
            _layer_spec(l, (1, CONF_W)),
            _layer_spec(l, (1, CONF_W)),
            _layer_spec(l, (CONF_W, CONF_W)),
            _layer_spec(l, (1, CONF_W)),
        ],
        out_specs=pl.BlockSpec((CH_ROWS, CONF_W), lambda i: (i, 0)),
        scratch_shapes=[pltpu.VMEM((CONF_W // LANE, CH_ROWS + 2 * CONF_HALO, LANE), F32),
                        pltpu.VMEM((CH_ROWS, CONF_W), F32)],
        compiler_params=_cparams(("arbitrary",)),
        name="conformer",
    )(cols, cols, cols, w, b, lng, lnb, pw, pwb)


def _s5_kernel(direction, *refs):
    if direction == 0:
        u_ref, bmat_ref, ar_ref, ai_ref, cmat_ref, o_ref, hbuf, st = refs
    else:
        (u_ref, bmat_ref, ar_ref, ai_ref, cmat_ref, yf_ref, dsk_ref, gw_ref, gb_ref,
         o_ref, hbuf, st) = refs
    i = pl.program_id(0)

    @pl.when(i == 0)
    def _():
        st[...] = jnp.zeros_like(st)

    u = u_ref[...]
    rb = 2 * LANE
    for k in range(u.shape[0] // rb):
        hbuf[k * rb:(k + 1) * rb, :] = jnp.dot(u[k * rb:(k + 1) * rb, :].astype(BF16), bmat_ref[...],
                                                preferred_element_type=F32)
    ar = jnp.broadcast_to(ar_ref[...], (NB, S5_S))
    ai = jnp.broadcast_to(ai_ref[...], (NB, S5_S))

    def step(j, carry):
        hr, hi = carry
        t = j if direction == 0 else S5_TL - 1 - j
        r0 = pl.multiple_of(t * NB, NB)
        nr = ar * hr - ai * hi + hbuf[pl.ds(r0, NB), 0:S5_S]
        ni = ar * hi + ai * hr + hbuf[pl.ds(r0, NB), S5_S:]
        hbuf[pl.ds(r0, NB), 0:S5_S] = nr
        hbuf[pl.ds(r0, NB), S5_S:] = ni
        return nr, ni

    hr, hi = lax.fori_loop(0, S5_TL, step, (st[0], st[1]), unroll=2)
    st[0] = hr
    st[1] = hi
    for k in range(u.shape[0] // rb):
        rows = slice(k * rb, (k + 1) * rb)
        o_ref[rows, :] = jnp.dot(hbuf[rows, :].astype(BF16), cmat_ref[...], preferred_element_type=F32)
    if direction == 1:
        for k in range(u.shape[0] // rb):
            rows = slice(k * rb, (k + 1) * rb)
            v = jax.nn.gelu(o_ref[rows, :] + yf_ref[rows, :] + dsk_ref[...] * u[rows, :], approximate=True)
            gv = jnp.dot(v.astype(BF16), gw_ref[...], preferred_element_type=F32) + gb_ref[...]
            o_ref[rows, :] = gv[:, :S5_W] * jax.nn.sigmoid(gv[:, S5_W:])


def _s5_pass(l, direction, u, n_ctx_rows, bmat, ar, ai, cmat, yf, dsk, gw, gb):
    rows = u.shape[0]
    tr = S5_TL * NB
    nblk = rows // tr
    nb_ctx = n_ctx_rows // tr
    blk = functools.partial(_chunk_of, direction, nc_ctx=nb_ctx, nc=nblk)

    def dir_spec(shape):
        nd = len(shape)
        return pl.BlockSpec((None, None) + tuple(shape), lambda i: (l, direction) + (0,) * nd)

    in_specs = [
        pl.BlockSpec((tr, S5_W), lambda i: (blk(i), 0)),
        dir_spec((S5_W, 2 * S5_S)),
        dir_spec((1, S5_S)),
        dir_spec((1, S5_S)),
        _layer_spec(l, (2 * S5_S, S5_W)),
    ]
    args = [u, bmat, ar, ai, cmat]
    if direction == 1:
        in_specs += [
            pl.BlockSpec((tr, S5_W), lambda i: (blk(i), 0)),
            _layer_spec(l, (1, S5_W)),
            _layer_spec(l, (S5_W, 2 * S5_W)),
            _layer_spec(l, (1, 2 * S5_W)),
        ]
        args += [yf, dsk, gw, gb]
    return pl.pallas_call(
        functools.partial(_s5_kernel, direction),
        out_shape=jax.ShapeDtypeStruct((rows, S5_W), F32),
        grid=(nblk,),
        in_specs=in_specs,
        out_specs=pl.BlockSpec((tr, S5_W), lambda i: (blk(i), 0)),
        scratch_shapes=[pltpu.VMEM((tr, 2 * S5_S), F32), pltpu.VMEM((2, NB, S5_S), F32)],
        compiler_params=_cparams(("arbitrary",)),
        name="s5_fwd" if direction == 0 else "s5_bwd",
    )(*args)


def _s5_operators(lam_re, lam_im, log_step, b_re, b_im, c_re, c_im):
    depth = lam_re.shape[0]
    eye = jnp.eye(S5_G, dtype=F32)
    step = jnp.exp(log_step)[..., None]
    mag = jnp.exp(lam_re * step)
    ar, ai = mag * jnp.cos(lam_im * step), mag * jnp.sin(lam_im * step)
    inv_den = 1.0 / (lam_re * lam_re + lam_im * lam_im)
    cr = ((ar - 1.0) * lam_re + ai * lam_im) * inv_den
    ci = (ai * lam_re - (ar - 1.0) * lam_im) * inv_den
    bre, bim = b_re[:, None], b_im[:, None]
    bbr = cr[..., None] * bre - ci[..., None] * bim
    bbi = cr[..., None] * bim + ci[..., None] * bre
    bd_r = jnp.einsum("ldgph,gk->ldghkp", bbr, eye).reshape(depth, 2, S5_W, S5_S)
    bd_i = jnp.einsum("ldgph,gk->ldghkp", bbi, eye).reshape(depth, 2, S5_W, S5_S)
    bmat = jnp.concatenate([bd_r, bd_i], axis=-1).astype(BF16)
    cre = jnp.einsum("lghp,gk->lgpkh", c_re, eye).reshape(depth, S5_S, S5_W)
    cim = jnp.einsum("lghp,gk->lgpkh", c_im, eye).reshape(depth, S5_S, S5_W)
    cmat = jnp.concatenate([cre, -cim], axis=1).astype(BF16)
    return bmat, ar.reshape(depth, 2, 1, S5_S), ai.reshape(depth, 2, 1, S5_S), cmat


def _route_rows(lg):
    m = jnp.max(lg, axis=0, keepdims=True)
    ex = jnp.exp(lg - m)
    probs = ex / jnp.sum(ex, axis=0, keepdims=True)
    p = [probs[e:e + 1, :] for e in range(NE)]
    tops = []
    for q in range(NGRP):
        v = p[q * EPG:(q + 1) * EPG]
        m1 = functools.reduce(jnp.maximum, v)
        i1 = jnp.full_like(m1, float(EPG - 1))
        for j in range(EPG - 2, -1, -1):
            i1 = jnp.where(v[j] == m1, float(j), i1)
        rest = [jnp.where(i1 == float(j), -jnp.inf, v[j]) for j in range(EPG)]
        m2 = functools.reduce(jnp.maximum, rest)
        i2 = jnp.full_like(m2, float(EPG - 1))
        for j in range(EPG - 2, -1, -1):
            i2 = jnp.where(jnp.logical_and(rest[j] == m2, i1 != float(j)), float(j), i2)
        tops.append((m1, i1, m2, i2))
    score = [t[0] + t[2] for t in tops]
    best = functools.reduce(jnp.maximum, score)
    sel = tops[NGRP - 1] + (jnp.full_like(best, float(NGRP - 1)),)
    for q in range(NGRP - 2, -1, -1):
        hit = score[q] == best
        sel = tuple(jnp.where(hit, a, b) for a, b in zip(tops[q] + (jnp.full_like(best, float(q)),), sel))
    m1, i1, m2, i2, grp = sel
    den = m1 + m2
    return grp * EPG + i1, grp * EPG + i2, m1 / den, m2 / den


def _dot_nt_3pass(a, b):
    nt = (((1,), (1,)), ((), ()))
    a_hi = a.astype(BF16)
    b_hi = b.astype(BF16)
    a_lo = (a - a_hi.astype(F32)).astype(BF16)
    b_lo = (b - b_hi.astype(F32)).astype(BF16)
    out = lax.dot_general(a_hi, b_hi, nt, preferred_element_type=F32)
    out = out + lax.dot_general(a_lo, b_hi, nt, preferred_element_type=F32)
    return out + lax.dot_general(a_hi, b_lo, nt, preferred_element_type=F32)


def _out_kernel(a_ref, b_ref, s_ref, x_ref, mod_ref, g_ref, w_ref, rw_ref, rb_ref,
                x1_ref, h2_ref, rt_ref, gc_ref, cnt_ref):
    tm = x_ref.shape[0]
    rb = 2 * LANE
    for k in range(tm // rb):
        rows = slice(k * rb, (k + 1) * rb)
        mix = jnp.dot(a_ref[rows, :].astype(BF16), w_ref[0:SSD_W, :], preferred_element_type=F32)
        mix = mix + jnp.dot(b_ref[rows, :].astype(BF16), w_ref[SSD_W:SSD_W + CONF_W, :],
                            preferred_element_type=F32)
        mix = mix + jnp.dot(s_ref[rows, :].astype(BF16), w_ref[SSD_W + CONF_W:, :], preferred_element_type=F32)
        x1 = (x_ref[rows, :].reshape(rb // NB, NB, D) + mod_ref[2][None] * mix.reshape(rb // NB, NB, D))
        x1 = x1.reshape(rb, D)
        x1_ref[rows, :] = x1
        ms = jnp.mean(x1 * x1, axis=-1, keepdims=True)
        xn = x1 * lax.rsqrt(ms + EPS) * g_ref[...]
        h2_ref[rows, :] = (xn.reshape(rb // NB, NB, D) * (1.0 + mod_ref[4])[None] + mod_ref[3][None]).reshape(rb, D)
    lg = _dot_nt_3pass(rw_ref[...], h2_ref[...]) + rb_ref[...]
    e1, e2, g1, g2 = _route_rows(lg)
    eid = lax.broadcasted_iota(I32, (NE, tm), 0).astype(F32)
    oh0 = (eid == e1).astype(F32)
    oh1 = (eid == e2).astype(F32)
    t0 = lax.broadcasted_iota(I32, (tm, tm), 0)
    t1 = lax.broadcasted_iota(I32, (tm, tm), 1)
    before = (t0 < t1).astype(BF16)
    pre0 = jnp.dot(oh0.astype(BF16), before, preferred_element_type=F32)
    pre1 = jnp.dot(oh1.astype(BF16), before, preferred_element_type=F32)
    lr0 = jnp.sum(oh0 * pre0, axis=0, keepdims=True)
    lr1 = jnp.sum(oh1 * pre1, axis=0, keepdims=True)
    rt_ref[...] = jnp.concatenate([e1, e2, lr0, lr1, jnp.zeros((NB - 4, tm), F32)], axis=0)
    gates = jnp.concatenate([g1, g2, jnp.zeros((LANE - 2, tm), F32)], axis=0)
    gc_ref[...] = gates.T
    cnt = jnp.concatenate([jnp.sum(oh0, axis=1, keepdims=True), jnp.sum(oh1, axis=1, keepdims=True)], axis=0)
    cnt_ref[...] = jnp.broadcast_to(cnt, (2 * NE, LANE))


def _out_proj(l, a, bconf, s, x, modtabs, g_ffn, w_out, rw_t, rb, n_ctx_rows, conf_first, blk0, nblk):
    tm = ROW_TM
    ctx_blk = n_ctx_rows // tm
    rows = nblk * tm
    return pl.pallas_call(
        _out_kernel,
        out_shape=[
            jax.ShapeDtypeStruct((rows, D), F32),
            jax.ShapeDtypeStruct((rows, D), F32),
            jax.ShapeDtypeStruct((NB, rows), F32),
            jax.ShapeDtypeStruct((rows, LANE), F32),
            jax.ShapeDtypeStruct((nblk, 2 * NE, LANE), F32),
        ],
        grid=(nblk,),
        in_specs=[
            pl.BlockSpec((tm, SSD_W), lambda i: (i + blk0, 0)),
            pl.BlockSpec((tm, CONF_W), lambda i: (i + blk0 - conf_first, 0)),
            pl.BlockSpec((tm, S5_W), lambda i: (i + blk0, 0)),
            pl.BlockSpec((tm, D), lambda i: (i + blk0, 0)),
            _mod_spec(l, ctx_blk, blk0),
            _layer_spec(l, (1, D)),
            _layer_spec(l, (D, D)),
            _const_spec((NE, D)),
            _const_spec((NE, 1)),
        ],
        out_specs=[
            pl.BlockSpec((tm, D), lambda i: (i, 0)),
            pl.BlockSpec((tm, D), lambda i: (i, 0)),
            pl.BlockSpec((NB, tm), lambda i: (0, i)),
            pl.BlockSpec((tm, LANE), lambda i: (i, 0)),
            pl.BlockSpec((None, 2 * NE, LANE), lambda i: (i, 0, 0)),
        ],
        compiler_params=_cparams(("arbitrary",)),
        name="out_proj_route",
    )(a, bconf, s, x, modtabs, g_ffn, w_out, rw_t, rb)


def _moe_kernel(l, second, it_ref, ie_ref, lo_ref, hi_ref, nx_ref, h_ref, wg_hbm, wu_hbm, wd_hbm, *rest):
    rest = rest[1:] if second else rest
    out_ref, stage, wgb, wub, wdb, sem, cur = rest
    o_ref = out_ref if second else out_ref.at[0]
    w = pl.program_id(0)
    lo = lo_ref[w]
    hi = hi_ref[w]
    new_tile = jnp.logical_or(w == 0, it_ref[w] != it_ref[jnp.maximum(w - 1, 0)])

    def fetch(e, slot):
        return [pltpu.make_async_copy(src.at[l, e], stage.at[slot, k], sem.at[slot, k])
                for k, src in enumerate((wg_hbm, wu_hbm, wd_hbm))]

    @pl.when(w == 0)
    def _():
        cur[0] = -1
        cur[1] = nx_ref[0]
        cur[2] = 0

        @pl.when(nx_ref[0] >= 0)
        def _():
            for c in fetch(nx_ref[0], 0):
                c.start()

    @pl.when(new_tile)
    def _():
        out_ref[...] = jnp.zeros_like(out_ref)

    @pl.when(hi > lo)
    def _():
        e = ie_ref[w]

        @pl.when(cur[0] != e)
        def _():
            slot = cur[2]
            for c in fetch(e, slot):
                c.wait()
            nxt = nx_ref[1 + e]

            @pl.when(nxt >= 0)
            def _():
                for c in fetch(nxt, 1 - slot):
                    c.start()

            wgb[...] = stage[slot, 0].astype(BF16)
            wub[...] = stage[slot, 1].astype(BF16)
            wdb[...] = stage[slot, 2].astype(BF16)
            cur[0] = e
            cur[1] = nxt
            cur[2] = 1 - slot

        h = h_ref[...].astype(BF16)
        gate = jnp.dot(h, wgb[...], preferred_element_type=F32)
        up = jnp.dot(h, wub[...], preferred_element_type=F32)
        act = (_silu(gate) * up).astype(BF16)
        y = jnp.dot(act, wdb[...], preferred_element_type=F32)
        r = lax.broadcasted_iota(I32, (MOE_TM, 1), 0)
        keep = jnp.logical_and(r >= lo, r < hi)
        o_ref[...] = jnp.where(keep, y, o_ref[...])


def _moe_experts(l, hs, items, tile0, w_gate, w_up, w_down, prev):
    item_tile, item_expert, item_lo, item_hi, next_expert = items
    n_items = item_tile.shape[0]
    n_prefetch = 5
    second = prev is not None
    hbm = pl.BlockSpec(memory_space=pl.ANY)
    in_specs = [pl.BlockSpec((MOE_TM, D), lambda w, it, ie, lo, hi, nx: (it[w] - tile0, 0)), hbm, hbm, hbm]
    args = [item_tile, item_expert, item_lo, item_hi, next_expert, hs, w_gate, w_up, w_down]
    aliases = {}
    if second:
        in_specs.append(hbm)
        args.append(prev)
        aliases = {len(args) - 1: 0}
        out_spec = pl.BlockSpec((None, MOE_TM, D), lambda w, it, ie, lo, hi, nx: (1, it[w] - tile0, 0))
    else:
        out_spec = pl.BlockSpec((2, MOE_TM, D), lambda w, it, ie, lo, hi, nx: (0, it[w] - tile0, 0))
    n_mats = 3
    return pl.pallas_call(
        functools.partial(_moe_kernel, l, second),
        out_shape=jax.ShapeDtypeStruct((2, hs.shape[0], D), F32),
        grid_spec=pltpu.PrefetchScalarGridSpec(
            num_scalar_prefetch=n_prefetch,
            grid=(n_items,),
            in_specs=in_specs,
            out_specs=out_spec,
            scratch_shapes=[pltpu.VMEM((2, n_mats, D, D), F32)] + [pltpu.VMEM((D, D), BF16)] * n_mats
            + [pltpu.SemaphoreType.DMA((2, n_mats)), pltpu.SMEM((3,), I32)],
        ),
        input_output_aliases=aliases,
        compiler_params=_cparams(("arbitrary",)),
        name="moe_experts",
    )(*args)


def _moe_plan(route, cnt, tile_parts):
    rows = route.shape[1]
    nblk = cnt.shape[0]
    n_flat = 2 * rows
    e = route[0:2].astype(I32)
    lrank = route[2:4].astype(I32)
    counts = cnt[:, :, 0].astype(I32).reshape(nblk, 2, NE)
    per = jnp.transpose(counts, (1, 0, 2)).reshape(2 * nblk, NE)
    before = jnp.cumsum(per, axis=0) - per
    gcount = jnp.sum(per, axis=0)
    gend = jnp.cumsum(gcount)
    gstart = gend - gcount
    base = jnp.transpose((before + gstart[None, :]).reshape(2, nblk, NE), (2, 0, 1))[..., None]
    onehot = e.reshape(1, 2, nblk, ROW_TM) == jnp.arange(NE, dtype=I32).reshape(NE, 1, 1, 1)
    inv = jnp.sum(jnp.where(onehot, base, 0), axis=0).reshape(2, rows) + lrank
    keys = (e * 65536).reshape(-1) + jnp.arange(n_flat, dtype=I32)
    order = lax.sort(keys, is_stable=False) & 0xFFFF
    tok = jnp.where(order >= rows, order - rows, order)
    items = []
    for t0, tn in tile_parts:
        row_lo, row_hi = t0 * MOE_TM, (t0 + tn) * MOE_TM
        tile_start = (t0 + jnp.arange(tn, dtype=I32)) * MOE_TM
        bnd = lax.sort(jnp.concatenate([tile_start, jnp.clip(gstart[1:], row_lo, row_hi)]), is_stable=False)
        bnd_hi = jnp.concatenate([bnd[1:], jnp.full((1,), row_hi, I32)])
        item_tile = jnp.minimum(bnd // MOE_TM, t0 + tn - 1)
        first_row = jnp.minimum(bnd, row_hi - 1)[:, None]
        item_expert = jnp.minimum(jnp.sum((gend[None, :] <= first_row).astype(I32), axis=1), NE - 1)
        ids = jnp.arange(NE, dtype=I32)
        present = jnp.logical_and(gend > row_lo, gstart < row_hi) & (gcount > 0)
        later = jnp.logical_and(present[None, :], ids[None, :] > jnp.arange(-1, NE, dtype=I32)[:, None])
        next_expert = jnp.min(jnp.where(later, ids[None, :], NE), axis=1)
        next_expert = jnp.where(next_expert == NE, -1, next_expert).astype(I32)
        items.append((item_tile, item_expert, bnd - item_tile * MOE_TM, bnd_hi - item_tile * MOE_TM, next_expert))
    return tok, inv, items


def _final_kernel(x_ref, ya_ref, yb_ref, gc_ref, mod_ref, g_ref, o_ref, scr):
    tm = x_ref.shape[0]
    gc = gc_ref[...]
    ff = (gc[:, 0:1] * ya_ref[...] + gc[:, 1:2] * yb_ref[...]).reshape(tm // NB, NB, D)
    x2 = (x_ref[...].reshape(tm // NB, NB, D) + mod_ref[5][None] * ff).reshape(tm, D)
    ms = jnp.mean(x2 * x2, axis=-1, keepdims=True)
    x2 = x2 * lax.rsqrt(ms + EPS) * g_ref[...]
    ncb = D // LANE
    for k in range(ncb):
        scr[k] = x2[:, k * LANE:(k + 1) * LANE]
    for b in range(NB):
        for k in range(ncb):
            o_ref[b, :, k * LANE:(k + 1) * LANE] = scr[k, pl.ds(b, ROW_TQ, stride=NB), :]


def _final_combine(l, x1, ya, yb, gcol, modtabs, g_final):
    tm = ROW_TM
    rows = x1.shape[0]
    row_spec = pl.BlockSpec((tm, D), lambda i: (i, 0))
    return pl.pallas_call(
        _final_kernel,
        out_shape=jax.ShapeDtypeStruct((NB, rows // NB, D), F32),
        grid=(rows // tm,),
        in_specs=[row_spec, row_spec, row_spec, pl.BlockSpec((tm, LANE), lambda i: (i, 0)),
                  _mod_spec(l, 0), _const_spec((1, D))],
        out_specs=pl.BlockSpec((NB, ROW_TQ, D), lambda i: (0, i, 0)),
        scratch_shapes=[pltpu.VMEM((D // LANE, tm, LANE), F32)],
        compiler_params=_cparams(("arbitrary",)),
        name="moe_combine_final",
    )(x1, ya, yb, gcol, modtabs, g_final)


def _grid_pos_embed(rows_n):
    rr, cc = jnp.meshgrid(jnp.arange(rows_n, dtype=F32), jnp.arange(GRID_W, dtype=F32), indexing="ij")
    quarter = D // 4
    inv_freq = jnp.exp(-math.log(10000.0) * jnp.arange(quarter, dtype=F32) / quarter)

    def emb(pos):
        ang = pos.reshape(-1)[:, None] * inv_freq[None, :]
        return jnp.concatenate([jnp.sin(ang), jnp.cos(ang)], axis=-1)

    return jnp.concatenate([emb(rr), emb(cc)], axis=-1)


def _take_rows(a, idx):
    return a.at[idx].get(mode="promise_in_bounds")


def _pad_last(v, width):
    return jnp.pad(v, [(0, 0)] * (v.ndim - 1) + [(0, width - v.shape[-1])])


def kernel(x, c, ctx, c_ctx, w_ada, b_ada, g_mix, w_in, ssd_conv_w, ssd_conv_b, ssd_dt_bias, ssd_a_log, ssd_d, ssd_norm_g, conf_dw_w, conf_dw_b, conf_ln_g, conf_ln_b, conf_pw_w, conf_pw_b, s5_lambda_re, s5_lambda_im, s5_log_step, s5_b_re, s5_b_im, s5_c_re, s5_c_im, s5_d, s5_glu_w, s5_glu_b, w_out, g_ffn, router_w, router_b, exp_w_gate, exp_w_up, exp_w_down, g_final):
    bsz, seq, _ = x.shape
    ctx_len = ctx.shape[1]
    depth = w_ada.shape[0]
    assert bsz == NB and seq % SSD_CHUNK == 0 and ctx_len % SSD_CHUNK == 0
    n_ctx_rows = ctx_len * NB
    rows_all = (ctx_len + seq) * NB

    cond = jnp.concatenate([c, c_ctx[None, :], jnp.zeros((2 * NB - bsz - 1, D), F32)], axis=0)
    mod = _modulation(cond, w_ada, b_ada)
    mod = mod.reshape(depth, 2 * NB, N_MOD, D)
    mod_lat = jnp.transpose(mod[:, :NB], (0, 2, 1, 3))
    mod_ctx = jnp.broadcast_to(mod[:, NB][:, :, None, :], (depth, N_MOD, NB, D))
    modtabs = jnp.stack([mod_ctx, mod_lat], axis=1)

    o_b = SSD_W + SSD_XBC + 2 * SSD_H
    w_in_p = jnp.concatenate(
        [w_in[:, :, :o_b], jnp.zeros((depth, D, DT_PAD - 2 * SSD_H), F32), w_in[:, :, o_b:]], axis=2).astype(BF16)
    g_mix3 = g_mix.reshape(depth, 1, D)
    ssd_cb = ssd_conv_b.reshape(depth, 1, SSD_XBC)
    ssd_dtb = _pad_last(ssd_dt_bias.reshape(depth, 1, 2 * SSD_H), DT_PAD)
    ssd_alog = _pad_last(ssd_a_log.reshape(depth, 1, 2 * SSD_H), DT_PAD)
    ssd_dsk = jnp.repeat(ssd_d, SSD_P, axis=1).reshape(depth, 1, SSD_W)
    ssd_ng = ssd_norm_g.reshape(depth, 1, SSD_W)
    conf_b = conf_dw_b.reshape(depth, 1, CONF_W)
    conf_lg = conf_ln_g.reshape(depth, 1, CONF_W)
    conf_lb = conf_ln_b.reshape(depth, 1, CONF_W)
    conf_pw = conf_pw_w.astype(BF16)
    conf_pb = conf_pw_b.reshape(depth, 1, CONF_W)
    s5_bmat, s5_ar, s5_ai, s5_cmat = _s5_operators(s5_lambda_re, s5_lambda_im, s5_log_step,
                                                   s5_b_re, s5_b_im, s5_c_re, s5_c_im)
    s5_dsk = s5_d.reshape(depth, 1, S5_W)
    s5_gw = s5_glu_w.astype(BF16)
    s5_gb = s5_glu_b.reshape(depth, 1, 2 * S5_W)
    w_out_b = w_out.astype(BF16)
    g_ffn3 = g_ffn.reshape(depth, 1, D)
    router_w_t = router_w.T
    router_b2 = router_b.reshape(NE, 1)
    g_final2 = g_final.reshape(1, D)
    pos = _grid_pos_embed(seq // GRID_W)

    pending = (ctx, x)
    for l in range(depth):
        last = l == depth - 1
        ssd_cols, conf_cols, s5_u, xall = _in_proj(l, l == 0, pending, pos, modtabs, g_mix3, w_in_p,
                                                   rows_all, n_ctx_rows)
        ypart = _ssd_pass(l, 0, ssd_cols, n_ctx_rows, ssd_conv_w, ssd_cb, ssd_dtb, ssd_alog, ssd_dsk, None)
        a_mix = _ssd_pass(l, 1, ssd_cols, n_ctx_rows, ssd_conv_w, ssd_cb, ssd_dtb, ssd_alog, ssd_ng, ypart)
        b_mix = _conformer(l, conf_cols, n_ctx_rows, not last, conf_dw_w, conf_b, conf_lg, conf_lb, conf_pw, conf_pb)
        yf = _s5_pass(l, 0, s5_u, n_ctx_rows, s5_bmat, s5_ar, s5_ai, s5_cmat, None, None, None, None)
        s_mix = _s5_pass(l, 1, s5_u, n_ctx_rows, s5_bmat, s5_ar, s5_ai, s5_cmat, yf, s5_dsk, s5_gw, s5_gb)
        ctx_blk = n_ctx_rows // ROW_TM
        first = ctx_blk if last else 0
        n_out = rows_all // ROW_TM - first
        x1, h2, route, gcol, cnt = _out_proj(l, a_mix, b_mix, s_mix, xall, modtabs, g_ffn3, w_out_b,
                                             router_w_t, router_b2, n_ctx_rows, first, first, n_out)
        n_tiles = 2 * n_out * ROW_TM // MOE_TM
        assert n_tiles % 2 == 0
        tile_parts = [(0, n_tiles // 2), (n_tiles // 2, n_tiles // 2)]
        tok, inv, items = _moe_plan(route, cnt, tile_parts)
        hs = [_take_rows(h2, tok[t0 * MOE_TM:(t0 + tn) * MOE_TM]) for t0, tn in tile_parts]
        y = None
        for (t0, tn), h, it in zip(tile_parts, hs, items):
            y = _moe_experts(l, h, it, t0, exp_w_gate, exp_w_up, exp_w_down, y)
        y = y.reshape(n_tiles * MOE_TM, D)
        ya = _take_rows(y, inv[0])
        yb = _take_rows(y, inv[1])
        pending = (x1, ya, yb, gcol)
    return _final_combine(depth - 1, *pending, modtabs, g_final2)
```

```python
import functools
import math

import jax
import jax.numpy as jnp
from jax import lax
from jax.experimental import pallas as pl
from jax.experimental.pallas import tpu as pltpu

F32 = jnp.float32
BF16 = jnp.bfloat16
I32 = jnp.int32
HIGHEST = lax.Precision.HIGHEST

NB = 8
D = 1024
GRID_W = 64
N_MOD = 6
EPS = 1e-6
LANE = 128
SSD_W = 512
SSD_P = 64
SSD_H = 8
SSD_G = 2
SSD_N = 128
SSD_K = 5
SSD_XBC = SSD_W + 2 * SSD_G * SSD_N
SSD_CHUNK = 128
DT_PAD = LANE
SSD_COLS_P = SSD_W + SSD_XBC + DT_PAD
CONF_W = 256
CONF_K = 31
S5_W = 256
S5_G = 16
S5_P = 64
S5_CH = 16
S5_S = S5_G * S5_P
S5_TL = 128
NE = 16
NGRP = 4
EPG = 4
MOE_TM = 256
IN_COLS_P = SSD_COLS_P + 2 * CONF_W + S5_W
ROW_TM = 512
ROW_TQ = ROW_TM // NB
CH_ROWS = SSD_CHUNK * NB
VMEM_LIMIT = 56 * 1024 * 1024


def _cparams(sem):
    return pltpu.CompilerParams(dimension_semantics=sem, vmem_limit_bytes=VMEM_LIMIT)


def _const_spec(shape):
    nd = len(shape)
    return pl.BlockSpec(shape, lambda *_: (0,) * nd)


def _layer_spec(l, shape):
    nd = len(shape)
    return pl.BlockSpec((None,) + tuple(shape), lambda *_: (l,) + (0,) * nd)


def _silu(v):
    return v * jax.nn.sigmoid(v)


def _mod_kernel(c_ref, w_ref, b_ref, o_ref):
    c = c_ref[...]
    h = _silu(c).astype(BF16)
    o_ref[...] = jnp.dot(h, w_ref[...].astype(BF16), preferred_element_type=F32) + b_ref[...]


def _modulation(cond, w_ada, b_ada):
    depth = w_ada.shape[0]
    nrow = cond.shape[0]
    return pl.pallas_call(
        _mod_kernel,
        out_shape=jax.ShapeDtypeStruct((depth, nrow, N_MOD * D), F32),
        grid=(depth, N_MOD),
        in_specs=[
            pl.BlockSpec((nrow, D), lambda l, j: (0, 0)),
            pl.BlockSpec((None, D, D), lambda l, j: (l, 0, j)),
            pl.BlockSpec((None, 1, D), lambda l, j: (l, 0, j)),
        ],
        out_specs=pl.BlockSpec((None, nrow, D), lambda l, j: (l, 0, j)),
        compiler_params=_cparams(("arbitrary", "arbitrary")),
        name="adaln_mod",
    )(cond, w_ada, b_ada.reshape(depth, 1, N_MOD * D))


def _mod_spec(l, ctx_blk, first=0):
    return pl.BlockSpec((None, None, N_MOD, NB, D),
                        lambda i, *_: (l, jnp.where(i + first < ctx_blk, 0, 1), 0, 0, 0))


def _norm_mod_project(x, mod_ref, g_ref, w_ref, ssd_ref, conf_ref, s5_ref):
    tm = x.shape[0]
    ms = jnp.mean(x * x, axis=-1, keepdims=True)
    xn = x * lax.rsqrt(ms + EPS) * g_ref[...]
    h = xn.reshape(tm // NB, NB, D) * (1.0 + mod_ref[1])[None] + mod_ref[0][None]
    h = h.reshape(tm, D).astype(BF16)
    ssd_ref[...] = jnp.dot(h, w_ref[:, :SSD_COLS_P], preferred_element_type=F32)
    conf_ref[...] = jnp.dot(h, w_ref[:, SSD_COLS_P:SSD_COLS_P + 2 * CONF_W], preferred_element_type=F32)
    s5_ref[...] = jnp.dot(h, w_ref[:, SSD_COLS_P + 2 * CONF_W:], preferred_element_type=F32)


def _in_first_kernel(ctx_blk, ctx_ref, x_ref, pos_ref, mod_ref, g_ref, w_ref,
                     ssd_ref, conf_ref, s5_ref, x0_ref, scr):
    i = pl.program_id(0)
    ncb = D // LANE

    @pl.when(i < ctx_blk)
    def _():
        for b in range(NB):
            for k in range(ncb):
                scr[k, pl.ds(b, ROW_TQ, stride=NB), :] = ctx_ref[b, :, k * LANE:(k + 1) * LANE]

    @pl.when(i >= ctx_blk)
    def _():
        for b in range(NB):
            for k in range(ncb):
                scr[k, pl.ds(b, ROW_TQ, stride=NB), :] = (
                    x_ref[b, :, k * LANE:(k + 1) * LANE] + pos_ref[:, k * LANE:(k + 1) * LANE])

    x = jnp.concatenate([scr[k] for k in range(ncb)], axis=1)
    x0_ref[...] = x
    _norm_mod_project(x, mod_ref, g_ref, w_ref, ssd_ref, conf_ref, s5_ref)


def _in_next_kernel(x1_ref, ya_ref, yb_ref, gc_ref, modp_ref, mod_ref, g_ref, w_ref,
                    ssd_ref, conf_ref, s5_ref, x_ref):
    tm = x1_ref.shape[0]
    gc = gc_ref[...]
    ff = (gc[:, 0:1] * ya_ref[...] + gc[:, 1:2] * yb_ref[...]).reshape(tm // NB, NB, D)
    x = (x1_ref[...].reshape(tm // NB, NB, D) + modp_ref[5][None] * ff).reshape(tm, D)
    x_ref[...] = x
    _norm_mod_project(x, mod_ref, g_ref, w_ref, ssd_ref, conf_ref, s5_ref)


def _in_proj(l, first, x_or_pair, pos, modtabs, g_mix, w_in_p, rows, n_ctx_rows):
    tm = ROW_TM
    nblk = rows // tm
    ctx_blk = n_ctx_rows // tm
    common_specs = [_mod_spec(l, ctx_blk), _layer_spec(l, (1, D)), _layer_spec(l, (D, IN_COLS_P))]
    common_args = [modtabs, g_mix, w_in_p]
    out_shape = [
        jax.ShapeDtypeStruct((rows, SSD_COLS_P), F32),
        jax.ShapeDtypeStruct((rows, 2 * CONF_W), F32),
        jax.ShapeDtypeStruct((rows, S5_W), F32),
    ]
    out_specs = [
        pl.BlockSpec((tm, SSD_COLS_P), lambda i: (i, 0)),
        pl.BlockSpec((tm, 2 * CONF_W), lambda i: (i, 0)),
        pl.BlockSpec((tm, S5_W), lambda i: (i, 0)),
    ]
    if first:
        ctx, x = x_or_pair
        body = functools.partial(_in_first_kernel, ctx_blk)
        in_specs = [
            pl.BlockSpec((NB, ROW_TQ, D), lambda i: (0, jnp.minimum(i, ctx_blk - 1), 0)),
            pl.BlockSpec((NB, ROW_TQ, D), lambda i: (0, jnp.maximum(i - ctx_blk, 0), 0)),
            pl.BlockSpec((ROW_TQ, D), lambda i: (jnp.maximum(i - ctx_blk, 0), 0)),
        ] + common_specs
        args = [ctx, x, pos] + common_args
        scratch = [pltpu.VMEM((D // LANE, tm, LANE), F32)]
    else:
        x1, ya, yb, gcol = x_or_pair
        body = _in_next_kernel
        in_specs = [
            pl.BlockSpec((tm, D), lambda i: (i, 0)),
            pl.BlockSpec((tm, D), lambda i: (i, 0)),
            pl.BlockSpec((tm, D), lambda i: (i, 0)),
            pl.BlockSpec((tm, LANE), lambda i: (i, 0)),
            _mod_spec(l - 1, ctx_blk),
        ] + common_specs
        args = [x1, ya, yb, gcol, modtabs] + common_args
        scratch = []
    out_shape.append(jax.ShapeDtypeStruct((rows, D), F32))
    out_specs.append(pl.BlockSpec((tm, D), lambda i: (i, 0)))
    return pl.pallas_call(
        body,
        out_shape=out_shape,
        grid=(nblk,),
        in_specs=in_specs,
        out_specs=out_specs,
        scratch_shapes=scratch,
        compiler_params=_cparams(("arbitrary",)),
        name="in_proj",
    )(*args)


def _chunk_of(direction, i, nc_ctx, nc):
    if direction == 0:
        return i
    return jnp.where(i < nc_ctx, nc_ctx - 1 - i, nc - 1 - (i - nc_ctx))


def _has_prev(c, nc_ctx):
    return jnp.logical_and(c != 0, c != nc_ctx)


def _has_next(c, nc_ctx, nc):
    return jnp.logical_and(c != nc_ctx - 1, c != nc - 1)


SSD_HALO = 2 * NB


def _ssd_conv(c, nc_ctx, nc, main_ref, prev_ref, next_ref, cw_ref, cb_ref, act, ext):
    has_prev = _has_prev(c, nc_ctx)
    has_next = _has_next(c, nc_ctx, nc)
    nblk = SSD_XBC // LANE
    n_rb = CH_ROWS // LANE
    for k in range(nblk):
        lo = SSD_W + k * LANE
        ext[0, k, 0:SSD_HALO, :] = jnp.where(has_prev, prev_ref[:, lo:lo + LANE], 0.0)
        ext[0, k, SSD_HALO:, :] = main_ref[0:LANE + SSD_HALO, lo:lo + LANE]
        ext[1, k, 0:LANE + SSD_HALO, :] = main_ref[CH_ROWS - LANE - SSD_HALO:, lo:lo + LANE]
        ext[1, k, LANE + SSD_HALO:, :] = jnp.where(has_next, next_ref[:, lo:lo + LANE], 0.0)

    def conv_block(tap_rows, r0):
        for k in range(nblk):
            lo = k * LANE
            acc = jnp.broadcast_to(cb_ref[:, lo:lo + LANE], (LANE, LANE))
            for tap in range(SSD_K):
                acc = acc + tap_rows(k, tap) * cw_ref[tap:tap + 1, lo:lo + LANE]
            act[k, pl.ds(r0, LANE), :] = _silu(acc)

    def conv_rb(rb, carry):
        r0 = pl.multiple_of(rb * LANE, LANE)
        conv_block(lambda k, tap: main_ref[pl.ds(r0 + NB * tap - SSD_HALO, LANE),
                                           SSD_W + k * LANE:SSD_W + (k + 1) * LANE], r0)
        return carry

    conv_block(lambda k, tap: ext[0, k, NB * tap:NB * tap + LANE, :], 0)
    lax.fori_loop(1, n_rb - 1, conv_rb, 0)
    conv_block(lambda k, tap: ext[1, k, NB * tap:NB * tap + LANE, :], CH_ROWS - LANE)


def _ssd_kernel(direction, nc_ctx, nc, *refs):
    if direction == 0:
        (main_ref, prev_ref, next_ref, cw_ref, cb_ref, dtb_ref, alog_ref, dsk_ref,
         o_ref, act, ext, dts, yout, state, tabs) = refs
    else:
        (z_ref, dtraw_ref, act, dtb_ref, alog_ref, ng_ref, yp_ref,
         o_ref, dts, yout, state, tabs) = refs
    i = pl.program_id(0)

    @pl.when(i == 0)
    def _():
        state[...] = jnp.zeros_like(state)

    if direction == 0:
        _ssd_conv(i, nc_ctx, nc, main_ref, prev_ref, next_ref, cw_ref, cb_ref, act, ext)
        raw = main_ref[:, SSD_W + SSD_XBC:] + dtb_ref[...]
    else:
        raw = dtraw_ref[...] + dtb_ref[...]
    dts[...] = jnp.maximum(raw, 0.0) + jnp.log1p(jnp.exp(-jnp.abs(raw)))
    a_row = -jnp.exp(alog_ref[...])

    tt = lax.broadcasted_iota(I32, (SSD_CHUNK, SSD_CHUNK), 0)
    ss = lax.broadcasted_iota(I32, (SSD_CHUNK, SSD_CHUNK), 1)
    mask = (ss <= tt) if direction == 0 else (ss >= tt)
    tmat = mask.astype(F32)
    last = SSD_CHUNK - 1 if direction == 0 else 0
    hpg = SSD_H // SSD_G
    lane_id = lax.broadcasted_iota(I32, (1, LANE), 1)
    half_lo = lane_id < SSD_P

    def decay_tables(b, carry):
        dt_b = dts[pl.ds(b, SSD_CHUNK, stride=NB), :]
        cs = jnp.dot(tmat, dt_b * a_row, precision=HIGHEST, preferred_element_type=F32)
        tabs[b, 0] = cs
        tabs[b, 1] = cs.T
        tabs[b, 2] = dt_b.T
        return carry

    lax.fori_loop(0, NB, decay_tables, 0, unroll=True)

    def per_batch(b, carry):
        sl = pl.ds(b, SSD_CHUNK, stride=NB)
        cs = tabs[b, 0]
        cs_t = tabs[b, 1]
        dt_t = tabs[b, 2]
        for g in range(SSD_G):
            bg_t = act[SSD_W // LANE + g, sl, :].T
            cg = act[SSD_W // LANE + SSD_G + g, sl, :]
            gmat = jnp.dot(cg.astype(BF16), bg_t.astype(BF16), preferred_element_type=F32)
            for pr in range(hpg // 2):
                k = g * (hpg // 2) + pr
                xs_pair = act[k, sl, :]
                s_pair = state[b, k]
                xs_b = xs_pair.astype(BF16)
                rhs = jnp.concatenate([xs_b, s_pair.astype(BF16)], axis=0)
                ys, ss, etot = [], [], []
                for j in range(2):
                    ln = direction * SSD_H + 2 * k + j
                    colb = jnp.broadcast_to(cs[:, ln:ln + 1], (SSD_CHUNK, SSD_CHUNK))
                    row = cs_t[ln:ln + 1, :]
                    dtr = dt_t[ln:ln + 1, :]
                    decay = jnp.where(mask, jnp.exp(jnp.where(mask, colb - row, 0.0)) * dtr, 0.0)
                    lhs = jnp.concatenate([(gmat * decay).astype(BF16),
                                           (cg * jnp.exp(colb)).astype(BF16)], axis=1)
                    ys.append(jnp.dot(lhs, rhs, preferred_element_type=F32))
                    tot = cs[last:last + 1, ln:ln + 1]
                    wrow = jnp.exp(tot - row) * dtr
                    ss.append(jnp.dot((bg_t * wrow).astype(BF16), xs_b, preferred_element_type=F32))
                    etot.append(jnp.exp(tot))
                acc_y = jnp.where(half_lo, ys[0], ys[1])
                state[b, k] = s_pair * jnp.where(half_lo, etot[0], etot[1]) + jnp.where(half_lo, ss[0], ss[1])
                if direction == 0:
                    acc_y = acc_y + dsk_ref[:, k * LANE:(k + 1) * LANE] * xs_pair
                yout[k, sl, :] = acc_y
        return carry

    lax.fori_loop(0, NB, per_batch, 0, unroll=2)

    nyb = SSD_W // LANE
    if direction == 0:
        for k in range(nyb):
            o_ref[:, k * LANE:(k + 1) * LANE] = yout[k]
    else:
        def fin(rb, carry):
            r0 = pl.multiple_of(rb * LANE, LANE)
            y = jnp.concatenate([yout[k, pl.ds(r0, LANE), :] for k in range(nyb)], axis=1)
            y = y + yp_ref[pl.ds(r0, LANE), :]
            y = y * _silu(z_ref[pl.ds(r0, LANE), :])
            ms = jnp.mean(y * y, axis=-1, keepdims=True)
            o_ref[pl.ds(r0, LANE), :] = y * lax.rsqrt(ms + EPS) * ng_ref[...]
            return carry

        lax.fori_loop(0, CH_ROWS // LANE, fin, 0, unroll=4)


def _ssd_pass(l, direction, cols, n_ctx_rows, cw, cb, dtb, alog, extra, fwd_out):
    rows = cols.shape[0]
    nc = rows // CH_ROWS
    nc_ctx = n_ctx_rows // CH_ROWS
    hpc = CH_ROWS // SSD_HALO
    n_halo = rows // SSD_HALO
    nact = SSD_XBC // LANE
    chunk = functools.partial(_chunk_of, direction, nc_ctx=nc_ctx, nc=nc)
    y_spec = pl.BlockSpec((CH_ROWS, SSD_W), lambda i: (chunk(i), 0))
    act_spec = pl.BlockSpec((nact, CH_ROWS, LANE), lambda i: (0, chunk(i), 0))
    small = [_layer_spec(l, (1, DT_PAD)), _layer_spec(l, (1, DT_PAD)), _layer_spec(l, (1, SSD_W))]
    scratch = [
        pltpu.VMEM((CH_ROWS, DT_PAD), F32),
        pltpu.VMEM((SSD_W // LANE, CH_ROWS, LANE), F32),
        pltpu.VMEM((NB, SSD_W // LANE, SSD_N, LANE), F32),
        pltpu.VMEM((NB, 3, SSD_CHUNK, DT_PAD), F32),
    ]
    if direction == 0:
        in_specs = [
            pl.BlockSpec((CH_ROWS, SSD_COLS_P), lambda i: (chunk(i), 0)),
            pl.BlockSpec((SSD_HALO, SSD_COLS_P), lambda i: (jnp.maximum(chunk(i) * hpc - 1, 0), 0)),
            pl.BlockSpec((SSD_HALO, SSD_COLS_P), lambda i: (jnp.minimum((chunk(i) + 1) * hpc, n_halo - 1), 0)),
            _layer_spec(l, (SSD_K, SSD_XBC)),
            _layer_spec(l, (1, SSD_XBC)),
        ] + small
        args = [cols, cols, cols, cw, cb, dtb, alog, extra]
        out_shape = [jax.ShapeDtypeStruct((rows, SSD_W), F32), jax.ShapeDtypeStruct((nact, rows, LANE), F32)]
        out_specs = [y_spec, act_spec]
        scratch = [pltpu.VMEM((2, nact, LANE + 2 * SSD_HALO, LANE), F32)] + scratch
    else:
        ypart, act = fwd_out
        dt_col = (SSD_W + SSD_XBC) // DT_PAD
        in_specs = [y_spec, pl.BlockSpec((CH_ROWS, DT_PAD), lambda i: (chunk(i), dt_col)), act_spec] + small + [y_spec]
        args = [cols, cols, act, dtb, alog, extra, ypart]
        out_shape = jax.ShapeDtypeStruct((rows, SSD_W), F32)
        out_specs = y_spec
    return pl.pallas_call(
        functools.partial(_ssd_kernel, direction, nc_ctx, nc),
        out_shape=out_shape,
        grid=(nc,),
        in_specs=in_specs,
        out_specs=out_specs,
        scratch_shapes=scratch,
        compiler_params=_cparams(("arbitrary",)),
        name="ssd_fwd" if direction == 0 else "ssd_bwd",
    )(*args)


CONF_HALO = 128
CONF_RB = 256


def _conf_kernel(first_chunk, nc_ctx, nc, main_ref, prev_ref, next_ref, w_ref, b_ref, lng_ref, lnb_ref,
                 pw_ref, pwb_ref, o_ref, ext, cbuf):
    c = pl.program_id(0) + first_chunk
    has_prev = _has_prev(c, nc_ctx)
    has_next = _has_next(c, nc_ctx, nc)

    def glu(ref):
        v = ref[...]
        return v[:, :CONF_W] * jax.nn.sigmoid(v[:, CONF_W:])

    um = glu(main_ref)
    up = jnp.where(has_prev, glu(prev_ref), 0.0)
    un = jnp.where(has_next, glu(next_ref), 0.0)
    nblk = CONF_W // LANE
    for k in range(nblk):
        ext[k, 0:CONF_HALO, :] = up[:, k * LANE:(k + 1) * LANE]
        ext[k, CONF_HALO:CONF_HALO + CH_ROWS, :] = um[:, k * LANE:(k + 1) * LANE]
        ext[k, CONF_HALO + CH_ROWS:, :] = un[:, k * LANE:(k + 1) * LANE]
    off = CONF_HALO - NB * (CONF_K // 2)

    def conv_rb(rb, carry):
        r0 = pl.multiple_of(rb * LANE, LANE)
        for k in range(nblk):
            lo = k * LANE
            acc = jnp.broadcast_to(b_ref[:, lo:lo + LANE], (LANE, LANE))
            for tap in range(CONF_K):
                acc = acc + ext[k, pl.ds(r0 + NB * tap + off, LANE), :] * w_ref[tap:tap + 1, lo:lo + LANE]
            cbuf[pl.ds(r0, LANE), lo:lo + LANE] = acc
        return carry

    lax.fori_loop(0, CH_ROWS // LANE, conv_rb, 0)

    def norm_rb(rb, carry):
        r0 = pl.multiple_of(rb * CONF_RB, CONF_RB)
        u = cbuf[pl.ds(r0, CONF_RB), :]
        xc = u - jnp.mean(u, axis=-1, keepdims=True)
        y = xc * lax.rsqrt(jnp.mean(xc * xc, axis=-1, keepdims=True) + EPS) * lng_ref[...] + lnb_ref[...]
        y = _silu(y).astype(BF16)
        o_ref[pl.ds(r0, CONF_RB), :] = jnp.dot(y, pw_ref[...], preferred_element_type=F32) + pwb_ref[...]
        return carry

    lax.fori_loop(0, CH_ROWS // CONF_RB, norm_rb, 0, unroll=True)


def _conformer(l, cols, n_ctx_rows, with_ctx, w, b, lng, lnb, pw, pwb):
    rows = cols.shape[0]
    nc = rows // CH_ROWS
    nc_ctx = n_ctx_rows // CH_ROWS
    first = 0 if with_ctx else nc_ctx
    hpc = CH_ROWS // CONF_HALO
    n_halo = rows // CONF_HALO
    return pl.pallas_call(
        functools.partial(_conf_kernel, first, nc_ctx, nc),
        out_shape=jax.ShapeDtypeStruct(((nc - first) * CH_ROWS, CONF_W), F32),
        grid=(nc - first,),
        in_specs=[
            pl.BlockSpec((CH_ROWS, 2 * CONF_W), lambda i: (i + first, 0)),
            pl.BlockSpec((CONF_HALO, 2 * CONF_W), lambda i: (jnp.maximum((i + first) * hpc - 1, 0), 0)),
            pl.BlockSpec((CONF_HALO, 2 * CONF_W), lambda i: (jnp.minimum((i + first + 1) * hpc, n_halo - 1), 0)),
            _layer_spec(l, (CONF_K, CONF_W)),
            _layer_spec(l, (1, CONF_W)),
            _layer_spec(l, (1, CONF_W)),
            _layer_spec(l, (1, CONF_W)),
            _layer_spec(l, (CONF_W, CONF_W)),
            _layer_spec(l, (1, CONF_W)),
        ],
        out_specs=pl.BlockSpec((CH_ROWS, CONF_W), lambda i: (i, 0)),
        scratch_shapes=[pltpu.VMEM((CONF_W // LANE, CH_ROWS + 2 * CONF_HALO, LANE), F32),
                        pltpu.VMEM((CH_ROWS, CONF_W), F32)],
        compiler_params=_cparams(("arbitrary",)),
        name="conformer",
    )(cols, cols, cols, w, b, lng, lnb, pw, pwb)


def _s5_kernel(direction, *refs):
    if direction == 0:
        u_ref, bmat_ref, ar_ref, ai_ref, cmat_ref, o_ref, hbuf, st = refs
    else:
        (u_ref, bmat_ref, ar_ref, ai_ref, cmat_ref, yf_ref, dsk_ref, gw_ref, gb_ref,
         o_ref, hbuf, st) = refs
    i = pl.program_id(0)

    @pl.when(i == 0)
    def _():
        st[...] = jnp.zeros_like(st)

    u = u_ref[...]
    rb = 2 * LANE
    for k in range(u.shape[0] // rb):
        hbuf[k * rb:(k + 1) * rb, :] = jnp.dot(u[k * rb:(k + 1) * rb, :].astype(BF16), bmat_ref[...],
                                                preferred_element_type=F32)
    ar = jnp.broadcast_to(ar_ref[...], (NB, S5_S))
    ai = jnp.broadcast_to(ai_ref[...], (NB, S5_S))

    def step(j, carry):
        hr, hi = carry
        t = j if direction == 0 else S5_TL - 1 - j
        r0 = pl.multiple_of(t * NB, NB)
        nr = ar * hr - ai * hi + hbuf[pl.ds(r0, NB), 0:S5_S]
        ni = ar * hi + ai * hr + hbuf[pl.ds(r0, NB), S5_S:]
        hbuf[pl.ds(r0, NB), 0:S5_S] = nr
        hbuf[pl.ds(r0, NB), S5_S:] = ni
        return nr, ni

    hr, hi = lax.fori_loop(0, S5_TL, step, (st[0], st[1]), unroll=2)
    st[0] = hr
    st[1] = hi
    for k in range(u.shape[0] // rb):
        rows = slice(k * rb, (k + 1) * rb)
        o_ref[rows, :] = jnp.dot(hbuf[rows, :].astype(BF16), cmat_ref[...], preferred_element_type=F32)
    if direction == 1:
        for k in range(u.shape[0] // rb):
            rows = slice(k * rb, (k + 1) * rb)
            v = jax.nn.gelu(o_ref[rows, :] + yf_ref[rows, :] + dsk_ref[...] * u[rows, :], approximate=True)
            gv = jnp.dot(v.astype(BF16), gw_ref[...], preferred_element_type=F32) + gb_ref[...]
            o_ref[rows, :] = gv[:, :S5_W] * jax.nn.sigmoid(gv[:, S5_W:])


def _s5_pass(l, direction, u, n_ctx_rows, bmat, ar, ai, cmat, yf, dsk, gw, gb):
    rows = u.shape[0]
    tr = S5_TL * NB
    nblk = rows // tr
    nb_ctx = n_ctx_rows // tr
    blk = functools.partial(_chunk_of, direction, nc_ctx=nb_ctx, nc=nblk)

    def dir_spec(shape):
        nd = len(shape)
        return pl.BlockSpec((None, None) + tuple(shape), lambda i: (l, direction) + (0,) * nd)

    in_specs = [
        pl.BlockSpec((tr, S5_W), lambda i: (blk(i), 0)),
        dir_spec((S5_W, 2 * S5_S)),
        dir_spec((1, S5_S)),
        dir_spec((1, S5_S)),
        _layer_spec(l, (2 * S5_S, S5_W)),
    ]
    args = [u, bmat, ar, ai, cmat]
    if direction == 1:
        in_specs += [
            pl.BlockSpec((tr, S5_W), lambda i: (blk(i), 0)),
            _layer_spec(l, (1, S5_W)),
            _layer_spec(l, (S5_W, 2 * S5_W)),
            _layer_spec(l, (1, 2 * S5_W)),
        ]
        args += [yf, dsk, gw, gb]
    return pl.pallas_call(
        functools.partial(_s5_kernel, direction),
        out_shape=jax.ShapeDtypeStruct((rows, S5_W), F32),
        grid=(nblk,),
        in_specs=in_specs,
        out_specs=pl.BlockSpec((tr, S5_W), lambda i: (blk(i), 0)),
        scratch_shapes=[pltpu.VMEM((tr, 2 * S5_S), F32), pltpu.VMEM((2, NB, S5_S), F32)],
        compiler_params=_cparams(("arbitrary",)),
        name="s5_fwd" if direction == 0 else "s5_bwd",
    )(*args)


def _s5_operators(lam_re, lam_im, log_step, b_re, b_im, c_re, c_im):
    depth = lam_re.shape[0]
    eye = jnp.eye(S5_G, dtype=F32)
    step = jnp.exp(log_step)[..., None]
    mag = jnp.exp(lam_re * step)
    ar, ai = mag * jnp.cos(lam_im * step), mag * jnp.sin(lam_im * step)
    inv_den = 1.0 / (lam_re * lam_re + lam_im * lam_im)
    cr = ((ar - 1.0) * lam_re + ai * lam_im) * inv_den
    ci = (ai * lam_re - (ar - 1.0) * lam_im) * inv_den
    bre, bim = b_re[:, None], b_im[:, None]
    bbr = cr[..., None] * bre - ci[..., None] * bim
    bbi = cr[..., None] * bim + ci[..., None] * bre
    bd_r = jnp.einsum("ldgph,gk->ldghkp", bbr, eye).reshape(depth, 2, S5_W, S5_S)
    bd_i = jnp.einsum("ldgph,gk->ldghkp", bbi, eye).reshape(depth, 2, S5_W, S5_S)
    bmat = jnp.concatenate([bd_r, bd_i], axis=-1).astype(BF16)
    cre = jnp.einsum("lghp,gk->lgpkh", c_re, eye).reshape(depth, S5_S, S5_W)
    cim = jnp.einsum("lghp,gk->lgpkh", c_im, eye).reshape(depth, S5_S, S5_W)
    cmat = jnp.concatenate([cre, -cim], axis=1).astype(BF16)
    return bmat, ar.reshape(depth, 2, 1, S5_S), ai.reshape(depth, 2, 1, S5_S), cmat


def _route_rows(lg):
    m = jnp.max(lg, axis=0, keepdims=True)
    ex = jnp.exp(lg - m)
    probs = ex / jnp.sum(ex, axis=0, keepdims=True)
    p = [probs[e:e + 1, :] for e in range(NE)]
    tops = []
    for q in range(NGRP):
        v = p[q * EPG:(q + 1) * EPG]
        m1 = functools.reduce(jnp.maximum, v)
        i1 = jnp.full_like(m1, float(EPG - 1))
        for j in range(EPG - 2, -1, -1):
            i1 = jnp.where(v[j] == m1, float(j), i1)
        rest = [jnp.where(i1 == float(j), -jnp.inf, v[j]) for j in range(EPG)]
        m2 = functools.reduce(jnp.maximum, rest)
        i2 = jnp.full_like(m2, float(EPG - 1))
        for j in range(EPG - 2, -1, -1):
            i2 = jnp.where(jnp.logical_and(rest[j] == m2, i1 != float(j)), float(j), i2)
        tops.append((m1, i1, m2, i2))
    score = [t[0] + t[2] for t in tops]
    best = functools.reduce(jnp.maximum, score)
    sel = tops[NGRP - 1] + (jnp.full_like(best, float(NGRP - 1)),)
    for q in range(NGRP - 2, -1, -1):
        hit = score[q] == best
        sel = tuple(jnp.where(hit, a, b) for a, b in zip(tops[q] + (jnp.full_like(best, float(q)),), sel))
    m1, i1, m2, i2, grp = sel
    den = m1 + m2
    return grp * EPG + i1, grp * EPG + i2, m1 / den, m2 / den


def _dot_nt_3pass(a, b):
    nt = (((1,), (1,)), ((), ()))
    a_hi = a.astype(BF16)
    b_hi = b.astype(BF16)
    a_lo = (a - a_hi.astype(F32)).astype(BF16)
    b_lo = (b - b_hi.astype(F32)).astype(BF16)
    out = lax.dot_general(a_hi, b_hi, nt, preferred_element_type=F32)
    out = out + lax.dot_general(a_lo, b_hi, nt, preferred_element_type=F32)
    return out + lax.dot_general(a_hi, b_lo, nt, preferred_element_type=F32)


def _out_kernel(a_ref, b_ref, s_ref, x_ref, mod_ref, g_ref, w_ref, rw_ref, rb_ref,
                x1_ref, h2_ref, rt_ref, gc_ref, cnt_ref):
    tm = x_ref.shape[0]
    rb = 2 * LANE
    for k in range(tm // rb):
        rows = slice(k * rb, (k + 1) * rb)
        mix = jnp.dot(a_ref[rows, :].astype(BF16), w_ref[0:SSD_W, :], preferred_element_type=F32)
        mix = mix + jnp.dot(b_ref[rows, :].astype(BF16), w_ref[SSD_W:SSD_W + CONF_W, :],
                            preferred_element_type=F32)
        mix = mix + jnp.dot(s_ref[rows, :].astype(BF16), w_ref[SSD_W + CONF_W:, :], preferred_element_type=F32)
        x1 = (x_ref[rows, :].reshape(rb // NB, NB, D) + mod_ref[2][None] * mix.reshape(rb // NB, NB, D))
        x1 = x1.reshape(rb, D)
        x1_ref[rows, :] = x1
        ms = jnp.mean(x1 * x1, axis=-1, keepdims=True)
        xn = x1 * lax.rsqrt(ms + EPS) * g_ref[...]
        h2_ref[rows, :] = (xn.reshape(rb // NB, NB, D) * (1.0 + mod_ref[4])[None] + mod_ref[3][None]).reshape(rb, D)
    lg = _dot_nt_3pass(rw_ref[...], h2_ref[...]) + rb_ref[...]
    e1, e2, g1, g2 = _route_rows(lg)
    eid = lax.broadcasted_iota(I32, (NE, tm), 0).astype(F32)
    oh0 = (eid == e1).astype(F32)
    oh1 = (eid == e2).astype(F32)
    t0 = lax.broadcasted_iota(I32, (tm, tm), 0)
    t1 = lax.broadcasted_iota(I32, (tm, tm), 1)
    before = (t0 < t1).astype(BF16)
    pre0 = jnp.dot(oh0.astype(BF16), before, preferred_element_type=F32)
    pre1 = jnp.dot(oh1.astype(BF16), before, preferred_element_type=F32)
    lr0 = jnp.sum(oh0 * pre0, axis=0, keepdims=True)
    lr1 = jnp.sum(oh1 * pre1, axis=0, keepdims=True)
    rt_ref[...] = jnp.concatenate([e1, e2, lr0, lr1, jnp.zeros((NB - 4, tm), F32)], axis=0)
    gates = jnp.concatenate([g1, g2, jnp.zeros((LANE - 2, tm), F32)], axis=0)
    gc_ref[...] = gates.T
    cnt = jnp.concatenate([jnp.sum(oh0, axis=1, keepdims=True), jnp.sum(oh1, axis=1, keepdims=True)], axis=0)
    cnt_ref[...] = jnp.broadcast_to(cnt, (2 * NE, LANE))


def _out_proj(l, a, bconf, s, x, modtabs, g_ffn, w_out, rw_t, rb, n_ctx_rows, conf_first, blk0, nblk):
    tm = ROW_TM
    ctx_blk = n_ctx_rows // tm
    rows = nblk * tm
    return pl.pallas_call(
        _out_kernel,
        out_shape=[
            jax.ShapeDtypeStruct((rows, D), F32),
            jax.ShapeDtypeStruct((rows, D), F32),
            jax.ShapeDtypeStruct((NB, rows), F32),
            jax.ShapeDtypeStruct((rows, LANE), F32),
            jax.ShapeDtypeStruct((nblk, 2 * NE, LANE), F32),
        ],
        grid=(nblk,),
        in_specs=[
            pl.BlockSpec((tm, SSD_W), lambda i: (i + blk0, 0)),
            pl.BlockSpec((tm, CONF_W), lambda i: (i + blk0 - conf_first, 0)),
            pl.BlockSpec((tm, S5_W), lambda i: (i + blk0, 0)),
            pl.BlockSpec((tm, D), lambda i: (i + blk0, 0)),
            _mod_spec(l, ctx_blk, blk0),
            _layer_spec(l, (1, D)),
            _layer_spec(l, (D, D)),
            _const_spec((NE, D)),
            _const_spec((NE, 1)),
        ],
        out_specs=[
            pl.BlockSpec((tm, D), lambda i: (i, 0)),
            pl.BlockSpec((tm, D), lambda i: (i, 0)),
            pl.BlockSpec((NB, tm), lambda i: (0, i)),
            pl.BlockSpec((tm, LANE), lambda i: (i, 0)),
            pl.BlockSpec((None, 2 * NE, LANE), lambda i: (i, 0, 0)),
        ],
        compiler_params=_cparams(("arbitrary",)),
        name="out_proj_route",
    )(a, bconf, s, x, modtabs, g_ffn, w_out, rw_t, rb)


def _moe_kernel(l, second, it_ref, ie_ref, lo_ref, hi_ref, nx_ref, h_ref, wg_hbm, wu_hbm, wd_hbm, *rest):
    rest = rest[1:] if second else rest
    out_ref, stage, wgb, wub, wdb, sem, cur = rest
    o_ref = out_ref if second else out_ref.at[0]
    w = pl.program_id(0)
    lo = lo_ref[w]
    hi = hi_ref[w]
    new_tile = jnp.logical_or(w == 0, it_ref[w] != it_ref[jnp.maximum(w - 1, 0)])

    def fetch(e, slot):
        return [pltpu.make_async_copy(src.at[l, e], stage.at[slot, k], sem.at[slot, k])
                for k, src in enumerate((wg_hbm, wu_hbm, wd_hbm))]

    @pl.when(w == 0)
    def _():
        cur[0] = -1
        cur[1] = nx_ref[0]
        cur[2] = 0

        @pl.when(nx_ref[0] >= 0)
        def _():
            for c in fetch(nx_ref[0], 0):
                c.start()

    @pl.when(new_tile)
    def _():
        out_ref[...] = jnp.zeros_like(out_ref)

    @pl.when(hi > lo)
    def _():
        e = ie_ref[w]

        @pl.when(cur[0] != e)
        def _():
            slot = cur[2]
            for c in fetch(e, slot):
                c.wait()
            nxt = nx_ref[1 + e]

            @pl.when(nxt >= 0)
            def _():
                for c in fetch(nxt, 1 - slot):
                    c.start()

            wgb[...] = stage[slot, 0].astype(BF16)
            wub[...] = stage[slot, 1].astype(BF16)
            wdb[...] = stage[slot, 2].astype(BF16)
            cur[0] = e
            cur[1] = nxt
            cur[2] = 1 - slot

        h = h_ref[...].astype(BF16)
        gate = jnp.dot(h, wgb[...], preferred_element_type=F32)
        up = jnp.dot(h, wub[...], preferred_element_type=F32)
        act = (_silu(gate) * up).astype(BF16)
        y = jnp.dot(act, wdb[...], preferred_element_type=F32)
        r = lax.broadcasted_iota(I32, (MOE_TM, 1), 0)
        keep = jnp.logical_and(r >= lo, r < hi)
        o_ref[...] = jnp.where(keep, y, o_ref[...])


def _moe_experts(l, hs, items, tile0, w_gate, w_up, w_down, prev):
    item_tile, item_expert, item_lo, item_hi, next_expert = items
    n_items = item_tile.shape[0]
    n_prefetch = 5
    second = prev is not None
    hbm = pl.BlockSpec(memory_space=pl.ANY)
    in_specs = [pl.BlockSpec((MOE_TM, D), lambda w, it, ie, lo, hi, nx: (it[w] - tile0, 0)), hbm, hbm, hbm]
    args = [item_tile, item_expert, item_lo, item_hi, next_expert, hs, w_gate, w_up, w_down]
    aliases = {}
    if second:
        in_specs.append(hbm)
        args.append(prev)
        aliases = {len(args) - 1: 0}
        out_spec = pl.BlockSpec((None, MOE_TM, D), lambda w, it, ie, lo, hi, nx: (1, it[w] - tile0, 0))
    else:
        out_spec = pl.BlockSpec((2, MOE_TM, D), lambda w, it, ie, lo, hi, nx: (0, it[w] - tile0, 0))
    n_mats = 3
    return pl.pallas_call(
        functools.partial(_moe_kernel, l, second),
        out_shape=jax.ShapeDtypeStruct((2, hs.shape[0], D), F32),
        grid_spec=pltpu.PrefetchScalarGridSpec(
            num_scalar_prefetch=n_prefetch,
            grid=(n_items,),
            in_specs=in_specs,
            out_specs=out_spec,
            scratch_shapes=[pltpu.VMEM((2, n_mats, D, D), F32)] + [pltpu.VMEM((D, D), BF16)] * n_mats
            + [pltpu.SemaphoreType.DMA((2, n_mats)), pltpu.SMEM((3,), I32)],
        ),
        input_output_aliases=aliases,
        compiler_params=_cparams(("arbitrary",)),
        name="moe_experts",
    )(*args)


def _moe_plan(route, cnt, tile_parts):
    rows = route.shape[1]
    nblk = cnt.shape[0]
    n_flat = 2 * rows
    e = route[0:2].astype(I32)
    lrank = route[2:4].astype(I32)
    counts = cnt[:, :, 0].astype(I32).reshape(nblk, 2, NE)
    per = jnp.transpose(counts, (1, 0, 2)).reshape(2 * nblk, NE)
    before = jnp.cumsum(per, axis=0) - per
    gcount = jnp.sum(per, axis=0)
    gend = jnp.cumsum(gcount)
    gstart = gend - gcount
    base = jnp.transpose((before + gstart[None, :]).reshape(2, nblk, NE), (2, 0, 1))[..., None]
    onehot = e.reshape(1, 2, nblk, ROW_TM) == jnp.arange(NE, dtype=I32).reshape(NE, 1, 1, 1)
    inv = jnp.sum(jnp.where(onehot, base, 0), axis=0).reshape(2, rows) + lrank
    keys = (e * 65536).reshape(-1) + jnp.arange(n_flat, dtype=I32)
    order = lax.sort(keys, is_stable=False) & 0xFFFF
    tok = jnp.where(order >= rows, order - rows, order)
    items = []
    for t0, tn in tile_parts:
        row_lo, row_hi = t0 * MOE_TM, (t0 + tn) * MOE_TM
        tile_start = (t0 + jnp.arange(tn, dtype=I32)) * MOE_TM
        bnd = lax.sort(jnp.concatenate([tile_start, jnp.clip(gstart[1:], row_lo, row_hi)]), is_stable=False)
        bnd_hi = jnp.concatenate([bnd[1:], jnp.full((1,), row_hi, I32)])
        item_tile = jnp.minimum(bnd // MOE_TM, t0 + tn - 1)
        first_row = jnp.minimum(bnd, row_hi - 1)[:, None]
        item_expert = jnp.minimum(jnp.sum((gend[None, :] <= first_row).astype(I32), axis=1), NE - 1)
        ids = jnp.arange(NE, dtype=I32)
        present = jnp.logical_and(gend > row_lo, gstart < row_hi) & (gcount > 0)
        later = jnp.logical_and(present[None, :], ids[None, :] > jnp.arange(-1, NE, dtype=I32)[:, None])
        next_expert = jnp.min(jnp.where(later, ids[None, :], NE), axis=1)
        next_expert = jnp.where(next_expert == NE, -1, next_expert).astype(I32)
        items.append((item_tile, item_expert, bnd - item_tile * MOE_TM, bnd_hi - item_tile * MOE_TM, next_expert))
    return tok, inv, items


def _final_kernel(x_ref, ya_ref, yb_ref, gc_ref, mod_ref, g_ref, o_ref, scr):
    tm = x_ref.shape[0]
    gc = gc_ref[...]
    ff = (gc[:, 0:1] * ya_ref[...] + gc[:, 1:2] * yb_ref[...]).reshape(tm // NB, NB, D)
    x2 = (x_ref[...].reshape(tm // NB, NB, D) + mod_ref[5][None] * ff).reshape(tm, D)
    ms = jnp.mean(x2 * x2, axis=-1, keepdims=True)
    x2 = x2 * lax.rsqrt(ms + EPS) * g_ref[...]
    ncb = D // LANE
    for k in range(ncb):
        scr[k] = x2[:, k * LANE:(k + 1) * LANE]
    for b in range(NB):
        for k in range(ncb):
            o_ref[b, :, k * LANE:(k + 1) * LANE] = scr[k, pl.ds(b, ROW_TQ, stride=NB), :]


def _final_combine(l, x1, ya, yb, gcol, modtabs, g_final):
    tm = ROW_TM
    rows = x1.shape[0]
    row_spec = pl.BlockSpec((tm, D), lambda i: (i, 0))
    return pl.pallas_call(
        _final_kernel,
        out_shape=jax.ShapeDtypeStruct((NB, rows // NB, D), F32),
        grid=(rows // tm,),
        in_specs=[row_spec, row_spec, row_spec, pl.BlockSpec((tm, LANE), lambda i: (i, 0)),
                  _mod_spec(l, 0), _const_spec((1, D))],
        out_specs=pl.BlockSpec((NB, ROW_TQ, D), lambda i: (0, i, 0)),
        scratch_shapes=[pltpu.VMEM((D // LANE, tm, LANE), F32)],
        compiler_params=_cparams(("arbitrary",)),
        name="moe_combine_final",
    )(x1, ya, yb, gcol, modtabs, g_final)


def _grid_pos_embed(rows_n):
    rr, cc = jnp.meshgrid(jnp.arange(rows_n, dtype=F32), jnp.arange(GRID_W, dtype=F32), indexing="ij")
    quarter = D // 4
    inv_freq = jnp.exp(-math.log(10000.0) * jnp.arange(quarter, dtype=F32) / quarter)

    def emb(pos):
        ang = pos.reshape(-1)[:, None] * inv_freq[None, :]
        return jnp.concatenate([jnp.sin(ang), jnp.cos(ang)], axis=-1)

    return jnp.concatenate([emb(rr), emb(cc)], axis=-1)


def _take_rows(a, idx):
    return a.at[idx].get(mode="promise_in_bounds")


def _pad_last(v, width):
    return jnp.pad(v, [(0, 0)] * (v.ndim - 1) + [(0, width - v.shape[-1])])


def kernel(x, c, ctx, c_ctx, w_ada, b_ada, g_mix, w_in, ssd_conv_w, ssd_conv_b, ssd_dt_bias, ssd_a_log, ssd_d, ssd_norm_g, conf_dw_w, conf_dw_b, conf_ln_g, conf_ln_b, conf_pw_w, conf_pw_b, s5_lambda_re, s5_lambda_im, s5_log_step, s5_b_re, s5_b_im, s5_c_re, s5_c_im, s5_d, s5_glu_w, s5_glu_b, w_out, g_ffn, router_w, router_b, exp_w_gate, exp_w_up, exp_w_down, g_final):
    bsz, seq, _ = x.shape
    ctx_len = ctx.shape[1]
    depth = w_ada.shape[0]
    assert bsz == NB and seq % SSD_CHUNK == 0 and ctx_len % SSD_CHUNK == 0
    n_ctx_rows = ctx_len * NB
    rows_all = (ctx_len + seq) * NB

    cond = jnp.concatenate([c, c_ctx[None, :], jnp.zeros((2 * NB - bsz - 1, D), F32)], axis=0)
    mod = _modulation(cond, w_ada, b_ada)
    mod = mod.reshape(depth, 2 * NB, N_MOD, D)
    mod_lat = jnp.transpose(mod[:, :NB], (0, 2, 1, 3))
    mod_ctx = jnp.broadcast_to(mod[:, NB][:, :, None, :], (depth, N_MOD, NB, D))
    modtabs = jnp.stack([mod_ctx, mod_lat], axis=1)

    o_b = SSD_W + SSD_XBC + 2 * SSD_H
    w_in_p = jnp.concatenate(
        [w_in[:, :, :o_b], jnp.zeros((depth, D, DT_PAD - 2 * SSD_H), F32), w_in[:, :, o_b:]], axis=2).astype(BF16)
    g_mix3 = g_mix.reshape(depth, 1, D)
    ssd_cb = ssd_conv_b.reshape(depth, 1, SSD_XBC)
    ssd_dtb = _pad_last(ssd_dt_bias.reshape(depth, 1, 2 * SSD_H), DT_PAD)
    ssd_alog = _pad_last(ssd_a_log.reshape(depth, 1, 2 * SSD_H), DT_PAD)
    ssd_dsk = jnp.repeat(ssd_d, SSD_P, axis=1).reshape(depth, 1, SSD_W)
    ssd_ng = ssd_norm_g.reshape(depth, 1, SSD_W)
    conf_b = conf_dw_b.reshape(depth, 1, CONF_W)
    conf_lg = conf_ln_g.reshape(depth, 1, CONF_W)
    conf_lb = conf_ln_b.reshape(depth, 1, CONF_W)
    conf_pw = conf_pw_w.astype(BF16)
    conf_pb = conf_pw_b.reshape(depth, 1, CONF_W)
    s5_bmat, s5_ar, s5_ai, s5_cmat = _s5_operators(s5_lambda_re, s5_lambda_im, s5_log_step,
                                                   s5_b_re, s5_b_im, s5_c_re, s5_c_im)
    s5_dsk = s5_d.reshape(depth, 1, S5_W)
    s5_gw = s5_glu_w.astype(BF16)
    s5_gb = s5_glu_b.reshape(depth, 1, 2 * S5_W)
    w_out_b = w_out.astype(BF16)
    g_ffn3 = g_ffn.reshape(depth, 1, D)
    router_w_t = router_w.T
    router_b2 = router_b.reshape(NE, 1)
    g_final2 = g_final.reshape(1, D)
    pos = _grid_pos_embed(seq // GRID_W)

    pending = (ctx, x)
    for l in range(depth):
        last = l == depth - 1
        ssd_cols, conf_cols, s5_u, xall = _in_proj(l, l == 0, pending, pos, modtabs, g_mix3, w_in_p,
                                                   rows_all, n_ctx_rows)
        ssd_fwd = _ssd_pass(l, 0, ssd_cols, n_ctx_rows, ssd_conv_w, ssd_cb, ssd_dtb, ssd_alog, ssd_dsk, None)
        a_mix = _ssd_pass(l, 1, ssd_cols, n_ctx_rows, None, None, ssd_dtb, ssd_alog, ssd_ng, ssd_fwd)
        b_mix = _conformer(l, conf_cols, n_ctx_rows, not last, conf_dw_w, conf_b, conf_lg, conf_lb, conf_pw, conf_pb)
        yf = _s5_pass(l, 0, s5_u, n_ctx_rows, s5_bmat, s5_ar, s5_ai, s5_cmat, None, None, None, None)
        s_mix = _s5_pass(l, 1, s5_u, n_ctx_rows, s5_bmat, s5_ar, s5_ai, s5_cmat, yf, s5_dsk, s5_gw, s5_gb)
        ctx_blk = n_ctx_rows // ROW_TM
        first = ctx_blk if last else 0
        n_out = rows_all // ROW_TM - first
        x1, h2, route, gcol, cnt = _out_proj(l, a_mix, b_mix, s_mix, xall, modtabs, g_ffn3, w_out_b,
                                             router_w_t, router_b2, n_ctx_rows, first, first, n_out)
        n_tiles = 2 * n_out * ROW_TM // MOE_TM
        assert n_tiles % 2 == 0
        tile_parts = [(0, n_tiles // 2), (n_tiles // 2, n_tiles // 2)]
        tok, inv, items = _moe_plan(route, cnt, tile_parts)
        hs = [_take_rows(h2, tok[t0 * MOE_TM:(t0 + tn) * MOE_TM]) for t0, tn in tile_parts]
        y = None
        for (t0, tn), h, it in zip(tile_parts, hs, items):
            y = _moe_experts(l, h, it, t0, exp_w_gate, exp_w_up, exp_w_down, y)
        y = y.reshape(n_tiles * MOE_TM, D)
        ya = _take_rows(y, inv[0])
        yb = _take_rows(y, inv[1])
        pending = (x1, ya, yb, gcol)
    return _final_combine(depth - 1, *pending, modtabs, g_final2)
```

```python
import functools
import math

import jax
import jax.numpy as jnp
from jax import lax
from jax.experimental import pallas as pl
from jax.experimental.pallas import tpu as pltpu

F32 = jnp.float32
BF16 = jnp.bfloat16
I32 = jnp.int32
HIGHEST = lax.Precision.HIGHEST

NB = 8
D = 1024
GRID_W = 64
N_MOD = 6
EPS = 1e-6
LANE = 128
SSD_W = 512
SSD_P = 64
SSD_H = 8
SSD_G = 2
SSD_N = 128
SSD_K = 5
SSD_XBC = SSD_W + 2 * SSD_G * SSD_N
SSD_CHUNK = 128
DT_PAD = LANE
SSD_COLS_P = SSD_W + SSD_XBC + DT_PAD
CONF_W = 256
CONF_K = 31
S5_W = 256
S5_G = 16
S5_P = 64
S5_CH = 16
S5_S = S5_G * S5_P
S5_TL = 128
NE = 16
NGRP = 4
EPG = 4
MOE_TM = 512
MOE_SUB = 256
IN_COLS_P = SSD_COLS_P + 2 * CONF_W + S5_W
ROW_TM = 512
ROW_TQ = ROW_TM // NB
CH_ROWS = SSD_CHUNK * NB
VMEM_LIMIT = 56 * 1024 * 1024


def _cparams(sem):
    return pltpu.CompilerParams(dimension_semantics=sem, vmem_limit_bytes=VMEM_LIMIT)


def _const_spec(shape):
    nd = len(shape)
    return pl.BlockSpec(shape, lambda *_: (0,) * nd)


def _layer_spec(l, shape):
    nd = len(shape)
    return pl.BlockSpec((None,) + tuple(shape), lambda *_: (l,) + (0,) * nd)


def _silu(v):
    return v * jax.nn.sigmoid(v)


def _mod_kernel(c_ref, w_ref, b_ref, o_ref):
    c = c_ref[...]
    h = _silu(c).astype(BF16)
    o_ref[...] = jnp.dot(h, w_ref[...].astype(BF16), preferred_element_type=F32) + b_ref[...]


def _modulation(cond, w_ada, b_ada):
    depth = w_ada.shape[0]
    nrow = cond.shape[0]
    return pl.pallas_call(
        _mod_kernel,
        out_shape=jax.ShapeDtypeStruct((depth, nrow, N_MOD * D), F32),
        grid=(depth, N_MOD),
        in_specs=[
            pl.BlockSpec((nrow, D), lambda l, j: (0, 0)),
            pl.BlockSpec((None, D, D), lambda l, j: (l, 0, j)),
            pl.BlockSpec((None, 1, D), lambda l, j: (l, 0, j)),
        ],
        out_specs=pl.BlockSpec((None, nrow, D), lambda l, j: (l, 0, j)),
        compiler_params=_cparams(("arbitrary", "arbitrary")),
        name="adaln_mod",
    )(cond, w_ada, b_ada.reshape(depth, 1, N_MOD * D))


def _mod_spec(l, ctx_blk, first=0):
    return pl.BlockSpec((None, None, N_MOD, NB, D),
                        lambda i, *_: (l, jnp.where(i + first < ctx_blk, 0, 1), 0, 0, 0))


def _norm_mod_project(x, mod_ref, g_ref, w_ref, ssd_ref, conf_ref, s5_ref):
    tm = x.shape[0]
    ms = jnp.mean(x * x, axis=-1, keepdims=True)
    xn = x * lax.rsqrt(ms + EPS) * g_ref[...]
    h = xn.reshape(tm // NB, NB, D) * (1.0 + mod_ref[1])[None] + mod_ref[0][None]
    h = h.reshape(tm, D).astype(BF16)
    ssd_ref[...] = jnp.dot(h, w_ref[:, :SSD_COLS_P], preferred_element_type=F32)
    conf_ref[...] = jnp.dot(h, w_ref[:, SSD_COLS_P:SSD_COLS_P + 2 * CONF_W], preferred_element_type=F32)
    s5_ref[...] = jnp.dot(h, w_ref[:, SSD_COLS_P + 2 * CONF_W:], preferred_element_type=F32)


def _in_first_kernel(ctx_blk, ctx_ref, x_ref, pos_ref, mod_ref, g_ref, w_ref,
                     ssd_ref, conf_ref, s5_ref, x0_ref, scr):
    i = pl.program_id(0)
    ncb = D // LANE

    @pl.when(i < ctx_blk)
    def _():
        for b in range(NB):
            for k in range(ncb):
                scr[k, pl.ds(b, ROW_TQ, stride=NB), :] = ctx_ref[b, :, k * LANE:(k + 1) * LANE]

    @pl.when(i >= ctx_blk)
    def _():
        for b in range(NB):
            for k in range(ncb):
                scr[k, pl.ds(b, ROW_TQ, stride=NB), :] = (
                    x_ref[b, :, k * LANE:(k + 1) * LANE] + pos_ref[:, k * LANE:(k + 1) * LANE])

    x = jnp.concatenate([scr[k] for k in range(ncb)], axis=1)
    x0_ref[...] = x
    _norm_mod_project(x, mod_ref, g_ref, w_ref, ssd_ref, conf_ref, s5_ref)


def _in_next_kernel(x1_ref, ya_ref, yb_ref, gc_ref, modp_ref, mod_ref, g_ref, w_ref,
                    ssd_ref, conf_ref, s5_ref, x_ref):
    tm = x1_ref.shape[0]
    gc = gc_ref[...]
    ff = (gc[:, 0:1] * ya_ref[...] + gc[:, 1:2] * yb_ref[...]).reshape(tm // NB, NB, D)
    x = (x1_ref[...].reshape(tm // NB, NB, D) + modp_ref[5][None] * ff).reshape(tm, D)
    x_ref[...] = x
    _norm_mod_project(x, mod_ref, g_ref, w_ref, ssd_ref, conf_ref, s5_ref)


def _in_proj(l, first, x_or_pair, pos, modtabs, g_mix, w_in_p, rows, n_ctx_rows):
    tm = ROW_TM
    nblk = rows // tm
    ctx_blk = n_ctx_rows // tm
    common_specs = [_mod_spec(l, ctx_blk), _layer_spec(l, (1, D)), _layer_spec(l, (D, IN_COLS_P))]
    common_args = [modtabs, g_mix, w_in_p]
    out_shape = [
        jax.ShapeDtypeStruct((rows, SSD_COLS_P), F32),
        jax.ShapeDtypeStruct((rows, 2 * CONF_W), F32),
        jax.ShapeDtypeStruct((rows, S5_W), F32),
    ]
    out_specs = [
        pl.BlockSpec((tm, SSD_COLS_P), lambda i: (i, 0)),
        pl.BlockSpec((tm, 2 * CONF_W), lambda i: (i, 0)),
        pl.BlockSpec((tm, S5_W), lambda i: (i, 0)),
    ]
    if first:
        ctx, x = x_or_pair
        body = functools.partial(_in_first_kernel, ctx_blk)
        in_specs = [
            pl.BlockSpec((NB, ROW_TQ, D), lambda i: (0, jnp.minimum(i, ctx_blk - 1), 0)),
            pl.BlockSpec((NB, ROW_TQ, D), lambda i: (0, jnp.maximum(i - ctx_blk, 0), 0)),
            pl.BlockSpec((ROW_TQ, D), lambda i: (jnp.maximum(i - ctx_blk, 0), 0)),
        ] + common_specs
        args = [ctx, x, pos] + common_args
        scratch = [pltpu.VMEM((D // LANE, tm, LANE), F32)]
    else:
        x1, ya, yb, gcol = x_or_pair
        body = _in_next_kernel
        in_specs = [
            pl.BlockSpec((tm, D), lambda i: (i, 0)),
            pl.BlockSpec((tm, D), lambda i: (i, 0)),
            pl.BlockSpec((tm, D), lambda i: (i, 0)),
            pl.BlockSpec((tm, LANE), lambda i: (i, 0)),
            _mod_spec(l - 1, ctx_blk),
        ] + common_specs
        args = [x1, ya, yb, gcol, modtabs] + common_args
        scratch = []
    out_shape.append(jax.ShapeDtypeStruct((rows, D), F32))
    out_specs.append(pl.BlockSpec((tm, D), lambda i: (i, 0)))
    return pl.pallas_call(
        body,
        out_shape=out_shape,
        grid=(nblk,),
        in_specs=in_specs,
        out_specs=out_specs,
        scratch_shapes=scratch,
        compiler_params=_cparams(("arbitrary",)),
        name="in_proj",
    )(*args)


def _chunk_of(direction, i, nc_ctx, nc):
    if direction == 0:
        return i
    return jnp.where(i < nc_ctx, nc_ctx - 1 - i, nc - 1 - (i - nc_ctx))


def _has_prev(c, nc_ctx):
    return jnp.logical_and(c != 0, c != nc_ctx)


def _has_next(c, nc_ctx, nc):
    return jnp.logical_and(c != nc_ctx - 1, c != nc - 1)


SSD_HALO = 2 * NB


def _ssd_conv(c, nc_ctx, nc, main_ref, prev_ref, next_ref, cw_ref, cb_ref, act, ext):
    has_prev = _has_prev(c, nc_ctx)
    has_next = _has_next(c, nc_ctx, nc)
    nblk = SSD_XBC // LANE
    n_rb = CH_ROWS // LANE
    for k in range(nblk):
        lo = SSD_W + k * LANE
        ext[0, k, 0:SSD_HALO, :] = jnp.where(has_prev, prev_ref[:, lo:lo + LANE], 0.0)
        ext[0, k, SSD_HALO:, :] = main_ref[0:LANE + SSD_HALO, lo:lo + LANE]
        ext[1, k, 0:LANE + SSD_HALO, :] = main_ref[CH_ROWS - LANE - SSD_HALO:, lo:lo + LANE]
        ext[1, k, LANE + SSD_HALO:, :] = jnp.where(has_next, next_ref[:, lo:lo + LANE], 0.0)

    def conv_block(tap_rows, r0):
        for k in range(nblk):
            lo = k * LANE
            acc = jnp.broadcast_to(cb_ref[:, lo:lo + LANE], (LANE, LANE))
            for tap in range(SSD_K):
                acc = acc + tap_rows(k, tap) * cw_ref[tap:tap + 1, lo:lo + LANE]
            act[k, pl.ds(r0, LANE), :] = _silu(acc)

    def conv_rb(rb, carry):
        r0 = pl.multiple_of(rb * LANE, LANE)
        conv_block(lambda k, tap: main_ref[pl.ds(r0 + NB * tap - SSD_HALO, LANE),
                                           SSD_W + k * LANE:SSD_W + (k + 1) * LANE], r0)
        return carry

    conv_block(lambda k, tap: ext[0, k, NB * tap:NB * tap + LANE, :], 0)
    lax.fori_loop(1, n_rb - 1, conv_rb, 0)
    conv_block(lambda k, tap: ext[1, k, NB * tap:NB * tap + LANE, :], CH_ROWS - LANE)


def _ssd_kernel(direction, nc_ctx, nc, *refs):
    if direction == 0:
        (main_ref, prev_ref, next_ref, cw_ref, cb_ref, dtb_ref, alog_ref, dsk_ref,
         o_ref, act, ext, dts, yout, state, tabs) = refs
    else:
        (z_ref, dtraw_ref, act, dtb_ref, alog_ref, ng_ref, yp_ref,
         o_ref, dts, yout, state, tabs) = refs
    i = pl.program_id(0)

    @pl.when(i == 0)
    def _():
        state[...] = jnp.zeros_like(state)

    if direction == 0:
        _ssd_conv(i, nc_ctx, nc, main_ref, prev_ref, next_ref, cw_ref, cb_ref, act, ext)
        raw = main_ref[:, SSD_W + SSD_XBC:] + dtb_ref[...]
    else:
        raw = dtraw_ref[...] + dtb_ref[...]
    dts[...] = jnp.maximum(raw, 0.0) + jnp.log1p(jnp.exp(-jnp.abs(raw)))
    a_row = -jnp.exp(alog_ref[...])

    tt = lax.broadcasted_iota(I32, (SSD_CHUNK, SSD_CHUNK), 0)
    ss = lax.broadcasted_iota(I32, (SSD_CHUNK, SSD_CHUNK), 1)
    mask = (ss <= tt) if direction == 0 else (ss >= tt)
    tmat = mask.astype(F32)
    last = SSD_CHUNK - 1 if direction == 0 else 0
    hpg = SSD_H // SSD_G
    lane_id = lax.broadcasted_iota(I32, (1, LANE), 1)
    half_lo = lane_id < SSD_P

    def decay_tables(b, carry):
        dt_b = dts[pl.ds(b, SSD_CHUNK, stride=NB), :]
        cs = jnp.dot(tmat, dt_b * a_row, precision=HIGHEST, preferred_element_type=F32)
        tabs[b, 0] = cs
        tabs[b, 1] = cs.T
        tabs[b, 2] = dt_b.T
        return carry

    lax.fori_loop(0, NB, decay_tables, 0, unroll=True)

    def per_batch(b, carry):
        sl = pl.ds(b, SSD_CHUNK, stride=NB)
        cs = tabs[b, 0]
        cs_t = tabs[b, 1]
        dt_t = tabs[b, 2]
        for g in range(SSD_G):
            bg_t = act[SSD_W // LANE + g, sl, :].T
            cg = act[SSD_W // LANE + SSD_G + g, sl, :]
            gmat = jnp.dot(cg.astype(BF16), bg_t.astype(BF16), preferred_element_type=F32)
            for pr in range(hpg // 2):
                k = g * (hpg // 2) + pr
                xs_pair = act[k, sl, :]
                s_pair = state[b, k]
                xs_b = xs_pair.astype(BF16)
                rhs = jnp.concatenate([xs_b, s_pair.astype(BF16)], axis=0)
                ys, ss, etot = [], [], []
                for j in range(2):
                    ln = direction * SSD_H + 2 * k + j
                    colb = jnp.broadcast_to(cs[:, ln:ln + 1], (SSD_CHUNK, SSD_CHUNK))
                    row = cs_t[ln:ln + 1, :]
                    dtr = dt_t[ln:ln + 1, :]
                    decay = jnp.where(mask, jnp.exp(jnp.where(mask, colb - row, 0.0)) * dtr, 0.0)
                    lhs = jnp.concatenate([(gmat * decay).astype(BF16),
                                           (cg * jnp.exp(colb)).astype(BF16)], axis=1)
                    ys.append(jnp.dot(lhs, rhs, preferred_element_type=F32))
                    tot = cs[last:last + 1, ln:ln + 1]
                    wrow = jnp.exp(tot - row) * dtr
                    ss.append(jnp.dot((bg_t * wrow).astype(BF16), xs_b, preferred_element_type=F32))
                    etot.append(jnp.exp(tot))
                acc_y = jnp.where(half_lo, ys[0], ys[1])
                state[b, k] = s_pair * jnp.where(half_lo, etot[0], etot[1]) + jnp.where(half_lo, ss[0], ss[1])
                if direction == 0:
                    acc_y = acc_y + dsk_ref[:, k * LANE:(k + 1) * LANE] * xs_pair
                yout[k, sl, :] = acc_y
        return carry

    lax.fori_loop(0, NB, per_batch, 0, unroll=2)

    nyb = SSD_W // LANE
    if direction == 0:
        for k in range(nyb):
            o_ref[:, k * LANE:(k + 1) * LANE] = yout[k]
    else:
        def fin(rb, carry):
            r0 = pl.multiple_of(rb * LANE, LANE)
            y = jnp.concatenate([yout[k, pl.ds(r0, LANE), :] for k in range(nyb)], axis=1)
            y = y + yp_ref[pl.ds(r0, LANE), :]
            y = y * _silu(z_ref[pl.ds(r0, LANE), :])
            ms = jnp.mean(y * y, axis=-1, keepdims=True)
            o_ref[pl.ds(r0, LANE), :] = y * lax.rsqrt(ms + EPS) * ng_ref[...]
            return carry

        lax.fori_loop(0, CH_ROWS // LANE, fin, 0, unroll=4)


def _ssd_pass(l, direction, cols, n_ctx_rows, cw, cb, dtb, alog, extra, fwd_out):
    rows = cols.shape[0]
    nc = rows // CH_ROWS
    nc_ctx = n_ctx_rows // CH_ROWS
    hpc = CH_ROWS // SSD_HALO
    n_halo = rows // SSD_HALO
    nact = SSD_XBC // LANE
    chunk = functools.partial(_chunk_of, direction, nc_ctx=nc_ctx, nc=nc)
    y_spec = pl.BlockSpec((CH_ROWS, SSD_W), lambda i: (chunk(i), 0))
    act_spec = pl.BlockSpec((nact, CH_ROWS, LANE), lambda i: (0, chunk(i), 0))
    small = [_layer_spec(l, (1, DT_PAD)), _layer_spec(l, (1, DT_PAD)), _layer_spec(l, (1, SSD_W))]
    scratch = [
        pltpu.VMEM((CH_ROWS, DT_PAD), F32),
        pltpu.VMEM((SSD_W // LANE, CH_ROWS, LANE), F32),
        pltpu.VMEM((NB, SSD_W // LANE, SSD_N, LANE), F32),
        pltpu.VMEM((NB, 3, SSD_CHUNK, DT_PAD), F32),
    ]
    if direction == 0:
        in_specs = [
            pl.BlockSpec((CH_ROWS, SSD_COLS_P), lambda i: (chunk(i), 0)),
            pl.BlockSpec((SSD_HALO, SSD_COLS_P), lambda i: (jnp.maximum(chunk(i) * hpc - 1, 0), 0)),
            pl.BlockSpec((SSD_HALO, SSD_COLS_P), lambda i: (jnp.minimum((chunk(i) + 1) * hpc, n_halo - 1), 0)),
            _layer_spec(l, (SSD_K, SSD_XBC)),
            _layer_spec(l, (1, SSD_XBC)),
        ] + small
        args = [cols, cols, cols, cw, cb, dtb, alog, extra]
        out_shape = [jax.ShapeDtypeStruct((rows, SSD_W), F32), jax.ShapeDtypeStruct((nact, rows, LANE), F32)]
        out_specs = [y_spec, act_spec]
        scratch = [pltpu.VMEM((2, nact, LANE + 2 * SSD_HALO, LANE), F32)] + scratch
    else:
        ypart, act = fwd_out
        dt_col = (SSD_W + SSD_XBC) // DT_PAD
        in_specs = [y_spec, pl.BlockSpec((CH_ROWS, DT_PAD), lambda i: (chunk(i), dt_col)), act_spec] + small + [y_spec]
        args = [cols, cols, act, dtb, alog, extra, ypart]
        out_shape = jax.ShapeDtypeStruct((rows, SSD_W), F32)
        out_specs = y_spec
    return pl.pallas_call(
        functools.partial(_ssd_kernel, direction, nc_ctx, nc),
        out_shape=out_shape,
        grid=(nc,),
        in_specs=in_specs,
        out_specs=out_specs,
        scratch_shapes=scratch,
        compiler_params=_cparams(("arbitrary",)),
        name="ssd_fwd" if direction == 0 else "ssd_bwd",
    )(*args)


CONF_HALO = 128
CONF_RB = 256


def _conf_kernel(first_chunk, nc_ctx, nc, main_ref, prev_ref, next_ref, w_ref, b_ref, lng_ref, lnb_ref,
                 pw_ref, pwb_ref, o_ref, ext, cbuf):
    c = pl.program_id(0) + first_chunk
    has_prev = _has_prev(c, nc_ctx)
    has_next = _has_next(c, nc_ctx, nc)

    def glu(ref):
        v = ref[...]
        return v[:, :CONF_W] * jax.nn.sigmoid(v[:, CONF_W:])

    um = glu(main_ref)
    up = jnp.where(has_prev, glu(prev_ref), 0.0)
    un = jnp.where(has_next, glu(next_ref), 0.0)
    nblk = CONF_W // LANE
    for k in range(nblk):
        ext[k, 0:CONF_HALO, :] = up[:, k * LANE:(k + 1) * LANE]
        ext[k, CONF_HALO:CONF_HALO + CH_ROWS, :] = um[:, k * LANE:(k + 1) * LANE]
        ext[k, CONF_HALO + CH_ROWS:, :] = un[:, k * LANE:(k + 1) * LANE]
    off = CONF_HALO - NB * (CONF_K // 2)

    def conv_rb(rb, carry):
        r0 = pl.multiple_of(rb * LANE, LANE)
        for k in range(nblk):
            lo = k * LANE
            acc = jnp.broadcast_to(b_ref[:, lo:lo + LANE], (LANE, LANE))
            for tap in range(CONF_K):
                acc = acc + ext[k, pl.ds(r0 + NB * tap + off, LANE), :] * w_ref[tap:tap + 1, lo:lo + LANE]
            cbuf[pl.ds(r0, LANE), lo:lo + LANE] = acc
        return carry

    lax.fori_loop(0, CH_ROWS // LANE, conv_rb, 0)

    def norm_rb(rb, carry):
        r0 = pl.multiple_of(rb * CONF_RB, CONF_RB)
        u = cbuf[pl.ds(r0, CONF_RB), :]
        xc = u - jnp.mean(u, axis=-1, keepdims=True)
        y = xc * lax.rsqrt(jnp.mean(xc * xc, axis=-1, keepdims=True) + EPS) * lng_ref[...] + lnb_ref[...]
        y = _silu(y).astype(BF16)
        o_ref[pl.ds(r0, CONF_RB), :] = jnp.dot(y, pw_ref[...], preferred_element_type=F32) + pwb_ref[...]
        return carry

    lax.fori_loop(0, CH_ROWS // CONF_RB, norm_rb, 0, unroll=True)


def _conformer(l, cols, n_ctx_rows, with_ctx, w, b, lng, lnb, pw, pwb):
    rows = cols.shape[0]
    nc = rows // CH_ROWS
    nc_ctx = n_ctx_rows // CH_ROWS
    first = 0 if with_ctx else nc_ctx
    hpc = CH_ROWS // CONF_HALO
    n_halo = rows // CONF_HALO
    return pl.pallas_call(
        functools.partial(_conf_kernel, first, nc_ctx, nc),
        out_shape=jax.ShapeDtypeStruct(((nc - first) * CH_ROWS, CONF_W), F32),
        grid=(nc - first,),
        in_specs=[
            pl.BlockSpec((CH_ROWS, 2 * CONF_W), lambda i: (i + first, 0)),
            pl.BlockSpec((CONF_HALO, 2 * CONF_W), lambda i: (jnp.maximum((i + first) * hpc - 1, 0), 0)),
            pl.BlockSpec((CONF_HALO, 2 * CONF_W), lambda i: (jnp.minimum((i + first + 1) * hpc, n_halo - 1), 0)),
            _layer_spec(l, (CONF_K, CONF_W)),
            _layer_spec(l, (1, CONF_W)),
            _layer_spec(l, (1, CONF_W)),
            _layer_spec(l, (1, CONF_W)),
            _layer_spec(l, (CONF_W, CONF_W)),
            _layer_spec(l, (1, CONF_W)),
        ],
        out_specs=pl.BlockSpec((CH_ROWS, CONF_W), lambda i: (i, 0)),
        scratch_shapes=[pltpu.VMEM((CONF_W // LANE, CH_ROWS + 2 * CONF_HALO, LANE), F32),
                        pltpu.VMEM((CH_ROWS, CONF_W), F32)],
        compiler_params=_cparams(("arbitrary",)),
        name="conformer",
    )(cols, cols, cols, w, b, lng, lnb, pw, pwb)


def _s5_kernel(direction, *refs):
    if direction == 0:
        u_ref, bmat_ref, ar_ref, ai_ref, cmat_ref, o_ref, hbuf, st = refs
    else:
        (u_ref, bmat_ref, ar_ref, ai_ref, cmat_ref, yf_ref, dsk_ref, gw_ref, gb_ref,
         o_ref, hbuf, st) = refs
    i = pl.program_id(0)

    @pl.when(i == 0)
    def _():
        st[...] = jnp.zeros_like(st)

    u = u_ref[...]
    rb = 2 * LANE
    for k in range(u.shape[0] // rb):
        hbuf[k * rb:(k + 1) * rb, :] = jnp.dot(u[k * rb:(k + 1) * rb, :].astype(BF16), bmat_ref[...],
                                                preferred_element_type=F32)
    ar = jnp.broadcast_to(ar_ref[...], (NB, S5_S))
    ai = jnp.broadcast_to(ai_ref[...], (NB, S5_S))

    def step(j, carry):
        hr, hi = carry
        t = j if direction == 0 else S5_TL - 1 - j
        r0 = pl.multiple_of(t * NB, NB)
        nr = ar * hr - ai * hi + hbuf[pl.ds(r0, NB), 0:S5_S]
        ni = ar * hi + ai * hr + hbuf[pl.ds(r0, NB), S5_S:]
        hbuf[pl.ds(r0, NB), 0:S5_S] = nr
        hbuf[pl.ds(r0, NB), S5_S:] = ni
        return nr, ni

    hr, hi = lax.fori_loop(0, S5_TL, step, (st[0], st[1]), unroll=2)
    st[0] = hr
    st[1] = hi
    for k in range(u.shape[0] // rb):
        rows = slice(k * rb, (k + 1) * rb)
        o_ref[rows, :] = jnp.dot(hbuf[rows, :].astype(BF16), cmat_ref[...], preferred_element_type=F32)
    if direction == 1:
        for k in range(u.shape[0] // rb):
            rows = slice(k * rb, (k + 1) * rb)
            v = jax.nn.gelu(o_ref[rows, :] + yf_ref[rows, :] + dsk_ref[...] * u[rows, :], approximate=True)
            gv = jnp.dot(v.astype(BF16), gw_ref[...], preferred_element_type=F32) + gb_ref[...]
            o_ref[rows, :] = gv[:, :S5_W] * jax.nn.sigmoid(gv[:, S5_W:])


def _s5_pass(l, direction, u, n_ctx_rows, bmat, ar, ai, cmat, yf, dsk, gw, gb):
    rows = u.shape[0]
    tr = S5_TL * NB
    nblk = rows // tr
    nb_ctx = n_ctx_rows // tr
    blk = functools.partial(_chunk_of, direction, nc_ctx=nb_ctx, nc=nblk)

    def dir_spec(shape):
        nd = len(shape)
        return pl.BlockSpec((None, None) + tuple(shape), lambda i: (l, direction) + (0,) * nd)

    in_specs = [
        pl.BlockSpec((tr, S5_W), lambda i: (blk(i), 0)),
        dir_spec((S5_W, 2 * S5_S)),
        dir_spec((1, S5_S)),
        dir_spec((1, S5_S)),
        _layer_spec(l, (2 * S5_S, S5_W)),
    ]
    args = [u, bmat, ar, ai, cmat]
    if direction == 1:
        in_specs += [
            pl.BlockSpec((tr, S5_W), lambda i: (blk(i), 0)),
            _layer_spec(l, (1, S5_W)),
            _layer_spec(l, (S5_W, 2 * S5_W)),
            _layer_spec(l, (1, 2 * S5_W)),
        ]
        args += [yf, dsk, gw, gb]
    return pl.pallas_call(
        functools.partial(_s5_kernel, direction),
        out_shape=jax.ShapeDtypeStruct((rows, S5_W), F32),
        grid=(nblk,),
        in_specs=in_specs,
        out_specs=pl.BlockSpec((tr, S5_W), lambda i: (blk(i), 0)),
        scratch_shapes=[pltpu.VMEM((tr, 2 * S5_S), F32), pltpu.VMEM((2, NB, S5_S), F32)],
        compiler_params=_cparams(("arbitrary",)),
        name="s5_fwd" if direction == 0 else "s5_bwd",
    )(*args)


def _s5_operators(lam_re, lam_im, log_step, b_re, b_im, c_re, c_im):
    depth = lam_re.shape[0]
    eye = jnp.eye(S5_G, dtype=F32)
    step = jnp.exp(log_step)[..., None]
    mag = jnp.exp(lam_re * step)
    ar, ai = mag * jnp.cos(lam_im * step), mag * jnp.sin(lam_im * step)
    inv_den = 1.0 / (lam_re * lam_re + lam_im * lam_im)
    cr = ((ar - 1.0) * lam_re + ai * lam_im) * inv_den
    ci = (ai * lam_re - (ar - 1.0) * lam_im) * inv_den
    bre, bim = b_re[:, None], b_im[:, None]
    bbr = cr[..., None] * bre - ci[..., None] * bim
    bbi = cr[..., None] * bim + ci[..., None] * bre
    bd_r = jnp.einsum("ldgph,gk->ldghkp", bbr, eye).reshape(depth, 2, S5_W, S5_S)
    bd_i = jnp.einsum("ldgph,gk->ldghkp", bbi, eye).reshape(depth, 2, S5_W, S5_S)
    bmat = jnp.concatenate([bd_r, bd_i], axis=-1).astype(BF16)
    cre = jnp.einsum("lghp,gk->lgpkh", c_re, eye).reshape(depth, S5_S, S5_W)
    cim = jnp.einsum("lghp,gk->lgpkh", c_im, eye).reshape(depth, S5_S, S5_W)
    cmat = jnp.concatenate([cre, -cim], axis=1).astype(BF16)
    return bmat, ar.reshape(depth, 2, 1, S5_S), ai.reshape(depth, 2, 1, S5_S), cmat


def _route_rows(lg):
    m = jnp.max(lg, axis=0, keepdims=True)
    ex = jnp.exp(lg - m)
    probs = ex / jnp.sum(ex, axis=0, keepdims=True)
    p = [probs[e:e + 1, :] for e in range(NE)]
    tops = []
    for q in range(NGRP):
        v = p[q * EPG:(q + 1) * EPG]
        m1 = functools.reduce(jnp.maximum, v)
        i1 = jnp.full_like(m1, float(EPG - 1))
        for j in range(EPG - 2, -1, -1):
            i1 = jnp.where(v[j] == m1, float(j), i1)
        rest = [jnp.where(i1 == float(j), -jnp.inf, v[j]) for j in range(EPG)]
        m2 = functools.reduce(jnp.maximum, rest)
        i2 = jnp.full_like(m2, float(EPG - 1))
        for j in range(EPG - 2, -1, -1):
            i2 = jnp.where(jnp.logical_and(rest[j] == m2, i1 != float(j)), float(j), i2)
        tops.append((m1, i1, m2, i2))
    score = [t[0] + t[2] for t in tops]
    best = functools.reduce(jnp.maximum, score)
    sel = tops[NGRP - 1] + (jnp.full_like(best, float(NGRP - 1)),)
    for q in range(NGRP - 2, -1, -1):
        hit = score[q] == best
        sel = tuple(jnp.where(hit, a, b) for a, b in zip(tops[q] + (jnp.full_like(best, float(q)),), sel))
    m1, i1, m2, i2, grp = sel
    den = m1 + m2
    return grp * EPG + i1, grp * EPG + i2, m1 / den, m2 / den


def _dot_nt_3pass(a, b):
    nt = (((1,), (1,)), ((), ()))
    a_hi = a.astype(BF16)
    b_hi = b.astype(BF16)
    a_lo = (a - a_hi.astype(F32)).astype(BF16)
    b_lo = (b - b_hi.astype(F32)).astype(BF16)
    out = lax.dot_general(a_hi, b_hi, nt, preferred_element_type=F32)
    out = out + lax.dot_general(a_lo, b_hi, nt, preferred_element_type=F32)
    return out + lax.dot_general(a_hi, b_lo, nt, preferred_element_type=F32)


def _out_kernel(a_ref, b_ref, s_ref, x_ref, mod_ref, g_ref, w_ref, rw_ref, rb_ref,
                x1_ref, h2_ref, rt_ref, gc_ref, cnt_ref):
    tm = x_ref.shape[0]
    rb = 2 * LANE
    for k in range(tm // rb):
        rows = slice(k * rb, (k + 1) * rb)
        mix = jnp.dot(a_ref[rows, :].astype(BF16), w_ref[0:SSD_W, :], preferred_element_type=F32)
        mix = mix + jnp.dot(b_ref[rows, :].astype(BF16), w_ref[SSD_W:SSD_W + CONF_W, :],
                            preferred_element_type=F32)
        mix = mix + jnp.dot(s_ref[rows, :].astype(BF16), w_ref[SSD_W + CONF_W:, :], preferred_element_type=F32)
        x1 = (x_ref[rows, :].reshape(rb // NB, NB, D) + mod_ref[2][None] * mix.reshape(rb // NB, NB, D))
        x1 = x1.reshape(rb, D)
        x1_ref[rows, :] = x1
        ms = jnp.mean(x1 * x1, axis=-1, keepdims=True)
        xn = x1 * lax.rsqrt(ms + EPS) * g_ref[...]
        h2_ref[rows, :] = (xn.reshape(rb // NB, NB, D) * (1.0 + mod_ref[4])[None] + mod_ref[3][None]).reshape(rb, D)
    lg = _dot_nt_3pass(rw_ref[...], h2_ref[...]) + rb_ref[...]
    e1, e2, g1, g2 = _route_rows(lg)
    eid = lax.broadcasted_iota(I32, (NE, tm), 0).astype(F32)
    oh0 = (eid == e1).astype(F32)
    oh1 = (eid == e2).astype(F32)
    t0 = lax.broadcasted_iota(I32, (tm, tm), 0)
    t1 = lax.broadcasted_iota(I32, (tm, tm), 1)
    before = (t0 < t1).astype(BF16)
    pre0 = jnp.dot(oh0.astype(BF16), before, preferred_element_type=F32)
    pre1 = jnp.dot(oh1.astype(BF16), before, preferred_element_type=F32)
    lr0 = jnp.sum(oh0 * pre0, axis=0, keepdims=True)
    lr1 = jnp.sum(oh1 * pre1, axis=0, keepdims=True)
    rt_ref[...] = jnp.concatenate([e1, e2, lr0, lr1, jnp.zeros((NB - 4, tm), F32)], axis=0)
    gates = jnp.concatenate([g1, g2, jnp.zeros((LANE - 2, tm), F32)], axis=0)
    gc_ref[...] = gates.T
    cnt = jnp.concatenate([jnp.sum(oh0, axis=1, keepdims=True), jnp.sum(oh1, axis=1, keepdims=True)], axis=0)
    cnt_ref[...] = jnp.broadcast_to(cnt, (2 * NE, LANE))


def _out_proj(l, a, bconf, s, x, modtabs, g_ffn, w_out, rw_t, rb, n_ctx_rows, conf_first, blk0, nblk):
    tm = ROW_TM
    ctx_blk = n_ctx_rows // tm
    rows = nblk * tm
    return pl.pallas_call(
        _out_kernel,
        out_shape=[
            jax.ShapeDtypeStruct((rows, D), F32),
            jax.ShapeDtypeStruct((rows, D), F32),
            jax.ShapeDtypeStruct((NB, rows), F32),
            jax.ShapeDtypeStruct((rows, LANE), F32),
            jax.ShapeDtypeStruct((nblk, 2 * NE, LANE), F32),
        ],
        grid=(nblk,),
        in_specs=[
            pl.BlockSpec((tm, SSD_W), lambda i: (i + blk0, 0)),
            pl.BlockSpec((tm, CONF_W), lambda i: (i + blk0 - conf_first, 0)),
            pl.BlockSpec((tm, S5_W), lambda i: (i + blk0, 0)),
            pl.BlockSpec((tm, D), lambda i: (i + blk0, 0)),
            _mod_spec(l, ctx_blk, blk0),
            _layer_spec(l, (1, D)),
            _layer_spec(l, (D, D)),
            _const_spec((NE, D)),
            _const_spec((NE, 1)),
        ],
        out_specs=[
            pl.BlockSpec((tm, D), lambda i: (i, 0)),
            pl.BlockSpec((tm, D), lambda i: (i, 0)),
            pl.BlockSpec((NB, tm), lambda i: (0, i)),
            pl.BlockSpec((tm, LANE), lambda i: (i, 0)),
            pl.BlockSpec((None, 2 * NE, LANE), lambda i: (i, 0, 0)),
        ],
        compiler_params=_cparams(("arbitrary",)),
        name="out_proj_route",
    )(a, bconf, s, x, modtabs, g_ffn, w_out, rw_t, rb)


def _moe_kernel(l, second, it_ref, ie_ref, lo_ref, hi_ref, nx_ref, h_ref, wg_hbm, wu_hbm, wd_hbm, *rest):
    rest = rest[1:] if second else rest
    out_ref, stage, wgb, wub, wdb, sem, cur = rest
    o_ref = out_ref if second else out_ref.at[0]
    w = pl.program_id(0)
    lo = lo_ref[w]
    hi = hi_ref[w]
    new_tile = jnp.logical_or(w == 0, it_ref[w] != it_ref[jnp.maximum(w - 1, 0)])

    def fetch(e, slot):
        return [pltpu.make_async_copy(src.at[l, e], stage.at[slot, k], sem.at[slot, k])
                for k, src in enumerate((wg_hbm, wu_hbm, wd_hbm))]

    @pl.when(w == 0)
    def _():
        cur[0] = -1
        cur[1] = nx_ref[0]
        cur[2] = 0

        @pl.when(nx_ref[0] >= 0)
        def _():
            for c in fetch(nx_ref[0], 0):
                c.start()

    @pl.when(new_tile)
    def _():
        out_ref[...] = jnp.zeros_like(out_ref)

    @pl.when(hi > lo)
    def _():
        e = ie_ref[w]

        @pl.when(cur[0] != e)
        def _():
            slot = cur[2]
            for c in fetch(e, slot):
                c.wait()
            nxt = nx_ref[1 + e]

            @pl.when(nxt >= 0)
            def _():
                for c in fetch(nxt, 1 - slot):
                    c.start()

            wgb[...] = stage[slot, 0].astype(BF16)
            wub[...] = stage[slot, 1].astype(BF16)
            wdb[...] = stage[slot, 2].astype(BF16)
            cur[0] = e
            cur[1] = nxt
            cur[2] = 1 - slot

        for r0 in range(0, MOE_TM, MOE_SUB):
            @pl.when(jnp.logical_and(lo < r0 + MOE_SUB, hi > r0))
            def _():
                rows = slice(r0, r0 + MOE_SUB)
                h = h_ref[rows, :].astype(BF16)
                gate = jnp.dot(h, wgb[...], preferred_element_type=F32)
                up = jnp.dot(h, wub[...], preferred_element_type=F32)
                act = (_silu(gate) * up).astype(BF16)
                y = jnp.dot(act, wdb[...], preferred_element_type=F32)
                r = r0 + lax.broadcasted_iota(I32, (MOE_SUB, 1), 0)
                keep = jnp.logical_and(r >= lo, r < hi)
                o_ref[rows, :] = jnp.where(keep, y, o_ref[rows, :])


def _moe_experts(l, hs, items, tile0, w_gate, w_up, w_down, prev):
    item_tile, item_expert, item_lo, item_hi, next_expert = items
    n_items = item_tile.shape[0]
    n_prefetch = 5
    second = prev is not None
    hbm = pl.BlockSpec(memory_space=pl.ANY)
    in_specs = [pl.BlockSpec((MOE_TM, D), lambda w, it, ie, lo, hi, nx: (it[w] - tile0, 0)), hbm, hbm, hbm]
    args = [item_tile, item_expert, item_lo, item_hi, next_expert, hs, w_gate, w_up, w_down]
    aliases = {}
    if second:
        in_specs.append(hbm)
        args.append(prev)
        aliases = {len(args) - 1: 0}
        out_spec = pl.BlockSpec((None, MOE_TM, D), lambda w, it, ie, lo, hi, nx: (1, it[w] - tile0, 0))
    else:
        out_spec = pl.BlockSpec((2, MOE_TM, D), lambda w, it, ie, lo, hi, nx: (0, it[w] - tile0, 0))
    n_mats = 3
    return pl.pallas_call(
        functools.partial(_moe_kernel, l, second),
        out_shape=jax.ShapeDtypeStruct((2, hs.shape[0], D), F32),
        grid_spec=pltpu.PrefetchScalarGridSpec(
            num_scalar_prefetch=n_prefetch,
            grid=(n_items,),
            in_specs=in_specs,
            out_specs=out_spec,
            scratch_shapes=[pltpu.VMEM((2, n_mats, D, D), F32)] + [pltpu.VMEM((D, D), BF16)] * n_mats
            + [pltpu.SemaphoreType.DMA((2, n_mats)), pltpu.SMEM((3,), I32)],
        ),
        input_output_aliases=aliases,
        compiler_params=_cparams(("arbitrary",)),
        name="moe_experts",
    )(*args)


def _moe_plan(route, cnt, tile_parts):
    rows = route.shape[1]
    nblk = cnt.shape[0]
    n_flat = 2 * rows
    e = route[0:2].astype(I32)
    lrank = route[2:4].astype(I32)
    counts = cnt[:, :, 0].astype(I32).reshape(nblk, 2, NE)
    per = jnp.transpose(counts, (1, 0, 2)).reshape(2 * nblk, NE)
    before = jnp.cumsum(per, axis=0) - per
    gcount = jnp.sum(per, axis=0)
    gend = jnp.cumsum(gcount)
    gstart = gend - gcount
    base = jnp.transpose((before + gstart[None, :]).reshape(2, nblk, NE), (2, 0, 1))[..., None]
    onehot = e.reshape(1, 2, nblk, ROW_TM) == jnp.arange(NE, dtype=I32).reshape(NE, 1, 1, 1)
    inv = jnp.sum(jnp.where(onehot, base, 0), axis=0).reshape(2, rows) + lrank
    keys = (e * 65536).reshape(-1) + jnp.arange(n_flat, dtype=I32)
    order = lax.sort(keys, is_stable=False) & 0xFFFF
    tok = jnp.where(order >= rows, order - rows, order)
    items = []
    for t0, tn in tile_parts:
        row_lo, row_hi = t0 * MOE_TM, (t0 + tn) * MOE_TM
        tile_start = (t0 + jnp.arange(tn, dtype=I32)) * MOE_TM
        bnd = lax.sort(jnp.concatenate([tile_start, jnp.clip(gstart[1:], row_lo, row_hi)]), is_stable=False)
        bnd_hi = jnp.concatenate([bnd[1:], jnp.full((1,), row_hi, I32)])
        item_tile = jnp.minimum(bnd // MOE_TM, t0 + tn - 1)
        first_row = jnp.minimum(bnd, row_hi - 1)[:, None]
        item_expert = jnp.minimum(jnp.sum((gend[None, :] <= first_row).astype(I32), axis=1), NE - 1)
        ids = jnp.arange(NE, dtype=I32)
        present = jnp.logical_and(gend > row_lo, gstart < row_hi) & (gcount > 0)
        later = jnp.logical_and(present[None, :], ids[None, :] > jnp.arange(-1, NE, dtype=I32)[:, None])
        next_expert = jnp.min(jnp.where(later, ids[None, :], NE), axis=1)
        next_expert = jnp.where(next_expert == NE, -1, next_expert).astype(I32)
        items.append((item_tile, item_expert, bnd - item_tile * MOE_TM, bnd_hi - item_tile * MOE_TM, next_expert))
    return tok, inv, items


def _final_kernel(x_ref, ya_ref, yb_ref, gc_ref, mod_ref, g_ref, o_ref, scr):
    tm = x_ref.shape[0]
    gc = gc_ref[...]
    ff = (gc[:, 0:1] * ya_ref[...] + gc[:, 1:2] * yb_ref[...]).reshape(tm // NB, NB, D)
    x2 = (x_ref[...].reshape(tm // NB, NB, D) + mod_ref[5][None] * ff).reshape(tm, D)
    ms = jnp.mean(x2 * x2, axis=-1, keepdims=True)
    x2 = x2 * lax.rsqrt(ms + EPS) * g_ref[...]
    ncb = D // LANE
    for k in range(ncb):
        scr[k] = x2[:, k * LANE:(k + 1) * LANE]
    for b in range(NB):
        for k in range(ncb):
            o_ref[b, :, k * LANE:(k + 1) * LANE] = scr[k, pl.ds(b, ROW_TQ, stride=NB), :]


def _final_combine(l, x1, ya, yb, gcol, modtabs, g_final):
    tm = ROW_TM
    rows = x1.shape[0]
    row_spec = pl.BlockSpec((tm, D), lambda i: (i, 0))
    return pl.pallas_call(
        _final_kernel,
        out_shape=jax.ShapeDtypeStruct((NB, rows // NB, D), F32),
        grid=(rows // tm,),
        in_specs=[row_spec, row_spec, row_spec, pl.BlockSpec((tm, LANE), lambda i: (i, 0)),
                  _mod_spec(l, 0), _const_spec((1, D))],
        out_specs=pl.BlockSpec((NB, ROW_TQ, D), lambda i: (0, i, 0)),
        scratch_shapes=[pltpu.VMEM((D // LANE, tm, LANE), F32)],
        compiler_params=_cparams(("arbitrary",)),
        name="moe_combine_final",
    )(x1, ya, yb, gcol, modtabs, g_final)


def _grid_pos_embed(rows_n):
    rr, cc = jnp.meshgrid(jnp.arange(rows_n, dtype=F32), jnp.arange(GRID_W, dtype=F32), indexing="ij")
    quarter = D // 4
    inv_freq = jnp.exp(-math.log(10000.0) * jnp.arange(quarter, dtype=F32) / quarter)

    def emb(pos):
        ang = pos.reshape(-1)[:, None] * inv_freq[None, :]
        return jnp.concatenate([jnp.sin(ang), jnp.cos(ang)], axis=-1)

    return jnp.concatenate([emb(rr), emb(cc)], axis=-1)


def _take_rows(a, idx):
    return a.at[idx].get(mode="promise_in_bounds")


def _pad_last(v, width):
    return jnp.pad(v, [(0, 0)] * (v.ndim - 1) + [(0, width - v.shape[-1])])


def kernel(x, c, ctx, c_ctx, w_ada, b_ada, g_mix, w_in, ssd_conv_w, ssd_conv_b, ssd_dt_bias, ssd_a_log, ssd_d, ssd_norm_g, conf_dw_w, conf_dw_b, conf_ln_g, conf_ln_b, conf_pw_w, conf_pw_b, s5_lambda_re, s5_lambda_im, s5_log_step, s5_b_re, s5_b_im, s5_c_re, s5_c_im, s5_d, s5_glu_w, s5_glu_b, w_out, g_ffn, router_w, router_b, exp_w_gate, exp_w_up, exp_w_down, g_final):
    bsz, seq, _ = x.shape
    ctx_len = ctx.shape[1]
    depth = w_ada.shape[0]
    assert bsz == NB and seq % SSD_CHUNK == 0 and ctx_len % SSD_CHUNK == 0
    n_ctx_rows = ctx_len * NB
    rows_all = (ctx_len + seq) * NB

    cond = jnp.concatenate([c, c_ctx[None, :], jnp.zeros((2 * NB - bsz - 1, D), F32)], axis=0)
    mod = _modulation(cond, w_ada, b_ada)
    mod = mod.reshape(depth, 2 * NB, N_MOD, D)
    mod_lat = jnp.transpose(mod[:, :NB], (0, 2, 1, 3))
    mod_ctx = jnp.broadcast_to(mod[:, NB][:, :, None, :], (depth, N_MOD, NB, D))
    modtabs = jnp.stack([mod_ctx, mod_lat], axis=1)

    o_b = SSD_W + SSD_XBC + 2 * SSD_H
    w_in_p = jnp.concatenate(
        [w_in[:, :, :o_b], jnp.zeros((depth, D, DT_PAD - 2 * SSD_H), F32), w_in[:, :, o_b:]], axis=2).astype(BF16)
    g_mix3 = g_mix.reshape(depth, 1, D)
    ssd_cb = ssd_conv_b.reshape(depth, 1, SSD_XBC)
    ssd_dtb = _pad_last(ssd_dt_bias.reshape(depth, 1, 2 * SSD_H), DT_PAD)
    ssd_alog = _pad_last(ssd_a_log.reshape(depth, 1, 2 * SSD_H), DT_PAD)
    ssd_dsk = jnp.repeat(ssd_d, SSD_P, axis=1).reshape(depth, 1, SSD_W)
    ssd_ng = ssd_norm_g.reshape(depth, 1, SSD_W)
    conf_b = conf_dw_b.reshape(depth, 1, CONF_W)
    conf_lg = conf_ln_g.reshape(depth, 1, CONF_W)
    conf_lb = conf_ln_b.reshape(depth, 1, CONF_W)
    conf_pw = conf_pw_w.astype(BF16)
    conf_pb = conf_pw_b.reshape(depth, 1, CONF_W)
    s5_bmat, s5_ar, s5_ai, s5_cmat = _s5_operators(s5_lambda_re, s5_lambda_im, s5_log_step,
                                                   s5_b_re, s5_b_im, s5_c_re, s5_c_im)
    s5_dsk = s5_d.reshape(depth, 1, S5_W)
    s5_gw = s5_glu_w.astype(BF16)
    s5_gb = s5_glu_b.reshape(depth, 1, 2 * S5_W)
    w_out_b = w_out.astype(BF16)
    g_ffn3 = g_ffn.reshape(depth, 1, D)
    router_w_t = router_w.T
    router_b2 = router_b.reshape(NE, 1)
    g_final2 = g_final.reshape(1, D)
    pos = _grid_pos_embed(seq // GRID_W)

    pending = (ctx, x)
    for l in range(depth):
        last = l == depth - 1
        ssd_cols, conf_cols, s5_u, xall = _in_proj(l, l == 0, pending, pos, modtabs, g_mix3, w_in_p,
                                                   rows_all, n_ctx_rows)
        ssd_fwd = _ssd_pass(l, 0, ssd_cols, n_ctx_rows, ssd_conv_w, ssd_cb, ssd_dtb, ssd_alog, ssd_dsk, None)
        a_mix = _ssd_pass(l, 1, ssd_cols, n_ctx_rows, None, None, ssd_dtb, ssd_alog, ssd_ng, ssd_fwd)
        b_mix = _conformer(l, conf_cols, n_ctx_rows, not last, conf_dw_w, conf_b, conf_lg, conf_lb, conf_pw, conf_pb)
        yf = _s5_pass(l, 0, s5_u, n_ctx_rows, s5_bmat, s5_ar, s5_ai, s5_cmat, None, None, None, None)
        s_mix = _s5_pass(l, 1, s5_u, n_ctx_rows, s5_bmat, s5_ar, s5_ai, s5_cmat, yf, s5_dsk, s5_gw, s5_gb)
        ctx_blk = n_ctx_rows // ROW_TM
        first = ctx_blk if last else 0
        n_out = rows_all // ROW_TM - first
        x1, h2, route, gcol, cnt = _out_proj(l, a_mix, b_mix, s_mix, xall, modtabs, g_ffn3, w_out_b,
                                             router_w_t, router_b2, n_ctx_rows, first, first, n_out)
        n_tiles = 2 * n_out * ROW_TM // MOE_TM
        assert n_tiles % 2 == 0
        tile_parts = [(0, n_tiles // 2), (n_tiles // 2, n_tiles // 2)]
        tok, inv, items = _moe_plan(route, cnt, tile_parts)
        hs = [_take_rows(h2, tok[t0 * MOE_TM:(t0 + tn) * MOE_TM]) for t0, tn in tile_parts]
        y = None
        for (t0, tn), h, it in zip(tile_parts, hs, items):
            y = _moe_experts(l, h, it, t0, exp_w_gate, exp_w_up, exp_w_down, y)
        y = y.reshape(n_tiles * MOE_TM, D)
        ya = _take_rows(y, inv[0])
        yb = _take_rows(y, inv[1])
        pending = (x1, ya, yb, gcol)
    return _final_combine(depth - 1, *pending, modtabs, g_final2)
```

```python
import functools
import math

import jax
import jax.numpy as jnp
from jax import lax
from jax.experimental import pallas as pl
from jax.experimental.pallas import tpu as pltpu

F32 = jnp.float32
BF16 = jnp.bfloat16
I32 = jnp.int32
HIGHEST = lax.Precision.HIGHEST

NB = 8
D = 1024
GRID_W = 64
N_MOD = 6
EPS = 1e-6
LANE = 128
SSD_W = 512
SSD_P = 64
SSD_H = 8
SSD_G = 2
SSD_N = 128
SSD_K = 5
SSD_XBC = SSD_W + 2 * SSD_G * SSD_N
SSD_CHUNK = 128
DT_PAD = LANE
SSD_COLS_P = SSD_W + SSD_XBC + DT_PAD
CONF_W = 256
CONF_K = 31
S5_W = 256
S5_G = 16
S5_P = 64
S5_CH = 16
S5_S = S5_G * S5_P
S5_TL = 128
NE = 16
NGRP = 4
EPG = 4
MOE_TM = 512
MOE_SUB = 256
IN_COLS_P = SSD_COLS_P + 2 * CONF_W + S5_W
ROW_TM = 512
ROW_TQ = ROW_TM // NB
CH_ROWS = SSD_CHUNK * NB
VMEM_LIMIT = 56 * 1024 * 1024


def _cparams(sem):
    return pltpu.CompilerParams(dimension_semantics=sem, vmem_limit_bytes=VMEM_LIMIT)


def _const_spec(shape):
    nd = len(shape)
    return pl.BlockSpec(shape, lambda *_: (0,) * nd)


def _layer_spec(l, shape):
    nd = len(shape)
    return pl.BlockSpec((None,) + tuple(shape), lambda *_: (l,) + (0,) * nd)


def _silu(v):
    return v * jax.nn.sigmoid(v)


def _mod_kernel(c_ref, w_ref, b_ref, o_ref):
    c = c_ref[...]
    h = _silu(c).astype(BF16)
    o_ref[...] = jnp.dot(h, w_ref[...].astype(BF16), preferred_element_type=F32) + b_ref[...]


def _modulation(cond, w_ada, b_ada):
    depth = w_ada.shape[0]
    nrow = cond.shape[0]
    return pl.pallas_call(
        _mod_kernel,
        out_shape=jax.ShapeDtypeStruct((depth, nrow, N_MOD * D), F32),
        grid=(depth, N_MOD),
        in_specs=[
            pl.BlockSpec((nrow, D), lambda l, j: (0, 0)),
            pl.BlockSpec((None, D, D), lambda l, j: (l, 0, j)),
            pl.BlockSpec((None, 1, D), lambda l, j: (l, 0, j)),
        ],
        out_specs=pl.BlockSpec((None, nrow, D), lambda l, j: (l, 0, j)),
        compiler_params=_cparams(("arbitrary", "arbitrary")),
        name="adaln_mod",
    )(cond, w_ada, b_ada.reshape(depth, 1, N_MOD * D))


def _mod_spec(l, ctx_blk, first=0):
    return pl.BlockSpec((None, None, N_MOD, NB, D),
                        lambda i, *_: (l, jnp.where(i + first < ctx_blk, 0, 1), 0, 0, 0))


def _norm_mod_project(x, mod_ref, g_ref, w_ref, ssd_ref, conf_ref, s5_ref):
    tm = x.shape[0]
    ms = jnp.mean(x * x, axis=-1, keepdims=True)
    xn = x * lax.rsqrt(ms + EPS) * g_ref[...]
    h = xn.reshape(tm // NB, NB, D) * (1.0 + mod_ref[1])[None] + mod_ref[0][None]
    h = h.reshape(tm, D).astype(BF16)
    ssd_ref[...] = jnp.dot(h, w_ref[:, :SSD_COLS_P], preferred_element_type=F32)
    conf_ref[...] = jnp.dot(h, w_ref[:, SSD_COLS_P:SSD_COLS_P + 2 * CONF_W], preferred_element_type=F32)
    s5_ref[...] = jnp.dot(h, w_ref[:, SSD_COLS_P + 2 * CONF_W:], preferred_element_type=F32)


def _in_first_kernel(ctx_blk, ctx_ref, x_ref, pos_ref, mod_ref, g_ref, w_ref,
                     ssd_ref, conf_ref, s5_ref, x0_ref, scr):
    i = pl.program_id(0)
    ncb = D // LANE

    @pl.when(i < ctx_blk)
    def _():
        for b in range(NB):
            for k in range(ncb):
                scr[k, pl.ds(b, ROW_TQ, stride=NB), :] = ctx_ref[b, :, k * LANE:(k + 1) * LANE]

    @pl.when(i >= ctx_blk)
    def _():
        for b in range(NB):
            for k in range(ncb):
                scr[k, pl.ds(b, ROW_TQ, stride=NB), :] = (
                    x_ref[b, :, k * LANE:(k + 1) * LANE] + pos_ref[:, k * LANE:(k + 1) * LANE])

    x = jnp.concatenate([scr[k] for k in range(ncb)], axis=1)
    x0_ref[...] = x
    _norm_mod_project(x, mod_ref, g_ref, w_ref, ssd_ref, conf_ref, s5_ref)


def _in_next_kernel(x1_ref, ya_ref, yb_ref, gc_ref, modp_ref, mod_ref, g_ref, w_ref,
                    ssd_ref, conf_ref, s5_ref, x_ref):
    tm = x1_ref.shape[0]
    gc = gc_ref[...]
    ff = (gc[:, 0:1] * ya_ref[...] + gc[:, 1:2] * yb_ref[...]).reshape(tm // NB, NB, D)
    x = (x1_ref[...].reshape(tm // NB, NB, D) + modp_ref[5][None] * ff).reshape(tm, D)
    x_ref[...] = x
    _norm_mod_project(x, mod_ref, g_ref, w_ref, ssd_ref, conf_ref, s5_ref)


def _in_proj(l, first, x_or_pair, pos, modtabs, g_mix, w_in_p, rows, n_ctx_rows):
    tm = ROW_TM
    nblk = rows // tm
    ctx_blk = n_ctx_rows // tm
    common_specs = [_mod_spec(l, ctx_blk), _layer_spec(l, (1, D)), _layer_spec(l, (D, IN_COLS_P))]
    common_args = [modtabs, g_mix, w_in_p]
    out_shape = [
        jax.ShapeDtypeStruct((rows, SSD_COLS_P), F32),
        jax.ShapeDtypeStruct((rows, 2 * CONF_W), F32),
        jax.ShapeDtypeStruct((rows, S5_W), F32),
    ]
    out_specs = [
        pl.BlockSpec((tm, SSD_COLS_P), lambda i: (i, 0)),
        pl.BlockSpec((tm, 2 * CONF_W), lambda i: (i, 0)),
        pl.BlockSpec((tm, S5_W), lambda i: (i, 0)),
    ]
    if first:
        ctx, x = x_or_pair
        body = functools.partial(_in_first_kernel, ctx_blk)
        in_specs = [
            pl.BlockSpec((NB, ROW_TQ, D), lambda i: (0, jnp.minimum(i, ctx_blk - 1), 0)),
            pl.BlockSpec((NB, ROW_TQ, D), lambda i: (0, jnp.maximum(i - ctx_blk, 0), 0)),
            pl.BlockSpec((ROW_TQ, D), lambda i: (jnp.maximum(i - ctx_blk, 0), 0)),
        ] + common_specs
        args = [ctx, x, pos] + common_args
        scratch = [pltpu.VMEM((D // LANE, tm, LANE), F32)]
    else:
        x1, ya, yb, gcol = x_or_pair
        body = _in_next_kernel
        in_specs = [
            pl.BlockSpec((tm, D), lambda i: (i, 0)),
            pl.BlockSpec((tm, D), lambda i: (i, 0)),
            pl.BlockSpec((tm, D), lambda i: (i, 0)),
            pl.BlockSpec((tm, LANE), lambda i: (i, 0)),
            _mod_spec(l - 1, ctx_blk),
        ] + common_specs
        args = [x1, ya, yb, gcol, modtabs] + common_args
        scratch = []
    out_shape.append(jax.ShapeDtypeStruct((rows, D), F32))
    out_specs.append(pl.BlockSpec((tm, D), lambda i: (i, 0)))
    return pl.pallas_call(
        body,
        out_shape=out_shape,
        grid=(nblk,),
        in_specs=in_specs,
        out_specs=out_specs,
        scratch_shapes=scratch,
        compiler_params=_cparams(("arbitrary",)),
        name="in_proj",
    )(*args)


def _chunk_of(direction, i, nc_ctx, nc):
    if direction == 0:
        return i
    return jnp.where(i < nc_ctx, nc_ctx - 1 - i, nc - 1 - (i - nc_ctx))


def _has_prev(c, nc_ctx):
    return jnp.logical_and(c != 0, c != nc_ctx)


def _has_next(c, nc_ctx, nc):
    return jnp.logical_and(c != nc_ctx - 1, c != nc - 1)


SSD_HALO = 2 * NB


def _dot_exact_lhs(a_b, x):
    x_hi = x.astype(BF16)
    r1 = x - x_hi.astype(F32)
    x_mid = r1.astype(BF16)
    x_lo = (r1 - x_mid.astype(F32)).astype(BF16)
    out = jnp.dot(a_b, x_hi, preferred_element_type=F32)
    out = out + jnp.dot(a_b, x_mid, preferred_element_type=F32)
    return out + jnp.dot(a_b, x_lo, preferred_element_type=F32)


def _ssd_conv(c, nc_ctx, nc, main_ref, prev_ref, next_ref, cw_ref, cb_ref, act, ext):
    has_prev = _has_prev(c, nc_ctx)
    has_next = _has_next(c, nc_ctx, nc)
    nblk = SSD_XBC // LANE
    n_rb = CH_ROWS // LANE
    for k in range(nblk):
        lo = SSD_W + k * LANE
        ext[0, k, 0:SSD_HALO, :] = jnp.where(has_prev, prev_ref[:, lo:lo + LANE], 0.0)
        ext[0, k, SSD_HALO:, :] = main_ref[0:LANE + SSD_HALO, lo:lo + LANE]
        ext[1, k, 0:LANE + SSD_HALO, :] = main_ref[CH_ROWS - LANE - SSD_HALO:, lo:lo + LANE]
        ext[1, k, LANE + SSD_HALO:, :] = jnp.where(has_next, next_ref[:, lo:lo + LANE], 0.0)

    def conv_block(tap_rows, r0):
        for k in range(nblk):
            lo = k * LANE
            acc = jnp.broadcast_to(cb_ref[:, lo:lo + LANE], (LANE, LANE))
            for tap in range(SSD_K):
                acc = acc + tap_rows(k, tap) * cw_ref[tap:tap + 1, lo:lo + LANE]
            act[k, pl.ds(r0, LANE), :] = _silu(acc)

    def conv_rb(rb, carry):
        r0 = pl.multiple_of(rb * LANE, LANE)
        conv_block(lambda k, tap: main_ref[pl.ds(r0 + NB * tap - SSD_HALO, LANE),
                                           SSD_W + k * LANE:SSD_W + (k + 1) * LANE], r0)
        return carry

    conv_block(lambda k, tap: ext[0, k, NB * tap:NB * tap + LANE, :], 0)
    lax.fori_loop(1, n_rb - 1, conv_rb, 0)
    conv_block(lambda k, tap: ext[1, k, NB * tap:NB * tap + LANE, :], CH_ROWS - LANE)


def _ssd_kernel(direction, nc_ctx, nc, *refs):
    if direction == 0:
        (main_ref, prev_ref, next_ref, cw_ref, cb_ref, dtb_ref, alog_ref, dsk_ref,
         o_ref, act, ext, dts, yout, state, tabs) = refs
    else:
        (z_ref, dtraw_ref, act, dtb_ref, alog_ref, ng_ref, yp_ref,
         o_ref, dts, yout, state, tabs) = refs
    i = pl.program_id(0)

    @pl.when(i == 0)
    def _():
        state[...] = jnp.zeros_like(state)

    if direction == 0:
        _ssd_conv(i, nc_ctx, nc, main_ref, prev_ref, next_ref, cw_ref, cb_ref, act, ext)
        raw = main_ref[:, SSD_W + SSD_XBC:] + dtb_ref[...]
    else:
        raw = dtraw_ref[...] + dtb_ref[...]
    dts[...] = jnp.maximum(raw, 0.0) + jnp.log1p(jnp.exp(-jnp.abs(raw)))
    a_row = -jnp.exp(alog_ref[...])

    tt = lax.broadcasted_iota(I32, (SSD_CHUNK, SSD_CHUNK), 0)
    ss = lax.broadcasted_iota(I32, (SSD_CHUNK, SSD_CHUNK), 1)
    mask = (ss <= tt) if direction == 0 else (ss >= tt)
    tmat_b = mask.astype(BF16)
    last = SSD_CHUNK - 1 if direction == 0 else 0
    hpg = SSD_H // SSD_G
    lane_id = lax.broadcasted_iota(I32, (1, LANE), 1)
    half_lo = lane_id < SSD_P

    def decay_tables(b, carry):
        dt_b = dts[pl.ds(b, SSD_CHUNK, stride=NB), :]
        cs = _dot_exact_lhs(tmat_b, dt_b * a_row)
        tabs[b, 0] = cs
        tabs[b, 1] = cs.T
        tabs[b, 2] = dt_b.T
        return carry

    lax.fori_loop(0, NB, decay_tables, 0, unroll=True)

    def per_batch(b, carry):
        sl = pl.ds(b, SSD_CHUNK, stride=NB)
        cs = tabs[b, 0]
        cs_t = tabs[b, 1]
        dt_t = tabs[b, 2]
        for g in range(SSD_G):
            bg_t = act[SSD_W // LANE + g, sl, :].T
            cg = act[SSD_W // LANE + SSD_G + g, sl, :]
            gmat = jnp.dot(cg.astype(BF16), bg_t.astype(BF16), preferred_element_type=F32)
            for pr in range(hpg // 2):
                k = g * (hpg // 2) + pr
                xs_pair = act[k, sl, :]
                s_pair = state[b, k]
                xs_b = xs_pair.astype(BF16)
                rhs = jnp.concatenate([xs_b, s_pair.astype(BF16)], axis=0)
                ys, ss, etot = [], [], []
                for j in range(2):
                    ln = direction * SSD_H + 2 * k + j
                    colb = jnp.broadcast_to(cs[:, ln:ln + 1], (SSD_CHUNK, SSD_CHUNK))
                    row = cs_t[ln:ln + 1, :]
                    dtr = dt_t[ln:ln + 1, :]
                    decay = jnp.where(mask, jnp.exp(jnp.where(mask, colb - row, 0.0)) * dtr, 0.0)
                    lhs = jnp.concatenate([(gmat * decay).astype(BF16),
                                           (cg * jnp.exp(colb)).astype(BF16)], axis=1)
                    ys.append(jnp.dot(lhs, rhs, preferred_element_type=F32))
                    tot = cs[last:last + 1, ln:ln + 1]
                    wrow = jnp.exp(tot - row) * dtr
                    ss.append(jnp.dot((bg_t * wrow).astype(BF16), xs_b, preferred_element_type=F32))
                    etot.append(jnp.exp(tot))
                acc_y = jnp.where(half_lo, ys[0], ys[1])
                state[b, k] = s_pair * jnp.where(half_lo, etot[0], etot[1]) + jnp.where(half_lo, ss[0], ss[1])
                if direction == 0:
                    acc_y = acc_y + dsk_ref[:, k * LANE:(k + 1) * LANE] * xs_pair
                yout[k, sl, :] = acc_y
        return carry

    lax.fori_loop(0, NB, per_batch, 0, unroll=2)

    nyb = SSD_W // LANE
    if direction == 0:
        for k in range(nyb):
            o_ref[:, k * LANE:(k + 1) * LANE] = yout[k]
    else:
        def fin(rb, carry):
            r0 = pl.multiple_of(rb * LANE, LANE)
            y = jnp.concatenate([yout[k, pl.ds(r0, LANE), :] for k in range(nyb)], axis=1)
            y = y + yp_ref[pl.ds(r0, LANE), :]
            y = y * _silu(z_ref[pl.ds(r0, LANE), :])
            ms = jnp.mean(y * y, axis=-1, keepdims=True)
            o_ref[pl.ds(r0, LANE), :] = y * lax.rsqrt(ms + EPS) * ng_ref[...]
            return carry

        lax.fori_loop(0, CH_ROWS // LANE, fin, 0, unroll=4)


def _ssd_pass(l, direction, cols, n_ctx_rows, cw, cb, dtb, alog, extra, fwd_out):
    rows = cols.shape[0]
    nc = rows // CH_ROWS
    nc_ctx = n_ctx_rows // CH_ROWS
    hpc = CH_ROWS // SSD_HALO
    n_halo = rows // SSD_HALO
    nact = SSD_XBC // LANE
    chunk = functools.partial(_chunk_of, direction, nc_ctx=nc_ctx, nc=nc)
    y_spec = pl.BlockSpec((CH_ROWS, SSD_W), lambda i: (chunk(i), 0))
    act_spec = pl.BlockSpec((nact, CH_ROWS, LANE), lambda i: (0, chunk(i), 0))
    small = [_layer_spec(l, (1, DT_PAD)), _layer_spec(l, (1, DT_PAD)), _layer_spec(l, (1, SSD_W))]
    scratch = [
        pltpu.VMEM((CH_ROWS, DT_PAD), F32),
        pltpu.VMEM((SSD_W // LANE, CH_ROWS, LANE), F32),
        pltpu.VMEM((NB, SSD_W // LANE, SSD_N, LANE), F32),
        pltpu.VMEM((NB, 3, SSD_CHUNK, DT_PAD), F32),
    ]
    if direction == 0:
        in_specs = [
            pl.BlockSpec((CH_ROWS, SSD_COLS_P), lambda i: (chunk(i), 0)),
            pl.BlockSpec((SSD_HALO, SSD_COLS_P), lambda i: (jnp.maximum(chunk(i) * hpc - 1, 0), 0)),
            pl.BlockSpec((SSD_HALO, SSD_COLS_P), lambda i: (jnp.minimum((chunk(i) + 1) * hpc, n_halo - 1), 0)),
            _layer_spec(l, (SSD_K, SSD_XBC)),
            _layer_spec(l, (1, SSD_XBC)),
        ] + small
        args = [cols, cols, cols, cw, cb, dtb, alog, extra]
        out_shape = [jax.ShapeDtypeStruct((rows, SSD_W), F32), jax.ShapeDtypeStruct((nact, rows, LANE), F32)]
        out_specs = [y_spec, act_spec]
        scratch = [pltpu.VMEM((2, nact, LANE + 2 * SSD_HALO, LANE), F32)] + scratch
    else:
        ypart, act = fwd_out
        dt_col = (SSD_W + SSD_XBC) // DT_PAD
        in_specs = [y_spec, pl.BlockSpec((CH_ROWS, DT_PAD), lambda i: (chunk(i), dt_col)), act_spec] + small + [y_spec]
        args = [cols, cols, act, dtb, alog, extra, ypart]
        out_shape = jax.ShapeDtypeStruct((rows, SSD_W), F32)
        out_specs = y_spec
    return pl.pallas_call(
        functools.partial(_ssd_kernel, direction, nc_ctx, nc),
        out_shape=out_shape,
        grid=(nc,),
        in_specs=in_specs,
        out_specs=out_specs,
        scratch_shapes=scratch,
        compiler_params=_cparams(("arbitrary",)),
        name="ssd_fwd" if direction == 0 else "ssd_bwd",
    )(*args)


CONF_HALO = 128
CONF_RB = 256


def _conf_kernel(first_chunk, nc_ctx, nc, main_ref, prev_ref, next_ref, w_ref, b_ref, lng_ref, lnb_ref,
                 pw_ref, pwb_ref, o_ref, ext, cbuf):
    c = pl.program_id(0) + first_chunk
    has_prev = _has_prev(c, nc_ctx)
    has_next = _has_next(c, nc_ctx, nc)

    def glu(ref):
        v = ref[...]
        return v[:, :CONF_W] * jax.nn.sigmoid(v[:, CONF_W:])

    um = glu(main_ref)
    up = jnp.where(has_prev, glu(prev_ref), 0.0)
    un = jnp.where(has_next, glu(next_ref), 0.0)
    nblk = CONF_W // LANE
    for k in range(nblk):
        ext[k, 0:CONF_HALO, :] = up[:, k * LANE:(k + 1) * LANE]
        ext[k, CONF_HALO:CONF_HALO + CH_ROWS, :] = um[:, k * LANE:(k + 1) * LANE]
        ext[k, CONF_HALO + CH_ROWS:, :] = un[:, k * LANE:(k + 1) * LANE]
    off = CONF_HALO - NB * (CONF_K // 2)

    def conv_rb(rb, carry):
        r0 = pl.multiple_of(rb * LANE, LANE)
        for k in range(nblk):
            lo = k * LANE
            acc = jnp.broadcast_to(b_ref[:, lo:lo + LANE], (LANE, LANE))
            for tap in range(CONF_K):
                acc = acc + ext[k, pl.ds(r0 + NB * tap + off, LANE), :] * w_ref[tap:tap + 1, lo:lo + LANE]
            cbuf[pl.ds(r0, LANE), lo:lo + LANE] = acc
        return carry

    lax.fori_loop(0, CH_ROWS // LANE, conv_rb, 0)

    def norm_rb(rb, carry):
        r0 = pl.multiple_of(rb * CONF_RB, CONF_RB)
        u = cbuf[pl.ds(r0, CONF_RB), :]
        xc = u - jnp.mean(u, axis=-1, keepdims=True)
        y = xc * lax.rsqrt(jnp.mean(xc * xc, axis=-1, keepdims=True) + EPS) * lng_ref[...] + lnb_ref[...]
        y = _silu(y).astype(BF16)
        o_ref[pl.ds(r0, CONF_RB), :] = jnp.dot(y, pw_ref[...], preferred_element_type=F32) + pwb_ref[...]
        return carry

    lax.fori_loop(0, CH_ROWS // CONF_RB, norm_rb, 0, unroll=True)


def _conformer(l, cols, n_ctx_rows, with_ctx, w, b, lng, lnb, pw, pwb):
    rows = cols.shape[0]
    nc = rows // CH_ROWS
    nc_ctx = n_ctx_rows // CH_ROWS
    first = 0 if with_ctx else nc_ctx
    hpc = CH_ROWS // CONF_HALO
    n_halo = rows // CONF_HALO
    return pl.pallas_call(
        functools.partial(_conf_kernel, first, nc_ctx, nc),
        out_shape=jax.ShapeDtypeStruct(((nc - first) * CH_ROWS, CONF_W), F32),
        grid=(nc - first,),
        in_specs=[
            pl.BlockSpec((CH_ROWS, 2 * CONF_W), lambda i: (i + first, 0)),
            pl.BlockSpec((CONF_HALO, 2 * CONF_W), lambda i: (jnp.maximum((i + first) * hpc - 1, 0), 0)),
            pl.BlockSpec((CONF_HALO, 2 * CONF_W), lambda i: (jnp.minimum((i + first + 1) * hpc, n_halo - 1), 0)),
            _layer_spec(l, (CONF_K, CONF_W)),
            _layer_spec(l, (1, CONF_W)),
            _layer_spec(l, (1, CONF_W)),
            _layer_spec(l, (1, CONF_W)),
            _layer_spec(l, (CONF_W, CONF_W)),
            _layer_spec(l, (1, CONF_W)),
        ],
        out_specs=pl.BlockSpec((CH_ROWS, CONF_W), lambda i: (i, 0)),
        scratch_shapes=[pltpu.VMEM((CONF_W // LANE, CH_ROWS + 2 * CONF_HALO, LANE), F32),
                        pltpu.VMEM((CH_ROWS, CONF_W), F32)],
        compiler_params=_cparams(("arbitrary",)),
        name="conformer",
    )(cols, cols, cols, w, b, lng, lnb, pw, pwb)


def _s5_kernel(direction, *refs):
    if direction == 0:
        u_ref, bmat_ref, ar_ref, ai_ref, cmat_ref, o_ref, hbuf, st = refs
    else:
        (u_ref, bmat_ref, ar_ref, ai_ref, cmat_ref, yf_ref, dsk_ref, gw_ref, gb_ref,
         o_ref, hbuf, st) = refs
    i = pl.program_id(0)

    @pl.when(i == 0)
    def _():
        st[...] = jnp.zeros_like(st)

    u = u_ref[...]
    rb = 2 * LANE
    for k in range(u.shape[0] // rb):
        hbuf[k * rb:(k + 1) * rb, :] = jnp.dot(u[k * rb:(k + 1) * rb, :].astype(BF16), bmat_ref[...],
                                                preferred_element_type=F32)
    ar = jnp.broadcast_to(ar_ref[...], (NB, S5_S))
    ai = jnp.broadcast_to(ai_ref[...], (NB, S5_S))

    def step(j, carry):
        hr, hi = carry
        t = j if direction == 0 else S5_TL - 1 - j
        r0 = pl.multiple_of(t * NB, NB)
        nr = ar * hr - ai * hi + hbuf[pl.ds(r0, NB), 0:S5_S]
        ni = ar * hi + ai * hr + hbuf[pl.ds(r0, NB), S5_S:]
        hbuf[pl.ds(r0, NB), 0:S5_S] = nr
        hbuf[pl.ds(r0, NB), S5_S:] = ni
        return nr, ni

    hr, hi = lax.fori_loop(0, S5_TL, step, (st[0], st[1]), unroll=2)
    st[0] = hr
    st[1] = hi
    for k in range(u.shape[0] // rb):
        rows = slice(k * rb, (k + 1) * rb)
        o_ref[rows, :] = jnp.dot(hbuf[rows, :].astype(BF16), cmat_ref[...], preferred_element_type=F32)
    if direction == 1:
        for k in range(u.shape[0] // rb):
            rows = slice(k * rb, (k + 1) * rb)
            v = jax.nn.gelu(o_ref[rows, :] + yf_ref[rows, :] + dsk_ref[...] * u[rows, :], approximate=True)
            gv = jnp.dot(v.astype(BF16), gw_ref[...], preferred_element_type=F32) + gb_ref[...]
            o_ref[rows, :] = gv[:, :S5_W] * jax.nn.sigmoid(gv[:, S5_W:])


def _s5_pass(l, direction, u, n_ctx_rows, bmat, ar, ai, cmat, yf, dsk, gw, gb):
    rows = u.shape[0]
    tr = S5_TL * NB
    nblk = rows // tr
    nb_ctx = n_ctx_rows // tr
    blk = functools.partial(_chunk_of, direction, nc_ctx=nb_ctx, nc=nblk)

    def dir_spec(shape):
        nd = len(shape)
        return pl.BlockSpec((None, None) + tuple(shape), lambda i: (l, direction) + (0,) * nd)

    in_specs = [
        pl.BlockSpec((tr, S5_W), lambda i: (blk(i), 0)),
        dir_spec((S5_W, 2 * S5_S)),
        dir_spec((1, S5_S)),
        dir_spec((1, S5_S)),
        _layer_spec(l, (2 * S5_S, S5_W)),
    ]
    args = [u, bmat, ar, ai, cmat]
    if direction == 1:
        in_specs += [
            pl.BlockSpec((tr, S5_W), lambda i: (blk(i), 0)),
            _layer_spec(l, (1, S5_W)),
            _layer_spec(l, (S5_W, 2 * S5_W)),
            _layer_spec(l, (1, 2 * S5_W)),
        ]
        args += [yf, dsk, gw, gb]
    return pl.pallas_call(
        functools.partial(_s5_kernel, direction),
        out_shape=jax.ShapeDtypeStruct((rows, S5_W), F32),
        grid=(nblk,),
        in_specs=in_specs,
        out_specs=pl.BlockSpec((tr, S5_W), lambda i: (blk(i), 0)),
        scratch_shapes=[pltpu.VMEM((tr, 2 * S5_S), F32), pltpu.VMEM((2, NB, S5_S), F32)],
        compiler_params=_cparams(("arbitrary",)),
        name="s5_fwd" if direction == 0 else "s5_bwd",
    )(*args)


def _s5_operators(lam_re, lam_im, log_step, b_re, b_im, c_re, c_im):
    depth = lam_re.shape[0]
    eye = jnp.eye(S5_G, dtype=F32)
    step = jnp.exp(log_step)[..., None]
    mag = jnp.exp(lam_re * step)
    ar, ai = mag * jnp.cos(lam_im * step), mag * jnp.sin(lam_im * step)
    inv_den = 1.0 / (lam_re * lam_re + lam_im * lam_im)
    cr = ((ar - 1.0) * lam_re + ai * lam_im) * inv_den
    ci = (ai * lam_re - (ar - 1.0) * lam_im) * inv_den
    bre, bim = b_re[:, None], b_im[:, None]
    bbr = cr[..., None] * bre - ci[..., None] * bim
    bbi = cr[..., None] * bim + ci[..., None] * bre
    bd_r = jnp.einsum("ldgph,gk->ldghkp", bbr, eye).reshape(depth, 2, S5_W, S5_S)
    bd_i = jnp.einsum("ldgph,gk->ldghkp", bbi, eye).reshape(depth, 2, S5_W, S5_S)
    bmat = jnp.concatenate([bd_r, bd_i], axis=-1).astype(BF16)
    cre = jnp.einsum("lghp,gk->lgpkh", c_re, eye).reshape(depth, S5_S, S5_W)
    cim = jnp.einsum("lghp,gk->lgpkh", c_im, eye).reshape(depth, S5_S, S5_W)
    cmat = jnp.concatenate([cre, -cim], axis=1).astype(BF16)
    return bmat, ar.reshape(depth, 2, 1, S5_S), ai.reshape(depth, 2, 1, S5_S), cmat


def _route_rows(lg):
    m = jnp.max(lg, axis=0, keepdims=True)
    ex = jnp.exp(lg - m)
    probs = ex / jnp.sum(ex, axis=0, keepdims=True)
    p = [probs[e:e + 1, :] for e in range(NE)]
    tops = []
    for q in range(NGRP):
        v = p[q * EPG:(q + 1) * EPG]
        m1 = functools.reduce(jnp.maximum, v)
        i1 = jnp.full_like(m1, float(EPG - 1))
        for j in range(EPG - 2, -1, -1):
            i1 = jnp.where(v[j] == m1, float(j), i1)
        rest = [jnp.where(i1 == float(j), -jnp.inf, v[j]) for j in range(EPG)]
        m2 = functools.reduce(jnp.maximum, rest)
        i2 = jnp.full_like(m2, float(EPG - 1))
        for j in range(EPG - 2, -1, -1):
            i2 = jnp.where(jnp.logical_and(rest[j] == m2, i1 != float(j)), float(j), i2)
        tops.append((m1, i1, m2, i2))
    score = [t[0] + t[2] for t in tops]
    best = functools.reduce(jnp.maximum, score)
    sel = tops[NGRP - 1] + (jnp.full_like(best, float(NGRP - 1)),)
    for q in range(NGRP - 2, -1, -1):
        hit = score[q] == best
        sel = tuple(jnp.where(hit, a, b) for a, b in zip(tops[q] + (jnp.full_like(best, float(q)),), sel))
    m1, i1, m2, i2, grp = sel
    den = m1 + m2
    return grp * EPG + i1, grp * EPG + i2, m1 / den, m2 / den


def _dot_nt_3pass(a, b):
    nt = (((1,), (1,)), ((), ()))
    a_hi = a.astype(BF16)
    b_hi = b.astype(BF16)
    a_lo = (a - a_hi.astype(F32)).astype(BF16)
    b_lo = (b - b_hi.astype(F32)).astype(BF16)
    out = lax.dot_general(a_hi, b_hi, nt, preferred_element_type=F32)
    out = out + lax.dot_general(a_lo, b_hi, nt, preferred_element_type=F32)
    return out + lax.dot_general(a_hi, b_lo, nt, preferred_element_type=F32)


def _out_kernel(a_ref, b_ref, s_ref, x_ref, mod_ref, g_ref, w_ref, rw_ref, rb_ref,
                x1_ref, h2_ref, rt_ref, gc_ref, cnt_ref):
    tm = x_ref.shape[0]
    rb = 2 * LANE
    for k in range(tm // rb):
        rows = slice(k * rb, (k + 1) * rb)
        mix = jnp.dot(a_ref[rows, :].astype(BF16), w_ref[0:SSD_W, :], preferred_element_type=F32)
        mix = mix + jnp.dot(b_ref[rows, :].astype(BF16), w_ref[SSD_W:SSD_W + CONF_W, :],
                            preferred_element_type=F32)
        mix = mix + jnp.dot(s_ref[rows, :].astype(BF16), w_ref[SSD_W + CONF_W:, :], preferred_element_type=F32)
        x1 = (x_ref[rows, :].reshape(rb // NB, NB, D) + mod_ref[2][None] * mix.reshape(rb // NB, NB, D))
        x1 = x1.reshape(rb, D)
        x1_ref[rows, :] = x1
        ms = jnp.mean(x1 * x1, axis=-1, keepdims=True)
        xn = x1 * lax.rsqrt(ms + EPS) * g_ref[...]
        h2_ref[rows, :] = (xn.reshape(rb // NB, NB, D) * (1.0 + mod_ref[4])[None] + mod_ref[3][None]).reshape(rb, D)
    lg = _dot_nt_3pass(rw_ref[...], h2_ref[...]) + rb_ref[...]
    e1, e2, g1, g2 = _route_rows(lg)
    eid = lax.broadcasted_iota(I32, (NE, tm), 0).astype(F32)
    oh0 = (eid == e1).astype(F32)
    oh1 = (eid == e2).astype(F32)
    t0 = lax.broadcasted_iota(I32, (tm, tm), 0)
    t1 = lax.broadcasted_iota(I32, (tm, tm), 1)
    before = (t0 < t1).astype(BF16)
    pre0 = jnp.dot(oh0.astype(BF16), before, preferred_element_type=F32)
    pre1 = jnp.dot(oh1.astype(BF16), before, preferred_element_type=F32)
    lr0 = jnp.sum(oh0 * pre0, axis=0, keepdims=True)
    lr1 = jnp.sum(oh1 * pre1, axis=0, keepdims=True)
    rt_ref[...] = jnp.concatenate([e1, e2, lr0, lr1, jnp.zeros((NB - 4, tm), F32)], axis=0)
    gates = jnp.concatenate([g1, g2, jnp.zeros((LANE - 2, tm), F32)], axis=0)
    gc_ref[...] = gates.T
    cnt = jnp.concatenate([jnp.sum(oh0, axis=1, keepdims=True), jnp.sum(oh1, axis=1, keepdims=True)], axis=0)
    cnt_ref[...] = jnp.broadcast_to(cnt, (2 * NE, LANE))


def _out_proj(l, a, bconf, s, x, modtabs, g_ffn, w_out, rw_t, rb, n_ctx_rows, conf_first, blk0, nblk):
    tm = ROW_TM
    ctx_blk = n_ctx_rows // tm
    rows = nblk * tm
    return pl.pallas_call(
        _out_kernel,
        out_shape=[
            jax.ShapeDtypeStruct((rows, D), F32),
            jax.ShapeDtypeStruct((rows, D), F32),
            jax.ShapeDtypeStruct((NB, rows), F32),
            jax.ShapeDtypeStruct((rows, LANE), F32),
            jax.ShapeDtypeStruct((nblk, 2 * NE, LANE), F32),
        ],
        grid=(nblk,),
        in_specs=[
            pl.BlockSpec((tm, SSD_W), lambda i: (i + blk0, 0)),
            pl.BlockSpec((tm, CONF_W), lambda i: (i + blk0 - conf_first, 0)),
            pl.BlockSpec((tm, S5_W), lambda i: (i + blk0, 0)),
            pl.BlockSpec((tm, D), lambda i: (i + blk0, 0)),
            _mod_spec(l, ctx_blk, blk0),
            _layer_spec(l, (1, D)),
            _layer_spec(l, (D, D)),
            _const_spec((NE, D)),
            _const_spec((NE, 1)),
        ],
        out_specs=[
            pl.BlockSpec((tm, D), lambda i: (i, 0)),
            pl.BlockSpec((tm, D), lambda i: (i, 0)),
            pl.BlockSpec((NB, tm), lambda i: (0, i)),
            pl.BlockSpec((tm, LANE), lambda i: (i, 0)),
            pl.BlockSpec((None, 2 * NE, LANE), lambda i: (i, 0, 0)),
        ],
        compiler_params=_cparams(("arbitrary",)),
        name="out_proj_route",
    )(a, bconf, s, x, modtabs, g_ffn, w_out, rw_t, rb)


def _moe_kernel(l, second, it_ref, ie_ref, lo_ref, hi_ref, nx_ref, h_ref, wg_hbm, wu_hbm, wd_hbm, *rest):
    rest = rest[1:] if second else rest
    out_ref, stage, wgb, wub, wdb, sem, cur = rest
    o_ref = out_ref if second else out_ref.at[0]
    w = pl.program_id(0)
    lo = lo_ref[w]
    hi = hi_ref[w]

    def fetch(e, slot):
        return [pltpu.make_async_copy(src.at[l, e], stage.at[slot, k], sem.at[slot, k])
                for k, src in enumerate((wg_hbm, wu_hbm, wd_hbm))]

    @pl.when(w == 0)
    def _():
        cur[0] = -1
        cur[1] = nx_ref[0]
        cur[2] = 0

        @pl.when(nx_ref[0] >= 0)
        def _():
            for c in fetch(nx_ref[0], 0):
                c.start()

    @pl.when(hi > lo)
    def _():
        e = ie_ref[w]

        @pl.when(cur[0] != e)
        def _():
            slot = cur[2]
            for c in fetch(e, slot):
                c.wait()
            nxt = nx_ref[1 + e]

            @pl.when(nxt >= 0)
            def _():
                for c in fetch(nxt, 1 - slot):
                    c.start()

            wgb[...] = stage[slot, 0].astype(BF16)
            wub[...] = stage[slot, 1].astype(BF16)
            wdb[...] = stage[slot, 2].astype(BF16)
            cur[0] = e
            cur[1] = nxt
            cur[2] = 1 - slot

        for r0 in range(0, MOE_TM, MOE_SUB):
            @pl.when(jnp.logical_and(lo < r0 + MOE_SUB, hi > r0))
            def _():
                rows = slice(r0, r0 + MOE_SUB)
                h = h_ref[rows, :].astype(BF16)
                gate = jnp.dot(h, wgb[...], preferred_element_type=F32)
                up = jnp.dot(h, wub[...], preferred_element_type=F32)
                act = (_silu(gate) * up).astype(BF16)
                y = jnp.dot(act, wdb[...], preferred_element_type=F32)
                r = r0 + lax.broadcasted_iota(I32, (MOE_SUB, 1), 0)
                keep = jnp.logical_and(r >= lo, r < hi)

                @pl.when(lo <= r0)
                def _():
                    o_ref[rows, :] = jnp.where(keep, y, 0.0)
                    if not second:
                        out_ref[1, rows, :] = jnp.zeros((MOE_SUB, D), F32)

                @pl.when(lo > r0)
                def _():
                    o_ref[rows, :] = jnp.where(keep, y, o_ref[rows, :])


def _moe_experts(l, hs, items, tile0, w_gate, w_up, w_down, prev):
    item_tile, item_expert, item_lo, item_hi, next_expert = items
    n_items = item_tile.shape[0]
    n_prefetch = 5
    second = prev is not None
    hbm = pl.BlockSpec(memory_space=pl.ANY)
    in_specs = [pl.BlockSpec((MOE_TM, D), lambda w, it, ie, lo, hi, nx: (it[w] - tile0, 0)), hbm, hbm, hbm]
    args = [item_tile, item_expert, item_lo, item_hi, next_expert, hs, w_gate, w_up, w_down]
    aliases = {}
    if second:
        in_specs.append(hbm)
        args.append(prev)
        aliases = {len(args) - 1: 0}
        out_spec = pl.BlockSpec((None, MOE_TM, D), lambda w, it, ie, lo, hi, nx: (1, it[w] - tile0, 0))
    else:
        out_spec = pl.BlockSpec((2, MOE_TM, D), lambda w, it, ie, lo, hi, nx: (0, it[w] - tile0, 0))
    n_mats = 3
    return pl.pallas_call(
        functools.partial(_moe_kernel, l, second),
        out_shape=jax.ShapeDtypeStruct((2, hs.shape[0], D), F32),
        grid_spec=pltpu.PrefetchScalarGridSpec(
            num_scalar_prefetch=n_prefetch,
            grid=(n_items,),
            in_specs=in_specs,
            out_specs=out_spec,
            scratch_shapes=[pltpu.VMEM((2, n_mats, D, D), F32)] + [pltpu.VMEM((D, D), BF16)] * n_mats
            + [pltpu.SemaphoreType.DMA((2, n_mats)), pltpu.SMEM((3,), I32)],
        ),
        input_output_aliases=aliases,
        compiler_params=_cparams(("arbitrary",)),
        name="moe_experts",
    )(*args)


def _moe_plan(route, cnt, tile_parts):
    rows = route.shape[1]
    nblk = cnt.shape[0]
    n_flat = 2 * rows
    e = route[0:2].astype(I32)
    lrank = route[2:4].astype(I32)
    counts = cnt[:, :, 0].astype(I32).reshape(nblk, 2, NE)
    per = jnp.transpose(counts, (1, 0, 2)).reshape(2 * nblk, NE)
    before = jnp.cumsum(per, axis=0) - per
    gcount = jnp.sum(per, axis=0)
    gend = jnp.cumsum(gcount)
    gstart = gend - gcount
    base = jnp.transpose((before + gstart[None, :]).reshape(2, nblk, NE), (2, 0, 1))[..., None]
    onehot = e.reshape(1, 2, nblk, ROW_TM) == jnp.arange(NE, dtype=I32).reshape(NE, 1, 1, 1)
    inv = jnp.sum(jnp.where(onehot, base, 0), axis=0).reshape(2, rows) + lrank
    keys = (e * 65536).reshape(-1) + jnp.arange(n_flat, dtype=I32)
    order = lax.sort(keys, is_stable=False) & 0xFFFF
    tok = jnp.where(order >= rows, order - rows, order)
    items = []
    for t0, tn in tile_parts:
        row_lo, row_hi = t0 * MOE_TM, (t0 + tn) * MOE_TM
        tile_start = (t0 + jnp.arange(tn, dtype=I32)) * MOE_TM
        bnd = lax.sort(jnp.concatenate([tile_start, jnp.clip(gstart[1:], row_lo, row_hi)]), is_stable=False)
        bnd_hi = jnp.concatenate([bnd[1:], jnp.full((1,), row_hi, I32)])
        item_tile = jnp.minimum(bnd // MOE_TM, t0 + tn - 1)
        first_row = jnp.minimum(bnd, row_hi - 1)[:, None]
        item_expert = jnp.minimum(jnp.sum((gend[None, :] <= first_row).astype(I32), axis=1), NE - 1)
        ids = jnp.arange(NE, dtype=I32)
        present = jnp.logical_and(gend > row_lo, gstart < row_hi) & (gcount > 0)
        later = jnp.logical_and(present[None, :], ids[None, :] > jnp.arange(-1, NE, dtype=I32)[:, None])
        next_expert = jnp.min(jnp.where(later, ids[None, :], NE), axis=1)
        next_expert = jnp.where(next_expert == NE, -1, next_expert).astype(I32)
        items.append((item_tile, item_expert, bnd - item_tile * MOE_TM, bnd_hi - item_tile * MOE_TM, next_expert))
    return tok, inv, items


def _final_kernel(x_ref, ya_ref, yb_ref, gc_ref, mod_ref, g_ref, o_ref, scr):
    tm = x_ref.shape[0]
    gc = gc_ref[...]
    ff = (gc[:, 0:1] * ya_ref[...] + gc[:, 1:2] * yb_ref[...]).reshape(tm // NB, NB, D)
    x2 = (x_ref[...].reshape(tm // NB, NB, D) + mod_ref[5][None] * ff).reshape(tm, D)
    ms = jnp.mean(x2 * x2, axis=-1, keepdims=True)
    x2 = x2 * lax.rsqrt(ms + EPS) * g_ref[...]
    ncb = D // LANE
    for k in range(ncb):
        scr[k] = x2[:, k * LANE:(k + 1) * LANE]
    for b in range(NB):
        for k in range(ncb):
            o_ref[b, :, k * LANE:(k + 1) * LANE] = scr[k, pl.ds(b, ROW_TQ, stride=NB), :]


def _final_combine(l, x1, ya, yb, gcol, modtabs, g_final):
    tm = ROW_TM
    rows = x1.shape[0]
    row_spec = pl.BlockSpec((tm, D), lambda i: (i, 0))
    return pl.pallas_call(
        _final_kernel,
        out_shape=jax.ShapeDtypeStruct((NB, rows // NB, D), F32),
        grid=(rows // tm,),
        in_specs=[row_spec, row_spec, row_spec, pl.BlockSpec((tm, LANE), lambda i: (i, 0)),
                  _mod_spec(l, 0), _const_spec((1, D))],
        out_specs=pl.BlockSpec((NB, ROW_TQ, D), lambda i: (0, i, 0)),
        scratch_shapes=[pltpu.VMEM((D // LANE, tm, LANE), F32)],
        compiler_params=_cparams(("arbitrary",)),
        name="moe_combine_final",
    )(x1, ya, yb, gcol, modtabs, g_final)


def _grid_pos_embed(rows_n):
    rr, cc = jnp.meshgrid(jnp.arange(rows_n, dtype=F32), jnp.arange(GRID_W, dtype=F32), indexing="ij")
    quarter = D // 4
    inv_freq = jnp.exp(-math.log(10000.0) * jnp.arange(quarter, dtype=F32) / quarter)

    def emb(pos):
        ang = pos.reshape(-1)[:, None] * inv_freq[None, :]
        return jnp.concatenate([jnp.sin(ang), jnp.cos(ang)], axis=-1)

    return jnp.concatenate([emb(rr), emb(cc)], axis=-1)


def _take_rows(a, idx):
    return a.at[idx].get(mode="promise_in_bounds")


def _pad_last(v, width):
    return jnp.pad(v, [(0, 0)] * (v.ndim - 1) + [(0, width - v.shape[-1])])


def kernel(x, c, ctx, c_ctx, w_ada, b_ada, g_mix, w_in, ssd_conv_w, ssd_conv_b, ssd_dt_bias, ssd_a_log, ssd_d, ssd_norm_g, conf_dw_w, conf_dw_b, conf_ln_g, conf_ln_b, conf_pw_w, conf_pw_b, s5_lambda_re, s5_lambda_im, s5_log_step, s5_b_re, s5_b_im, s5_c_re, s5_c_im, s5_d, s5_glu_w, s5_glu_b, w_out, g_ffn, router_w, router_b, exp_w_gate, exp_w_up, exp_w_down, g_final):
    bsz, seq, _ = x.shape
    ctx_len = ctx.shape[1]
    depth = w_ada.shape[0]
    assert bsz == NB and seq % SSD_CHUNK == 0 and ctx_len % SSD_CHUNK == 0
    n_ctx_rows = ctx_len * NB
    rows_all = (ctx_len + seq) * NB

    cond = jnp.concatenate([c, c_ctx[None, :], jnp.zeros((2 * NB - bsz - 1, D), F32)], axis=0)
    mod = _modulation(cond, w_ada, b_ada)
    mod = mod.reshape(depth, 2 * NB, N_MOD, D)
    mod_lat = jnp.transpose(mod[:, :NB], (0, 2, 1, 3))
    mod_ctx = jnp.broadcast_to(mod[:, NB][:, :, None, :], (depth, N_MOD, NB, D))
    modtabs = jnp.stack([mod_ctx, mod_lat], axis=1)

    o_b = SSD_W + SSD_XBC + 2 * SSD_H
    w_in_p = jnp.concatenate(
        [w_in[:, :, :o_b], jnp.zeros((depth, D, DT_PAD - 2 * SSD_H), F32), w_in[:, :, o_b:]], axis=2).astype(BF16)
    g_mix3 = g_mix.reshape(depth, 1, D)
    ssd_cb = ssd_conv_b.reshape(depth, 1, SSD_XBC)
    ssd_dtb = _pad_last(ssd_dt_bias.reshape(depth, 1, 2 * SSD_H), DT_PAD)
    ssd_alog = _pad_last(ssd_a_log.reshape(depth, 1, 2 * SSD_H), DT_PAD)
    ssd_dsk = jnp.repeat(ssd_d, SSD_P, axis=1).reshape(depth, 1, SSD_W)
    ssd_ng = ssd_norm_g.reshape(depth, 1, SSD_W)
    conf_b = conf_dw_b.reshape(depth, 1, CONF_W)
    conf_lg = conf_ln_g.reshape(depth, 1, CONF_W)
    conf_lb = conf_ln_b.reshape(depth, 1, CONF_W)
    conf_pw = conf_pw_w.astype(BF16)
    conf_pb = conf_pw_b.reshape(depth, 1, CONF_W)
    s5_bmat, s5_ar, s5_ai, s5_cmat = _s5_operators(s5_lambda_re, s5_lambda_im, s5_log_step,
                                                   s5_b_re, s5_b_im, s5_c_re, s5_c_im)
    s5_dsk = s5_d.reshape(depth, 1, S5_W)
    s5_gw = s5_glu_w.astype(BF16)
    s5_gb = s5_glu_b.reshape(depth, 1, 2 * S5_W)
    w_out_b = w_out.astype(BF16)
    g_ffn3 = g_ffn.reshape(depth, 1, D)
    router_w_t = router_w.T
    router_b2 = router_b.reshape(NE, 1)
    g_final2 = g_final.reshape(1, D)
    pos = _grid_pos_embed(seq // GRID_W)

    pending = (ctx, x)
    for l in range(depth):
        last = l == depth - 1
        ssd_cols, conf_cols, s5_u, xall = _in_proj(l, l == 0, pending, pos, modtabs, g_mix3, w_in_p,
                                                   rows_all, n_ctx_rows)
        ssd_fwd = _ssd_pass(l, 0, ssd_cols, n_ctx_rows, ssd_conv_w, ssd_cb, ssd_dtb, ssd_alog, ssd_dsk, None)
        a_mix = _ssd_pass(l, 1, ssd_cols, n_ctx_rows, None, None, ssd_dtb, ssd_alog, ssd_ng, ssd_fwd)
        b_mix = _conformer(l, conf_cols, n_ctx_rows, not last, conf_dw_w, conf_b, conf_lg, conf_lb, conf_pw, conf_pb)
        yf = _s5_pass(l, 0, s5_u, n_ctx_rows, s5_bmat, s5_ar, s5_ai, s5_cmat, None, None, None, None)
        s_mix = _s5_pass(l, 1, s5_u, n_ctx_rows, s5_bmat, s5_ar, s5_ai, s5_cmat, yf, s5_dsk, s5_gw, s5_gb)
        ctx_blk = n_ctx_rows // ROW_TM
        first = ctx_blk if last else 0
        n_out = rows_all // ROW_TM - first
        x1, h2, route, gcol, cnt = _out_proj(l, a_mix, b_mix, s_mix, xall, modtabs, g_ffn3, w_out_b,
                                             router_w_t, router_b2, n_ctx_rows, first, first, n_out)
        n_tiles = 2 * n_out * ROW_TM // MOE_TM
        assert n_tiles % 2 == 0
        tile_parts = [(0, n_tiles // 2), (n_tiles // 2, n_tiles // 2)]
        tok, inv, items = _moe_plan(route, cnt, tile_parts)
        hs = [_take_rows(h2, tok[t0 * MOE_TM:(t0 + tn) * MOE_TM]) for t0, tn in tile_parts]
        y = None
        for (t0, tn), h, it in zip(tile_parts, hs, items):
            y = _moe_experts(l, h, it, t0, exp_w_gate, exp_w_up, exp_w_down, y)
        y = y.reshape(n_tiles * MOE_TM, D)
        ya = _take_rows(y, inv[0])
        yb = _take_rows(y, inv[1])
        pending = (x1, ya, yb, gcol)
    return _final_combine(depth - 1, *pending, modtabs, g_final2)
```
